```python
import jax
import jax.numpy as jnp
from jax import lax
import numpy as np

D_MODEL = 2048
BATCH = 4
SEQ = 2048
DEPTH = 2
DEC_BATCH = 16
DEC_SEQ = 16
PAST_LEN = 4096

CHUNK = 64
EPS = 1e-6
SSD_INNER = D_MODEL
SSD_HEADDIM = 64
SSD_HEADS = SSD_INNER // SSD_HEADDIM
SSD_GROUPS = 4
SSD_STATE = 128
SSD_CONV = 4
SSD_CONV_DIM = SSD_INNER + 2 * SSD_GROUPS * SSD_STATE
SSD_CHUNK = CHUNK
GLA_HEADS = 4
GLA_KEY = D_MODEL // 2
GLA_VAL = D_MODEL
GLA_HEAD_K = GLA_KEY // GLA_HEADS
GLA_HEAD_V = GLA_VAL // GLA_HEADS
GLA_RANK = 16
GLA_GATE_NORM = 16.0
GLA_CHUNK = 16
N_BRANCH = 2
IN_SPLITS = (SSD_INNER, SSD_CONV_DIM, SSD_HEADS, GLA_KEY, GLA_KEY, GLA_VAL, GLA_VAL, GLA_RANK, N_BRANCH * D_MODEL)
IN_DIM = SSD_INNER + SSD_CONV_DIM + SSD_HEADS + 2 * GLA_KEY + 2 * GLA_VAL + GLA_RANK + N_BRANCH * D_MODEL
MEM_TOKENS = 256
MEM_HEADS = 4
MEM_HEAD_DIM = D_MODEL // MEM_HEADS
D_FF = 5632
FFN_CONV = 3

kernel_name = "hybrid_ssd_gla_streaming_encoder_step"


def _split(x, sizes):
    offsets = [int(o) for o in np.cumsum(sizes)[:-1]]
    return jnp.split(x, offsets, axis=-1)


def group_rmsnorm(x, g, groups):
    shp = x.shape
    xf = x.astype(jnp.float32).reshape(shp[:-1] + (groups, shp[-1] // groups))
    xf = xf * lax.rsqrt(jnp.mean(xf * xf, axis=-1, keepdims=True) + EPS)
    return xf.reshape(shp).astype(x.dtype) * g


def causal_dwconv(x, buf, w, b):
    width = w.shape[0]
    L = x.shape[1]
    xp = jnp.concatenate([buf.astype(x.dtype), x], axis=1)
    y = xp[:, 0:L] * w[0]
    for i in range(1, width):
        y = y + xp[:, i:i + L] * w[i]
    return y + b, xp[:, L:]


def _pad_time(t, pad):
    return jnp.pad(t, [(0, 0), (0, pad)] + [(0, 0)] * (t.ndim - 2))


def ssd_scan(x, dt, a, b_in, c_in, h0):
    out_dtype = x.dtype
    bt, L, H, P = x.shape
    G, N = b_in.shape[2], b_in.shape[3]
    R = H // G
    Q = min(SSD_CHUNK, L)
    pad = (-L) % Q
    f32 = jnp.float32
    x, dt, b_in, c_in = x.astype(f32), dt.astype(f32), b_in.astype(f32), c_in.astype(f32)
    if pad:
        x, dt, b_in, c_in = _pad_time(x, pad), _pad_time(dt, pad), _pad_time(b_in, pad), _pad_time(c_in, pad)
    nc = (L + pad) // Q
    xr = x.reshape(bt, nc, Q, G, R, P)
    dtr = dt.reshape(bt, nc, Q, G, R)
    br = b_in.reshape(bt, nc, Q, G, N)
    cr = c_in.reshape(bt, nc, Q, G, N)
    acum = jnp.cumsum(dtr * a.astype(f32).reshape(G, R), axis=2)
    causal = jnp.tril(jnp.ones((Q, Q), dtype=bool))
    seg = acum[:, :, :, None] - acum[:, :, None, :]
    decay = jnp.exp(jnp.where(causal[:, :, None, None], seg, -jnp.inf))
    cb = jnp.einsum('bcign,bcjgn->bcijg', cr, br)
    w_ij = cb[..., None] * decay * dtr[:, :, None]
    y_diag = jnp.einsum('bcijgr,bcjgrp->bcigrp', w_ij, xr)
    to_end = jnp.exp(acum[:, :, -1:] - acum) * dtr
    chunk_dec = jnp.exp(acum[:, :, -1])
    exp_acum = jnp.exp(acum)

    def step(h, inp):
        b_c, te_c, x_c, c_c, ea_c, dec_c = inp
        y_off = jnp.einsum('bign,bgrpn->bigrp', c_c, h) * ea_c[..., None]
        h_new = h * dec_c[..., None, None] + jnp.einsum('bjgn,bjgr,bjgrp->bgrpn', b_c, te_c, x_c)
        return h_new, y_off

    xs = tuple(jnp.moveaxis(t, 1, 0) for t in (br, to_end, xr, cr, exp_acum, chunk_dec))
    h_last, y_off = lax.scan(step, h0.astype(f32).reshape(bt, G, R, P, N), xs)
    y = y_diag + jnp.moveaxis(y_off, 0, 1)
    y = y.reshape(bt, nc * Q, H, P)[:, :L]
    return y.astype(out_dtype), h_last.reshape(bt, H, P, N).astype(h0.dtype)


def gla_scan(q, k, v, log_a, s0):
    out_dtype = v.dtype
    bt, L, H, K = q.shape
    V = v.shape[-1]
    Q = min(GLA_CHUNK, L)
    pad = (-L) % Q
    f32 = jnp.float32
    q, k, v, log_a = q.astype(f32), k.astype(f32), v.astype(f32), log_a.astype(f32)
    if pad:
        q, k, v, log_a = _pad_time(q, pad), _pad_time(k, pad), _pad_time(v, pad), _pad_time(log_a, pad)
    nc = (L + pad) // Q
    qr = q.reshape(bt, nc, Q, H, K)
    kr = k.reshape(bt, nc, Q, H, K)
    vr = v.reshape(bt, nc, Q, H, V)
    bcum = jnp.cumsum(log_a.reshape(bt, nc, Q, H, K), axis=2)
    q_dec = qr * jnp.exp(bcum)
    k_inv = kr * jnp.exp(-bcum)
    k_end = kr * jnp.exp(bcum[:, :, -1:] - bcum)
    chunk_dec = jnp.exp(bcum[:, :, -1])
    causal = jnp.tril(jnp.ones((Q, Q), dtype=bool))
    att = jnp.where(causal, jnp.einsum('bcihk,bcjhk->bchij', q_dec, k_inv), 0.0)
    o_intra = jnp.einsum('bchij,bcjhv->bcihv', att, vr)

    def step(s, inp):
        q_c, k_c, v_c, dec_c = inp
        o = jnp.einsum('bihk,bhkv->bihv', q_c, s)
        s_new = s * dec_c[..., None] + jnp.einsum('bjhk,bjhv->bhkv', k_c, v_c)
        return s_new, o

    xs = tuple(jnp.moveaxis(t, 1, 0) for t in (q_dec, k_end, vr, chunk_dec))
    s_last, o_inter = lax.scan(step, s0.astype(f32), xs)
    o = o_intra + jnp.moveaxis(o_inter, 0, 1)
    o = o.reshape(bt, nc * Q, H, V)[:, :L]
    return o.astype(out_dtype), s_last.astype(s0.dtype)


def token_mixer(h, w_in, ssd_conv_w, ssd_conv_b, ssd_dt_bias, ssd_a_log, ssd_d, ssd_norm,
                gla_wa2, gla_ba, gla_norm, w_branch, w_out, ssd_h0, ssd_conv0, gla_s0):
    bt, L, _ = h.shape
    z, xbc, dt, q, k, v, g, a_lr, gates = _split(h @ w_in, IN_SPLITS)
    xbc, ssd_conv_new = causal_dwconv(xbc, ssd_conv0, ssd_conv_w, ssd_conv_b)
    xbc = jax.nn.silu(xbc)
    xs, bs, cs = _split(xbc, (SSD_INNER, SSD_GROUPS * SSD_STATE, SSD_GROUPS * SSD_STATE))
    xs = xs.reshape(bt, L, SSD_HEADS, SSD_HEADDIM)
    bs = bs.reshape(bt, L, SSD_GROUPS, SSD_STATE)
    cs = cs.reshape(bt, L, SSD_GROUPS, SSD_STATE)
    dt = jax.nn.softplus(dt + ssd_dt_bias)
    y_ssd, ssd_h = ssd_scan(xs, dt, -jnp.exp(ssd_a_log), bs, cs, ssd_h0)
    y_ssd = (y_ssd + ssd_d[:, None] * xs).reshape(bt, L, SSD_INNER)
    y_ssd = group_rmsnorm(y_ssd * jax.nn.silu(z), ssd_norm, SSD_GROUPS)
    q = q.reshape(bt, L, GLA_HEADS, GLA_HEAD_K) * (GLA_HEAD_K ** -0.5)
    k = k.reshape(bt, L, GLA_HEADS, GLA_HEAD_K)
    v = v.reshape(bt, L, GLA_HEADS, GLA_HEAD_V)
    log_a = jax.nn.log_sigmoid((a_lr @ gla_wa2 + gla_ba).astype(jnp.float32)) / GLA_GATE_NORM
    log_a = log_a.reshape(bt, L, GLA_HEADS, GLA_HEAD_K)
    y_gla, gla_s = gla_scan(q, k, v, log_a, gla_s0)
    y_gla = group_rmsnorm(y_gla, gla_norm, 1).reshape(bt, L, GLA_VAL) * jax.nn.silu(g)
    branches = jnp.stack([y_ssd, y_gla], axis=2)
    branch_out = jnp.einsum('blnc,ncd->blnd', branches, w_branch)
    gate = jax.nn.sigmoid(gates.reshape(bt, L, N_BRANCH, D_MODEL))
    merged = jnp.sum(gate * branch_out, axis=2)
    return merged @ w_out, ssd_h, ssd_conv_new, gla_s


def memory_cross_attn(h, mem_k, mem_v, w_mq, w_mo):
    bt, L, _ = h.shape
    q = (h @ w_mq).reshape(bt, L, MEM_HEADS, MEM_HEAD_DIM)
    s = jnp.einsum('blhd,bmhd->bhlm', q, mem_k).astype(jnp.float32) * (MEM_HEAD_DIM ** -0.5)
    p = jax.nn.softmax(s, axis=-1).astype(mem_v.dtype)
    o = jnp.einsum('bhlm,bmhd->blhd', p, mem_v).reshape(bt, L, D_MODEL)
    return o @ w_mo


def conv_ffn(h, w_ffn_in, ffn_conv_w, ffn_conv_b, w_ffn_out, conv0):
    up, conv_new = causal_dwconv(h @ w_ffn_in, conv0, ffn_conv_w, ffn_conv_b)
    u, gt = _split(up, (D_FF, D_FF))
    return (jax.nn.silu(gt) * u) @ w_ffn_out, conv_new


def run_trunk(x, mem_k, mem_v, st_ssd, st_ssd_conv, st_gla, st_ffn_conv, layer_w, norm_final):
    (norm_mix, w_in, ssd_conv_w, ssd_conv_b, ssd_dt_bias, ssd_a_log, ssd_d, ssd_norm,
     gla_wa2, gla_ba, gla_norm, w_branch, w_out, norm_mem, w_mq, w_mo,
     norm_ffn, w_ffn_in, ffn_conv_w, ffn_conv_b, w_ffn_out) = layer_w
    new_ssd, new_ssd_conv, new_gla, new_ffn = [], [], [], []
    for i in range(DEPTH):
        mix, s_h, s_c, g_s = token_mixer(
            group_rmsnorm(x, norm_mix[i], 1), w_in[i], ssd_conv_w[i], ssd_conv_b[i], ssd_dt_bias[i],
            ssd_a_log[i], ssd_d[i], ssd_norm[i], gla_wa2[i], gla_ba[i], gla_norm[i], w_branch[i], w_out[i],
            st_ssd[i], st_ssd_conv[i], st_gla[i])
        x = x + mix
        x = x + memory_cross_attn(group_rmsnorm(x, norm_mem[i], 1), mem_k[i], mem_v[i], w_mq[i], w_mo[i])
        f, f_c = conv_ffn(group_rmsnorm(x, norm_ffn[i], 1), w_ffn_in[i], ffn_conv_w[i], ffn_conv_b[i],
                          w_ffn_out[i], st_ffn_conv[i])
        x = x + f
        new_ssd.append(s_h)
        new_ssd_conv.append(s_c)
        new_gla.append(g_s)
        new_ffn.append(f_c)
    y = group_rmsnorm(x, norm_final, 1)
    return y, jnp.stack(new_ssd), jnp.stack(new_ssd_conv), jnp.stack(new_gla), jnp.stack(new_ffn)


def setup_inputs(seed: int = 0) -> dict:
    key = jax.random.key(seed)
    ks = jax.random.split(key, 40)
    f32 = jnp.float32

    def nrm(k, shape, scale=1.0):
        return jax.random.normal(k, shape, f32) * scale

    def gain(k, shape):
        return 1.0 + 0.02 * jax.random.normal(k, shape, f32)

    dt0 = jnp.exp(jax.random.uniform(ks[12], (DEPTH, SSD_HEADS), f32, np.log(1e-3), np.log(1e-1)))
    return {
        "x_prompt": nrm(ks[0], (BATCH, SEQ, D_MODEL)),
        "x_sample": nrm(ks[1], (DEC_BATCH, DEC_SEQ, D_MODEL)),
        "mem_prompt": nrm(ks[2], (BATCH, MEM_TOKENS, D_MODEL)),
        "state_ssd": nrm(ks[3], (DEPTH, DEC_BATCH, SSD_HEADS, SSD_HEADDIM, SSD_STATE), 0.5),
        "state_ssd_conv": nrm(ks[4], (DEPTH, DEC_BATCH, SSD_CONV - 1, SSD_CONV_DIM)),
        "state_gla": nrm(ks[5], (DEPTH, DEC_BATCH, GLA_HEADS, GLA_HEAD_K, GLA_HEAD_V), 0.5),
        "state_ffn_conv": nrm(ks[6], (DEPTH, DEC_BATCH, FFN_CONV - 1, 2 * D_FF)),
        "cache_mem_k": nrm(ks[7], (DEPTH, DEC_BATCH, MEM_TOKENS, MEM_HEADS, MEM_HEAD_DIM)),
        "cache_mem_v": nrm(ks[8], (DEPTH, DEC_BATCH, MEM_TOKENS, MEM_HEADS, MEM_HEAD_DIM)),
        "norm_mix": gain(ks[9], (DEPTH, D_MODEL)),
        "w_in": nrm(ks[10], (DEPTH, D_MODEL, IN_DIM), D_MODEL ** -0.5),
        "ssd_conv_w": nrm(ks[11], (DEPTH, SSD_CONV, SSD_CONV_DIM), SSD_CONV ** -0.5),
        "ssd_conv_b": nrm(ks[13], (DEPTH, SSD_CONV_DIM), 0.02),
        "ssd_dt_bias": dt0 + jnp.log(-jnp.expm1(-dt0)),
        "ssd_a_log": jnp.log(jax.random.uniform(ks[14], (DEPTH, SSD_HEADS), f32, 1.0, 16.0)),
        "ssd_d": 1.0 + nrm(ks[15], (DEPTH, SSD_HEADS), 0.1),
        "ssd_norm": gain(ks[16], (DEPTH, SSD_INNER)),
        "gla_wa2": nrm(ks[17], (DEPTH, GLA_RANK, GLA_KEY), GLA_RANK ** -0.5),
        "gla_ba": nrm(ks[18], (DEPTH, GLA_KEY), 0.01),
        "gla_norm": gain(ks[19], (DEPTH, GLA_HEAD_V)),
        "w_branch": nrm(ks[20], (DEPTH, N_BRANCH, D_MODEL, D_MODEL), D_MODEL ** -0.5),
        "w_out": nrm(ks[21], (DEPTH, D_MODEL, D_MODEL), D_MODEL ** -0.5),
        "norm_mem": gain(ks[22], (DEPTH, D_MODEL)),
        "w_mq": nrm(ks[23], (DEPTH, D_MODEL, D_MODEL), D_MODEL ** -0.5),
        "w_mk": nrm(ks[24], (DEPTH, D_MODEL, D_MODEL), D_MODEL ** -0.5),
        "w_mv": nrm(ks[25], (DEPTH, D_MODEL, D_MODEL), D_MODEL ** -0.5),
        "w_mo": nrm(ks[26], (DEPTH, D_MODEL, D_MODEL), D_MODEL ** -0.5),
        "norm_ffn": gain(ks[27], (DEPTH, D_MODEL)),
        "w_ffn_in": nrm(ks[28], (DEPTH, D_MODEL, 2 * D_FF), D_MODEL ** -0.5),
        "ffn_conv_w": nrm(ks[29], (DEPTH, FFN_CONV, 2 * D_FF), FFN_CONV ** -0.5),
        "ffn_conv_b": nrm(ks[30], (DEPTH, 2 * D_FF), 0.02),
        "w_ffn_out": nrm(ks[31], (DEPTH, D_FF, D_MODEL), D_FF ** -0.5),
        "norm_final": gain(ks[32], (D_MODEL,)),
    }


def reference(x_prompt, x_sample, mem_prompt, state_ssd, state_ssd_conv, state_gla, state_ffn_conv,
              cache_mem_k, cache_mem_v, norm_mix, w_in, ssd_conv_w, ssd_conv_b, ssd_dt_bias, ssd_a_log,
              ssd_d, ssd_norm, gla_wa2, gla_ba, gla_norm, w_branch, w_out, norm_mem, w_mq, w_mk, w_mv,
              w_mo, norm_ffn, w_ffn_in, ffn_conv_w, ffn_conv_b, w_ffn_out, norm_final):
    layer_w = (norm_mix, w_in, ssd_conv_w, ssd_conv_b, ssd_dt_bias, ssd_a_log, ssd_d, ssd_norm,
               gla_wa2, gla_ba, gla_norm, w_branch, w_out, norm_mem, w_mq, w_mo,
               norm_ffn, w_ffn_in, ffn_conv_w, ffn_conv_b, w_ffn_out)
    dt = x_prompt.dtype
    p_mem_k = jnp.einsum('bmd,ldc->lbmc', mem_prompt, w_mk).reshape(DEPTH, BATCH, MEM_TOKENS, MEM_HEADS, MEM_HEAD_DIM)
    p_mem_v = jnp.einsum('bmd,ldc->lbmc', mem_prompt, w_mv).reshape(DEPTH, BATCH, MEM_TOKENS, MEM_HEADS, MEM_HEAD_DIM)
    y_prompt, p_ssd, p_ssd_conv, p_gla, p_ffn_conv = run_trunk(
        x_prompt, p_mem_k, p_mem_v,
        jnp.zeros((DEPTH, BATCH, SSD_HEADS, SSD_HEADDIM, SSD_STATE), dt),
        jnp.zeros((DEPTH, BATCH, SSD_CONV - 1, SSD_CONV_DIM), dt),
        jnp.zeros((DEPTH, BATCH, GLA_HEADS, GLA_HEAD_K, GLA_HEAD_V), dt),
        jnp.zeros((DEPTH, BATCH, FFN_CONV - 1, 2 * D_FF), dt),
        layer_w, norm_final)
    y_sample, s_ssd, s_ssd_conv, s_gla, s_ffn_conv = run_trunk(
        x_sample, cache_mem_k, cache_mem_v, state_ssd, state_ssd_conv, state_gla, state_ffn_conv,
        layer_w, norm_final)
    return (y_prompt, y_sample, p_ssd, p_ssd_conv, p_gla, p_ffn_conv, p_mem_k, p_mem_v,
            s_ssd, s_ssd_conv, s_gla, s_ffn_conv)
```

```python
import functools

import jax
import jax.numpy as jnp
from jax import lax
from jax.experimental import pallas as pl
from jax.experimental.pallas import tpu as pltpu

F32 = jnp.float32
BF16 = jnp.bfloat16

D_MODEL = 2048
DEPTH = 2
EPS = 1e-6
SSD_HEADS = 32
SSD_HEADDIM = 64
SSD_GROUPS = 4
SSD_STATE = 128
SSD_CONV = 4
SSD_CONV_DIM = 3072
GLA_HEADS = 4
GLA_HEAD_K = 256
GLA_HEAD_V = 512
GLA_RANK = 16
GLA_GATE_NORM = 16.0
GLA_CHUNK = 16
MEM_TOKENS = 256
MEM_HEADS = 4
MEM_HEAD_DIM = 512
D_FF = 5632
FFN_CONV = 3

LANES = 128
VMEM_CAP_BYTES = 56 * 2**20

P_Z, P_XS, P_V, P_G, P_GATE, P_BC, P_Q, P_K, P_N = 0, 2048, 4096, 6144, 8192, 12288, 13312, 14336, 15360
SM_DT, SM_ALR = 0, 32

NT_DIMS = (((1,), (1,)), ((), ()))
TN_DIMS = (((0,), (0,)), ((), ()))
HI = lax.Precision.HIGHEST


def _vmem_limit(*block_bytes, scratch=0):
    need = 2 * sum(block_bytes) + scratch + (4 << 20)
    return int(min(max(need, 16 << 20), VMEM_CAP_BYTES))


def _nbytes(shape, dtype):
    n = 1
    for s in shape:
        n *= s
    return n * jnp.dtype(dtype).itemsize


def _silu(x):
    return x * jax.nn.sigmoid(x)


def _softplus(x):
    return jnp.maximum(x, 0.0) + jnp.log1p(jnp.exp(-jnp.abs(x)))


def _rms(x, gain):
    ms = jnp.mean(x * x, axis=-1, keepdims=True)
    return x * lax.rsqrt(ms + EPS) * gain


def _split2(x):
    hi = x.astype(BF16)
    lo = (x - hi.astype(F32)).astype(BF16)
    return hi, lo


def _norm_mm_kernel(*refs, has_gain, has_small):
    it = iter(refs)
    x_ref = next(it)
    g_ref = next(it) if has_gain else None
    w_ref = next(it)
    ws_ref = next(it) if has_small else None
    o_ref = next(it)
    os_ref = next(it) if has_small else None
    xn_ref = next(it)

    @pl.when(pl.program_id(1) == 0)
    def _():
        x = x_ref[...].astype(F32)
        if has_gain:
            x = _rms(x, g_ref[...])
        xn_ref[...] = x.astype(BF16)
        if has_small:
            os_ref[...] = jnp.dot(xn_ref[...], ws_ref[...], preferred_element_type=F32)

    o_ref[...] = jnp.dot(xn_ref[...], w_ref[...], preferred_element_type=F32).astype(o_ref.dtype)


def _norm_mm(x, gain, w, w_small, *, tm, tn, out_dtype):
    m, k = x.shape
    n = w.shape[1]
    has_gain = gain is not None
    has_small = w_small is not None
    in_specs = [pl.BlockSpec((tm, k), lambda i, j: (i, 0))]
    args = [x]
    if has_gain:
        in_specs.append(pl.BlockSpec((1, k), lambda i, j: (0, 0)))
        args.append(gain.reshape(1, k))
    in_specs.append(pl.BlockSpec((k, tn), lambda i, j: (0, j)))
    args.append(w)
    out_shape = [jax.ShapeDtypeStruct((m, n), out_dtype)]
    out_specs = [pl.BlockSpec((tm, tn), lambda i, j: (i, j))]
    if has_small:
        in_specs.append(pl.BlockSpec((k, LANES), lambda i, j: (0, 0)))
        args.append(w_small)
        out_shape.append(jax.ShapeDtypeStruct((m, LANES), F32))
        out_specs.append(pl.BlockSpec((tm, LANES), lambda i, j: (i, 0)))
    limit = _vmem_limit(_nbytes((tm, k), x.dtype), _nbytes((k, tn), BF16), _nbytes((tm, tn), out_dtype),
                        _nbytes((k, LANES), BF16), _nbytes((tm, LANES), F32), scratch=_nbytes((tm, k), BF16))
    res = pl.pallas_call(
        functools.partial(_norm_mm_kernel, has_gain=has_gain, has_small=has_small),
        grid=(m // tm, n // tn),
        in_specs=in_specs,
        out_specs=out_specs,
        out_shape=out_shape,
        scratch_shapes=[pltpu.VMEM((tm, k), BF16)],
        compiler_params=pltpu.CompilerParams(dimension_semantics=("parallel", "arbitrary"),
                                             vmem_limit_bytes=limit),
        name="norm_mm",
    )(*args)
    return res if has_small else res[0]


def _mm_res_kernel(x_ref, w_ref, r_ref, o_ref):
    o_ref[...] = r_ref[...] + jnp.dot(x_ref[...], w_ref[...], preferred_element_type=F32)


def _mm_res(x, w, res, *, tm, tn):
    m, k = x.shape
    n = w.shape[1]
    limit = _vmem_limit(_nbytes((tm, k), BF16), _nbytes((k, tn), BF16), 2 * _nbytes((tm, tn), F32))
    return pl.pallas_call(
        _mm_res_kernel,
        grid=(m // tm, n // tn),
        in_specs=[pl.BlockSpec((tm, k), lambda i, j: (i, 0)),
                  pl.BlockSpec((k, tn), lambda i, j: (0, j)),
                  pl.BlockSpec((tm, tn), lambda i, j: (i, j))],
        out_specs=pl.BlockSpec((tm, tn), lambda i, j: (i, j)),
        out_shape=jax.ShapeDtypeStruct((m, n), F32),
        compiler_params=pltpu.CompilerParams(dimension_semantics=("parallel", "parallel"),
                                             vmem_limit_bytes=limit),
        name="mm_res",
    )(x, w, res)


def _merge_kernel(y0_ref, y1_ref, g0_ref, g1_ref, w_ref, o_ref):
    b0 = jnp.dot(y0_ref[...], w_ref[0], preferred_element_type=F32)
    b1 = jnp.dot(y1_ref[...], w_ref[1], preferred_element_type=F32)
    g0 = jax.nn.sigmoid(g0_ref[...].astype(F32))
    g1 = jax.nn.sigmoid(g1_ref[...].astype(F32))
    o_ref[...] = (g0 * b0 + g1 * b1).astype(o_ref.dtype)


def _merge(y_ssd, y_gla, proj, w_branch, *, tm, tn):
    m, k = y_ssd.shape
    n = D_MODEL
    gb = P_GATE // tn
    nb = n // tn
    limit = _vmem_limit(2 * _nbytes((tm, k), BF16), 2 * _nbytes((k, tn), BF16), 3 * _nbytes((tm, tn), BF16))
    return pl.pallas_call(
        _merge_kernel,
        grid=(m // tm, nb),
        in_specs=[pl.BlockSpec((tm, k), lambda i, j: (i, 0)),
                  pl.BlockSpec((tm, k), lambda i, j: (i, 0)),
                  pl.BlockSpec((tm, tn), lambda i, j: (i, gb + j)),
                  pl.BlockSpec((tm, tn), lambda i, j: (i, gb + nb + j)),
                  pl.BlockSpec((2, k, tn), lambda i, j: (0, 0, j))],
        out_specs=pl.BlockSpec((tm, tn), lambda i, j: (i, j)),
        out_shape=jax.ShapeDtypeStruct((m, n), BF16),
        compiler_params=pltpu.CompilerParams(dimension_semantics=("parallel", "parallel"),
                                             vmem_limit_bytes=limit),
        name="merge",
    )(y_ssd, y_gla, proj, proj, w_branch)


def _final_norm_kernel(x_ref, g_ref, o_ref):
    o_ref[...] = _rms(x_ref[...], g_ref[...])


def _final_norm(x, gain, *, tm):
    m, k = x.shape
    return pl.pallas_call(
        _final_norm_kernel,
        grid=(m // tm,),
        in_specs=[pl.BlockSpec((tm, k), lambda i: (i, 0)), pl.BlockSpec((1, k), lambda i: (0, 0))],
        out_specs=pl.BlockSpec((tm, k), lambda i: (i, 0)),
        out_shape=jax.ShapeDtypeStruct((m, k), F32),
        compiler_params=pltpu.CompilerParams(dimension_semantics=("parallel",)),
        name="final_norm",
    )(x, gain.reshape(1, k))


def _causal_conv(x, tail_ref, w_ref, b, width):
    row = lax.broadcasted_iota(jnp.int32, x.shape, 0)
    y = x * w_ref[width - 1:width, :]
    for s in range(1, width):
        shifted = pltpu.roll(x, s, axis=0)
        for t in range(s):
            shifted = jnp.where(row == t, tail_ref[8 + t - s:9 + t - s, :], shifted)
        y = y + shifted * w_ref[width - 1 - s:width - s, :]
    return y + b


def _ssd_kernel(z_ref, xs_ref, bc_ref, sm_ref, conv0_ref, h0_ref, cwx_ref, cwb_ref, cbx_ref, cbb_ref,
                dtb_ref, a_ref, dx_ref, nrm_ref, exp_ref,
                y_ref, h_ref, convout_ref, tailx_ref, tailb_ref, *, Q, zero_init):
    c = pl.program_id(1)
    hp = SSD_HEADDIM
    gw = SSD_HEADS // SSD_GROUPS * hp

    @pl.when(c == 0)
    def _():
        tailx_ref[...] = jnp.zeros_like(tailx_ref)
        tailb_ref[...] = jnp.zeros_like(tailb_ref)
        if zero_init:
            h_ref[...] = jnp.zeros_like(h_ref)
        else:
            h_ref[...] = h0_ref[...]
            tailx_ref[5:8, :] = conv0_ref[0, :, 0:D_MODEL]
            tailb_ref[5:8, :] = conv0_ref[0, :, D_MODEL:SSD_CONV_DIM]

    xs_raw = xs_ref[...].astype(F32)
    bc_raw = bc_ref[...].astype(F32)
    xs = _silu(_causal_conv(xs_raw, tailx_ref, cwx_ref, cbx_ref[...], SSD_CONV))
    bc = _silu(_causal_conv(bc_raw, tailb_ref, cwb_ref, cbb_ref[...], SSD_CONV))
    tailx_ref[...] = xs_raw[Q - 8:Q, :]
    tailb_ref[...] = bc_raw[Q - 8:Q, :]

    @pl.when(c == pl.num_programs(1) - 1)
    def _():
        convout_ref[0, :, 0:D_MODEL] = tailx_ref[5:8, :]
        convout_ref[0, :, D_MODEL:SSD_CONV_DIM] = tailb_ref[5:8, :]

    dt = _softplus(sm_ref[...] + dtb_ref[...])
    adt = dt * a_ref[...]
    ri = lax.broadcasted_iota(jnp.int32, (Q, Q), 0)
    ci = lax.broadcasted_iota(jnp.int32, (Q, Q), 1)
    causal = ci <= ri
    acum = jnp.dot(causal.astype(F32), adt, precision=HI, preferred_element_type=F32)
    eye = (lax.broadcasted_iota(jnp.int32, (LANES, LANES), 0)
           == lax.broadcasted_iota(jnp.int32, (LANES, LANES), 1)).astype(F32)
    acum_t = lax.dot_general(eye, acum, NT_DIMS, precision=HI, preferred_element_type=F32)
    dt_t = lax.dot_general(eye, dt, NT_DIMS, precision=HI, preferred_element_type=F32)
    a_last = acum[Q - 1:Q, :]
    ea = jnp.exp(acum)
    te = jnp.exp(a_last - acum) * dt
    dec_rows = jnp.broadcast_to(jnp.exp(acum_t[:, Q - 1:Q]), (LANES, LANES))
    ea_hi, ea_lo = _split2(ea)
    te_hi, te_lo = _split2(te)
    expand = exp_ref[...]
    ea_x = (jnp.dot(ea_hi, expand, preferred_element_type=F32)
            + jnp.dot(ea_lo, expand, preferred_element_type=F32))
    te_x = (jnp.dot(te_hi, expand, preferred_element_type=F32)
            + jnp.dot(te_lo, expand, preferred_element_type=F32))
    lane = lax.broadcasted_iota(jnp.int32, (Q, LANES), 1)

    for g in range(SSD_GROUPS):
        gl = slice(g * gw, (g + 1) * gw)
        b_g = bc[:, g * SSD_STATE:(g + 1) * SSD_STATE].astype(BF16)
        c_g = bc[:, (SSD_GROUPS + g) * SSD_STATE:(SSD_GROUPS + g + 1) * SSD_STATE].astype(BF16)
        cb = lax.dot_general(c_g, b_g, NT_DIMS, preferred_element_type=F32)
        h_g = h_ref[0, 8 * g:8 * g + 8].reshape(gw, SSD_STATE)
        y_off = lax.dot_general(c_g, h_g.astype(BF16), NT_DIMS, preferred_element_type=F32)
        x_g = xs[:, gl]
        pairs = []
        for p in range(4):
            x_p = x_g[:, p * LANES:(p + 1) * LANES]
            acc = None
            for s in range(2):
                hh = g * 8 + p * 2 + s
                seg = jnp.broadcast_to(acum[:, hh:hh + 1], (Q, Q)) - jnp.broadcast_to(acum_t[hh:hh + 1, :], (Q, Q))
                decay = jnp.where(causal, jnp.exp(seg), 0.0)
                w_h = (cb * decay * dt_t[hh:hh + 1, :]).astype(BF16)
                x_m = jnp.where((lane // hp) == s, x_p, 0.0).astype(BF16)
                r = jnp.dot(w_h, x_m, preferred_element_type=F32)
                acc = r if acc is None else acc + r
            pairs.append(acc)
        y_g = jnp.concatenate(pairs, axis=1) + y_off * ea_x[:, gl] + dx_ref[:, gl] * x_g
        x_t = (x_g * te_x[:, gl]).astype(BF16)
        upd = lax.dot_general(x_t, b_g, TN_DIMS, preferred_element_type=F32)
        for h in range(8):
            hh = g * 8 + h
            h_ref[0, hh] = h_ref[0, hh] * dec_rows[hh:hh + 1, :] + upd[h * hp:(h + 1) * hp, :]
        z_g = z_ref[:, gl].astype(F32)
        y_ref[:, gl] = _rms(y_g * _silu(z_g), nrm_ref[:, gl]).astype(y_ref.dtype)


def _ssd(proj, small, conv0, h0, wl, *, bt, L, Q):
    zero_init = h0 is None
    nc = L // Q
    T = bt * L
    kern = functools.partial(_ssd_kernel, Q=Q, zero_init=zero_init)
    if zero_init:
        conv0 = jnp.zeros((1, SSD_CONV - 1, SSD_CONV_DIM), F32)
        h0 = jnp.zeros((1, SSD_HEADS, SSD_HEADDIM, SSD_STATE), F32)
        st_map3 = lambda b, c: (0, 0, 0)
        st_map4 = lambda b, c: (0, 0, 0, 0)
    else:
        st_map3 = lambda b, c: (b, 0, 0)
        st_map4 = lambda b, c: (b, 0, 0, 0)
    full2 = lambda b, c: (0, 0)
    row = lambda col: (lambda b, c: (b * nc + c, col))
    in_specs = [
        pl.BlockSpec((Q, D_MODEL), row(P_Z // D_MODEL)),
        pl.BlockSpec((Q, D_MODEL), row(P_XS // D_MODEL)),
        pl.BlockSpec((Q, 1024), row(P_BC // 1024)),
        pl.BlockSpec((Q, LANES), row(0)),
        pl.BlockSpec((1, SSD_CONV - 1, SSD_CONV_DIM), st_map3),
        pl.BlockSpec((1, SSD_HEADS, SSD_HEADDIM, SSD_STATE), st_map4),
        pl.BlockSpec((SSD_CONV, D_MODEL), full2),
        pl.BlockSpec((SSD_CONV, 1024), full2),
        pl.BlockSpec((1, D_MODEL), full2),
        pl.BlockSpec((1, 1024), full2),
        pl.BlockSpec((1, LANES), full2),
        pl.BlockSpec((1, LANES), full2),
        pl.BlockSpec((1, D_MODEL), full2),
        pl.BlockSpec((1, D_MODEL), full2),
        pl.BlockSpec((LANES, D_MODEL), full2),
    ]
    out_shape = [jax.ShapeDtypeStruct((T, D_MODEL), BF16),
                 jax.ShapeDtypeStruct((bt, SSD_HEADS, SSD_HEADDIM, SSD_STATE), F32),
                 jax.ShapeDtypeStruct((bt, SSD_CONV - 1, SSD_CONV_DIM), F32)]
    out_specs = [pl.BlockSpec((Q, D_MODEL), row(0)),
                 pl.BlockSpec((1, SSD_HEADS, SSD_HEADDIM, SSD_STATE), lambda b, c: (b, 0, 0, 0)),
                 pl.BlockSpec((1, SSD_CONV - 1, SSD_CONV_DIM), lambda b, c: (b, 0, 0))]
    return pl.pallas_call(
        kern,
        grid=(bt, nc),
        in_specs=in_specs,
        out_specs=out_specs,
        out_shape=out_shape,
        scratch_shapes=[pltpu.VMEM((8, D_MODEL), F32), pltpu.VMEM((8, 1024), F32)],
        compiler_params=pltpu.CompilerParams(dimension_semantics=("parallel", "arbitrary"),
                                             vmem_limit_bytes=48 << 20),
        name="ssd",
    )(proj, proj, proj, small, conv0, h0, wl["ssd_cw_x"], wl["ssd_cw_b"], wl["ssd_cb_x"], wl["ssd_cb_b"],
      wl["ssd_dtb"], wl["ssd_a"], wl["ssd_dx"], wl["ssd_norm"], wl["expand"])


def _log_sigmoid(x):
    return -_softplus(-x)


def _gla_kernel(q_ref, k_ref, v_ref, g_ref, sm_ref, s0_ref, wa_ref, ba_ref, wat_ref, bat_ref, gn_ref,
                y_ref, s_ref, o_scr, *, TB, zero_init):
    c = pl.program_id(1)
    ck = GLA_CHUNK
    ns = TB // ck
    hk, hv = GLA_HEAD_K, GLA_HEAD_V

    @pl.when(c == 0)
    def _():
        if zero_init:
            s_ref[...] = jnp.zeros_like(s_ref)
        else:
            s_ref[...] = s0_ref[...]

    smb = sm_ref[...].astype(BF16)
    la = _log_sigmoid(jnp.dot(smb, wa_ref[...], preferred_element_type=F32) + ba_ref[...]) * (1.0 / GLA_GATE_NORM)
    ri = lax.broadcasted_iota(jnp.int32, (TB, TB), 0)
    ci = lax.broadcasted_iota(jnp.int32, (TB, TB), 1)
    same = (ri // ck) == (ci // ck)
    bcum = jnp.dot((same & (ci <= ri)).astype(F32), la, precision=HI, preferred_element_type=F32)
    tot = jnp.dot(same.astype(F32), la, precision=HI, preferred_element_type=F32)
    kf = k_ref[...].astype(F32)
    qd = (q_ref[...].astype(F32) * (hk ** -0.5) * jnp.exp(bcum)).astype(BF16)
    ki = (kf * jnp.exp(-bcum)).astype(BF16)
    ke = (kf * jnp.exp(tot - bcum)).astype(BF16)
    la_t = _log_sigmoid(lax.dot_general(wat_ref[...], smb, NT_DIMS, preferred_element_type=F32)
                        + bat_ref[:, 0:1]) * (1.0 / GLA_GATE_NORM)
    sel = ((lax.broadcasted_iota(jnp.int32, (TB, LANES), 0) // ck)
           == lax.broadcasted_iota(jnp.int32, (TB, LANES), 1)).astype(F32)
    dec_t = jnp.exp(jnp.dot(la_t, sel, precision=HI, preferred_element_type=F32))
    r16 = lax.broadcasted_iota(jnp.int32, (ck, ck), 0)
    c16 = lax.broadcasted_iota(jnp.int32, (ck, ck), 1)
    causal = c16 <= r16

    for cc in range(ns):
        rows = slice(cc * ck, (cc + 1) * ck)
        for h in range(GLA_HEADS):
            kl = slice(h * hk, (h + 1) * hk)
            vl = slice(h * hv, (h + 1) * hv)
            q_c = qd[rows, kl]
            v_c = v_ref[rows, vl]
            att = jnp.where(causal, lax.dot_general(q_c, ki[rows, kl], NT_DIMS, preferred_element_type=F32), 0.0)
            s_old = s_ref[0, h]
            o_scr[rows, vl] = (jnp.dot(att.astype(BF16), v_c, preferred_element_type=F32)
                               + jnp.dot(q_c, s_old.astype(BF16), preferred_element_type=F32))
            dcol = jnp.broadcast_to(dec_t[kl, cc:cc + 1], (hk, hv))
            s_ref[0, h] = s_old * dcol + lax.dot_general(ke[rows, kl], v_c, TN_DIMS, preferred_element_type=F32)

    for h in range(GLA_HEADS):
        vl = slice(h * hv, (h + 1) * hv)
        y_ref[:, vl] = (_rms(o_scr[:, vl], gn_ref[...]) * _silu(g_ref[:, vl].astype(F32))).astype(y_ref.dtype)


def _gla(proj, small, s0, wl, *, bt, L, TB):
    zero_init = s0 is None
    nc = L // TB
    T = bt * L
    if zero_init:
        s0 = jnp.zeros((1, GLA_HEADS, GLA_HEAD_K, GLA_HEAD_V), F32)
        st_map = lambda b, c: (0, 0, 0, 0)
    else:
        st_map = lambda b, c: (b, 0, 0, 0)
    full2 = lambda b, c: (0, 0)
    row = lambda col: (lambda b, c: (b * nc + c, col))
    kd = GLA_HEADS * GLA_HEAD_K
    in_specs = [
        pl.BlockSpec((TB, kd), row(P_Q // kd)),
        pl.BlockSpec((TB, kd), row(P_K // kd)),
        pl.BlockSpec((TB, D_MODEL), row(P_V // D_MODEL)),
        pl.BlockSpec((TB, D_MODEL), row(P_G // D_MODEL)),
        pl.BlockSpec((TB, LANES), row(0)),
        pl.BlockSpec((1, GLA_HEADS, GLA_HEAD_K, GLA_HEAD_V), st_map),
        pl.BlockSpec((LANES, kd), full2),
        pl.BlockSpec((1, kd), full2),
        pl.BlockSpec((kd, LANES), full2),
        pl.BlockSpec((kd, LANES), full2),
        pl.BlockSpec((1, GLA_HEAD_V), full2),
    ]
    return pl.pallas_call(
        functools.partial(_gla_kernel, TB=TB, zero_init=zero_init),
        grid=(bt, nc),
        in_specs=in_specs,
        out_specs=[pl.BlockSpec((TB, D_MODEL), row(0)),
                   pl.BlockSpec((1, GLA_HEADS, GLA_HEAD_K, GLA_HEAD_V), lambda b, c: (b, 0, 0, 0))],
        out_shape=[jax.ShapeDtypeStruct((T, D_MODEL), BF16),
                   jax.ShapeDtypeStruct((bt, GLA_HEADS, GLA_HEAD_K, GLA_HEAD_V), F32)],
        scratch_shapes=[pltpu.VMEM((TB, D_MODEL), F32)],
        compiler_params=pltpu.CompilerParams(dimension_semantics=("parallel", "arbitrary"),
                                             vmem_limit_bytes=48 << 20),
        name="gla",
    )(proj, proj, proj, proj, small, s0, wl["gla_wa"], wl["gla_ba"], wl["gla_wa_t"], wl["gla_ba_t"], wl["gla_norm"])


def _attn_kernel(q_ref, k_ref, v_ref, o_ref, kb_ref, vb_ref):
    @pl.when(pl.program_id(1) == 0)
    def _():
        kb_ref[...] = k_ref[0].astype(BF16)
        vb_ref[...] = v_ref[0].astype(BF16)

    hd = MEM_HEAD_DIM
    for h in range(MEM_HEADS):
        hl = slice(h * hd, (h + 1) * hd)
        s = lax.dot_general(q_ref[:, hl], kb_ref[:, hl], NT_DIMS, preferred_element_type=F32) * (hd ** -0.5)
        e = jnp.exp(s - jnp.max(s, axis=-1, keepdims=True))
        p = e / jnp.sum(e, axis=-1, keepdims=True)
        o_ref[:, hl] = jnp.dot(p.astype(BF16), vb_ref[:, hl], preferred_element_type=F32).astype(o_ref.dtype)


def _attn(q, mem_k, mem_v, *, bt, L, tl):
    nl = L // tl
    T = bt * L
    return pl.pallas_call(
        _attn_kernel,
        grid=(bt, nl),
        in_specs=[pl.BlockSpec((tl, D_MODEL), lambda b, l: (b * nl + l, 0)),
                  pl.BlockSpec((1, MEM_TOKENS, D_MODEL), lambda b, l: (b, 0, 0)),
                  pl.BlockSpec((1, MEM_TOKENS, D_MODEL), lambda b, l: (b, 0, 0))],
        out_specs=pl.BlockSpec((tl, D_MODEL), lambda b, l: (b * nl + l, 0)),
        out_shape=jax.ShapeDtypeStruct((T, D_MODEL), BF16),
        scratch_shapes=[pltpu.VMEM((MEM_TOKENS, D_MODEL), BF16), pltpu.VMEM((MEM_TOKENS, D_MODEL), BF16)],
        compiler_params=pltpu.CompilerParams(dimension_semantics=("parallel", "arbitrary"),
                                             vmem_limit_bytes=40 << 20),
        name="mem_attn",
    )(q, mem_k, mem_v)


def _convgate_kernel(u_ref, t_ref, c0u_ref, c0t_ref, wu_ref, wt_ref, bu_ref, bt_ref,
                     act_ref, cnu_ref, cnt_ref, tailu_ref, tailt_ref, *, tl, zero_init):
    l = pl.program_id(2)

    @pl.when(l == 0)
    def _():
        tailu_ref[...] = jnp.zeros_like(tailu_ref)
        tailt_ref[...] = jnp.zeros_like(tailt_ref)
        if not zero_init:
            tailu_ref[6:8, :] = c0u_ref[0]
            tailt_ref[6:8, :] = c0t_ref[0]

    u_raw = u_ref[...].astype(F32)
    t_raw = t_ref[...].astype(F32)
    u = _causal_conv(u_raw, tailu_ref, wu_ref, bu_ref[...], FFN_CONV)
    t = _causal_conv(t_raw, tailt_ref, wt_ref, bt_ref[...], FFN_CONV)
    act_ref[...] = (_silu(t) * u).astype(act_ref.dtype)
    tailu_ref[...] = u_raw[tl - 8:tl, :]
    tailt_ref[...] = t_raw[tl - 8:tl, :]

    @pl.when(l == pl.num_programs(2) - 1)
    def _():
        cnu_ref[0] = tailu_ref[6:8, :]
        cnt_ref[0] = tailt_ref[6:8, :]


def _convgate(up, conv0, conv_w, conv_b, *, bt, L, tl, tc):
    zero_init = conv0 is None
    nl = L // tl
    ncol = D_FF // tc
    T = bt * L
    if zero_init:
        conv0 = jnp.zeros((1, FFN_CONV - 1, 2 * D_FF), F32)
        c0u = lambda b, j, l: (0, 0, j)
        c0t = lambda b, j, l: (0, 0, ncol + j)
    else:
        c0u = lambda b, j, l: (b, 0, j)
        c0t = lambda b, j, l: (b, 0, ncol + j)
    in_specs = [
        pl.BlockSpec((tl, tc), lambda b, j, l: (b * nl + l, j)),
        pl.BlockSpec((tl, tc), lambda b, j, l: (b * nl + l, ncol + j)),
        pl.BlockSpec((1, FFN_CONV - 1, tc), c0u),
        pl.BlockSpec((1, FFN_CONV - 1, tc), c0t),
        pl.BlockSpec((FFN_CONV, tc), lambda b, j, l: (0, j)),
        pl.BlockSpec((FFN_CONV, tc), lambda b, j, l: (0, ncol + j)),
        pl.BlockSpec((1, tc), lambda b, j, l: (0, j)),
        pl.BlockSpec((1, tc), lambda b, j, l: (0, ncol + j)),
    ]
    out_specs = [pl.BlockSpec((tl, tc), lambda b, j, l: (b * nl + l, j)),
                 pl.BlockSpec((1, FFN_CONV - 1, tc), lambda b, j, l: (b, 0, j)),
                 pl.BlockSpec((1, FFN_CONV - 1, tc), lambda b, j, l: (b, 0, j))]
    out_shape = [jax.ShapeDtypeStruct((T, D_FF), BF16),
                 jax.ShapeDtypeStruct((bt, FFN_CONV - 1, D_FF), F32),
                 jax.ShapeDtypeStruct((bt, FFN_CONV - 1, D_FF), F32)]
    act, cnu, cnt = pl.pallas_call(
        functools.partial(_convgate_kernel, tl=tl, zero_init=zero_init),
        grid=(bt, ncol, nl),
        in_specs=in_specs,
        out_specs=out_specs,
        out_shape=out_shape,
        scratch_shapes=[pltpu.VMEM((8, tc), F32), pltpu.VMEM((8, tc), F32)],
        compiler_params=pltpu.CompilerParams(dimension_semantics=("parallel", "parallel", "arbitrary")),
        name="convgate",
    )(up, up, conv0, conv0, conv_w, conv_w, conv_b, conv_b)
    return act, jnp.concatenate([cnu, cnt], axis=-1)


def _prep_weights(norm_mix, w_in, ssd_conv_w, ssd_conv_b, ssd_dt_bias, ssd_a_log, ssd_d, ssd_norm, gla_wa2, gla_ba,
                  gla_norm, w_branch, w_out, norm_mem, w_mq, w_mo, norm_ffn, w_ffn_in, ffn_conv_w, ffn_conv_b,
                  w_ffn_out):
    o_z, o_xs, o_bc, o_dt, o_q, o_k, o_v, o_g, o_alr, o_gate, o_end = (
        0, 2048, 4096, 5120, 5152, 6176, 7200, 9248, 11296, 11312, 15408)
    w_in_p = jnp.concatenate(
        [w_in[:, :, o_z:o_xs], w_in[:, :, o_xs:o_bc], w_in[:, :, o_v:o_g], w_in[:, :, o_g:o_alr],
         w_in[:, :, o_gate:o_end], w_in[:, :, o_bc:o_dt], w_in[:, :, o_q:o_k], w_in[:, :, o_k:o_v]],
        axis=-1).astype(BF16)
    pad_sm = LANES - SSD_HEADS - GLA_RANK
    w_small = jnp.concatenate(
        [w_in[:, :, o_dt:o_q], w_in[:, :, o_alr:o_gate], jnp.zeros((DEPTH, D_MODEL, pad_sm), F32)],
        axis=-1).astype(BF16)
    pad_h = LANES - SSD_HEADS
    kd = GLA_HEADS * GLA_HEAD_K
    wa = jnp.concatenate([jnp.zeros((DEPTH, SM_ALR, kd), F32), gla_wa2,
                          jnp.zeros((DEPTH, LANES - SM_ALR - GLA_RANK, kd), F32)], axis=1).astype(BF16)
    expand = (jnp.arange(D_MODEL)[None, :] // SSD_HEADDIM == jnp.arange(LANES)[:, None]).astype(BF16)
    layers = []
    for i in range(DEPTH):
        layers.append(dict(
            norm_mix=norm_mix[i], w_in=w_in_p[i], w_small=w_small[i],
            ssd_cw_x=ssd_conv_w[i][:, :D_MODEL], ssd_cw_b=ssd_conv_w[i][:, D_MODEL:],
            ssd_cb_x=ssd_conv_b[i][None, :D_MODEL], ssd_cb_b=ssd_conv_b[i][None, D_MODEL:],
            ssd_dtb=jnp.pad(ssd_dt_bias[i], (0, pad_h))[None, :],
            ssd_a=jnp.pad(-jnp.exp(ssd_a_log[i]), (0, pad_h))[None, :],
            ssd_dx=jnp.repeat(ssd_d[i], SSD_HEADDIM)[None, :],
            ssd_norm=ssd_norm[i][None, :], expand=expand,
            gla_wa=wa[i], gla_ba=gla_ba[i][None, :], gla_wa_t=wa[i].T,
            gla_ba_t=jnp.broadcast_to(gla_ba[i][:, None], (kd, LANES)),
            gla_norm=gla_norm[i][None, :],
            w_branch=w_branch[i].astype(BF16), w_out=w_out[i].astype(BF16),
            norm_mem=norm_mem[i], w_mq=w_mq[i].astype(BF16), w_mo=w_mo[i].astype(BF16),
            norm_ffn=norm_ffn[i], w_ffn_in=w_ffn_in[i].astype(BF16),
            ffn_conv_w=ffn_conv_w[i], ffn_conv_b=ffn_conv_b[i][None, :],
            w_ffn_out=w_ffn_out[i].astype(BF16)))
    return layers


def _run_trunk(x, mem_k, mem_v, st_ssd, st_ssd_conv, st_gla, st_ffn_conv, layers, norm_final, *, bt, L, cfg):
    tm = cfg["tm"]
    new_ssd, new_ssd_conv, new_gla, new_ffn = [], [], [], []
    for i, wl in enumerate(layers):
        proj, small = _norm_mm(x, wl["norm_mix"], wl["w_in"], wl["w_small"], tm=tm, tn=1024, out_dtype=BF16)
        y_ssd, s_h, s_c = _ssd(proj, small, None if st_ssd is None else st_ssd_conv[i],
                               None if st_ssd is None else st_ssd[i], wl, bt=bt, L=L, Q=cfg["ssd_q"])
        y_gla, g_s = _gla(proj, small, None if st_gla is None else st_gla[i], wl, bt=bt, L=L, TB=cfg["gla_tb"])
        merged = _merge(y_ssd, y_gla, proj, wl["w_branch"], tm=tm, tn=512)
        x = _mm_res(merged, wl["w_out"], x, tm=tm, tn=1024)
        q = _norm_mm(x, wl["norm_mem"], wl["w_mq"], None, tm=tm, tn=1024, out_dtype=BF16)
        o = _attn(q, mem_k[i], mem_v[i], bt=bt, L=L, tl=cfg["attn_tl"])
        x = _mm_res(o, wl["w_mo"], x, tm=tm, tn=1024)
        up = _norm_mm(x, wl["norm_ffn"], wl["w_ffn_in"], None, tm=tm, tn=1024, out_dtype=BF16)
        act, f_c = _convgate(up, None if st_ffn_conv is None else st_ffn_conv[i], wl["ffn_conv_w"],
                             wl["ffn_conv_b"], bt=bt, L=L, tl=cfg["ffn_tl"], tc=512)
        x = _mm_res(act, wl["w_ffn_out"], x, tm=cfg["tm_ffn"], tn=512)
        new_ssd.append(s_h)
        new_ssd_conv.append(s_c)
        new_gla.append(g_s)
        new_ffn.append(f_c)
    y = _final_norm(x, norm_final, tm=cfg["tm_norm"])
    return y, jnp.stack(new_ssd), jnp.stack(new_ssd_conv), jnp.stack(new_gla), jnp.stack(new_ffn)


def _group_cfg(bt, L):
    T = bt * L
    return dict(tm=min(T, 1024), tm_ffn=min(T, 512), tm_norm=min(T, 512), ssd_q=min(L, 128), gla_tb=min(L, 128),
                attn_tl=min(L, 512), ffn_tl=min(L, 512))


def kernel(x_prompt, x_sample, mem_prompt, state_ssd, state_ssd_conv, state_gla, state_ffn_conv, cache_mem_k, cache_mem_v, norm_mix, w_in, ssd_conv_w, ssd_conv_b, ssd_dt_bias, ssd_a_log, ssd_d, ssd_norm, gla_wa2, gla_ba, gla_norm, w_branch, w_out, norm_mem, w_mq, w_mk, w_mv, w_mo, norm_ffn, w_ffn_in, ffn_conv_w, ffn_conv_b, w_ffn_out, norm_final):
    layers = _prep_weights(norm_mix, w_in, ssd_conv_w, ssd_conv_b, ssd_dt_bias, ssd_a_log, ssd_d, ssd_norm, gla_wa2,
                           gla_ba, gla_norm, w_branch, w_out, norm_mem, w_mq, w_mo, norm_ffn, w_ffn_in, ffn_conv_w,
                           ffn_conv_b, w_ffn_out)
    pb, pl_len, _ = x_prompt.shape
    sb, sl_len, _ = x_sample.shape
    mem2d = mem_prompt.reshape(pb * MEM_TOKENS, D_MODEL)
    p_k, p_v = [], []
    for i in range(DEPTH):
        w_kv = jnp.concatenate([w_mk[i], w_mv[i]], axis=-1).astype(BF16)
        kv = _norm_mm(mem2d, None, w_kv, None, tm=min(pb * MEM_TOKENS, 1024), tn=1024, out_dtype=F32)
        p_k.append(kv[:, :D_MODEL].reshape(pb, MEM_TOKENS, D_MODEL))
        p_v.append(kv[:, D_MODEL:].reshape(pb, MEM_TOKENS, D_MODEL))
    p_mem_k = jnp.stack(p_k)
    p_mem_v = jnp.stack(p_v)

    y_p, p_ssd, p_ssd_conv, p_gla, p_ffn = _run_trunk(
        x_prompt.reshape(pb * pl_len, D_MODEL), p_mem_k, p_mem_v, None, None, None, None, layers, norm_final,
        bt=pb, L=pl_len, cfg=_group_cfg(pb, pl_len))
    y_s, s_ssd, s_ssd_conv, s_gla, s_ffn = _run_trunk(
        x_sample.reshape(sb * sl_len, D_MODEL),
        cache_mem_k.reshape(DEPTH, sb, MEM_TOKENS, D_MODEL), cache_mem_v.reshape(DEPTH, sb, MEM_TOKENS, D_MODEL),
        state_ssd, state_ssd_conv, state_gla, state_ffn_conv, layers, norm_final,
        bt=sb, L=sl_len, cfg=_group_cfg(sb, sl_len))
    kv_shape = (DEPTH, pb, MEM_TOKENS, MEM_HEADS, MEM_HEAD_DIM)
    return (y_p.reshape(pb, pl_len, D_MODEL), y_s.reshape(sb, sl_len, D_MODEL),
            p_ssd, p_ssd_conv, p_gla, p_ffn, p_mem_k.reshape(kv_shape), p_mem_v.reshape(kv_shape),
            s_ssd, s_ssd_conv, s_gla, s_ffn)
```

```python
import functools

import jax
import jax.numpy as jnp
from jax import lax
from jax.experimental import pallas as pl
from jax.experimental.pallas import tpu as pltpu

F32 = jnp.float32
BF16 = jnp.bfloat16

D_MODEL = 2048
DEPTH = 2
EPS = 1e-6
SSD_HEADS = 32
SSD_HEADDIM = 64
SSD_GROUPS = 4
SSD_STATE = 128
SSD_CONV = 4
SSD_CONV_DIM = 3072
GLA_HEADS = 4
GLA_HEAD_K = 256
GLA_HEAD_V = 512
GLA_RANK = 16
GLA_GATE_NORM = 16.0
GLA_CHUNK = 16
MEM_TOKENS = 256
MEM_HEADS = 4
MEM_HEAD_DIM = 512
D_FF = 5632
FFN_CONV = 3

LANES = 128
SUBLANES = 8
VMEM_CAP_BYTES = 56 * 2**20

P_Z, P_XS, P_V, P_G, P_GATE, P_BC, P_Q, P_K, P_N = 0, 2048, 4096, 6144, 8192, 12288, 13312, 14336, 15360
SM_DT, SM_ALR = 0, 32

NT_DIMS = (((1,), (1,)), ((), ()))
TN_DIMS = (((0,), (0,)), ((), ()))
HI = lax.Precision.HIGHEST


def _vmem_limit(*block_bytes, scratch=0):
    need = 2 * sum(block_bytes) + scratch + (4 << 20)
    return int(min(max(need, 16 << 20), VMEM_CAP_BYTES))


def _nbytes(shape, dtype):
    n = 1
    for s in shape:
        n *= s
    return n * jnp.dtype(dtype).itemsize


def _silu(x):
    return x * jax.nn.sigmoid(x)


def _softplus(x):
    return jnp.maximum(x, 0.0) + jnp.log1p(jnp.exp(-jnp.abs(x)))


def _log_sigmoid(x):
    return -_softplus(-x)


def _rms(x, gain):
    ms = jnp.mean(x * x, axis=-1, keepdims=True)
    return x * lax.rsqrt(ms + EPS) * gain


def _split2(x):
    hi = x.astype(BF16)
    lo = (x - hi.astype(F32)).astype(BF16)
    return hi, lo


def _select_rows(sel, x):
    hi = x.astype(BF16)
    r1 = x - hi.astype(F32)
    mid = r1.astype(BF16)
    lo = (r1 - mid.astype(F32)).astype(BF16)
    return (jnp.dot(sel, hi, preferred_element_type=F32) + jnp.dot(sel, mid, preferred_element_type=F32)
            + jnp.dot(sel, lo, preferred_element_type=F32))


def _stacked_out(prev, shape, dtype):
    return jax.ShapeDtypeStruct(shape, dtype) if prev is None else jax.ShapeDtypeStruct(prev.shape, prev.dtype)


def _norm_mm_kernel(*refs, has_gain, has_small):
    it = iter(refs)
    x_ref = next(it)
    g_ref = next(it) if has_gain else None
    w_ref = next(it)
    ws_ref = next(it) if has_small else None
    o_ref = next(it)
    os_ref = next(it) if has_small else None
    xn_ref = next(it)

    @pl.when(pl.program_id(1) == 0)
    def _():
        x = x_ref[...].astype(F32)
        if has_gain:
            x = _rms(x, g_ref[...])
        xn_ref[...] = x.astype(BF16)
        if has_small:
            os_ref[...] = jnp.dot(xn_ref[...], ws_ref[...], preferred_element_type=F32)

    o_ref[...] = jnp.dot(xn_ref[...], w_ref[...].astype(BF16), preferred_element_type=F32).astype(o_ref.dtype)


def _norm_mm(x, gain, w, layer, w_small, *, tm, tn, out_dtype):
    m, k = x.shape
    n = w.shape[2]
    has_gain = gain is not None
    has_small = w_small is not None
    in_specs = [pl.BlockSpec((tm, k), lambda i, j: (i, 0))]
    args = [x]
    if has_gain:
        in_specs.append(pl.BlockSpec((None, 1, k), lambda i, j: (layer, 0, 0)))
        args.append(gain.reshape(DEPTH, 1, k))
    in_specs.append(pl.BlockSpec((None, k, tn), lambda i, j: (layer, 0, j)))
    args.append(w)
    out_shape = [jax.ShapeDtypeStruct((m, n), out_dtype)]
    out_specs = [pl.BlockSpec((tm, tn), lambda i, j: (i, j))]
    if has_small:
        in_specs.append(pl.BlockSpec((None, k, LANES), lambda i, j: (layer, 0, 0)))
        args.append(w_small)
        out_shape.append(jax.ShapeDtypeStruct((m, LANES), F32))
        out_specs.append(pl.BlockSpec((tm, LANES), lambda i, j: (i, 0)))
    limit = _vmem_limit(_nbytes((tm, k), x.dtype), _nbytes((k, tn), w.dtype), _nbytes((tm, tn), out_dtype),
                        _nbytes((k, LANES), BF16), _nbytes((tm, LANES), F32), scratch=_nbytes((tm, k), BF16))
    res = pl.pallas_call(
        functools.partial(_norm_mm_kernel, has_gain=has_gain, has_small=has_small),
        grid=(m // tm, n // tn),
        in_specs=in_specs,
        out_specs=out_specs,
        out_shape=out_shape,
        scratch_shapes=[pltpu.VMEM((tm, k), BF16)],
        compiler_params=pltpu.CompilerParams(dimension_semantics=("parallel", "arbitrary"),
                                             vmem_limit_bytes=limit),
        name="norm_mm",
    )(*args)
    return res if has_small else res[0]


def _mem_kv_kernel(x_ref, w_ref, o_ref, xb_ref):
    @pl.when((pl.program_id(0) == 0) & (pl.program_id(1) == 0))
    def _():
        xb_ref[...] = x_ref[...].astype(BF16)

    o_ref[...] = jnp.dot(xb_ref[...], w_ref[...].astype(BF16), preferred_element_type=F32)


def _mem_kv(x, w, *, tn):
    m, k = x.shape
    n = w.shape[2]
    limit = _vmem_limit(_nbytes((m, k), F32), _nbytes((k, tn), F32), _nbytes((m, tn), F32),
                        scratch=_nbytes((m, k), BF16))
    return pl.pallas_call(
        _mem_kv_kernel,
        grid=(DEPTH, n // tn),
        in_specs=[pl.BlockSpec((m, k), lambda d, j: (0, 0)),
                  pl.BlockSpec((None, k, tn), lambda d, j: (d, 0, j))],
        out_specs=pl.BlockSpec((None, m, tn), lambda d, j: (d, 0, j)),
        out_shape=jax.ShapeDtypeStruct((DEPTH, m, n), F32),
        scratch_shapes=[pltpu.VMEM((m, k), BF16)],
        compiler_params=pltpu.CompilerParams(dimension_semantics=("arbitrary", "arbitrary"),
                                             vmem_limit_bytes=limit),
        name="mem_kv",
    )(x, w)


def _mm_res_kernel(x_ref, w_ref, r_ref, o_ref):
    o_ref[...] = r_ref[...] + jnp.dot(x_ref[...], w_ref[...].astype(BF16), preferred_element_type=F32)


def _mm_res(x, w, layer, res, *, tm, tn):
    m, k = x.shape
    n = w.shape[2]
    limit = _vmem_limit(_nbytes((tm, k), BF16), _nbytes((k, tn), F32), 2 * _nbytes((tm, tn), F32))
    return pl.pallas_call(
        _mm_res_kernel,
        grid=(m // tm, n // tn),
        in_specs=[pl.BlockSpec((tm, k), lambda i, j: (i, 0)),
                  pl.BlockSpec((None, k, tn), lambda i, j: (layer, 0, j)),
                  pl.BlockSpec((tm, tn), lambda i, j: (i, j))],
        out_specs=pl.BlockSpec((tm, tn), lambda i, j: (i, j)),
        out_shape=jax.ShapeDtypeStruct((m, n), F32),
        compiler_params=pltpu.CompilerParams(dimension_semantics=("parallel", "parallel"),
                                             vmem_limit_bytes=limit),
        name="mm_res",
    )(x, w, res)


def _merge_kernel(y0_ref, y1_ref, g0_ref, g1_ref, w_ref, o_ref):
    b0 = jnp.dot(y0_ref[...], w_ref[0].astype(BF16), preferred_element_type=F32)
    b1 = jnp.dot(y1_ref[...], w_ref[1].astype(BF16), preferred_element_type=F32)
    g0 = jax.nn.sigmoid(g0_ref[...].astype(F32))
    g1 = jax.nn.sigmoid(g1_ref[...].astype(F32))
    o_ref[...] = (g0 * b0 + g1 * b1).astype(o_ref.dtype)


def _merge(y_ssd, y_gla, proj, w_branch, layer, *, tm, tn):
    m, k = y_ssd.shape
    n = D_MODEL
    gb = P_GATE // tn
    nb = n // tn
    limit = _vmem_limit(2 * _nbytes((tm, k), BF16), 2 * _nbytes((k, tn), F32), 3 * _nbytes((tm, tn), BF16))
    return pl.pallas_call(
        _merge_kernel,
        grid=(m // tm, nb),
        in_specs=[pl.BlockSpec((tm, k), lambda i, j: (i, 0)),
                  pl.BlockSpec((tm, k), lambda i, j: (i, 0)),
                  pl.BlockSpec((tm, tn), lambda i, j: (i, gb + j)),
                  pl.BlockSpec((tm, tn), lambda i, j: (i, gb + nb + j)),
                  pl.BlockSpec((None, 2, k, tn), lambda i, j: (layer, 0, 0, j))],
        out_specs=pl.BlockSpec((tm, tn), lambda i, j: (i, j)),
        out_shape=jax.ShapeDtypeStruct((m, n), BF16),
        compiler_params=pltpu.CompilerParams(dimension_semantics=("parallel", "parallel"),
                                             vmem_limit_bytes=limit),
        name="merge",
    )(y_ssd, y_gla, proj, proj, w_branch)


def _final_norm_kernel(x_ref, g_ref, o_ref):
    o_ref[...] = _rms(x_ref[...], g_ref[...])


def _final_norm(x, gain, *, tm):
    m, k = x.shape
    return pl.pallas_call(
        _final_norm_kernel,
        grid=(m // tm,),
        in_specs=[pl.BlockSpec((tm, k), lambda i: (i, 0)), pl.BlockSpec((1, k), lambda i: (0, 0))],
        out_specs=pl.BlockSpec((tm, k), lambda i: (i, 0)),
        out_shape=jax.ShapeDtypeStruct((m, k), F32),
        compiler_params=pltpu.CompilerParams(dimension_semantics=("parallel",)),
        name="final_norm",
    )(x, gain.reshape(1, k))


def _ffn_up_kernel(*refs, tm, tn, seq_tiles, seg):
    short = seg > 0
    it = iter(refs)
    x_ref, g_ref, wu_ref, wt_ref, cwu_ref, cwt_ref, cbu_ref, cbt_ref = (next(it) for _ in range(8))
    stu_ref = next(it) if short else None
    stt_ref = next(it) if short else None
    act_ref, cnu_ref, cnt_ref = next(it), next(it), next(it)
    xn_ref, us_ref, ts_ref = next(it), next(it), next(it)
    tailu_ref = None if short else next(it)
    tailt_ref = None if short else next(it)
    i = pl.program_id(0)
    j = pl.program_id(1)
    sl = SUBLANES
    hw = FFN_CONV - 1

    @pl.when(j == 0)
    def _():
        xn_ref[...] = _rms(x_ref[...], g_ref[...]).astype(BF16)

    us_ref[sl:sl + tm, :] = jnp.dot(xn_ref[...], wu_ref[...].astype(BF16), preferred_element_type=F32)
    ts_ref[sl:sl + tm, :] = jnp.dot(xn_ref[...], wt_ref[...].astype(BF16), preferred_element_type=F32)

    if short:
        us_ref[0:sl, :] = jnp.zeros((sl, tn), F32)
        ts_ref[0:sl, :] = jnp.zeros((sl, tn), F32)
        nseq = tm // seg
        r = lax.broadcasted_iota(jnp.int32, (tm, hw * nseq), 0)
        c = lax.broadcasted_iota(jnp.int32, (tm, hw * nseq), 1)
        halo_u, halo_t = [], []
        for s in range(1, FFN_CONV):
            sel = (((r % seg) < s) & (c == hw * (r // seg) + hw - s + (r % seg))).astype(BF16)
            halo_u.append(_select_rows(sel, stu_ref[...]))
            halo_t.append(_select_rows(sel, stt_ref[...]))
        ro = lax.broadcasted_iota(jnp.int32, (hw * nseq, tm), 0)
        co = lax.broadcasted_iota(jnp.int32, (hw * nseq, tm), 1)
        sel_out = (co == (ro // hw) * seg + seg - hw + (ro % hw)).astype(BF16)
        cnu_ref[...] = _select_rows(sel_out, us_ref[sl:sl + tm, :])
        cnt_ref[...] = _select_rows(sel_out, ts_ref[sl:sl + tm, :])
    else:
        start = (i % seq_tiles) == 0

        @pl.when(start)
        def _():
            us_ref[0:sl, :] = jnp.zeros((sl, tn), F32)
            ts_ref[0:sl, :] = jnp.zeros((sl, tn), F32)

        @pl.when(jnp.logical_not(start))
        def _():
            us_ref[0:sl, :] = tailu_ref[j]
            ts_ref[0:sl, :] = tailt_ref[j]

        tailu_ref[j] = us_ref[tm:tm + sl, :]
        tailt_ref[j] = ts_ref[tm:tm + sl, :]
        cnu_ref[...] = us_ref[tm:tm + sl, :]
        cnt_ref[...] = ts_ref[tm:tm + sl, :]

    rc = min(tm, 256)
    for r0 in range(0, tm, rc):
        rowmod = (lax.broadcasted_iota(jnp.int32, (rc, tn), 0) + r0) % seg if short else None

        def conv(s_ref, cw_ref, cb_ref, halos):
            y = s_ref[sl + r0:sl + r0 + rc, :] * cw_ref[hw:hw + 1, :]
            for s in range(1, FFN_CONV):
                prev = s_ref[sl + r0 - s:sl + r0 - s + rc, :]
                if short:
                    prev = jnp.where(rowmod < s, halos[s - 1][r0:r0 + rc, :], prev)
                y = y + prev * cw_ref[hw - s:hw - s + 1, :]
            return y + cb_ref[...]

        u = conv(us_ref, cwu_ref, cbu_ref, halo_u if short else None)
        t = conv(ts_ref, cwt_ref, cbt_ref, halo_t if short else None)
        act_ref[r0:r0 + rc, :] = (_silu(t) * u).astype(act_ref.dtype)


def _ffn_up(x, gain, w, conv_w, conv_b, conv0, layer, *, L, tm, tn):
    m, k = x.shape
    bt = m // L
    nj = D_FF // tn
    hw = FFN_CONV - 1
    short = L < tm
    seg = L if short else 0
    seq_tiles = 1 if short else L // tm
    assert (conv0 is not None) == short, "history rows are only supported for sequences shorter than a block"
    in_specs = [
        pl.BlockSpec((tm, k), lambda i, j: (i, 0)),
        pl.BlockSpec((None, 1, k), lambda i, j: (layer, 0, 0)),
        pl.BlockSpec((None, k, tn), lambda i, j: (layer, 0, j)),
        pl.BlockSpec((None, k, tn), lambda i, j: (layer, 0, nj + j)),
        pl.BlockSpec((None, FFN_CONV, tn), lambda i, j: (layer, 0, j)),
        pl.BlockSpec((None, FFN_CONV, tn), lambda i, j: (layer, 0, nj + j)),
        pl.BlockSpec((None, 1, tn), lambda i, j: (layer, 0, j)),
        pl.BlockSpec((None, 1, tn), lambda i, j: (layer, 0, nj + j)),
    ]
    args = [x, gain.reshape(DEPTH, 1, k), w, w, conv_w, conv_w, conv_b.reshape(DEPTH, 1, 2 * D_FF),
            conv_b.reshape(DEPTH, 1, 2 * D_FF)]
    scratch = [pltpu.VMEM((tm, k), BF16), pltpu.VMEM((tm + SUBLANES, tn), F32), pltpu.VMEM((tm + SUBLANES, tn), F32)]
    if short:
        nst = (m // tm) * (tm // L) * hw
        st2d = conv0.reshape(DEPTH, nst, 2 * D_FF)
        rows = (tm // L) * hw
        in_specs += [pl.BlockSpec((None, rows, tn), lambda i, j: (layer, i, j)),
                     pl.BlockSpec((None, rows, tn), lambda i, j: (layer, i, nj + j))]
        args += [st2d, st2d]
        cn_shape = jax.ShapeDtypeStruct((nst, D_FF), F32)
        cn_spec = pl.BlockSpec((rows, tn), lambda i, j: (i, j))
    else:
        scratch += [pltpu.VMEM((nj, SUBLANES, tn), F32), pltpu.VMEM((nj, SUBLANES, tn), F32)]
        cn_shape = jax.ShapeDtypeStruct((m // tm, SUBLANES, D_FF), F32)
        cn_spec = pl.BlockSpec((None, SUBLANES, tn), lambda i, j: (i, 0, j))
    limit = _vmem_limit(_nbytes((tm, k), F32), 2 * _nbytes((k, tn), F32), _nbytes((tm, tn), BF16),
                        scratch=_nbytes((tm, k), BF16) + 2 * _nbytes((tm + SUBLANES, tn), F32))
    act, cnu, cnt = pl.pallas_call(
        functools.partial(_ffn_up_kernel, tm=tm, tn=tn, seq_tiles=seq_tiles, seg=seg),
        grid=(m // tm, nj),
        in_specs=in_specs,
        out_specs=[pl.BlockSpec((tm, tn), lambda i, j: (i, j)), cn_spec, cn_spec],
        out_shape=[jax.ShapeDtypeStruct((m, D_FF), BF16), cn_shape, cn_shape],
        scratch_shapes=scratch,
        compiler_params=pltpu.CompilerParams(dimension_semantics=("arbitrary", "arbitrary"),
                                             vmem_limit_bytes=limit),
        name="ffn_up",
    )(*args)
    if short:
        conv_new = jnp.concatenate([cnu, cnt], axis=-1).reshape(bt, hw, 2 * D_FF)
    else:
        last = slice(seq_tiles - 1, None, seq_tiles)
        conv_new = jnp.concatenate([cnu[last, SUBLANES - hw:, :], cnt[last, SUBLANES - hw:, :]], axis=-1)
    return act, conv_new


def _causal_conv(x, tail_ref, w_ref, b, width):
    row = lax.broadcasted_iota(jnp.int32, x.shape, 0)
    y = x * w_ref[width - 1:width, :]
    for s in range(1, width):
        shifted = pltpu.roll(x, s, axis=0)
        for t in range(s):
            shifted = jnp.where(row == t, tail_ref[8 + t - s:9 + t - s, :], shifted)
        y = y + shifted * w_ref[width - 1 - s:width - s, :]
    return y + b


def _ssd_kernel(*refs, Q, zero_init, aliased):
    it = iter(refs)
    z_ref, xs_ref, bc_ref, sm_ref = (next(it) for _ in range(4))
    conv0_ref = None if zero_init else next(it)
    h0_ref = None if zero_init else next(it)
    cwx_ref, cwb_ref, cbx_ref, cbb_ref, dtb_ref, a_ref, dx_ref, nrm_ref, exp_ref = (next(it) for _ in range(9))
    if aliased:
        next(it), next(it)
    y_ref, h_out, conv_out, tailx_ref, tailb_ref = (next(it) for _ in range(5))
    h_ref = h_out if aliased else h_out.at[0]
    convout_ref = conv_out if aliased else conv_out.at[0]
    c = pl.program_id(1)
    hp = SSD_HEADDIM
    gw = SSD_HEADS // SSD_GROUPS * hp

    @pl.when(c == 0)
    def _():
        if not aliased:
            h_out[1:] = jnp.zeros((DEPTH - 1,) + h_ref.shape, F32)
            conv_out[1:] = jnp.zeros((DEPTH - 1,) + convout_ref.shape, F32)
        tailx_ref[...] = jnp.zeros_like(tailx_ref)
        tailb_ref[...] = jnp.zeros_like(tailb_ref)
        if zero_init:
            h_ref[...] = jnp.zeros_like(h_ref)
        else:
            h_ref[...] = h0_ref[...]
            tailx_ref[5:8, :] = conv0_ref[:, 0:D_MODEL]
            tailb_ref[5:8, :] = conv0_ref[:, D_MODEL:SSD_CONV_DIM]

    xs_raw = xs_ref[...].astype(F32)
    bc_raw = bc_ref[...].astype(F32)
    xs = _silu(_causal_conv(xs_raw, tailx_ref, cwx_ref, cbx_ref[...], SSD_CONV))
    bc = _silu(_causal_conv(bc_raw, tailb_ref, cwb_ref, cbb_ref[...], SSD_CONV))
    tailx_ref[...] = xs_raw[Q - 8:Q, :]
    tailb_ref[...] = bc_raw[Q - 8:Q, :]

    @pl.when(c == pl.num_programs(1) - 1)
    def _():
        convout_ref[:, 0:D_MODEL] = tailx_ref[5:8, :]
        convout_ref[:, D_MODEL:SSD_CONV_DIM] = tailb_ref[5:8, :]

    dt = _softplus(sm_ref[...] + dtb_ref[...])
    adt = dt * a_ref[...]
    ri = lax.broadcasted_iota(jnp.int32, (Q, Q), 0)
    ci = lax.broadcasted_iota(jnp.int32, (Q, Q), 1)
    causal = ci <= ri
    acum = jnp.dot(causal.astype(F32), adt, precision=HI, preferred_element_type=F32)
    eye = (lax.broadcasted_iota(jnp.int32, (LANES, LANES), 0)
           == lax.broadcasted_iota(jnp.int32, (LANES, LANES), 1)).astype(F32)
    acum_t = lax.dot_general(eye, acum, NT_DIMS, precision=HI, preferred_element_type=F32)
    dt_t = lax.dot_general(eye, dt, NT_DIMS, precision=HI, preferred_element_type=F32)
    a_last = acum[Q - 1:Q, :]
    ea = jnp.exp(acum)
    te = jnp.exp(a_last - acum) * dt
    dec_rows = jnp.broadcast_to(jnp.exp(acum_t[:, Q - 1:Q]), (LANES, LANES))
    ea_hi, ea_lo = _split2(ea)
    te_hi, te_lo = _split2(te)
    expand = exp_ref[...]
    ea_x = (jnp.dot(ea_hi, expand, preferred_element_type=F32)
            + jnp.dot(ea_lo, expand, preferred_element_type=F32))
    te_x = (jnp.dot(te_hi, expand, preferred_element_type=F32)
            + jnp.dot(te_lo, expand, preferred_element_type=F32))
    lane = lax.broadcasted_iota(jnp.int32, (Q, LANES), 1)

    for g in range(SSD_GROUPS):
        gl = slice(g * gw, (g + 1) * gw)
        b_g = bc[:, g * SSD_STATE:(g + 1) * SSD_STATE].astype(BF16)
        c_g = bc[:, (SSD_GROUPS + g) * SSD_STATE:(SSD_GROUPS + g + 1) * SSD_STATE].astype(BF16)
        cb = lax.dot_general(c_g, b_g, NT_DIMS, preferred_element_type=F32)
        h_g = h_ref[8 * g:8 * g + 8].reshape(gw, SSD_STATE)
        y_off = lax.dot_general(c_g, h_g.astype(BF16), NT_DIMS, preferred_element_type=F32)
        x_g = xs[:, gl]
        pairs = []
        for p in range(4):
            x_p = x_g[:, p * LANES:(p + 1) * LANES]
            acc = None
            for s in range(2):
                hh = g * 8 + p * 2 + s
                seg = jnp.broadcast_to(acum[:, hh:hh + 1], (Q, Q)) - jnp.broadcast_to(acum_t[hh:hh + 1, :], (Q, Q))
                decay = jnp.where(causal, jnp.exp(seg), 0.0)
                w_h = (cb * decay * dt_t[hh:hh + 1, :]).astype(BF16)
                x_m = jnp.where((lane // hp) == s, x_p, 0.0).astype(BF16)
                r = jnp.dot(w_h, x_m, preferred_element_type=F32)
                acc = r if acc is None else acc + r
            pairs.append(acc)
        y_g = jnp.concatenate(pairs, axis=1) + y_off * ea_x[:, gl] + dx_ref[:, gl] * x_g
        x_t = (x_g * te_x[:, gl]).astype(BF16)
        upd = lax.dot_general(x_t, b_g, TN_DIMS, preferred_element_type=F32)
        for h in range(8):
            hh = g * 8 + h
            h_ref[hh] = h_ref[hh] * dec_rows[hh:hh + 1, :] + upd[h * hp:(h + 1) * hp, :]
        z_g = z_ref[:, gl].astype(F32)
        y_ref[:, gl] = _rms(y_g * _silu(z_g), nrm_ref[:, gl]).astype(y_ref.dtype)


def _ssd(proj, small, conv0, h0, wl, layer, prev_h, prev_conv, *, bt, L, Q):
    zero_init = h0 is None
    aliased = prev_h is not None
    nc = L // Q
    T = bt * L
    full2 = lambda b, c: (0, 0)
    lay3 = lambda b, c: (layer, 0, 0)
    row = lambda col: (lambda b, c: (b * nc + c, col))
    hshape = (DEPTH, bt, SSD_HEADS, SSD_HEADDIM, SSD_STATE)
    cshape = (DEPTH, bt, SSD_CONV - 1, SSD_CONV_DIM)
    hspec = pl.BlockSpec((None, None, SSD_HEADS, SSD_HEADDIM, SSD_STATE), lambda b, c: (layer, b, 0, 0, 0))
    cspec = pl.BlockSpec((None, None, SSD_CONV - 1, SSD_CONV_DIM), lambda b, c: (layer, b, 0, 0))
    h_out_spec = hspec if aliased else pl.BlockSpec((DEPTH, None, SSD_HEADS, SSD_HEADDIM, SSD_STATE),
                                                    lambda b, c: (0, b, 0, 0, 0))
    c_out_spec = cspec if aliased else pl.BlockSpec((DEPTH, None, SSD_CONV - 1, SSD_CONV_DIM),
                                                    lambda b, c: (0, b, 0, 0))
    in_specs = [
        pl.BlockSpec((Q, D_MODEL), row(P_Z // D_MODEL)),
        pl.BlockSpec((Q, D_MODEL), row(P_XS // D_MODEL)),
        pl.BlockSpec((Q, 1024), row(P_BC // 1024)),
        pl.BlockSpec((Q, LANES), row(0)),
    ]
    args = [proj, proj, proj, small]
    if not zero_init:
        in_specs += [cspec, hspec]
        args += [conv0, h0]
    in_specs += [
        pl.BlockSpec((None, SSD_CONV, D_MODEL), lay3),
        pl.BlockSpec((None, SSD_CONV, 1024), lambda b, c: (layer, 0, D_MODEL // 1024)),
        pl.BlockSpec((None, 1, D_MODEL), lay3),
        pl.BlockSpec((None, 1, 1024), lambda b, c: (layer, 0, D_MODEL // 1024)),
        pl.BlockSpec((None, 1, LANES), lay3),
        pl.BlockSpec((None, 1, LANES), lay3),
        pl.BlockSpec((None, 1, D_MODEL), lay3),
        pl.BlockSpec((None, 1, D_MODEL), lay3),
        pl.BlockSpec((LANES, D_MODEL), full2),
    ]
    args += [wl["ssd_conv_w"], wl["ssd_conv_w"], wl["ssd_conv_b"], wl["ssd_conv_b"], wl["ssd_dtb"], wl["ssd_a"],
             wl["ssd_dx"], wl["ssd_norm"], wl["expand"]]
    aliases = {}
    if aliased:
        aliases = {len(args): 1, len(args) + 1: 2}
        in_specs += [pl.BlockSpec(memory_space=pl.ANY), pl.BlockSpec(memory_space=pl.ANY)]
        args += [prev_h, prev_conv]
    return pl.pallas_call(
        functools.partial(_ssd_kernel, Q=Q, zero_init=zero_init, aliased=aliased),
        grid=(bt, nc),
        in_specs=in_specs,
        out_specs=[pl.BlockSpec((Q, D_MODEL), row(0)), h_out_spec, c_out_spec],
        out_shape=[jax.ShapeDtypeStruct((T, D_MODEL), BF16), _stacked_out(prev_h, hshape, F32),
                   _stacked_out(prev_conv, cshape, F32)],
        scratch_shapes=[pltpu.VMEM((8, D_MODEL), F32), pltpu.VMEM((8, 1024), F32)],
        input_output_aliases=aliases,
        compiler_params=pltpu.CompilerParams(dimension_semantics=("parallel", "arbitrary"),
                                             vmem_limit_bytes=48 << 20),
        name="ssd",
    )(*args)


def _gla_kernel(*refs, TB, zero_init, aliased):
    it = iter(refs)
    q_ref, k_ref, v_ref, g_ref, sm_ref = (next(it) for _ in range(5))
    s0_ref = None if zero_init else next(it)
    wa_ref, ba_ref, wat_ref, bat_ref, gn_ref = (next(it) for _ in range(5))
    if aliased:
        next(it)
    y_ref, s_out = next(it), next(it)
    s_ref = s_out if aliased else s_out.at[0]
    c = pl.program_id(1)
    ck = GLA_CHUNK
    ns = TB // ck
    hk, hv = GLA_HEAD_K, GLA_HEAD_V

    @pl.when(c == 0)
    def _():
        if not aliased:
            s_out[1:] = jnp.zeros((DEPTH - 1,) + s_ref.shape, F32)
        if zero_init:
            s_ref[...] = jnp.zeros_like(s_ref)
        else:
            s_ref[...] = s0_ref[...]

    smb = sm_ref[...].astype(BF16)
    la = _log_sigmoid(jnp.dot(smb, wa_ref[...], preferred_element_type=F32) + ba_ref[...]) * (1.0 / GLA_GATE_NORM)
    rb = lax.broadcasted_iota(jnp.int32, (TB, TB), 0)
    cb = lax.broadcasted_iota(jnp.int32, (TB, TB), 1)
    same = (rb // ck) == (cb // ck)
    msel = jnp.concatenate([same & (cb <= rb), same, (cb // ck) < (rb // ck)], axis=0).astype(BF16)
    sums = _select_rows(msel, la)
    bcum, tot, bprev = sums[0:TB], sums[TB:2 * TB], sums[2 * TB:3 * TB]
    kf = k_ref[...].astype(F32)
    qd = q_ref[...].astype(F32) * (hk ** -0.5) * jnp.exp(bcum)
    qd_b = qd.astype(BF16)
    qs_b = (qd * jnp.exp(bprev)).astype(BF16)
    ki_b = (kf * jnp.exp(-bcum)).astype(BF16)
    ke_b = (kf * jnp.exp(tot - bcum)).astype(BF16)
    la_t = _log_sigmoid(lax.dot_general(wat_ref[...], smb, NT_DIMS, preferred_element_type=F32)
                        + bat_ref[:, 0:1]) * (1.0 / GLA_GATE_NORM)
    rs = lax.broadcasted_iota(jnp.int32, (TB, LANES), 0)
    cs = lax.broadcasted_iota(jnp.int32, (TB, LANES), 1)
    selc = (((rs // ck) == cs) | (cs == ns)).astype(BF16)
    la_hi = la_t.astype(BF16)
    la_r = la_t - la_hi.astype(F32)
    la_mid = la_r.astype(BF16)
    la_lo = (la_r - la_mid.astype(F32)).astype(BF16)
    dec_t = jnp.exp(jnp.dot(la_hi, selc, preferred_element_type=F32) + jnp.dot(la_mid, selc, preferred_element_type=F32)
                    + jnp.dot(la_lo, selc, preferred_element_type=F32))
    eye = (lax.broadcasted_iota(jnp.int32, (hk, hk), 0) == lax.broadcasted_iota(jnp.int32, (hk, hk), 1)).astype(BF16)
    slot = lax.broadcasted_iota(jnp.int32, (hk, TB), 1) // ck
    row16 = lax.broadcasted_iota(jnp.int32, (ck, TB), 0)
    col16 = lax.broadcasted_iota(jnp.int32, (ck, TB), 1)

    for h in range(GLA_HEADS):
        kl = slice(h * hk, (h + 1) * hk)
        vl = slice(h * hv, (h + 1) * hv)
        ke_t = lax.dot_general(eye, ke_b[:, kl], NT_DIMS, preferred_element_type=F32)
        ki_t = lax.dot_general(eye, ki_b[:, kl], NT_DIMS, preferred_element_type=F32)
        v_h = v_ref[:, vl]
        kbuf = jnp.zeros((hk, TB), F32)
        att = []
        for cc in range(ns):
            in_c = slot == cc
            kall = jnp.where(in_c, ki_t, kbuf).astype(BF16)
            a = jnp.dot(qd_b[cc * ck:(cc + 1) * ck, kl], kall, preferred_element_type=F32)
            att.append(jnp.where(col16 <= row16 + cc * ck, a, 0.0))
            kbuf = jnp.where(in_c, ke_t, kbuf * dec_t[kl, cc:cc + 1])
        a_full = jnp.concatenate(att, axis=0).astype(BF16)
        s_old = s_ref[h]
        o = (jnp.dot(a_full, v_h, preferred_element_type=F32)
             + jnp.dot(qs_b[:, kl], s_old.astype(BF16), preferred_element_type=F32))
        y_ref[:, vl] = (_rms(o, gn_ref[...]) * _silu(g_ref[:, vl].astype(F32))).astype(y_ref.dtype)
        s_ref[h] = s_old * dec_t[kl, ns:ns + 1] + jnp.dot(kbuf.astype(BF16), v_h, preferred_element_type=F32)


def _gla(proj, small, s0, wl, layer, prev_s, *, bt, L, TB):
    zero_init = s0 is None
    aliased = prev_s is not None
    nc = L // TB
    T = bt * L
    lay3 = lambda b, c: (layer, 0, 0)
    row = lambda col: (lambda b, c: (b * nc + c, col))
    kd = GLA_HEADS * GLA_HEAD_K
    sshape = (DEPTH, bt, GLA_HEADS, GLA_HEAD_K, GLA_HEAD_V)
    sspec = pl.BlockSpec((None, None, GLA_HEADS, GLA_HEAD_K, GLA_HEAD_V), lambda b, c: (layer, b, 0, 0, 0))
    s_out_spec = sspec if aliased else pl.BlockSpec((DEPTH, None, GLA_HEADS, GLA_HEAD_K, GLA_HEAD_V),
                                                     lambda b, c: (0, b, 0, 0, 0))
    in_specs = [
        pl.BlockSpec((TB, kd), row(P_Q // kd)),
        pl.BlockSpec((TB, kd), row(P_K // kd)),
        pl.BlockSpec((TB, D_MODEL), row(P_V // D_MODEL)),
        pl.BlockSpec((TB, D_MODEL), row(P_G // D_MODEL)),
        pl.BlockSpec((TB, LANES), row(0)),
    ]
    args = [proj, proj, proj, proj, small]
    if not zero_init:
        in_specs.append(sspec)
        args.append(s0)
    in_specs += [
        pl.BlockSpec((None, LANES, kd), lay3),
        pl.BlockSpec((None, 1, kd), lay3),
        pl.BlockSpec((None, kd, LANES), lay3),
        pl.BlockSpec((None, kd, LANES), lay3),
        pl.BlockSpec((None, 1, GLA_HEAD_V), lay3),
    ]
    args += [wl["gla_wa"], wl["gla_ba"], wl["gla_wa_t"], wl["gla_ba_t"], wl["gla_norm"]]
    aliases = {}
    if aliased:
        aliases = {len(args): 1}
        in_specs.append(pl.BlockSpec(memory_space=pl.ANY))
        args.append(prev_s)
    return pl.pallas_call(
        functools.partial(_gla_kernel, TB=TB, zero_init=zero_init, aliased=aliased),
        grid=(bt, nc),
        in_specs=in_specs,
        out_specs=[pl.BlockSpec((TB, D_MODEL), row(0)), s_out_spec],
        out_shape=[jax.ShapeDtypeStruct((T, D_MODEL), BF16), _stacked_out(prev_s, sshape, F32)],
        input_output_aliases=aliases,
        compiler_params=pltpu.CompilerParams(dimension_semantics=("parallel", "arbitrary"),
                                             vmem_limit_bytes=48 << 20),
        name="gla",
    )(*args)


def _attn_kernel(q_ref, k_ref, v_ref, o_ref, kb_ref, vb_ref):
    @pl.when(pl.program_id(1) == 0)
    def _():
        kb_ref[...] = k_ref[...].astype(BF16)
        vb_ref[...] = v_ref[...].astype(BF16)

    hd = MEM_HEAD_DIM
    for h in range(MEM_HEADS):
        hl = slice(h * hd, (h + 1) * hd)
        s = lax.dot_general(q_ref[:, hl], kb_ref[:, hl], NT_DIMS, preferred_element_type=F32) * (hd ** -0.5)
        e = jnp.exp(s - jnp.max(s, axis=-1, keepdims=True))
        p = e / jnp.sum(e, axis=-1, keepdims=True)
        o_ref[:, hl] = jnp.dot(p.astype(BF16), vb_ref[:, hl], preferred_element_type=F32).astype(o_ref.dtype)


def _attn(q, mem_k, mem_v, layer, *, bt, L, tl):
    nl = L // tl
    T = bt * L
    kvspec = pl.BlockSpec((None, None, MEM_TOKENS, D_MODEL), lambda b, l: (layer, b, 0, 0))
    return pl.pallas_call(
        _attn_kernel,
        grid=(bt, nl),
        in_specs=[pl.BlockSpec((tl, D_MODEL), lambda b, l: (b * nl + l, 0)), kvspec, kvspec],
        out_specs=pl.BlockSpec((tl, D_MODEL), lambda b, l: (b * nl + l, 0)),
        out_shape=jax.ShapeDtypeStruct((T, D_MODEL), BF16),
        scratch_shapes=[pltpu.VMEM((MEM_TOKENS, D_MODEL), BF16), pltpu.VMEM((MEM_TOKENS, D_MODEL), BF16)],
        compiler_params=pltpu.CompilerParams(dimension_semantics=("parallel", "arbitrary"),
                                             vmem_limit_bytes=40 << 20),
        name="mem_attn",
    )(q, mem_k, mem_v)


def _prep_weights(w_in, ssd_conv_w, ssd_conv_b, ssd_dt_bias, ssd_a_log, ssd_d, ssd_norm, gla_wa2, gla_ba, gla_norm):
    o_z, o_xs, o_bc, o_dt, o_q, o_k, o_v, o_g, o_alr, o_gate, o_end = (
        0, 2048, 4096, 5120, 5152, 6176, 7200, 9248, 11296, 11312, 15408)
    w_in_p = jnp.concatenate(
        [w_in[:, :, o_z:o_xs], w_in[:, :, o_xs:o_bc], w_in[:, :, o_v:o_g], w_in[:, :, o_g:o_alr],
         w_in[:, :, o_gate:o_end], w_in[:, :, o_bc:o_dt], w_in[:, :, o_q:o_k], w_in[:, :, o_k:o_v]],
        axis=-1).astype(BF16)
    pad_sm = LANES - SSD_HEADS - GLA_RANK
    w_small = jnp.concatenate(
        [w_in[:, :, o_dt:o_q], w_in[:, :, o_alr:o_gate], jnp.zeros((DEPTH, D_MODEL, pad_sm), F32)],
        axis=-1).astype(BF16)
    pad_h = LANES - SSD_HEADS
    kd = GLA_HEADS * GLA_HEAD_K
    wa = jnp.concatenate([jnp.zeros((DEPTH, SM_ALR, kd), F32), gla_wa2,
                          jnp.zeros((DEPTH, LANES - SM_ALR - GLA_RANK, kd), F32)], axis=1).astype(BF16)
    expand = (jnp.arange(D_MODEL)[None, :] // SSD_HEADDIM == jnp.arange(LANES)[:, None]).astype(BF16)
    return dict(
        w_in=w_in_p, w_small=w_small,
        ssd_conv_w=ssd_conv_w, ssd_conv_b=ssd_conv_b[:, None, :],
        ssd_dtb=jnp.pad(ssd_dt_bias, ((0, 0), (0, pad_h)))[:, None, :],
        ssd_a=jnp.pad(-jnp.exp(ssd_a_log), ((0, 0), (0, pad_h)))[:, None, :],
        ssd_dx=jnp.repeat(ssd_d, SSD_HEADDIM, axis=1)[:, None, :],
        ssd_norm=ssd_norm[:, None, :], expand=expand,
        gla_wa=wa, gla_ba=gla_ba[:, None, :], gla_wa_t=jnp.swapaxes(wa, 1, 2),
        gla_ba_t=jnp.broadcast_to(gla_ba[:, :, None], (DEPTH, kd, LANES)),
        gla_norm=gla_norm[:, None, :])


def _run_trunk(x, mem_k, mem_v, st_ssd, st_ssd_conv, st_gla, st_ffn_conv, wl, big, norm_final, *, bt, L, cfg):
    tm = cfg["tm"]
    n_h = n_c = n_s = None
    new_ffn = []
    for i in range(DEPTH):
        proj, small = _norm_mm(x, big["norm_mix"], wl["w_in"], i, wl["w_small"], tm=tm, tn=1024, out_dtype=BF16)
        y_ssd, n_h, n_c = _ssd(proj, small, st_ssd_conv, st_ssd, wl, i, n_h, n_c, bt=bt, L=L, Q=cfg["ssd_q"])
        y_gla, n_s = _gla(proj, small, st_gla, wl, i, n_s, bt=bt, L=L, TB=cfg["gla_tb"])
        merged = _merge(y_ssd, y_gla, proj, big["w_branch"], i, tm=tm, tn=512)
        x = _mm_res(merged, big["w_out"], i, x, tm=tm, tn=512)
        q = _norm_mm(x, big["norm_mem"], big["w_mq"], i, None, tm=tm, tn=512, out_dtype=BF16)
        o = _attn(q, mem_k, mem_v, i, bt=bt, L=L, tl=cfg["attn_tl"])
        x = _mm_res(o, big["w_mo"], i, x, tm=tm, tn=512)
        act, f_c = _ffn_up(x, big["norm_ffn"], big["w_ffn_in"], big["ffn_conv_w"], big["ffn_conv_b"], st_ffn_conv, i,
                           L=L, tm=tm, tn=512)
        x = _mm_res(act, big["w_ffn_out"], i, x, tm=tm, tn=256)
        new_ffn.append(f_c)
    y = _final_norm(x, norm_final, tm=cfg["tm_norm"])
    return y, n_h, n_c, n_s, jnp.stack(new_ffn)


def _group_cfg(bt, L):
    T = bt * L
    return dict(tm=min(T, 1024), tm_norm=min(T, 512), ssd_q=min(L, 128), gla_tb=min(L, 128), attn_tl=min(L, 512))


def kernel(x_prompt, x_sample, mem_prompt, state_ssd, state_ssd_conv, state_gla, state_ffn_conv, cache_mem_k, cache_mem_v, norm_mix, w_in, ssd_conv_w, ssd_conv_b, ssd_dt_bias, ssd_a_log, ssd_d, ssd_norm, gla_wa2, gla_ba, gla_norm, w_branch, w_out, norm_mem, w_mq, w_mk, w_mv, w_mo, norm_ffn, w_ffn_in, ffn_conv_w, ffn_conv_b, w_ffn_out, norm_final):
    wl = _prep_weights(w_in, ssd_conv_w, ssd_conv_b, ssd_dt_bias, ssd_a_log, ssd_d, ssd_norm, gla_wa2, gla_ba, gla_norm)
    big = dict(norm_mix=norm_mix, w_branch=w_branch, w_out=w_out, norm_mem=norm_mem, w_mq=w_mq, w_mo=w_mo,
               norm_ffn=norm_ffn, w_ffn_in=w_ffn_in, ffn_conv_w=ffn_conv_w, ffn_conv_b=ffn_conv_b, w_ffn_out=w_ffn_out)
    pb, pl_len, _ = x_prompt.shape
    sb, sl_len, _ = x_sample.shape
    mem2d = mem_prompt.reshape(pb * MEM_TOKENS, D_MODEL)
    p_mem_k = _mem_kv(mem2d, w_mk, tn=512).reshape(DEPTH, pb, MEM_TOKENS, D_MODEL)
    p_mem_v = _mem_kv(mem2d, w_mv, tn=512).reshape(DEPTH, pb, MEM_TOKENS, D_MODEL)

    y_p, p_ssd, p_ssd_conv, p_gla, p_ffn = _run_trunk(
        x_prompt.reshape(pb * pl_len, D_MODEL), p_mem_k, p_mem_v, None, None, None, None, wl, big, norm_final,
        bt=pb, L=pl_len, cfg=_group_cfg(pb, pl_len))
    y_s, s_ssd, s_ssd_conv, s_gla, s_ffn = _run_trunk(
        x_sample.reshape(sb * sl_len, D_MODEL),
        cache_mem_k.reshape(DEPTH, sb, MEM_TOKENS, D_MODEL), cache_mem_v.reshape(DEPTH, sb, MEM_TOKENS, D_MODEL),
        state_ssd, state_ssd_conv, state_gla, state_ffn_conv, wl, big, norm_final,
        bt=sb, L=sl_len, cfg=_group_cfg(sb, sl_len))
    kv_shape = (DEPTH, pb, MEM_TOKENS, MEM_HEADS, MEM_HEAD_DIM)
    return (y_p.reshape(pb, pl_len, D_MODEL), y_s.reshape(sb, sl_len, D_MODEL),
            p_ssd, p_ssd_conv, p_gla, p_ffn, p_mem_k.reshape(kv_shape), p_mem_v.reshape(kv_shape),
            s_ssd, s_ssd_conv, s_gla, s_ffn)
```

```python
import functools

import jax
import jax.numpy as jnp
from jax import lax
from jax.experimental import pallas as pl
from jax.experimental.pallas import tpu as pltpu

F32 = jnp.float32
BF16 = jnp.bfloat16

D_MODEL = 2048
DEPTH = 2
EPS = 1e-6
SSD_HEADS = 32
SSD_HEADDIM = 64
SSD_GROUPS = 4
SSD_STATE = 128
SSD_CONV = 4
SSD_CONV_DIM = 3072
GLA_HEADS = 4
GLA_HEAD_K = 256
GLA_HEAD_V = 512
GLA_RANK = 16
GLA_GATE_NORM = 16.0
GLA_CHUNK = 16
MEM_TOKENS = 256
MEM_HEADS = 4
MEM_HEAD_DIM = 512
D_FF = 5632
FFN_CONV = 3

LANES = 128
SUBLANES = 8
VMEM_CAP_BYTES = 56 * 2**20

P_Z, P_XS, P_V, P_G, P_GATE, P_BC, P_Q, P_K, P_N = 0, 2048, 4096, 6144, 8192, 12288, 13312, 14336, 15360
SM_DT, SM_ALR = 0, 32

NT_DIMS = (((1,), (1,)), ((), ()))
TN_DIMS = (((0,), (0,)), ((), ()))
HI = lax.Precision.HIGHEST


def _vmem_limit(*block_bytes, scratch=0):
    need = 2 * sum(block_bytes) + scratch + (4 << 20)
    return int(min(max(need, 16 << 20), VMEM_CAP_BYTES))


def _nbytes(shape, dtype):
    n = 1
    for s in shape:
        n *= s
    return n * jnp.dtype(dtype).itemsize


def _silu(x):
    return x * jax.nn.sigmoid(x)


def _softplus(x):
    return jnp.maximum(x, 0.0) + jnp.log1p(jnp.exp(-jnp.abs(x)))


def _log_sigmoid(x):
    return -_softplus(-x)


def _rms(x, gain):
    ms = jnp.mean(x * x, axis=-1, keepdims=True)
    return x * lax.rsqrt(ms + EPS) * gain


def _split2(x):
    hi = x.astype(BF16)
    lo = (x - hi.astype(F32)).astype(BF16)
    return hi, lo


def _select_rows(sel, x):
    hi = x.astype(BF16)
    r1 = x - hi.astype(F32)
    mid = r1.astype(BF16)
    lo = (r1 - mid.astype(F32)).astype(BF16)
    return (jnp.dot(sel, hi, preferred_element_type=F32) + jnp.dot(sel, mid, preferred_element_type=F32)
            + jnp.dot(sel, lo, preferred_element_type=F32))


def _stacked_out(prev, shape, dtype):
    return jax.ShapeDtypeStruct(shape, dtype) if prev is None else jax.ShapeDtypeStruct(prev.shape, prev.dtype)


def _norm_mm_kernel(*refs, has_gain, has_small):
    it = iter(refs)
    x_ref = next(it)
    g_ref = next(it) if has_gain else None
    w_ref = next(it)
    ws_ref = next(it) if has_small else None
    o_ref = next(it)
    os_ref = next(it) if has_small else None
    xn_ref = next(it)

    @pl.when(pl.program_id(1) == 0)
    def _():
        x = x_ref[...].astype(F32)
        if has_gain:
            x = _rms(x, g_ref[...])
        xn_ref[...] = x.astype(BF16)
        if has_small:
            os_ref[...] = jnp.dot(xn_ref[...], ws_ref[...], preferred_element_type=F32)

    o_ref[...] = jnp.dot(xn_ref[...], w_ref[...].astype(BF16), preferred_element_type=F32).astype(o_ref.dtype)


def _norm_mm(x, gain, w, layer, w_small, *, tm, tn, out_dtype):
    m, k = x.shape
    n = w.shape[2]
    has_gain = gain is not None
    has_small = w_small is not None
    in_specs = [pl.BlockSpec((tm, k), lambda i, j: (i, 0))]
    args = [x]
    if has_gain:
        in_specs.append(pl.BlockSpec((None, 1, k), lambda i, j: (layer, 0, 0)))
        args.append(gain.reshape(DEPTH, 1, k))
    in_specs.append(pl.BlockSpec((None, k, tn), lambda i, j: (layer, 0, j)))
    args.append(w)
    out_shape = [jax.ShapeDtypeStruct((m, n), out_dtype)]
    out_specs = [pl.BlockSpec((tm, tn), lambda i, j: (i, j))]
    if has_small:
        in_specs.append(pl.BlockSpec((None, k, LANES), lambda i, j: (layer, 0, 0)))
        args.append(w_small)
        out_shape.append(jax.ShapeDtypeStruct((m, LANES), F32))
        out_specs.append(pl.BlockSpec((tm, LANES), lambda i, j: (i, 0)))
    limit = _vmem_limit(_nbytes((tm, k), x.dtype), _nbytes((k, tn), w.dtype), _nbytes((tm, tn), out_dtype),
                        _nbytes((k, LANES), BF16), _nbytes((tm, LANES), F32), scratch=_nbytes((tm, k), BF16))
    res = pl.pallas_call(
        functools.partial(_norm_mm_kernel, has_gain=has_gain, has_small=has_small),
        grid=(m // tm, n // tn),
        in_specs=in_specs,
        out_specs=out_specs,
        out_shape=out_shape,
        scratch_shapes=[pltpu.VMEM((tm, k), BF16)],
        compiler_params=pltpu.CompilerParams(dimension_semantics=("parallel", "arbitrary"),
                                             vmem_limit_bytes=limit),
        name="norm_mm",
    )(*args)
    return res if has_small else res[0]


def _mem_kv_kernel(x_ref, w_ref, o_ref, xb_ref):
    @pl.when((pl.program_id(0) == 0) & (pl.program_id(1) == 0))
    def _():
        xb_ref[...] = x_ref[...].astype(BF16)

    o_ref[...] = jnp.dot(xb_ref[...], w_ref[...].astype(BF16), preferred_element_type=F32)


def _mem_kv(x, w, *, tn):
    m, k = x.shape
    n = w.shape[2]
    limit = _vmem_limit(_nbytes((m, k), F32), _nbytes((k, tn), F32), _nbytes((m, tn), F32),
                        scratch=_nbytes((m, k), BF16))
    return pl.pallas_call(
        _mem_kv_kernel,
        grid=(DEPTH, n // tn),
        in_specs=[pl.BlockSpec((m, k), lambda d, j: (0, 0)),
                  pl.BlockSpec((None, k, tn), lambda d, j: (d, 0, j))],
        out_specs=pl.BlockSpec((None, m, tn), lambda d, j: (d, 0, j)),
        out_shape=jax.ShapeDtypeStruct((DEPTH, m, n), F32),
        scratch_shapes=[pltpu.VMEM((m, k), BF16)],
        compiler_params=pltpu.CompilerParams(dimension_semantics=("arbitrary", "arbitrary"),
                                             vmem_limit_bytes=limit),
        name="mem_kv",
    )(x, w)


def _norm_small_kernel(x_ref, g_ref, ws_ref, xn_ref, os_ref):
    xn = _rms(x_ref[...], g_ref[...]).astype(BF16)
    xn_ref[...] = xn
    os_ref[...] = jnp.dot(xn, ws_ref[...], preferred_element_type=F32)


def _norm_small(x, gain, w_small, layer, *, tm):
    m, k = x.shape
    return pl.pallas_call(
        _norm_small_kernel,
        grid=(m // tm,),
        in_specs=[pl.BlockSpec((tm, k), lambda i: (i, 0)),
                  pl.BlockSpec((None, 1, k), lambda i: (layer, 0, 0)),
                  pl.BlockSpec((None, k, LANES), lambda i: (layer, 0, 0))],
        out_specs=[pl.BlockSpec((tm, k), lambda i: (i, 0)), pl.BlockSpec((tm, LANES), lambda i: (i, 0))],
        out_shape=[jax.ShapeDtypeStruct((m, k), BF16), jax.ShapeDtypeStruct((m, LANES), F32)],
        compiler_params=pltpu.CompilerParams(dimension_semantics=("parallel",)),
        name="norm_small",
    )(x, gain.reshape(DEPTH, 1, k), w_small)


PROJ_TN = 1024
PROJ_SRC = (0, 1024, 2048, 3072, 7200, 8224, 9248, 10272, 11312, 12336, 13360, 14384, 4096, 5152, 6176)
PROJ_SHIFTS = (0, 32, 48)


def _proj_kernel(tbl_ref, xn_ref, wa_ref, wb_ref, o_ref, wbf_ref, *, k):
    j = pl.program_id(0)
    tn = PROJ_TN

    @pl.when(pl.program_id(1) == 0)
    def _():
        for sid, shift in enumerate(PROJ_SHIFTS):
            @pl.when(tbl_ref[1, j] == sid)
            def _():
                rc = 256
                for r0 in range(0, k, rc):
                    if shift == 0:
                        w = wa_ref[r0:r0 + rc, :]
                    else:
                        win = jnp.concatenate([wa_ref[r0:r0 + rc, :], wb_ref[r0:r0 + rc, :]], axis=1)
                        w = pltpu.roll(win, tn + LANES - shift, axis=1)[:, 0:tn]
                    wbf_ref[r0:r0 + rc, :] = w.astype(BF16)

    o_ref[...] = jnp.dot(xn_ref[...], wbf_ref[...], preferred_element_type=F32).astype(o_ref.dtype)


def _proj_in(xn, w_in, layer, *, tm):
    m, k = xn.shape
    tn = PROJ_TN
    nj = len(PROJ_SRC)
    tbl = jnp.array([[s // tn for s in PROJ_SRC], [PROJ_SHIFTS.index(s % LANES) for s in PROJ_SRC]], jnp.int32)
    limit = _vmem_limit(_nbytes((tm, k), BF16), _nbytes((k, tn + LANES), F32), _nbytes((tm, tn), BF16),
                        scratch=_nbytes((k, tn), BF16))
    grid_spec = pltpu.PrefetchScalarGridSpec(
        num_scalar_prefetch=1,
        grid=(nj, m // tm),
        in_specs=[pl.BlockSpec((tm, k), lambda j, i, t: (i, 0)),
                  pl.BlockSpec((None, k, tn), lambda j, i, t: (layer, 0, t[0, j])),
                  pl.BlockSpec((None, k, LANES), lambda j, i, t: (layer, 0, (t[0, j] + 1) * (tn // LANES)))],
        out_specs=pl.BlockSpec((tm, tn), lambda j, i, t: (i, j)),
        scratch_shapes=[pltpu.VMEM((k, tn), BF16)])
    return pl.pallas_call(
        functools.partial(_proj_kernel, k=k),
        grid_spec=grid_spec,
        out_shape=jax.ShapeDtypeStruct((m, nj * tn), BF16),
        compiler_params=pltpu.CompilerParams(dimension_semantics=("arbitrary", "arbitrary"),
                                             vmem_limit_bytes=limit),
        name="proj_in",
    )(tbl, xn, w_in, w_in)


def _mm_res_kernel(x_ref, w_ref, r_ref, o_ref, wbf_ref):
    @pl.when(pl.program_id(1) == 0)
    def _():
        wbf_ref[...] = w_ref[...].astype(BF16)

    o_ref[...] = r_ref[...] + jnp.dot(x_ref[...], wbf_ref[...], preferred_element_type=F32)


def _mm_res(x, w, layer, res, *, tm, tn):
    m, k = x.shape
    n = w.shape[2]
    limit = _vmem_limit(_nbytes((tm, k), BF16), _nbytes((k, tn), F32), 2 * _nbytes((tm, tn), F32),
                        scratch=_nbytes((k, tn), BF16))
    return pl.pallas_call(
        _mm_res_kernel,
        grid=(n // tn, m // tm),
        in_specs=[pl.BlockSpec((tm, k), lambda j, i: (i, 0)),
                  pl.BlockSpec((None, k, tn), lambda j, i: (layer, 0, j)),
                  pl.BlockSpec((tm, tn), lambda j, i: (i, j))],
        out_specs=pl.BlockSpec((tm, tn), lambda j, i: (i, j)),
        out_shape=jax.ShapeDtypeStruct((m, n), F32),
        scratch_shapes=[pltpu.VMEM((k, tn), BF16)],
        compiler_params=pltpu.CompilerParams(dimension_semantics=("parallel", "arbitrary"),
                                             vmem_limit_bytes=limit),
        name="mm_res",
    )(x, w, res)


def _merge_kernel(y0_ref, y1_ref, g0_ref, g1_ref, w_ref, o_ref, wbf_ref):
    @pl.when(pl.program_id(1) == 0)
    def _():
        wbf_ref[...] = w_ref[...].astype(BF16)

    b0 = jnp.dot(y0_ref[...], wbf_ref[0], preferred_element_type=F32)
    b1 = jnp.dot(y1_ref[...], wbf_ref[1], preferred_element_type=F32)
    g0 = jax.nn.sigmoid(g0_ref[...].astype(F32))
    g1 = jax.nn.sigmoid(g1_ref[...].astype(F32))
    o_ref[...] = (g0 * b0 + g1 * b1).astype(o_ref.dtype)


def _merge(y_ssd, y_gla, proj, w_branch, layer, *, tm, tn):
    m, k = y_ssd.shape
    n = D_MODEL
    gb = P_GATE // tn
    nb = n // tn
    limit = _vmem_limit(2 * _nbytes((tm, k), BF16), 2 * _nbytes((k, tn), F32), 3 * _nbytes((tm, tn), BF16),
                        scratch=2 * _nbytes((k, tn), BF16))
    return pl.pallas_call(
        _merge_kernel,
        grid=(nb, m // tm),
        in_specs=[pl.BlockSpec((tm, k), lambda j, i: (i, 0)),
                  pl.BlockSpec((tm, k), lambda j, i: (i, 0)),
                  pl.BlockSpec((tm, tn), lambda j, i: (i, gb + j)),
                  pl.BlockSpec((tm, tn), lambda j, i: (i, gb + nb + j)),
                  pl.BlockSpec((None, 2, k, tn), lambda j, i: (layer, 0, 0, j))],
        out_specs=pl.BlockSpec((tm, tn), lambda j, i: (i, j)),
        out_shape=jax.ShapeDtypeStruct((m, n), BF16),
        scratch_shapes=[pltpu.VMEM((2, k, tn), BF16)],
        compiler_params=pltpu.CompilerParams(dimension_semantics=("parallel", "arbitrary"),
                                             vmem_limit_bytes=limit),
        name="merge",
    )(y_ssd, y_gla, proj, proj, w_branch)


def _final_norm_kernel(x_ref, g_ref, o_ref):
    o_ref[...] = _rms(x_ref[...], g_ref[...])


def _final_norm(x, gain, *, tm):
    m, k = x.shape
    return pl.pallas_call(
        _final_norm_kernel,
        grid=(m // tm,),
        in_specs=[pl.BlockSpec((tm, k), lambda i: (i, 0)), pl.BlockSpec((1, k), lambda i: (0, 0))],
        out_specs=pl.BlockSpec((tm, k), lambda i: (i, 0)),
        out_shape=jax.ShapeDtypeStruct((m, k), F32),
        compiler_params=pltpu.CompilerParams(dimension_semantics=("parallel",)),
        name="final_norm",
    )(x, gain.reshape(1, k))


def _ffn_up_kernel(*refs, tm, tn, nj, seq_tiles, seg):
    short = seg > 0
    it = iter(refs)
    x_ref, g_ref, wu_ref, wt_ref, cwu_ref, cwt_ref, cbu_ref, cbt_ref = (next(it) for _ in range(8))
    stu_ref = next(it) if short else None
    stt_ref = next(it) if short else None
    act_ref, cnu_ref, cnt_ref = next(it), next(it), next(it)
    xn_ref, wbu_ref, wbt_ref = next(it), next(it), next(it)
    bufs = [(next(it), next(it)) for _ in range(2)]
    tailu_ref = None if short else next(it)
    tailt_ref = None if short else next(it)
    i = pl.program_id(0)
    j = pl.program_id(1)
    sl = SUBLANES
    hw = FFN_CONV - 1

    def step(buf_w, buf_r, jj):
        rc = min(tm, 256)
        re = 16
        if buf_w is not None:
            wbu_ref[...] = wu_ref[...].astype(BF16)
            wbt_ref[...] = wt_ref[...].astype(BF16)
        if buf_r is not None:
            us_ref, ts_ref = buf_r
            if short:
                us_ref[0:sl, :] = jnp.zeros((sl, tn), F32)
                ts_ref[0:sl, :] = jnp.zeros((sl, tn), F32)
                nseq = tm // seg
                r = lax.broadcasted_iota(jnp.int32, (tm, hw * nseq), 0)
                c = lax.broadcasted_iota(jnp.int32, (tm, hw * nseq), 1)
                halo_u, halo_t = [], []
                for s in range(1, FFN_CONV):
                    sel = (((r % seg) < s) & (c == hw * (r // seg) + hw - s + (r % seg))).astype(BF16)
                    halo_u.append(_select_rows(sel, stu_ref[...]))
                    halo_t.append(_select_rows(sel, stt_ref[...]))
                ro = lax.broadcasted_iota(jnp.int32, (hw * nseq, tm), 0)
                co = lax.broadcasted_iota(jnp.int32, (hw * nseq, tm), 1)
                sel_out = (co == (ro // hw) * seg + seg - hw + (ro % hw)).astype(BF16)
                cnu_ref[...] = _select_rows(sel_out, us_ref[sl:sl + tm, :])
                cnt_ref[...] = _select_rows(sel_out, ts_ref[sl:sl + tm, :])
            else:
                start = (i % seq_tiles) == 0
                us_ref[0:sl, :] = jnp.where(start, 0.0, tailu_ref[jj])
                ts_ref[0:sl, :] = jnp.where(start, 0.0, tailt_ref[jj])
                tailu_ref[jj] = us_ref[tm:tm + sl, :]
                tailt_ref[jj] = ts_ref[tm:tm + sl, :]
                cnu_ref[...] = us_ref[tm:tm + sl, :]
                cnt_ref[...] = ts_ref[tm:tm + sl, :]

        for r0 in range(0, tm, rc):
            if buf_w is not None:
                xn = xn_ref[r0:r0 + rc, :]
                buf_w[0][sl + r0:sl + r0 + rc, :] = jnp.dot(xn, wbu_ref[...], preferred_element_type=F32)
                buf_w[1][sl + r0:sl + r0 + rc, :] = jnp.dot(xn, wbt_ref[...], preferred_element_type=F32)
            for e0 in range(r0, r0 + rc, re) if buf_r is not None else ():
                rowmod = (lax.broadcasted_iota(jnp.int32, (re, tn), 0) + e0) % seg if short else None

                def conv(s_ref, cw_ref, cb_ref, halos):
                    y = s_ref[sl + e0:sl + e0 + re, :] * cw_ref[hw:hw + 1, :]
                    for s in range(1, FFN_CONV):
                        prev = s_ref[sl + e0 - s:sl + e0 - s + re, :]
                        if short:
                            prev = jnp.where(rowmod < s, halos[s - 1][e0:e0 + re, :], prev)
                        y = y + prev * cw_ref[hw - s:hw - s + 1, :]
                    return y + cb_ref[...]

                u = conv(us_ref, cwu_ref, cbu_ref, halo_u if short else None)
                t = conv(ts_ref, cwt_ref, cbt_ref, halo_t if short else None)
                act_ref[e0:e0 + re, :] = (_silu(t) * u).astype(act_ref.dtype)

    @pl.when(j == 0)
    def _():
        if not short:
            @pl.when(i == 0)
            def _():
                tailu_ref[...] = jnp.zeros_like(tailu_ref)
                tailt_ref[...] = jnp.zeros_like(tailt_ref)
        xn_ref[...] = _rms(x_ref[...], g_ref[...]).astype(BF16)
        step(bufs[0], None, None)

    for par in range(2):
        @pl.when((j > 0) & (j < nj) & (j % 2 == par))
        def _():
            step(bufs[par], bufs[1 - par], j - 1)

    @pl.when(j == nj)
    def _():
        step(None, bufs[(nj - 1) % 2], j - 1)


def _ffn_up(x, gain, w, conv_w, conv_b, conv0, layer, *, L, tm, tn):
    m, k = x.shape
    bt = m // L
    nj = D_FF // tn
    hw = FFN_CONV - 1
    short = L < tm
    seg = L if short else 0
    seq_tiles = 1 if short else L // tm
    assert (conv0 is not None) == short, "history rows are only supported for sequences shorter than a block"
    jw = lambda j: jnp.minimum(j, nj - 1)
    je = lambda j: jnp.maximum(j - 1, 0)
    in_specs = [
        pl.BlockSpec((tm, k), lambda i, j: (i, 0)),
        pl.BlockSpec((None, 1, k), lambda i, j: (layer, 0, 0)),
        pl.BlockSpec((None, k, tn), lambda i, j: (layer, 0, jw(j))),
        pl.BlockSpec((None, k, tn), lambda i, j: (layer, 0, nj + jw(j))),
        pl.BlockSpec((None, FFN_CONV, tn), lambda i, j: (layer, 0, je(j))),
        pl.BlockSpec((None, FFN_CONV, tn), lambda i, j: (layer, 0, nj + je(j))),
        pl.BlockSpec((None, 1, tn), lambda i, j: (layer, 0, je(j))),
        pl.BlockSpec((None, 1, tn), lambda i, j: (layer, 0, nj + je(j))),
    ]
    args = [x, gain.reshape(DEPTH, 1, k), w, w, conv_w, conv_w, conv_b.reshape(DEPTH, 1, 2 * D_FF),
            conv_b.reshape(DEPTH, 1, 2 * D_FF)]
    scratch = ([pltpu.VMEM((tm, k), BF16)] + [pltpu.VMEM((k, tn), BF16)] * 2
               + [pltpu.VMEM((tm + SUBLANES, tn), F32)] * 4)
    if short:
        nst = (m // tm) * (tm // L) * hw
        st2d = conv0.reshape(DEPTH, nst, 2 * D_FF)
        rows = (tm // L) * hw
        in_specs += [pl.BlockSpec((None, rows, tn), lambda i, j: (layer, i, je(j))),
                     pl.BlockSpec((None, rows, tn), lambda i, j: (layer, i, nj + je(j)))]
        args += [st2d, st2d]
        cn_shape = jax.ShapeDtypeStruct((nst, D_FF), F32)
        cn_spec = pl.BlockSpec((rows, tn), lambda i, j: (i, je(j)))
    else:
        scratch += [pltpu.VMEM((nj, SUBLANES, tn), F32), pltpu.VMEM((nj, SUBLANES, tn), F32)]
        cn_shape = jax.ShapeDtypeStruct((m // tm, SUBLANES, D_FF), F32)
        cn_spec = pl.BlockSpec((None, SUBLANES, tn), lambda i, j: (i, 0, je(j)))
    limit = _vmem_limit(_nbytes((tm, k), F32), 2 * _nbytes((k, tn), F32), _nbytes((tm, tn), BF16),
                        scratch=(_nbytes((tm, k), BF16) + 2 * _nbytes((k, tn), BF16)
                                 + 4 * _nbytes((tm + SUBLANES, tn), F32)))
    act, cnu, cnt = pl.pallas_call(
        functools.partial(_ffn_up_kernel, tm=tm, tn=tn, nj=nj, seq_tiles=seq_tiles, seg=seg),
        grid=(m // tm, nj + 1),
        in_specs=in_specs,
        out_specs=[pl.BlockSpec((tm, tn), lambda i, j: (i, je(j))), cn_spec, cn_spec],
        out_shape=[jax.ShapeDtypeStruct((m, D_FF), BF16), cn_shape, cn_shape],
        scratch_shapes=scratch,
        compiler_params=pltpu.CompilerParams(dimension_semantics=("arbitrary", "arbitrary"),
                                             vmem_limit_bytes=limit),
        name="ffn_up",
    )(*args)
    if short:
        conv_new = jnp.concatenate([cnu, cnt], axis=-1).reshape(bt, hw, 2 * D_FF)
    else:
        last = slice(seq_tiles - 1, None, seq_tiles)
        conv_new = jnp.concatenate([cnu[last, SUBLANES - hw:, :], cnt[last, SUBLANES - hw:, :]], axis=-1)
    return act, conv_new


def _causal_conv(x, tail_ref, w_ref, b, width):
    row = lax.broadcasted_iota(jnp.int32, x.shape, 0)
    y = x * w_ref[width - 1:width, :]
    for s in range(1, width):
        shifted = pltpu.roll(x, s, axis=0)
        for t in range(s):
            shifted = jnp.where(row == t, tail_ref[8 + t - s:9 + t - s, :], shifted)
        y = y + shifted * w_ref[width - 1 - s:width - s, :]
    return y + b


def _ssd_kernel(*refs, Q, zero_init, aliased):
    it = iter(refs)
    z_ref, xs_ref, bc_ref, sm_ref = (next(it) for _ in range(4))
    conv0_ref = None if zero_init else next(it)
    h0_ref = None if zero_init else next(it)
    cwx_ref, cwb_ref, cbx_ref, cbb_ref, dtb_ref, a_ref, dx_ref, nrm_ref, exp_ref = (next(it) for _ in range(9))
    if aliased:
        next(it), next(it)
    y_ref, h_out, conv_out, tailx_ref, tailb_ref = (next(it) for _ in range(5))
    h_ref = h_out if aliased else h_out.at[0]
    convout_ref = conv_out if aliased else conv_out.at[0]
    c = pl.program_id(1)
    hp = SSD_HEADDIM
    gw = SSD_HEADS // SSD_GROUPS * hp

    @pl.when(c == 0)
    def _():
        if not aliased:
            h_out[1:] = jnp.zeros((DEPTH - 1,) + h_ref.shape, F32)
            conv_out[1:] = jnp.zeros((DEPTH - 1,) + convout_ref.shape, F32)
        tailx_ref[...] = jnp.zeros_like(tailx_ref)
        tailb_ref[...] = jnp.zeros_like(tailb_ref)
        if zero_init:
            h_ref[...] = jnp.zeros_like(h_ref)
        else:
            h_ref[...] = h0_ref[...]
            tailx_ref[5:8, :] = conv0_ref[:, 0:D_MODEL]
            tailb_ref[5:8, :] = conv0_ref[:, D_MODEL:SSD_CONV_DIM]

    xs_raw = xs_ref[...].astype(F32)
    bc_raw = bc_ref[...].astype(F32)
    xs = _silu(_causal_conv(xs_raw, tailx_ref, cwx_ref, cbx_ref[...], SSD_CONV))
    bc = _silu(_causal_conv(bc_raw, tailb_ref, cwb_ref, cbb_ref[...], SSD_CONV))
    tailx_ref[...] = xs_raw[Q - 8:Q, :]
    tailb_ref[...] = bc_raw[Q - 8:Q, :]

    @pl.when(c == pl.num_programs(1) - 1)
    def _():
        convout_ref[:, 0:D_MODEL] = tailx_ref[5:8, :]
        convout_ref[:, D_MODEL:SSD_CONV_DIM] = tailb_ref[5:8, :]

    dt = _softplus(sm_ref[...] + dtb_ref[...])
    adt = dt * a_ref[...]
    ri = lax.broadcasted_iota(jnp.int32, (Q, Q), 0)
    ci = lax.broadcasted_iota(jnp.int32, (Q, Q), 1)
    causal = ci <= ri
    acum = jnp.dot(causal.astype(F32), adt, precision=HI, preferred_element_type=F32)
    eye = (lax.broadcasted_iota(jnp.int32, (LANES, LANES), 0)
           == lax.broadcasted_iota(jnp.int32, (LANES, LANES), 1)).astype(F32)
    acum_t = lax.dot_general(eye, acum, NT_DIMS, precision=HI, preferred_element_type=F32)
    dt_t = lax.dot_general(eye, dt, NT_DIMS, precision=HI, preferred_element_type=F32)
    a_last = acum[Q - 1:Q, :]
    ea = jnp.exp(acum)
    te = jnp.exp(a_last - acum) * dt
    dec_rows = jnp.broadcast_to(jnp.exp(acum_t[:, Q - 1:Q]), (LANES, LANES))
    ea_hi, ea_lo = _split2(ea)
    te_hi, te_lo = _split2(te)
    expand = exp_ref[...]
    ea_x = (jnp.dot(ea_hi, expand, preferred_element_type=F32)
            + jnp.dot(ea_lo, expand, preferred_element_type=F32))
    te_x = (jnp.dot(te_hi, expand, preferred_element_type=F32)
            + jnp.dot(te_lo, expand, preferred_element_type=F32))
    lane = lax.broadcasted_iota(jnp.int32, (Q, LANES), 1)

    for g in range(SSD_GROUPS):
        gl = slice(g * gw, (g + 1) * gw)
        b_g = bc[:, g * SSD_STATE:(g + 1) * SSD_STATE].astype(BF16)
        c_g = bc[:, (SSD_GROUPS + g) * SSD_STATE:(SSD_GROUPS + g + 1) * SSD_STATE].astype(BF16)
        cb = lax.dot_general(c_g, b_g, NT_DIMS, preferred_element_type=F32)
        h_g = h_ref[8 * g:8 * g + 8].reshape(gw, SSD_STATE)
        y_off = lax.dot_general(c_g, h_g.astype(BF16), NT_DIMS, preferred_element_type=F32)
        x_g = xs[:, gl]
        pairs = []
        for p in range(4):
            x_p = x_g[:, p * LANES:(p + 1) * LANES]
            acc = None
            for s in range(2):
                hh = g * 8 + p * 2 + s
                seg = jnp.broadcast_to(acum[:, hh:hh + 1], (Q, Q)) - jnp.broadcast_to(acum_t[hh:hh + 1, :], (Q, Q))
                decay = jnp.where(causal, jnp.exp(seg), 0.0)
                w_h = (cb * decay * dt_t[hh:hh + 1, :]).astype(BF16)
                x_m = jnp.where((lane // hp) == s, x_p, 0.0).astype(BF16)
                r = jnp.dot(w_h, x_m, preferred_element_type=F32)
                acc = r if acc is None else acc + r
            pairs.append(acc)
        y_g = jnp.concatenate(pairs, axis=1) + y_off * ea_x[:, gl] + dx_ref[:, gl] * x_g
        x_t = (x_g * te_x[:, gl]).astype(BF16)
        upd = lax.dot_general(x_t, b_g, TN_DIMS, preferred_element_type=F32)
        for h in range(8):
            hh = g * 8 + h
            h_ref[hh] = h_ref[hh] * dec_rows[hh:hh + 1, :] + upd[h * hp:(h + 1) * hp, :]
        z_g = z_ref[:, gl].astype(F32)
        y_ref[:, gl] = _rms(y_g * _silu(z_g), nrm_ref[:, gl]).astype(y_ref.dtype)


def _ssd(proj, small, conv0, h0, wl, layer, prev_h, prev_conv, *, bt, L, Q):
    zero_init = h0 is None
    aliased = prev_h is not None
    nc = L // Q
    T = bt * L
    full2 = lambda b, c: (0, 0)
    lay3 = lambda b, c: (layer, 0, 0)
    row = lambda col: (lambda b, c: (b * nc + c, col))
    hshape = (DEPTH, bt, SSD_HEADS, SSD_HEADDIM, SSD_STATE)
    cshape = (DEPTH, bt, SSD_CONV - 1, SSD_CONV_DIM)
    hspec = pl.BlockSpec((None, None, SSD_HEADS, SSD_HEADDIM, SSD_STATE), lambda b, c: (layer, b, 0, 0, 0))
    cspec = pl.BlockSpec((None, None, SSD_CONV - 1, SSD_CONV_DIM), lambda b, c: (layer, b, 0, 0))
    h_out_spec = hspec if aliased else pl.BlockSpec((DEPTH, None, SSD_HEADS, SSD_HEADDIM, SSD_STATE),
                                                    lambda b, c: (0, b, 0, 0, 0))
    c_out_spec = cspec if aliased else pl.BlockSpec((DEPTH, None, SSD_CONV - 1, SSD_CONV_DIM),
                                                    lambda b, c: (0, b, 0, 0))
    in_specs = [
        pl.BlockSpec((Q, D_MODEL), row(P_Z // D_MODEL)),
        pl.BlockSpec((Q, D_MODEL), row(P_XS // D_MODEL)),
        pl.BlockSpec((Q, 1024), row(P_BC // 1024)),
        pl.BlockSpec((Q, LANES), row(0)),
    ]
    args = [proj, proj, proj, small]
    if not zero_init:
        in_specs += [cspec, hspec]
        args += [conv0, h0]
    in_specs += [
        pl.BlockSpec((None, SSD_CONV, D_MODEL), lay3),
        pl.BlockSpec((None, SSD_CONV, 1024), lambda b, c: (layer, 0, D_MODEL // 1024)),
        pl.BlockSpec((None, 1, D_MODEL), lay3),
        pl.BlockSpec((None, 1, 1024), lambda b, c: (layer, 0, D_MODEL // 1024)),
        pl.BlockSpec((None, 1, LANES), lay3),
        pl.BlockSpec((None, 1, LANES), lay3),
        pl.BlockSpec((None, 1, D_MODEL), lay3),
        pl.BlockSpec((None, 1, D_MODEL), lay3),
        pl.BlockSpec((LANES, D_MODEL), full2),
    ]
    args += [wl["ssd_conv_w"], wl["ssd_conv_w"], wl["ssd_conv_b"], wl["ssd_conv_b"], wl["ssd_dtb"], wl["ssd_a"],
             wl["ssd_dx"], wl["ssd_norm"], wl["expand"]]
    aliases = {}
    if aliased:
        aliases = {len(args): 1, len(args) + 1: 2}
        in_specs += [pl.BlockSpec(memory_space=pl.ANY), pl.BlockSpec(memory_space=pl.ANY)]
        args += [prev_h, prev_conv]
    return pl.pallas_call(
        functools.partial(_ssd_kernel, Q=Q, zero_init=zero_init, aliased=aliased),
        grid=(bt, nc),
        in_specs=in_specs,
        out_specs=[pl.BlockSpec((Q, D_MODEL), row(0)), h_out_spec, c_out_spec],
        out_shape=[jax.ShapeDtypeStruct((T, D_MODEL), BF16), _stacked_out(prev_h, hshape, F32),
                   _stacked_out(prev_conv, cshape, F32)],
        scratch_shapes=[pltpu.VMEM((8, D_MODEL), F32), pltpu.VMEM((8, 1024), F32)],
        input_output_aliases=aliases,
        compiler_params=pltpu.CompilerParams(dimension_semantics=("parallel", "arbitrary"),
                                             vmem_limit_bytes=48 << 20),
        name="ssd",
    )(*args)


def _gla_kernel(*refs, TB, zero_init, aliased):
    it = iter(refs)
    q_ref, k_ref, v_ref, g_ref, sm_ref = (next(it) for _ in range(5))
    s0_ref = None if zero_init else next(it)
    wa_ref, ba_ref, wat_ref, bat_ref, gn_ref = (next(it) for _ in range(5))
    if aliased:
        next(it)
    y_ref, s_out = next(it), next(it)
    s_ref = s_out if aliased else s_out.at[0]
    c = pl.program_id(1)
    ck = GLA_CHUNK
    ns = TB // ck
    hk, hv = GLA_HEAD_K, GLA_HEAD_V

    @pl.when(c == 0)
    def _():
        if not aliased:
            s_out[1:] = jnp.zeros((DEPTH - 1,) + s_ref.shape, F32)
        if zero_init:
            s_ref[...] = jnp.zeros_like(s_ref)
        else:
            s_ref[...] = s0_ref[...]

    smb = sm_ref[...].astype(BF16)
    la = _log_sigmoid(jnp.dot(smb, wa_ref[...], preferred_element_type=F32) + ba_ref[...]) * (1.0 / GLA_GATE_NORM)
    rb = lax.broadcasted_iota(jnp.int32, (TB, TB), 0)
    cb = lax.broadcasted_iota(jnp.int32, (TB, TB), 1)
    same = (rb // ck) == (cb // ck)
    msel = jnp.concatenate([same & (cb <= rb), same, (cb // ck) < (rb // ck)], axis=0).astype(BF16)
    sums = _select_rows(msel, la)
    bcum, tot, bprev = sums[0:TB], sums[TB:2 * TB], sums[2 * TB:3 * TB]
    kf = k_ref[...].astype(F32)
    qd = q_ref[...].astype(F32) * (hk ** -0.5) * jnp.exp(bcum)
    qd_b = qd.astype(BF16)
    qs_b = (qd * jnp.exp(bprev)).astype(BF16)
    ki_b = (kf * jnp.exp(-bcum)).astype(BF16)
    ke_b = (kf * jnp.exp(tot - bcum)).astype(BF16)
    la_t = _log_sigmoid(lax.dot_general(wat_ref[...], smb, NT_DIMS, preferred_element_type=F32)
                        + bat_ref[:, 0:1]) * (1.0 / GLA_GATE_NORM)
    rs = lax.broadcasted_iota(jnp.int32, (TB, LANES), 0)
    cs = lax.broadcasted_iota(jnp.int32, (TB, LANES), 1)
    selc = (((rs // ck) == cs) | (cs == ns)).astype(BF16)
    la_hi = la_t.astype(BF16)
    la_r = la_t - la_hi.astype(F32)
    la_mid = la_r.astype(BF16)
    la_lo = (la_r - la_mid.astype(F32)).astype(BF16)
    dec_t = jnp.exp(jnp.dot(la_hi, selc, preferred_element_type=F32) + jnp.dot(la_mid, selc, preferred_element_type=F32)
                    + jnp.dot(la_lo, selc, preferred_element_type=F32))
    eye = (lax.broadcasted_iota(jnp.int32, (hk, hk), 0) == lax.broadcasted_iota(jnp.int32, (hk, hk), 1)).astype(BF16)
    slot = lax.broadcasted_iota(jnp.int32, (hk, TB), 1) // ck
    row16 = lax.broadcasted_iota(jnp.int32, (ck, TB), 0)
    col16 = lax.broadcasted_iota(jnp.int32, (ck, TB), 1)

    for h in range(GLA_HEADS):
        kl = slice(h * hk, (h + 1) * hk)
        vl = slice(h * hv, (h + 1) * hv)
        ke_t = lax.dot_general(eye, ke_b[:, kl], NT_DIMS, preferred_element_type=F32)
        ki_t = lax.dot_general(eye, ki_b[:, kl], NT_DIMS, preferred_element_type=F32)
        v_h = v_ref[:, vl]
        kbuf = jnp.zeros((hk, TB), F32)
        att = []
        for cc in range(ns):
            in_c = slot == cc
            kall = jnp.where(in_c, ki_t, kbuf).astype(BF16)
            a = jnp.dot(qd_b[cc * ck:(cc + 1) * ck, kl], kall, preferred_element_type=F32)
            att.append(jnp.where(col16 <= row16 + cc * ck, a, 0.0))
            kbuf = jnp.where(in_c, ke_t, kbuf * dec_t[kl, cc:cc + 1])
        a_full = jnp.concatenate(att, axis=0).astype(BF16)
        s_old = s_ref[h]
        o = (jnp.dot(a_full, v_h, preferred_element_type=F32)
             + jnp.dot(qs_b[:, kl], s_old.astype(BF16), preferred_element_type=F32))
        y_ref[:, vl] = (_rms(o, gn_ref[...]) * _silu(g_ref[:, vl].astype(F32))).astype(y_ref.dtype)
        s_ref[h] = s_old * dec_t[kl, ns:ns + 1] + jnp.dot(kbuf.astype(BF16), v_h, preferred_element_type=F32)


def _gla(proj, small, s0, wl, layer, prev_s, *, bt, L, TB):
    zero_init = s0 is None
    aliased = prev_s is not None
    nc = L // TB
    T = bt * L
    lay3 = lambda b, c: (layer, 0, 0)
    row = lambda col: (lambda b, c: (b * nc + c, col))
    kd = GLA_HEADS * GLA_HEAD_K
    sshape = (DEPTH, bt, GLA_HEADS, GLA_HEAD_K, GLA_HEAD_V)
    sspec = pl.BlockSpec((None, None, GLA_HEADS, GLA_HEAD_K, GLA_HEAD_V), lambda b, c: (layer, b, 0, 0, 0))
    s_out_spec = sspec if aliased else pl.BlockSpec((DEPTH, None, GLA_HEADS, GLA_HEAD_K, GLA_HEAD_V),
                                                     lambda b, c: (0, b, 0, 0, 0))
    in_specs = [
        pl.BlockSpec((TB, kd), row(P_Q // kd)),
        pl.BlockSpec((TB, kd), row(P_K // kd)),
        pl.BlockSpec((TB, D_MODEL), row(P_V // D_MODEL)),
        pl.BlockSpec((TB, D_MODEL), row(P_G // D_MODEL)),
        pl.BlockSpec((TB, LANES), row(0)),
    ]
    args = [proj, proj, proj, proj, small]
    if not zero_init:
        in_specs.append(sspec)
        args.append(s0)
    in_specs += [
        pl.BlockSpec((None, LANES, kd), lay3),
        pl.BlockSpec((None, 1, kd), lay3),
        pl.BlockSpec((None, kd, LANES), lay3),
        pl.BlockSpec((None, kd, LANES), lay3),
        pl.BlockSpec((None, 1, GLA_HEAD_V), lay3),
    ]
    args += [wl["gla_wa"], wl["gla_ba"], wl["gla_wa_t"], wl["gla_ba_t"], wl["gla_norm"]]
    aliases = {}
    if aliased:
        aliases = {len(args): 1}
        in_specs.append(pl.BlockSpec(memory_space=pl.ANY))
        args.append(prev_s)
    return pl.pallas_call(
        functools.partial(_gla_kernel, TB=TB, zero_init=zero_init, aliased=aliased),
        grid=(bt, nc),
        in_specs=in_specs,
        out_specs=[pl.BlockSpec((TB, D_MODEL), row(0)), s_out_spec],
        out_shape=[jax.ShapeDtypeStruct((T, D_MODEL), BF16), _stacked_out(prev_s, sshape, F32)],
        input_output_aliases=aliases,
        compiler_params=pltpu.CompilerParams(dimension_semantics=("parallel", "arbitrary"),
                                             vmem_limit_bytes=48 << 20),
        name="gla",
    )(*args)


def _attn_kernel(q_ref, k_ref, v_ref, o_ref, kb_ref, vb_ref):
    @pl.when(pl.program_id(1) == 0)
    def _():
        kb_ref[...] = k_ref[...].astype(BF16)
        vb_ref[...] = v_ref[...].astype(BF16)

    hd = MEM_HEAD_DIM
    for h in range(MEM_HEADS):
        hl = slice(h * hd, (h + 1) * hd)
        s = lax.dot_general(q_ref[:, hl], kb_ref[:, hl], NT_DIMS, preferred_element_type=F32) * (hd ** -0.5)
        e = jnp.exp(s - jnp.max(s, axis=-1, keepdims=True))
        p = e / jnp.sum(e, axis=-1, keepdims=True)
        o_ref[:, hl] = jnp.dot(p.astype(BF16), vb_ref[:, hl], preferred_element_type=F32).astype(o_ref.dtype)


def _attn(q, mem_k, mem_v, layer, *, bt, L, tl):
    nl = L // tl
    T = bt * L
    kvspec = pl.BlockSpec((None, None, MEM_TOKENS, D_MODEL), lambda b, l: (layer, b, 0, 0))
    return pl.pallas_call(
        _attn_kernel,
        grid=(bt, nl),
        in_specs=[pl.BlockSpec((tl, D_MODEL), lambda b, l: (b * nl + l, 0)), kvspec, kvspec],
        out_specs=pl.BlockSpec((tl, D_MODEL), lambda b, l: (b * nl + l, 0)),
        out_shape=jax.ShapeDtypeStruct((T, D_MODEL), BF16),
        scratch_shapes=[pltpu.VMEM((MEM_TOKENS, D_MODEL), BF16), pltpu.VMEM((MEM_TOKENS, D_MODEL), BF16)],
        compiler_params=pltpu.CompilerParams(dimension_semantics=("parallel", "arbitrary"),
                                             vmem_limit_bytes=40 << 20),
        name="mem_attn",
    )(q, mem_k, mem_v)


def _prep_weights(w_in, ssd_conv_w, ssd_conv_b, ssd_dt_bias, ssd_a_log, ssd_d, ssd_norm, gla_wa2, gla_ba, gla_norm):
    o_dt, o_alr = 5120, 11296
    pad_sm = LANES - SSD_HEADS - GLA_RANK
    w_small = jnp.concatenate(
        [w_in[:, :, o_dt:o_dt + SSD_HEADS], w_in[:, :, o_alr:o_alr + GLA_RANK],
         jnp.zeros((DEPTH, D_MODEL, pad_sm), F32)], axis=-1).astype(BF16)
    pad_h = LANES - SSD_HEADS
    kd = GLA_HEADS * GLA_HEAD_K
    wa = jnp.concatenate([jnp.zeros((DEPTH, SM_ALR, kd), F32), gla_wa2,
                          jnp.zeros((DEPTH, LANES - SM_ALR - GLA_RANK, kd), F32)], axis=1).astype(BF16)
    expand = (jnp.arange(D_MODEL)[None, :] // SSD_HEADDIM == jnp.arange(LANES)[:, None]).astype(BF16)
    return dict(
        w_in=w_in, w_small=w_small,
        ssd_conv_w=ssd_conv_w, ssd_conv_b=ssd_conv_b[:, None, :],
        ssd_dtb=jnp.pad(ssd_dt_bias, ((0, 0), (0, pad_h)))[:, None, :],
        ssd_a=jnp.pad(-jnp.exp(ssd_a_log), ((0, 0), (0, pad_h)))[:, None, :],
        ssd_dx=jnp.repeat(ssd_d, SSD_HEADDIM, axis=1)[:, None, :],
        ssd_norm=ssd_norm[:, None, :], expand=expand,
        gla_wa=wa, gla_ba=gla_ba[:, None, :], gla_wa_t=jnp.swapaxes(wa, 1, 2),
        gla_ba_t=jnp.broadcast_to(gla_ba[:, :, None], (DEPTH, kd, LANES)),
        gla_norm=gla_norm[:, None, :])


def _run_trunk(x, mem_k, mem_v, st_ssd, st_ssd_conv, st_gla, st_ffn_conv, wl, big, norm_final, *, bt, L, cfg):
    tm = cfg["tm"]
    n_h = n_c = n_s = None
    new_ffn = []
    for i in range(DEPTH):
        xn, small = _norm_small(x, big["norm_mix"], wl["w_small"], i, tm=cfg["tm_norm"])
        proj = _proj_in(xn, wl["w_in"], i, tm=tm)
        y_ssd, n_h, n_c = _ssd(proj, small, st_ssd_conv, st_ssd, wl, i, n_h, n_c, bt=bt, L=L, Q=cfg["ssd_q"])
        y_gla, n_s = _gla(proj, small, st_gla, wl, i, n_s, bt=bt, L=L, TB=cfg["gla_tb"])
        merged = _merge(y_ssd, y_gla, proj, big["w_branch"], i, tm=tm, tn=512)
        x = _mm_res(merged, big["w_out"], i, x, tm=tm, tn=1024)
        q = _norm_mm(x, big["norm_mem"], big["w_mq"], i, None, tm=tm, tn=512, out_dtype=BF16)
        o = _attn(q, mem_k, mem_v, i, bt=bt, L=L, tl=cfg["attn_tl"])
        x = _mm_res(o, big["w_mo"], i, x, tm=tm, tn=1024)
        act, f_c = _ffn_up(x, big["norm_ffn"], big["w_ffn_in"], big["ffn_conv_w"], big["ffn_conv_b"], st_ffn_conv, i,
                           L=L, tm=tm, tn=512)
        x = _mm_res(act, big["w_ffn_out"], i, x, tm=cfg["tm_norm"], tn=512)
        new_ffn.append(f_c)
    y = _final_norm(x, norm_final, tm=cfg["tm_norm"])
    return y, n_h, n_c, n_s, jnp.stack(new_ffn)


def _group_cfg(bt, L):
    T = bt * L
    return dict(tm=min(T, 1024), tm_norm=min(T, 512), ssd_q=min(L, 128), gla_tb=min(L, 128), attn_tl=min(L, 512))


def kernel(x_prompt, x_sample, mem_prompt, state_ssd, state_ssd_conv, state_gla, state_ffn_conv, cache_mem_k, cache_mem_v, norm_mix, w_in, ssd_conv_w, ssd_conv_b, ssd_dt_bias, ssd_a_log, ssd_d, ssd_norm, gla_wa2, gla_ba, gla_norm, w_branch, w_out, norm_mem, w_mq, w_mk, w_mv, w_mo, norm_ffn, w_ffn_in, ffn_conv_w, ffn_conv_b, w_ffn_out, norm_final):
    wl = _prep_weights(w_in, ssd_conv_w, ssd_conv_b, ssd_dt_bias, ssd_a_log, ssd_d, ssd_norm, gla_wa2, gla_ba, gla_norm)
    big = dict(norm_mix=norm_mix, w_branch=w_branch, w_out=w_out, norm_mem=norm_mem, w_mq=w_mq, w_mo=w_mo,
               norm_ffn=norm_ffn, w_ffn_in=w_ffn_in, ffn_conv_w=ffn_conv_w, ffn_conv_b=ffn_conv_b, w_ffn_out=w_ffn_out)
    pb, pl_len, _ = x_prompt.shape
    sb, sl_len, _ = x_sample.shape
    mem2d = mem_prompt.reshape(pb * MEM_TOKENS, D_MODEL)
    p_mem_k = _mem_kv(mem2d, w_mk, tn=512).reshape(DEPTH, pb, MEM_TOKENS, D_MODEL)
    p_mem_v = _mem_kv(mem2d, w_mv, tn=512).reshape(DEPTH, pb, MEM_TOKENS, D_MODEL)

    y_p, p_ssd, p_ssd_conv, p_gla, p_ffn = _run_trunk(
        x_prompt.reshape(pb * pl_len, D_MODEL), p_mem_k, p_mem_v, None, None, None, None, wl, big, norm_final,
        bt=pb, L=pl_len, cfg=_group_cfg(pb, pl_len))
    y_s, s_ssd, s_ssd_conv, s_gla, s_ffn = _run_trunk(
        x_sample.reshape(sb * sl_len, D_MODEL),
        cache_mem_k.reshape(DEPTH, sb, MEM_TOKENS, D_MODEL), cache_mem_v.reshape(DEPTH, sb, MEM_TOKENS, D_MODEL),
        state_ssd, state_ssd_conv, state_gla, state_ffn_conv, wl, big, norm_final,
        bt=sb, L=sl_len, cfg=_group_cfg(sb, sl_len))
    kv_shape = (DEPTH, pb, MEM_TOKENS, MEM_HEADS, MEM_HEAD_DIM)
    return (y_p.reshape(pb, pl_len, D_MODEL), y_s.reshape(sb, sl_len, D_MODEL),
            p_ssd, p_ssd_conv, p_gla, p_ffn, p_mem_k.reshape(kv_shape), p_mem_v.reshape(kv_shape),
            s_ssd, s_ssd_conv, s_gla, s_ffn)
```

```python
import functools

import jax
import jax.numpy as jnp
from jax import lax
from jax.experimental import pallas as pl
from jax.experimental.pallas import tpu as pltpu

F32 = jnp.float32
BF16 = jnp.bfloat16

D_MODEL = 2048
DEPTH = 2
EPS = 1e-6
SSD_HEADS = 32
SSD_HEADDIM = 64
SSD_GROUPS = 4
SSD_STATE = 128
SSD_CONV = 4
SSD_CONV_DIM = 3072
GLA_HEADS = 4
GLA_HEAD_K = 256
GLA_HEAD_V = 512
GLA_RANK = 16
GLA_GATE_NORM = 16.0
GLA_CHUNK = 16
MEM_TOKENS = 256
MEM_HEADS = 4
MEM_HEAD_DIM = 512
D_FF = 5632
FFN_CONV = 3

LANES = 128
MXU_COLS = 256
SUBLANES = 8
VMEM_CAP_BYTES = 56 * 2**20

P_Z, P_XS, P_V, P_G, P_GATE, P_BC, P_Q, P_K, P_N = 0, 2048, 4096, 6144, 8192, 12288, 13312, 14336, 15360
SM_DT, SM_ALR = 0, 32

NT_DIMS = (((1,), (1,)), ((), ()))
TN_DIMS = (((0,), (0,)), ((), ()))
HI = lax.Precision.HIGHEST


def _vmem_limit(*block_bytes, scratch=0):
    need = 2 * sum(block_bytes) + scratch + (4 << 20)
    return int(min(max(need, 16 << 20), VMEM_CAP_BYTES))


def _nbytes(shape, dtype):
    n = 1
    for s in shape:
        n *= s
    return n * jnp.dtype(dtype).itemsize


def _silu(x):
    return x * jax.nn.sigmoid(x)


def _softplus(x):
    return jnp.maximum(x, 0.0) + jnp.log1p(jnp.exp(-jnp.abs(x)))


def _log_sigmoid(x):
    return -_softplus(-x)


def _rms(x, gain):
    ms = jnp.mean(x * x, axis=-1, keepdims=True)
    return x * lax.rsqrt(ms + EPS) * gain


def _split2(x):
    hi = x.astype(BF16)
    lo = (x - hi.astype(F32)).astype(BF16)
    return hi, lo


def _select_rows(sel, x):
    hi = x.astype(BF16)
    r1 = x - hi.astype(F32)
    mid = r1.astype(BF16)
    lo = (r1 - mid.astype(F32)).astype(BF16)
    return (jnp.dot(sel, hi, preferred_element_type=F32) + jnp.dot(sel, mid, preferred_element_type=F32)
            + jnp.dot(sel, lo, preferred_element_type=F32))


def _stacked_out(prev, shape, dtype):
    return jax.ShapeDtypeStruct(shape, dtype) if prev is None else jax.ShapeDtypeStruct(prev.shape, prev.dtype)


def _norm_mm_kernel(*refs, has_gain, has_small):
    it = iter(refs)
    x_ref = next(it)
    g_ref = next(it) if has_gain else None
    w_ref = next(it)
    ws_ref = next(it) if has_small else None
    o_ref = next(it)
    os_ref = next(it) if has_small else None
    xn_ref = next(it)

    @pl.when(pl.program_id(1) == 0)
    def _():
        x = x_ref[...].astype(F32)
        if has_gain:
            x = _rms(x, g_ref[...])
        xn_ref[...] = x.astype(BF16)
        if has_small:
            os_ref[...] = jnp.dot(xn_ref[...], ws_ref[...], preferred_element_type=F32)

    o_ref[...] = jnp.dot(xn_ref[...], w_ref[...].astype(BF16), preferred_element_type=F32).astype(o_ref.dtype)


def _norm_mm(x, gain, w, layer, w_small, *, tm, tn, out_dtype):
    m, k = x.shape
    n = w.shape[2]
    has_gain = gain is not None
    has_small = w_small is not None
    in_specs = [pl.BlockSpec((tm, k), lambda i, j: (i, 0))]
    args = [x]
    if has_gain:
        in_specs.append(pl.BlockSpec((None, 1, k), lambda i, j: (layer, 0, 0)))
        args.append(gain.reshape(DEPTH, 1, k))
    in_specs.append(pl.BlockSpec((None, k, tn), lambda i, j: (layer, 0, j)))
    args.append(w)
    out_shape = [jax.ShapeDtypeStruct((m, n), out_dtype)]
    out_specs = [pl.BlockSpec((tm, tn), lambda i, j: (i, j))]
    if has_small:
        in_specs.append(pl.BlockSpec((None, k, LANES), lambda i, j: (layer, 0, 0)))
        args.append(w_small)
        out_shape.append(jax.ShapeDtypeStruct((m, LANES), F32))
        out_specs.append(pl.BlockSpec((tm, LANES), lambda i, j: (i, 0)))
    limit = _vmem_limit(_nbytes((tm, k), x.dtype), _nbytes((k, tn), w.dtype), _nbytes((tm, tn), out_dtype),
                        _nbytes((k, LANES), BF16), _nbytes((tm, LANES), F32), scratch=_nbytes((tm, k), BF16))
    res = pl.pallas_call(
        functools.partial(_norm_mm_kernel, has_gain=has_gain, has_small=has_small),
        grid=(m // tm, n // tn),
        in_specs=in_specs,
        out_specs=out_specs,
        out_shape=out_shape,
        scratch_shapes=[pltpu.VMEM((tm, k), BF16)],
        compiler_params=pltpu.CompilerParams(dimension_semantics=("parallel", "arbitrary"),
                                             vmem_limit_bytes=limit),
        name="norm_mm",
    )(*args)
    return res if has_small else res[0]


def _mem_kv_kernel(x_ref, w_ref, o_ref, xb_ref):
    @pl.when((pl.program_id(0) == 0) & (pl.program_id(1) == 0))
    def _():
        xb_ref[...] = x_ref[...].astype(BF16)

    res = jnp.dot(xb_ref[...], w_ref[...].astype(BF16), preferred_element_type=F32)
    o_ref[...] = res.reshape(o_ref.shape)


def _mem_kv(x, w, *, nb):
    m, k = x.shape
    tn = MEM_HEAD_DIM
    limit = _vmem_limit(_nbytes((m, k), F32), _nbytes((k, tn), F32), _nbytes((m, tn), F32),
                        scratch=_nbytes((m, k), BF16))
    return pl.pallas_call(
        _mem_kv_kernel,
        grid=(DEPTH, MEM_HEADS),
        in_specs=[pl.BlockSpec((m, k), lambda d, h: (0, 0)),
                  pl.BlockSpec((None, k, tn), lambda d, h: (d, 0, h))],
        out_specs=pl.BlockSpec((None, nb, None, m // nb, tn), lambda d, h: (d, 0, h, 0, 0)),
        out_shape=jax.ShapeDtypeStruct((DEPTH, nb, MEM_HEADS, m // nb, tn), F32),
        scratch_shapes=[pltpu.VMEM((m, k), BF16)],
        compiler_params=pltpu.CompilerParams(dimension_semantics=("arbitrary", "arbitrary"),
                                             vmem_limit_bytes=limit),
        name="mem_kv",
    )(x, w)


def _norm_small_kernel(x_ref, g_ref, wst_ref, xn_ref, os_ref):
    xn = _rms(x_ref[...], g_ref[...]).astype(BF16)
    xn_ref[...] = xn
    os_ref[...] = lax.dot_general(xn, wst_ref[...].astype(BF16), NT_DIMS, preferred_element_type=F32)


def _norm_small(x, gain, w_small_t, layer, *, tm):
    m, k = x.shape
    return pl.pallas_call(
        _norm_small_kernel,
        grid=(m // tm,),
        in_specs=[pl.BlockSpec((tm, k), lambda i: (i, 0)),
                  pl.BlockSpec((None, 1, k), lambda i: (layer, 0, 0)),
                  pl.BlockSpec((None, LANES, k), lambda i: (layer, 0, 0))],
        out_specs=[pl.BlockSpec((tm, k), lambda i: (i, 0)), pl.BlockSpec((tm, LANES), lambda i: (i, 0))],
        out_shape=[jax.ShapeDtypeStruct((m, k), BF16), jax.ShapeDtypeStruct((m, LANES), F32)],
        compiler_params=pltpu.CompilerParams(dimension_semantics=("parallel",)),
        name="norm_small",
    )(x, gain.reshape(DEPTH, 1, k), w_small_t)


PROJ_TN = 1024
PROJ_SRC = (0, 1024, 2048, 3072, 7200, 8224, 9248, 10272, 11312, 12336, 13360, 14384, 4096, 5152, 6176)


def _proj_kernel(src_ref, xn_ref, wt_ref, o_ref, wbf_ref):
    @pl.when(pl.program_id(1) == 0)
    def _():
        wbf_ref[...] = wt_ref[0].astype(BF16)

    o_ref[...] = lax.dot_general(xn_ref[...], wbf_ref[...], NT_DIMS, preferred_element_type=F32).astype(o_ref.dtype)


def _proj_in(xn, w_in_t, layer, *, tm):
    m, k = xn.shape
    tn = PROJ_TN
    nj = len(PROJ_SRC)
    assert all(s % SUBLANES == 0 for s in PROJ_SRC)
    src = jnp.array([s // SUBLANES for s in PROJ_SRC], jnp.int32)
    limit = _vmem_limit(_nbytes((tm, k), BF16), _nbytes((tn, k), F32), _nbytes((tm, tn), BF16),
                        scratch=_nbytes((tn, k), BF16))
    grid_spec = pltpu.PrefetchScalarGridSpec(
        num_scalar_prefetch=1,
        grid=(nj, m // tm),
        in_specs=[pl.BlockSpec((tm, k), lambda j, i, s: (i, 0)),
                  pl.BlockSpec((pl.Element(1), pl.Element(tn), pl.Element(k)),
                               lambda j, i, s: (layer, s[j] * SUBLANES, 0))],
        out_specs=pl.BlockSpec((tm, tn), lambda j, i, s: (i, j)),
        scratch_shapes=[pltpu.VMEM((tn, k), BF16)])
    return pl.pallas_call(
        _proj_kernel,
        grid_spec=grid_spec,
        out_shape=jax.ShapeDtypeStruct((m, nj * tn), BF16),
        compiler_params=pltpu.CompilerParams(dimension_semantics=("arbitrary", "arbitrary"),
                                             vmem_limit_bytes=limit),
        name="proj_in",
    )(src, xn, w_in_t)


def _mm_res_kernel(x_ref, w_ref, r_ref, o_ref, wbf_ref):
    @pl.when(pl.program_id(1) == 0)
    def _():
        wbf_ref[...] = w_ref[...].astype(BF16)

    o_ref[...] = r_ref[...] + jnp.dot(x_ref[...], wbf_ref[...], preferred_element_type=F32)


def _mm_res(x, w, layer, res, *, tm, tn):
    m, k = x.shape
    n = w.shape[2]
    limit = _vmem_limit(_nbytes((tm, k), BF16), _nbytes((k, tn), F32), 2 * _nbytes((tm, tn), F32),
                        scratch=_nbytes((k, tn), BF16))
    return pl.pallas_call(
        _mm_res_kernel,
        grid=(n // tn, m // tm),
        in_specs=[pl.BlockSpec((tm, k), lambda j, i: (i, 0)),
                  pl.BlockSpec((None, k, tn), lambda j, i: (layer, 0, j)),
                  pl.BlockSpec((tm, tn), lambda j, i: (i, j))],
        out_specs=pl.BlockSpec((tm, tn), lambda j, i: (i, j)),
        out_shape=jax.ShapeDtypeStruct((m, n), F32),
        scratch_shapes=[pltpu.VMEM((k, tn), BF16)],
        compiler_params=pltpu.CompilerParams(dimension_semantics=("parallel", "arbitrary"),
                                             vmem_limit_bytes=limit),
        name="mm_res",
    )(x, w, res)


def _merge_kernel(y0_ref, y1_ref, g0_ref, g1_ref, w_ref, o_ref, wbf_ref):
    @pl.when(pl.program_id(1) == 0)
    def _():
        wbf_ref[...] = w_ref[...].astype(BF16)

    b0 = jnp.dot(y0_ref[...], wbf_ref[0], preferred_element_type=F32)
    b1 = jnp.dot(y1_ref[...], wbf_ref[1], preferred_element_type=F32)
    g0 = jax.nn.sigmoid(g0_ref[...].astype(F32))
    g1 = jax.nn.sigmoid(g1_ref[...].astype(F32))
    o_ref[...] = (g0 * b0 + g1 * b1).astype(o_ref.dtype)


def _merge(y_ssd, y_gla, proj, w_branch, layer, *, tm, tn):
    m, k = y_ssd.shape
    n = D_MODEL
    gb = P_GATE // tn
    nb = n // tn
    limit = _vmem_limit(2 * _nbytes((tm, k), BF16), 2 * _nbytes((k, tn), F32), 3 * _nbytes((tm, tn), BF16),
                        scratch=2 * _nbytes((k, tn), BF16))
    return pl.pallas_call(
        _merge_kernel,
        grid=(nb, m // tm),
        in_specs=[pl.BlockSpec((tm, k), lambda j, i: (i, 0)),
                  pl.BlockSpec((tm, k), lambda j, i: (i, 0)),
                  pl.BlockSpec((tm, tn), lambda j, i: (i, gb + j)),
                  pl.BlockSpec((tm, tn), lambda j, i: (i, gb + nb + j)),
                  pl.BlockSpec((None, 2, k, tn), lambda j, i: (layer, 0, 0, j))],
        out_specs=pl.BlockSpec((tm, tn), lambda j, i: (i, j)),
        out_shape=jax.ShapeDtypeStruct((m, n), BF16),
        scratch_shapes=[pltpu.VMEM((2, k, tn), BF16)],
        compiler_params=pltpu.CompilerParams(dimension_semantics=("parallel", "arbitrary"),
                                             vmem_limit_bytes=limit),
        name="merge",
    )(y_ssd, y_gla, proj, proj, w_branch)


def _final_norm_kernel(x_ref, g_ref, o_ref):
    o_ref[...] = _rms(x_ref[...], g_ref[...])


def _final_norm(x, gain, *, tm):
    m, k = x.shape
    return pl.pallas_call(
        _final_norm_kernel,
        grid=(m // tm,),
        in_specs=[pl.BlockSpec((tm, k), lambda i: (i, 0)), pl.BlockSpec((1, k), lambda i: (0, 0))],
        out_specs=pl.BlockSpec((tm, k), lambda i: (i, 0)),
        out_shape=jax.ShapeDtypeStruct((m, k), F32),
        compiler_params=pltpu.CompilerParams(dimension_semantics=("parallel",)),
        name="final_norm",
    )(x, gain.reshape(1, k))


def _ffn_up_kernel(*refs, tm, tn, nj, seq_tiles, seg):
    short = seg > 0
    it = iter(refs)
    x_ref, g_ref, wu_ref, wt_ref, cwu_ref, cwt_ref, cbu_ref, cbt_ref = (next(it) for _ in range(8))
    stu_ref = next(it) if short else None
    stt_ref = next(it) if short else None
    act_ref, cnu_ref, cnt_ref = next(it), next(it), next(it)
    xn_ref, wbu_ref, wbt_ref = next(it), next(it), next(it)
    bufs = [(next(it), next(it)) for _ in range(2)]
    tailu_ref = None if short else next(it)
    tailt_ref = None if short else next(it)
    i = pl.program_id(0)
    j = pl.program_id(1)
    sl = SUBLANES
    hw = FFN_CONV - 1

    def step(buf_w, buf_r, jj):
        re = 16
        if buf_r is not None:
            us_ref, ts_ref = buf_r
            if short:
                us_ref[0:sl, :] = jnp.zeros((sl, tn), F32)
                ts_ref[0:sl, :] = jnp.zeros((sl, tn), F32)
                nseq = tm // seg
                r = lax.broadcasted_iota(jnp.int32, (tm, hw * nseq), 0)
                c = lax.broadcasted_iota(jnp.int32, (tm, hw * nseq), 1)
                halo_u, halo_t = [], []
                for s in range(1, FFN_CONV):
                    sel = (((r % seg) < s) & (c == hw * (r // seg) + hw - s + (r % seg))).astype(BF16)
                    halo_u.append(_select_rows(sel, stu_ref[...]))
                    halo_t.append(_select_rows(sel, stt_ref[...]))
                ro = lax.broadcasted_iota(jnp.int32, (hw * nseq, tm), 0)
                co = lax.broadcasted_iota(jnp.int32, (hw * nseq, tm), 1)
                sel_out = (co == (ro // hw) * seg + seg - hw + (ro % hw)).astype(BF16)
                cnu_ref[...] = _select_rows(sel_out, us_ref[sl:sl + tm, :])
                cnt_ref[...] = _select_rows(sel_out, ts_ref[sl:sl + tm, :])
            else:
                start = (i % seq_tiles) == 0
                us_ref[0:sl, :] = jnp.where(start, 0.0, tailu_ref[jj])
                ts_ref[0:sl, :] = jnp.where(start, 0.0, tailt_ref[jj])
                tailu_ref[jj] = us_ref[tm:tm + sl, :]
                tailt_ref[jj] = ts_ref[tm:tm + sl, :]
                cnu_ref[...] = us_ref[tm:tm + sl, :]
                cnt_ref[...] = ts_ref[tm:tm + sl, :]

        parts = [(w, c0) for c0 in range(0, tn, MXU_COLS) for w in range(2)]
        pieces = list(range(0, tm, re))
        per_part = -(-len(pieces) // len(parts))
        for pi, (w, c0) in enumerate(parts):
            if buf_w is not None:
                cl = slice(c0, c0 + MXU_COLS)
                w_src, w_bf = ((wu_ref, wbu_ref), (wt_ref, wbt_ref))[w]
                w_bf[:, cl] = w_src[:, cl].astype(BF16)
                buf_w[w][sl:sl + tm, cl] = jnp.dot(xn_ref[...], w_bf[:, cl], preferred_element_type=F32)
            for e0 in pieces[pi * per_part:(pi + 1) * per_part] if buf_r is not None else ():
                rowmod = (lax.broadcasted_iota(jnp.int32, (re, tn), 0) + e0) % seg if short else None

                def conv(s_ref, cw_ref, cb_ref, halos):
                    y = s_ref[sl + e0:sl + e0 + re, :] * cw_ref[hw:hw + 1, :]
                    for s in range(1, FFN_CONV):
                        prev = s_ref[sl + e0 - s:sl + e0 - s + re, :]
                        if short:
                            prev = jnp.where(rowmod < s, halos[s - 1][e0:e0 + re, :], prev)
                        y = y + prev * cw_ref[hw - s:hw - s + 1, :]
                    return y + cb_ref[...]

                u = conv(us_ref, cwu_ref, cbu_ref, halo_u if short else None)
                t = conv(ts_ref, cwt_ref, cbt_ref, halo_t if short else None)
                act_ref[e0:e0 + re, :] = (_silu(t) * u).astype(act_ref.dtype)

    @pl.when(j == 0)
    def _():
        if not short:
            @pl.when(i == 0)
            def _():
                tailu_ref[...] = jnp.zeros_like(tailu_ref)
                tailt_ref[...] = jnp.zeros_like(tailt_ref)
        xn_ref[...] = _rms(x_ref[...], g_ref[...]).astype(BF16)
        step(bufs[0], None, None)

    for par in range(2):
        @pl.when((j > 0) & (j < nj) & (j % 2 == par))
        def _():
            step(bufs[par], bufs[1 - par], j - 1)

    @pl.when(j == nj)
    def _():
        step(None, bufs[(nj - 1) % 2], j - 1)


def _ffn_up(x, gain, w, conv_w, conv_b, conv0, layer, *, L, tm, tn):
    m, k = x.shape
    bt = m // L
    nj = D_FF // tn
    hw = FFN_CONV - 1
    short = L < tm
    seg = L if short else 0
    seq_tiles = 1 if short else L // tm
    assert (conv0 is not None) == short, "history rows are only supported for sequences shorter than a block"
    jw = lambda j: jnp.minimum(j, nj - 1)
    je = lambda j: jnp.maximum(j - 1, 0)
    in_specs = [
        pl.BlockSpec((tm, k), lambda i, j: (i, 0)),
        pl.BlockSpec((None, 1, k), lambda i, j: (layer, 0, 0)),
        pl.BlockSpec((None, k, tn), lambda i, j: (layer, 0, jw(j))),
        pl.BlockSpec((None, k, tn), lambda i, j: (layer, 0, nj + jw(j))),
        pl.BlockSpec((None, FFN_CONV, tn), lambda i, j: (layer, 0, je(j))),
        pl.BlockSpec((None, FFN_CONV, tn), lambda i, j: (layer, 0, nj + je(j))),
        pl.BlockSpec((None, 1, tn), lambda i, j: (layer, 0, je(j))),
        pl.BlockSpec((None, 1, tn), lambda i, j: (layer, 0, nj + je(j))),
    ]
    args = [x, gain.reshape(DEPTH, 1, k), w, w, conv_w, conv_w, conv_b.reshape(DEPTH, 1, 2 * D_FF),
            conv_b.reshape(DEPTH, 1, 2 * D_FF)]
    scratch = ([pltpu.VMEM((tm, k), BF16)] + [pltpu.VMEM((k, tn), BF16)] * 2
               + [pltpu.VMEM((tm + SUBLANES, tn), F32)] * 4)
    if short:
        nst = (m // tm) * (tm // L) * hw
        st2d = conv0.reshape(DEPTH, nst, 2 * D_FF)
        rows = (tm // L) * hw
        in_specs += [pl.BlockSpec((None, rows, tn), lambda i, j: (layer, i, je(j))),
                     pl.BlockSpec((None, rows, tn), lambda i, j: (layer, i, nj + je(j)))]
        args += [st2d, st2d]
        cn_shape = jax.ShapeDtypeStruct((nst, D_FF), F32)
        cn_spec = pl.BlockSpec((rows, tn), lambda i, j: (i, je(j)))
    else:
        scratch += [pltpu.VMEM((nj, SUBLANES, tn), F32), pltpu.VMEM((nj, SUBLANES, tn), F32)]
        cn_shape = jax.ShapeDtypeStruct((m // tm, SUBLANES, D_FF), F32)
        cn_spec = pl.BlockSpec((None, SUBLANES, tn), lambda i, j: (i, 0, je(j)))
    limit = _vmem_limit(_nbytes((tm, k), F32), 2 * _nbytes((k, tn), F32), _nbytes((tm, tn), BF16),
                        scratch=(_nbytes((tm, k), BF16) + 2 * _nbytes((k, tn), BF16)
                                 + 4 * _nbytes((tm + SUBLANES, tn), F32)))
    act, cnu, cnt = pl.pallas_call(
        functools.partial(_ffn_up_kernel, tm=tm, tn=tn, nj=nj, seq_tiles=seq_tiles, seg=seg),
        grid=(m // tm, nj + 1),
        in_specs=in_specs,
        out_specs=[pl.BlockSpec((tm, tn), lambda i, j: (i, je(j))), cn_spec, cn_spec],
        out_shape=[jax.ShapeDtypeStruct((m, D_FF), BF16), cn_shape, cn_shape],
        scratch_shapes=scratch,
        compiler_params=pltpu.CompilerParams(dimension_semantics=("arbitrary", "arbitrary"),
                                             vmem_limit_bytes=limit),
        name="ffn_up",
    )(*args)
    if short:
        conv_new = jnp.concatenate([cnu, cnt], axis=-1).reshape(bt, hw, 2 * D_FF)
    else:
        last = slice(seq_tiles - 1, None, seq_tiles)
        conv_new = jnp.concatenate([cnu[last, SUBLANES - hw:, :], cnt[last, SUBLANES - hw:, :]], axis=-1)
    return act, conv_new


def _causal_conv(x, tail_ref, w_ref, b, width):
    row = lax.broadcasted_iota(jnp.int32, x.shape, 0)
    y = x * w_ref[width - 1:width, :]
    for s in range(1, width):
        shifted = pltpu.roll(x, s, axis=0)
        for t in range(s):
            shifted = jnp.where(row == t, tail_ref[8 + t - s:9 + t - s, :], shifted)
        y = y + shifted * w_ref[width - 1 - s:width - s, :]
    return y + b


def _ssd_kernel(*refs, Q, zero_init, aliased):
    it = iter(refs)
    z_ref, xs_ref, bc_ref, sm_ref = (next(it) for _ in range(4))
    conv0_ref = None if zero_init else next(it)
    h0_ref = None if zero_init else next(it)
    cwx_ref, cwb_ref, cbx_ref, cbb_ref, dtb_ref, a_ref, dx_ref, nrm_ref, exp_ref = (next(it) for _ in range(9))
    if aliased:
        next(it), next(it)
    y_ref, h_out, conv_out, tailx_ref, tailb_ref = (next(it) for _ in range(5))
    h_ref = h_out if aliased else h_out.at[0]
    convout_ref = conv_out if aliased else conv_out.at[0]
    c = pl.program_id(1)
    hp = SSD_HEADDIM
    gw = SSD_HEADS // SSD_GROUPS * hp

    @pl.when(c == 0)
    def _():
        if not aliased:
            h_out[1:] = jnp.zeros((DEPTH - 1,) + h_ref.shape, F32)
            conv_out[1:] = jnp.zeros((DEPTH - 1,) + convout_ref.shape, F32)
        tailx_ref[...] = jnp.zeros_like(tailx_ref)
        tailb_ref[...] = jnp.zeros_like(tailb_ref)
        if zero_init:
            h_ref[...] = jnp.zeros_like(h_ref)
        else:
            h_ref[...] = h0_ref[...]
            tailx_ref[5:8, :] = conv0_ref[:, 0:D_MODEL]
            tailb_ref[5:8, :] = conv0_ref[:, D_MODEL:SSD_CONV_DIM]

    xs_raw = xs_ref[...].astype(F32)
    bc_raw = bc_ref[...].astype(F32)
    xs = _silu(_causal_conv(xs_raw, tailx_ref, cwx_ref, cbx_ref[...], SSD_CONV))
    bc = _silu(_causal_conv(bc_raw, tailb_ref, cwb_ref, cbb_ref[...], SSD_CONV))
    tailx_ref[...] = xs_raw[Q - 8:Q, :]
    tailb_ref[...] = bc_raw[Q - 8:Q, :]

    @pl.when(c == pl.num_programs(1) - 1)
    def _():
        convout_ref[:, 0:D_MODEL] = tailx_ref[5:8, :]
        convout_ref[:, D_MODEL:SSD_CONV_DIM] = tailb_ref[5:8, :]

    dt = _softplus(sm_ref[...] + dtb_ref[...])
    adt = dt * a_ref[...]
    ri = lax.broadcasted_iota(jnp.int32, (Q, Q), 0)
    ci = lax.broadcasted_iota(jnp.int32, (Q, Q), 1)
    causal = ci <= ri
    acum = jnp.dot(causal.astype(F32), adt, precision=HI, preferred_element_type=F32)
    eye = (lax.broadcasted_iota(jnp.int32, (LANES, LANES), 0)
           == lax.broadcasted_iota(jnp.int32, (LANES, LANES), 1)).astype(F32)
    acum_t = lax.dot_general(eye, acum, NT_DIMS, precision=HI, preferred_element_type=F32)
    dt_t = lax.dot_general(eye, dt, NT_DIMS, precision=HI, preferred_element_type=F32)
    a_last = acum[Q - 1:Q, :]
    ea = jnp.exp(acum)
    te = jnp.exp(a_last - acum) * dt
    dec_rows = jnp.broadcast_to(jnp.exp(acum_t[:, Q - 1:Q]), (LANES, LANES))
    ea_hi, ea_lo = _split2(ea)
    te_hi, te_lo = _split2(te)
    expand = exp_ref[...]
    ea_x = (jnp.dot(ea_hi, expand, preferred_element_type=F32)
            + jnp.dot(ea_lo, expand, preferred_element_type=F32))
    te_x = (jnp.dot(te_hi, expand, preferred_element_type=F32)
            + jnp.dot(te_lo, expand, preferred_element_type=F32))
    lane = lax.broadcasted_iota(jnp.int32, (Q, LANES), 1)

    for g in range(SSD_GROUPS):
        gl = slice(g * gw, (g + 1) * gw)
        b_g = bc[:, g * SSD_STATE:(g + 1) * SSD_STATE].astype(BF16)
        c_g = bc[:, (SSD_GROUPS + g) * SSD_STATE:(SSD_GROUPS + g + 1) * SSD_STATE].astype(BF16)
        cb = lax.dot_general(c_g, b_g, NT_DIMS, preferred_element_type=F32)
        h_g = h_ref[8 * g:8 * g + 8].reshape(gw, SSD_STATE)
        y_off = lax.dot_general(c_g, h_g.astype(BF16), NT_DIMS, preferred_element_type=F32)
        x_g = xs[:, gl]
        pairs = []
        for p in range(4):
            x_p = x_g[:, p * LANES:(p + 1) * LANES]
            acc = None
            for s in range(2):
                hh = g * 8 + p * 2 + s
                seg = jnp.broadcast_to(acum[:, hh:hh + 1], (Q, Q)) - jnp.broadcast_to(acum_t[hh:hh + 1, :], (Q, Q))
                decay = jnp.where(causal, jnp.exp(seg), 0.0)
                w_h = (cb * decay * dt_t[hh:hh + 1, :]).astype(BF16)
                x_m = jnp.where((lane // hp) == s, x_p, 0.0).astype(BF16)
                r = jnp.dot(w_h, x_m, preferred_element_type=F32)
                acc = r if acc is None else acc + r
            pairs.append(acc)
        y_g = jnp.concatenate(pairs, axis=1) + y_off * ea_x[:, gl] + dx_ref[:, gl] * x_g
        x_t = (x_g * te_x[:, gl]).astype(BF16)
        upd = lax.dot_general(x_t, b_g, TN_DIMS, preferred_element_type=F32)
        for h in range(8):
            hh = g * 8 + h
            h_ref[hh] = h_ref[hh] * dec_rows[hh:hh + 1, :] + upd[h * hp:(h + 1) * hp, :]
        z_g = z_ref[:, gl].astype(F32)
        y_ref[:, gl] = _rms(y_g * _silu(z_g), nrm_ref[:, gl]).astype(y_ref.dtype)


def _ssd(proj, small, conv0, h0, wl, layer, prev_h, prev_conv, *, bt, L, Q):
    zero_init = h0 is None
    aliased = prev_h is not None
    nc = L // Q
    T = bt * L
    full2 = lambda b, c: (0, 0)
    lay3 = lambda b, c: (layer, 0, 0)
    row = lambda col: (lambda b, c: (b * nc + c, col))
    hshape = (DEPTH, bt, SSD_HEADS, SSD_HEADDIM, SSD_STATE)
    cshape = (DEPTH, bt, SSD_CONV - 1, SSD_CONV_DIM)
    hspec = pl.BlockSpec((None, None, SSD_HEADS, SSD_HEADDIM, SSD_STATE), lambda b, c: (layer, b, 0, 0, 0))
    cspec = pl.BlockSpec((None, None, SSD_CONV - 1, SSD_CONV_DIM), lambda b, c: (layer, b, 0, 0))
    h_out_spec = hspec if aliased else pl.BlockSpec((DEPTH, None, SSD_HEADS, SSD_HEADDIM, SSD_STATE),
                                                    lambda b, c: (0, b, 0, 0, 0))
    c_out_spec = cspec if aliased else pl.BlockSpec((DEPTH, None, SSD_CONV - 1, SSD_CONV_DIM),
                                                    lambda b, c: (0, b, 0, 0))
    in_specs = [
        pl.BlockSpec((Q, D_MODEL), row(P_Z // D_MODEL)),
        pl.BlockSpec((Q, D_MODEL), row(P_XS // D_MODEL)),
        pl.BlockSpec((Q, 1024), row(P_BC // 1024)),
        pl.BlockSpec((Q, LANES), row(0)),
    ]
    args = [proj, proj, proj, small]
    if not zero_init:
        in_specs += [cspec, hspec]
        args += [conv0, h0]
    in_specs += [
        pl.BlockSpec((None, SSD_CONV, D_MODEL), lay3),
        pl.BlockSpec((None, SSD_CONV, 1024), lambda b, c: (layer, 0, D_MODEL // 1024)),
        pl.BlockSpec((None, 1, D_MODEL), lay3),
        pl.BlockSpec((None, 1, 1024), lambda b, c: (layer, 0, D_MODEL // 1024)),
        pl.BlockSpec((None, 1, LANES), lay3),
        pl.BlockSpec((None, 1, LANES), lay3),
        pl.BlockSpec((None, 1, D_MODEL), lay3),
        pl.BlockSpec((None, 1, D_MODEL), lay3),
        pl.BlockSpec((LANES, D_MODEL), full2),
    ]
    args += [wl["ssd_conv_w"], wl["ssd_conv_w"], wl["ssd_conv_b"], wl["ssd_conv_b"], wl["ssd_dtb"], wl["ssd_a"],
             wl["ssd_dx"], wl["ssd_norm"], wl["expand"]]
    aliases = {}
    if aliased:
        aliases = {len(args): 1, len(args) + 1: 2}
        in_specs += [pl.BlockSpec(memory_space=pl.ANY), pl.BlockSpec(memory_space=pl.ANY)]
        args += [prev_h, prev_conv]
    return pl.pallas_call(
        functools.partial(_ssd_kernel, Q=Q, zero_init=zero_init, aliased=aliased),
        grid=(bt, nc),
        in_specs=in_specs,
        out_specs=[pl.BlockSpec((Q, D_MODEL), row(0)), h_out_spec, c_out_spec],
        out_shape=[jax.ShapeDtypeStruct((T, D_MODEL), BF16), _stacked_out(prev_h, hshape, F32),
                   _stacked_out(prev_conv, cshape, F32)],
        scratch_shapes=[pltpu.VMEM((8, D_MODEL), F32), pltpu.VMEM((8, 1024), F32)],
        input_output_aliases=aliases,
        compiler_params=pltpu.CompilerParams(dimension_semantics=("parallel", "arbitrary"),
                                             vmem_limit_bytes=48 << 20),
        name="ssd",
    )(*args)


def _gla_kernel(*refs, TB, zero_init, aliased):
    it = iter(refs)
    q_ref, k_ref, v_ref, g_ref, sm_ref = (next(it) for _ in range(5))
    s0_ref = None if zero_init else next(it)
    wa_ref, ba_ref, wat_ref, bat_ref, gn_ref = (next(it) for _ in range(5))
    if aliased:
        next(it)
    y_ref, s_out = next(it), next(it)
    s_ref = s_out if aliased else s_out.at[0]
    c = pl.program_id(1)
    ck = GLA_CHUNK
    ns = TB // ck
    hk, hv = GLA_HEAD_K, GLA_HEAD_V

    @pl.when(c == 0)
    def _():
        if not aliased:
            s_out[1:] = jnp.zeros((DEPTH - 1,) + s_ref.shape, F32)
        if zero_init:
            s_ref[...] = jnp.zeros_like(s_ref)
        else:
            s_ref[...] = s0_ref[...]

    smb = sm_ref[...].astype(BF16)
    la = _log_sigmoid(jnp.dot(smb, wa_ref[...], preferred_element_type=F32) + ba_ref[...]) * (1.0 / GLA_GATE_NORM)
    rb = lax.broadcasted_iota(jnp.int32, (TB, TB), 0)
    cb = lax.broadcasted_iota(jnp.int32, (TB, TB), 1)
    same = (rb // ck) == (cb // ck)
    msel = jnp.concatenate([same & (cb <= rb), same, (cb // ck) < (rb // ck)], axis=0).astype(BF16)
    sums = _select_rows(msel, la)
    bcum, tot, bprev = sums[0:TB], sums[TB:2 * TB], sums[2 * TB:3 * TB]
    kf = k_ref[...].astype(F32)
    qd = q_ref[...].astype(F32) * (hk ** -0.5) * jnp.exp(bcum)
    qd_b = qd.astype(BF16)
    qs_b = (qd * jnp.exp(bprev)).astype(BF16)
    ki_b = (kf * jnp.exp(-bcum)).astype(BF16)
    ke_b = (kf * jnp.exp(tot - bcum)).astype(BF16)
    la_t = _log_sigmoid(lax.dot_general(wat_ref[...], smb, NT_DIMS, preferred_element_type=F32)
                        + bat_ref[:, 0:1]) * (1.0 / GLA_GATE_NORM)
    rs = lax.broadcasted_iota(jnp.int32, (TB, LANES), 0)
    cs = lax.broadcasted_iota(jnp.int32, (TB, LANES), 1)
    selc = (((rs // ck) == cs) | (cs == ns)).astype(BF16)
    la_hi = la_t.astype(BF16)
    la_r = la_t - la_hi.astype(F32)
    la_mid = la_r.astype(BF16)
    la_lo = (la_r - la_mid.astype(F32)).astype(BF16)
    dec_t = jnp.exp(jnp.dot(la_hi, selc, preferred_element_type=F32) + jnp.dot(la_mid, selc, preferred_element_type=F32)
                    + jnp.dot(la_lo, selc, preferred_element_type=F32))
    eye = (lax.broadcasted_iota(jnp.int32, (hk, hk), 0) == lax.broadcasted_iota(jnp.int32, (hk, hk), 1)).astype(BF16)
    slot = lax.broadcasted_iota(jnp.int32, (hk, TB), 1) // ck
    row16 = lax.broadcasted_iota(jnp.int32, (ck, TB), 0)
    col16 = lax.broadcasted_iota(jnp.int32, (ck, TB), 1)

    for h in range(GLA_HEADS):
        kl = slice(h * hk, (h + 1) * hk)
        vl = slice(h * hv, (h + 1) * hv)
        ke_t = lax.dot_general(eye, ke_b[:, kl], NT_DIMS, preferred_element_type=F32)
        ki_t = lax.dot_general(eye, ki_b[:, kl], NT_DIMS, preferred_element_type=F32)
        v_h = v_ref[:, vl]
        kbuf = jnp.zeros((hk, TB), F32)
        att = []
        for cc in range(ns):
            in_c = slot == cc
            kall = jnp.where(in_c, ki_t, kbuf).astype(BF16)
            a = jnp.dot(qd_b[cc * ck:(cc + 1) * ck, kl], kall, preferred_element_type=F32)
            att.append(jnp.where(col16 <= row16 + cc * ck, a, 0.0))
            kbuf = jnp.where(in_c, ke_t, kbuf * dec_t[kl, cc:cc + 1])
        a_full = jnp.concatenate(att, axis=0).astype(BF16)
        s_old = s_ref[h]
        o = (jnp.dot(a_full, v_h, preferred_element_type=F32)
             + jnp.dot(qs_b[:, kl], s_old.astype(BF16), preferred_element_type=F32))
        y_ref[:, vl] = (_rms(o, gn_ref[...]) * _silu(g_ref[:, vl].astype(F32))).astype(y_ref.dtype)
        s_ref[h] = s_old * dec_t[kl, ns:ns + 1] + jnp.dot(kbuf.astype(BF16), v_h, preferred_element_type=F32)


def _gla(proj, small, s0, wl, layer, prev_s, *, bt, L, TB):
    zero_init = s0 is None
    aliased = prev_s is not None
    nc = L // TB
    T = bt * L
    lay3 = lambda b, c: (layer, 0, 0)
    row = lambda col: (lambda b, c: (b * nc + c, col))
    kd = GLA_HEADS * GLA_HEAD_K
    sshape = (DEPTH, bt, GLA_HEADS, GLA_HEAD_K, GLA_HEAD_V)
    sspec = pl.BlockSpec((None, None, GLA_HEADS, GLA_HEAD_K, GLA_HEAD_V), lambda b, c: (layer, b, 0, 0, 0))
    s_out_spec = sspec if aliased else pl.BlockSpec((DEPTH, None, GLA_HEADS, GLA_HEAD_K, GLA_HEAD_V),
                                                     lambda b, c: (0, b, 0, 0, 0))
    in_specs = [
        pl.BlockSpec((TB, kd), row(P_Q // kd)),
        pl.BlockSpec((TB, kd), row(P_K // kd)),
        pl.BlockSpec((TB, D_MODEL), row(P_V // D_MODEL)),
        pl.BlockSpec((TB, D_MODEL), row(P_G // D_MODEL)),
        pl.BlockSpec((TB, LANES), row(0)),
    ]
    args = [proj, proj, proj, proj, small]
    if not zero_init:
        in_specs.append(sspec)
        args.append(s0)
    in_specs += [
        pl.BlockSpec((None, LANES, kd), lay3),
        pl.BlockSpec((None, 1, kd), lay3),
        pl.BlockSpec((None, kd, LANES), lay3),
        pl.BlockSpec((None, kd, LANES), lay3),
        pl.BlockSpec((None, 1, GLA_HEAD_V), lay3),
    ]
    args += [wl["gla_wa"], wl["gla_ba"], wl["gla_wa_t"], wl["gla_ba_t"], wl["gla_norm"]]
    aliases = {}
    if aliased:
        aliases = {len(args): 1}
        in_specs.append(pl.BlockSpec(memory_space=pl.ANY))
        args.append(prev_s)
    return pl.pallas_call(
        functools.partial(_gla_kernel, TB=TB, zero_init=zero_init, aliased=aliased),
        grid=(bt, nc),
        in_specs=in_specs,
        out_specs=[pl.BlockSpec((TB, D_MODEL), row(0)), s_out_spec],
        out_shape=[jax.ShapeDtypeStruct((T, D_MODEL), BF16), _stacked_out(prev_s, sshape, F32)],
        input_output_aliases=aliases,
        compiler_params=pltpu.CompilerParams(dimension_semantics=("parallel", "arbitrary"),
                                             vmem_limit_bytes=48 << 20),
        name="gla",
    )(*args)


def _attn_kernel(q_ref, k_ref, v_ref, o_ref, kb_ref, vb_ref):
    @pl.when(pl.program_id(1) == 0)
    def _():
        kb_ref[...] = k_ref[...].astype(BF16)
        vb_ref[...] = v_ref[...].astype(BF16)

    hd = MEM_HEAD_DIM
    for h in range(MEM_HEADS):
        hl = slice(h * hd, (h + 1) * hd)
        s = lax.dot_general(q_ref[:, hl], kb_ref[h], NT_DIMS, preferred_element_type=F32) * (hd ** -0.5)
        e = jnp.exp(s - jnp.max(s, axis=-1, keepdims=True))
        p = e / jnp.sum(e, axis=-1, keepdims=True)
        o_ref[:, hl] = jnp.dot(p.astype(BF16), vb_ref[h], preferred_element_type=F32).astype(o_ref.dtype)


def _attn(q, mem_k, mem_v, layer, *, bt, L, tl):
    nl = L // tl
    T = bt * L
    kvshape = (MEM_HEADS, MEM_TOKENS, MEM_HEAD_DIM)
    kvspec = pl.BlockSpec((None, None) + kvshape, lambda b, l: (layer, b, 0, 0, 0))
    return pl.pallas_call(
        _attn_kernel,
        grid=(bt, nl),
        in_specs=[pl.BlockSpec((tl, D_MODEL), lambda b, l: (b * nl + l, 0)), kvspec, kvspec],
        out_specs=pl.BlockSpec((tl, D_MODEL), lambda b, l: (b * nl + l, 0)),
        out_shape=jax.ShapeDtypeStruct((T, D_MODEL), BF16),
        scratch_shapes=[pltpu.VMEM(kvshape, BF16), pltpu.VMEM(kvshape, BF16)],
        compiler_params=pltpu.CompilerParams(dimension_semantics=("parallel", "arbitrary"),
                                             vmem_limit_bytes=40 << 20),
        name="mem_attn",
    )(q, mem_k, mem_v)


def _prep_weights(w_in, ssd_conv_w, ssd_conv_b, ssd_dt_bias, ssd_a_log, ssd_d, ssd_norm, gla_wa2, gla_ba, gla_norm):
    w_in_t = jnp.swapaxes(w_in, 1, 2)
    o_dt, o_alr = 5120, 11296
    pad_sm = LANES - SSD_HEADS - GLA_RANK
    w_small_t = jnp.concatenate(
        [w_in_t[:, o_dt:o_dt + SSD_HEADS, :], w_in_t[:, o_alr:o_alr + GLA_RANK, :],
         jnp.zeros((DEPTH, pad_sm, D_MODEL), F32)], axis=1)
    pad_h = LANES - SSD_HEADS
    kd = GLA_HEADS * GLA_HEAD_K
    wa = jnp.concatenate([jnp.zeros((DEPTH, SM_ALR, kd), F32), gla_wa2,
                          jnp.zeros((DEPTH, LANES - SM_ALR - GLA_RANK, kd), F32)], axis=1).astype(BF16)
    expand = (jnp.arange(D_MODEL)[None, :] // SSD_HEADDIM == jnp.arange(LANES)[:, None]).astype(BF16)
    return dict(
        w_in_t=w_in_t, w_small_t=w_small_t,
        ssd_conv_w=ssd_conv_w, ssd_conv_b=ssd_conv_b[:, None, :],
        ssd_dtb=jnp.pad(ssd_dt_bias, ((0, 0), (0, pad_h)))[:, None, :],
        ssd_a=jnp.pad(-jnp.exp(ssd_a_log), ((0, 0), (0, pad_h)))[:, None, :],
        ssd_dx=jnp.repeat(ssd_d, SSD_HEADDIM, axis=1)[:, None, :],
        ssd_norm=ssd_norm[:, None, :], expand=expand,
        gla_wa=wa, gla_ba=gla_ba[:, None, :], gla_wa_t=jnp.swapaxes(wa, 1, 2),
        gla_ba_t=jnp.broadcast_to(gla_ba[:, :, None], (DEPTH, kd, LANES)),
        gla_norm=gla_norm[:, None, :])


def _run_trunk(x, mem_k, mem_v, st_ssd, st_ssd_conv, st_gla, st_ffn_conv, wl, big, norm_final, *, bt, L, cfg):
    tm = cfg["tm"]
    n_h = n_c = n_s = None
    new_ffn = []
    for i in range(DEPTH):
        xn, small = _norm_small(x, big["norm_mix"], wl["w_small_t"], i, tm=cfg["tm_norm"])
        proj = _proj_in(xn, wl["w_in_t"], i, tm=tm)
        y_ssd, n_h, n_c = _ssd(proj, small, st_ssd_conv, st_ssd, wl, i, n_h, n_c, bt=bt, L=L, Q=cfg["ssd_q"])
        y_gla, n_s = _gla(proj, small, st_gla, wl, i, n_s, bt=bt, L=L, TB=cfg["gla_tb"])
        merged = _merge(y_ssd, y_gla, proj, big["w_branch"], i, tm=tm, tn=512)
        x = _mm_res(merged, big["w_out"], i, x, tm=tm, tn=1024)
        q = _norm_mm(x, big["norm_mem"], big["w_mq"], i, None, tm=tm, tn=512, out_dtype=BF16)
        o = _attn(q, mem_k, mem_v, i, bt=bt, L=L, tl=cfg["attn_tl"])
        x = _mm_res(o, big["w_mo"], i, x, tm=tm, tn=1024)
        act, f_c = _ffn_up(x, big["norm_ffn"], big["w_ffn_in"], big["ffn_conv_w"], big["ffn_conv_b"], st_ffn_conv, i,
                           L=L, tm=tm, tn=512)
        x = _mm_res(act, big["w_ffn_out"], i, x, tm=cfg["tm_norm"], tn=512)
        new_ffn.append(f_c)
    y = _final_norm(x, norm_final, tm=cfg["tm_norm"])
    return y, n_h, n_c, n_s, jnp.stack(new_ffn)


def _group_cfg(bt, L):
    T = bt * L
    return dict(tm=min(T, 1024), tm_norm=min(T, 512), ssd_q=min(L, 128), gla_tb=min(L, 128), attn_tl=min(L, 512))


def kernel(x_prompt, x_sample, mem_prompt, state_ssd, state_ssd_conv, state_gla, state_ffn_conv, cache_mem_k, cache_mem_v, norm_mix, w_in, ssd_conv_w, ssd_conv_b, ssd_dt_bias, ssd_a_log, ssd_d, ssd_norm, gla_wa2, gla_ba, gla_norm, w_branch, w_out, norm_mem, w_mq, w_mk, w_mv, w_mo, norm_ffn, w_ffn_in, ffn_conv_w, ffn_conv_b, w_ffn_out, norm_final):
    wl = _prep_weights(w_in, ssd_conv_w, ssd_conv_b, ssd_dt_bias, ssd_a_log, ssd_d, ssd_norm, gla_wa2, gla_ba, gla_norm)
    big = dict(norm_mix=norm_mix, w_branch=w_branch, w_out=w_out, norm_mem=norm_mem, w_mq=w_mq, w_mo=w_mo,
               norm_ffn=norm_ffn, w_ffn_in=w_ffn_in, ffn_conv_w=ffn_conv_w, ffn_conv_b=ffn_conv_b, w_ffn_out=w_ffn_out)
    pb, pl_len, _ = x_prompt.shape
    sb, sl_len, _ = x_sample.shape
    mem2d = mem_prompt.reshape(pb * MEM_TOKENS, D_MODEL)
    p_mem_k = _mem_kv(mem2d, w_mk, nb=pb)
    p_mem_v = _mem_kv(mem2d, w_mv, nb=pb)
    head_major = (0, 1, 3, 2, 4)

    y_p, p_ssd, p_ssd_conv, p_gla, p_ffn = _run_trunk(
        x_prompt.reshape(pb * pl_len, D_MODEL), p_mem_k, p_mem_v, None, None, None, None, wl, big, norm_final,
        bt=pb, L=pl_len, cfg=_group_cfg(pb, pl_len))
    y_s, s_ssd, s_ssd_conv, s_gla, s_ffn = _run_trunk(
        x_sample.reshape(sb * sl_len, D_MODEL),
        jnp.transpose(cache_mem_k, head_major), jnp.transpose(cache_mem_v, head_major),
        state_ssd, state_ssd_conv, state_gla, state_ffn_conv, wl, big, norm_final,
        bt=sb, L=sl_len, cfg=_group_cfg(sb, sl_len))
    return (y_p.reshape(pb, pl_len, D_MODEL), y_s.reshape(sb, sl_len, D_MODEL),
            p_ssd, p_ssd_conv, p_gla, p_ffn, jnp.transpose(p_mem_k, head_major), jnp.transpose(p_mem_v, head_major),
            s_ssd, s_ssd_conv, s_gla, s_ffn)
```

```python
import functools

import jax
import jax.numpy as jnp
from jax import lax
from jax.experimental import pallas as pl
from jax.experimental.pallas import tpu as pltpu

F32 = jnp.float32
BF16 = jnp.bfloat16

D_MODEL = 2048
DEPTH = 2
EPS = 1e-6
SSD_HEADS = 32
SSD_HEADDIM = 64
SSD_GROUPS = 4
SSD_STATE = 128
SSD_CONV = 4
SSD_CONV_DIM = 3072
GLA_HEADS = 4
GLA_HEAD_K = 256
GLA_HEAD_V = 512
GLA_RANK = 16
GLA_GATE_NORM = 16.0
GLA_CHUNK = 16
MEM_TOKENS = 256
MEM_HEADS = 4
MEM_HEAD_DIM = 512
D_FF = 5632
FFN_CONV = 3

LANES = 128
MXU_COLS = 256
SUBLANES = 8
VMEM_CAP_BYTES = 56 * 2**20

P_Z, P_XS, P_V, P_G, P_GATE, P_BC, P_Q, P_K, P_N = 0, 2048, 4096, 6144, 8192, 12288, 13312, 14336, 15360
SM_DT, SM_ALR = 0, 32

NT_DIMS = (((1,), (1,)), ((), ()))
TN_DIMS = (((0,), (0,)), ((), ()))
HI = lax.Precision.HIGHEST


def _vmem_limit(*block_bytes, scratch=0):
    need = 2 * sum(block_bytes) + scratch + (4 << 20)
    return int(min(max(need, 16 << 20), VMEM_CAP_BYTES))


def _nbytes(shape, dtype):
    n = 1
    for s in shape:
        n *= s
    return n * jnp.dtype(dtype).itemsize


def _silu(x):
    return x * jax.nn.sigmoid(x)


def _softplus(x):
    return jnp.maximum(x, 0.0) + jnp.log1p(jnp.exp(-jnp.abs(x)))


def _log_sigmoid(x):
    return -_softplus(-x)


def _rms(x, gain):
    ms = jnp.mean(x * x, axis=-1, keepdims=True)
    return x * lax.rsqrt(ms + EPS) * gain


def _split2(x):
    hi = x.astype(BF16)
    lo = (x - hi.astype(F32)).astype(BF16)
    return hi, lo


def _select_rows(sel, x):
    hi = x.astype(BF16)
    r1 = x - hi.astype(F32)
    mid = r1.astype(BF16)
    lo = (r1 - mid.astype(F32)).astype(BF16)
    return (jnp.dot(sel, hi, preferred_element_type=F32) + jnp.dot(sel, mid, preferred_element_type=F32)
            + jnp.dot(sel, lo, preferred_element_type=F32))


def _stacked_out(prev, shape, dtype):
    return jax.ShapeDtypeStruct(shape, dtype) if prev is None else jax.ShapeDtypeStruct(prev.shape, prev.dtype)


def _norm_mm_kernel(*refs, has_gain, has_small):
    it = iter(refs)
    x_ref = next(it)
    g_ref = next(it) if has_gain else None
    w_ref = next(it)
    ws_ref = next(it) if has_small else None
    o_ref = next(it)
    os_ref = next(it) if has_small else None
    xn_ref = next(it)

    @pl.when(pl.program_id(1) == 0)
    def _():
        x = x_ref[...].astype(F32)
        if has_gain:
            x = _rms(x, g_ref[...])
        xn_ref[...] = x.astype(BF16)
        if has_small:
            os_ref[...] = jnp.dot(xn_ref[...], ws_ref[...], preferred_element_type=F32)

    o_ref[...] = jnp.dot(xn_ref[...], w_ref[...].astype(BF16), preferred_element_type=F32).astype(o_ref.dtype)


def _norm_mm(x, gain, w, layer, w_small, *, tm, tn, out_dtype):
    m, k = x.shape
    n = w.shape[2]
    has_gain = gain is not None
    has_small = w_small is not None
    in_specs = [pl.BlockSpec((tm, k), lambda i, j: (i, 0))]
    args = [x]
    if has_gain:
        in_specs.append(pl.BlockSpec((None, 1, k), lambda i, j: (layer, 0, 0)))
        args.append(gain.reshape(DEPTH, 1, k))
    in_specs.append(pl.BlockSpec((None, k, tn), lambda i, j: (layer, 0, j)))
    args.append(w)
    out_shape = [jax.ShapeDtypeStruct((m, n), out_dtype)]
    out_specs = [pl.BlockSpec((tm, tn), lambda i, j: (i, j))]
    if has_small:
        in_specs.append(pl.BlockSpec((None, k, LANES), lambda i, j: (layer, 0, 0)))
        args.append(w_small)
        out_shape.append(jax.ShapeDtypeStruct((m, LANES), F32))
        out_specs.append(pl.BlockSpec((tm, LANES), lambda i, j: (i, 0)))
    limit = _vmem_limit(_nbytes((tm, k), x.dtype), _nbytes((k, tn), w.dtype), _nbytes((tm, tn), out_dtype),
                        _nbytes((k, LANES), BF16), _nbytes((tm, LANES), F32), scratch=_nbytes((tm, k), BF16))
    res = pl.pallas_call(
        functools.partial(_norm_mm_kernel, has_gain=has_gain, has_small=has_small),
        grid=(m // tm, n // tn),
        in_specs=in_specs,
        out_specs=out_specs,
        out_shape=out_shape,
        scratch_shapes=[pltpu.VMEM((tm, k), BF16)],
        compiler_params=pltpu.CompilerParams(dimension_semantics=("parallel", "arbitrary"),
                                             vmem_limit_bytes=limit),
        name="norm_mm",
    )(*args)
    return res if has_small else res[0]


def _mem_kv_kernel(x_ref, w_ref, o_ref, xb_ref):
    @pl.when((pl.program_id(0) == 0) & (pl.program_id(1) == 0))
    def _():
        xb_ref[...] = x_ref[...].astype(BF16)

    res = jnp.dot(xb_ref[...], w_ref[...].astype(BF16), preferred_element_type=F32)
    o_ref[...] = res.reshape(o_ref.shape)


def _mem_kv(x, w, *, nb):
    m, k = x.shape
    tn = MEM_HEAD_DIM
    limit = _vmem_limit(_nbytes((m, k), F32), _nbytes((k, tn), F32), _nbytes((m, tn), F32),
                        scratch=_nbytes((m, k), BF16))
    return pl.pallas_call(
        _mem_kv_kernel,
        grid=(DEPTH, MEM_HEADS),
        in_specs=[pl.BlockSpec((m, k), lambda d, h: (0, 0)),
                  pl.BlockSpec((None, k, tn), lambda d, h: (d, 0, h))],
        out_specs=pl.BlockSpec((None, nb, None, m // nb, tn), lambda d, h: (d, 0, h, 0, 0)),
        out_shape=jax.ShapeDtypeStruct((DEPTH, nb, MEM_HEADS, m // nb, tn), F32),
        scratch_shapes=[pltpu.VMEM((m, k), BF16)],
        compiler_params=pltpu.CompilerParams(dimension_semantics=("arbitrary", "arbitrary"),
                                             vmem_limit_bytes=limit),
        name="mem_kv",
    )(x, w)


def _norm_cast_kernel(x_ref, g_ref, xn_ref):
    xn_ref[...] = _rms(x_ref[...], g_ref[...]).astype(BF16)


def _norm_cast(x, gain, layer, *, tm):
    m, k = x.shape
    return pl.pallas_call(
        _norm_cast_kernel,
        grid=(m // tm,),
        in_specs=[pl.BlockSpec((tm, k), lambda i: (i, 0)), pl.BlockSpec((None, 1, k), lambda i: (layer, 0, 0))],
        out_specs=pl.BlockSpec((tm, k), lambda i: (i, 0)),
        out_shape=jax.ShapeDtypeStruct((m, k), BF16),
        compiler_params=pltpu.CompilerParams(dimension_semantics=("parallel",)),
        name="norm_cast",
    )(x, gain.reshape(DEPTH, 1, k))


def _norm_small_kernel(x_ref, g_ref, wst_ref, xn_ref, os_ref):
    xn = _rms(x_ref[...], g_ref[...]).astype(BF16)
    xn_ref[...] = xn
    os_ref[...] = lax.dot_general(xn, wst_ref[...].astype(BF16), NT_DIMS, preferred_element_type=F32)


def _norm_small(x, gain, w_small_t, layer, *, tm):
    m, k = x.shape
    return pl.pallas_call(
        _norm_small_kernel,
        grid=(m // tm,),
        in_specs=[pl.BlockSpec((tm, k), lambda i: (i, 0)),
                  pl.BlockSpec((None, 1, k), lambda i: (layer, 0, 0)),
                  pl.BlockSpec((None, LANES, k), lambda i: (layer, 0, 0))],
        out_specs=[pl.BlockSpec((tm, k), lambda i: (i, 0)), pl.BlockSpec((tm, LANES), lambda i: (i, 0))],
        out_shape=[jax.ShapeDtypeStruct((m, k), BF16), jax.ShapeDtypeStruct((m, LANES), F32)],
        compiler_params=pltpu.CompilerParams(dimension_semantics=("parallel",)),
        name="norm_small",
    )(x, gain.reshape(DEPTH, 1, k), w_small_t)


PROJ_TN = 1024
PROJ_SRC = (0, 1024, 2048, 3072, 7200, 8224, 9248, 10272, 11312, 12336, 13360, 14384, 4096, 5152, 6176)


def _proj_kernel(src_ref, xn_ref, wt_ref, o_ref, wbf_ref):
    @pl.when(pl.program_id(1) == 0)
    def _():
        wbf_ref[...] = wt_ref[0].astype(BF16)

    o_ref[...] = lax.dot_general(xn_ref[...], wbf_ref[...], NT_DIMS, preferred_element_type=F32).astype(o_ref.dtype)


def _proj_in(xn, w_in_t, layer, *, tm):
    m, k = xn.shape
    tn = PROJ_TN
    nj = len(PROJ_SRC)
    assert all(s % SUBLANES == 0 for s in PROJ_SRC)
    src = jnp.array([s // SUBLANES for s in PROJ_SRC], jnp.int32)
    limit = _vmem_limit(_nbytes((tm, k), BF16), _nbytes((tn, k), F32), _nbytes((tm, tn), BF16),
                        scratch=_nbytes((tn, k), BF16))
    grid_spec = pltpu.PrefetchScalarGridSpec(
        num_scalar_prefetch=1,
        grid=(nj, m // tm),
        in_specs=[pl.BlockSpec((tm, k), lambda j, i, s: (i, 0)),
                  pl.BlockSpec((pl.Element(1), pl.Element(tn), pl.Element(k)),
                               lambda j, i, s: (layer, s[j] * SUBLANES, 0))],
        out_specs=pl.BlockSpec((tm, tn), lambda j, i, s: (i, j)),
        scratch_shapes=[pltpu.VMEM((tn, k), BF16)])
    return pl.pallas_call(
        _proj_kernel,
        grid_spec=grid_spec,
        out_shape=jax.ShapeDtypeStruct((m, nj * tn), BF16),
        compiler_params=pltpu.CompilerParams(dimension_semantics=("arbitrary", "arbitrary"),
                                             vmem_limit_bytes=limit),
        name="proj_in",
    )(src, xn, w_in_t)


def _mm_res_kernel(x_ref, w_ref, r_ref, o_ref, wbf_ref):
    @pl.when(pl.program_id(1) == 0)
    def _():
        wbf_ref[...] = w_ref[...].astype(BF16)

    o_ref[...] = r_ref[...] + jnp.dot(x_ref[...], wbf_ref[...], preferred_element_type=F32)


def _mm_res(x, w, layer, res, *, tm, tn):
    m, k = x.shape
    n = w.shape[2]
    limit = _vmem_limit(_nbytes((tm, k), BF16), _nbytes((k, tn), F32), 2 * _nbytes((tm, tn), F32),
                        scratch=_nbytes((k, tn), BF16))
    return pl.pallas_call(
        _mm_res_kernel,
        grid=(n // tn, m // tm),
        in_specs=[pl.BlockSpec((tm, k), lambda j, i: (i, 0)),
                  pl.BlockSpec((None, k, tn), lambda j, i: (layer, 0, j)),
                  pl.BlockSpec((tm, tn), lambda j, i: (i, j))],
        out_specs=pl.BlockSpec((tm, tn), lambda j, i: (i, j)),
        out_shape=jax.ShapeDtypeStruct((m, n), F32),
        scratch_shapes=[pltpu.VMEM((k, tn), BF16)],
        compiler_params=pltpu.CompilerParams(dimension_semantics=("parallel", "arbitrary"),
                                             vmem_limit_bytes=limit),
        name="mm_res",
    )(x, w, res)


def _merge_kernel(y0_ref, y1_ref, g0_ref, g1_ref, w_ref, o_ref, wbf_ref):
    @pl.when(pl.program_id(1) == 0)
    def _():
        wbf_ref[...] = w_ref[...].astype(BF16)

    b0 = jnp.dot(y0_ref[...], wbf_ref[0], preferred_element_type=F32)
    b1 = jnp.dot(y1_ref[...], wbf_ref[1], preferred_element_type=F32)
    g0 = jax.nn.sigmoid(g0_ref[...].astype(F32))
    g1 = jax.nn.sigmoid(g1_ref[...].astype(F32))
    o_ref[...] = (g0 * b0 + g1 * b1).astype(o_ref.dtype)


def _merge(y_ssd, y_gla, proj, w_branch, layer, *, tm, tn):
    m, k = y_ssd.shape
    n = D_MODEL
    gb = P_GATE // tn
    nb = n // tn
    limit = _vmem_limit(2 * _nbytes((tm, k), BF16), 2 * _nbytes((k, tn), F32), 3 * _nbytes((tm, tn), BF16),
                        scratch=2 * _nbytes((k, tn), BF16))
    return pl.pallas_call(
        _merge_kernel,
        grid=(nb, m // tm),
        in_specs=[pl.BlockSpec((tm, k), lambda j, i: (i, 0)),
                  pl.BlockSpec((tm, k), lambda j, i: (i, 0)),
                  pl.BlockSpec((tm, tn), lambda j, i: (i, gb + j)),
                  pl.BlockSpec((tm, tn), lambda j, i: (i, gb + nb + j)),
                  pl.BlockSpec((None, 2, k, tn), lambda j, i: (layer, 0, 0, j))],
        out_specs=pl.BlockSpec((tm, tn), lambda j, i: (i, j)),
        out_shape=jax.ShapeDtypeStruct((m, n), BF16),
        scratch_shapes=[pltpu.VMEM((2, k, tn), BF16)],
        compiler_params=pltpu.CompilerParams(dimension_semantics=("parallel", "arbitrary"),
                                             vmem_limit_bytes=limit),
        name="merge",
    )(y_ssd, y_gla, proj, proj, w_branch)


def _final_norm_kernel(x_ref, g_ref, o_ref):
    o_ref[...] = _rms(x_ref[...], g_ref[...])


def _final_norm(x, gain, *, tm):
    m, k = x.shape
    return pl.pallas_call(
        _final_norm_kernel,
        grid=(m // tm,),
        in_specs=[pl.BlockSpec((tm, k), lambda i: (i, 0)), pl.BlockSpec((1, k), lambda i: (0, 0))],
        out_specs=pl.BlockSpec((tm, k), lambda i: (i, 0)),
        out_shape=jax.ShapeDtypeStruct((m, k), F32),
        compiler_params=pltpu.CompilerParams(dimension_semantics=("parallel",)),
        name="final_norm",
    )(x, gain.reshape(1, k))


def _ffn_up_kernel(*refs, tm, tn, seq_tiles, seg):
    short = seg > 0
    it = iter(refs)
    xn_ref, wu_ref, wt_ref, cwu_ref, cwt_ref, cbu_ref, cbt_ref = (next(it) for _ in range(7))
    stu_ref = next(it) if short else None
    stt_ref = next(it) if short else None
    act_ref, cnu_ref, cnt_ref = next(it), next(it), next(it)
    wbu_ref, wbt_ref = next(it), next(it)
    tailu_ref = None if short else next(it)
    tailt_ref = None if short else next(it)
    i = pl.program_id(1)
    sl = SUBLANES
    hw = FFN_CONV - 1

    @pl.when(i == 0)
    def _():
        wbu_ref[...] = wu_ref[...].astype(BF16)
        wbt_ref[...] = wt_ref[...].astype(BF16)
        if not short:
            tailu_ref[...] = jnp.zeros_like(tailu_ref)
            tailt_ref[...] = jnp.zeros_like(tailt_ref)

    xn = xn_ref[...]
    u_raw = jnp.dot(xn, wbu_ref[...], preferred_element_type=F32)
    t_raw = jnp.dot(xn, wbt_ref[...], preferred_element_type=F32)

    if short:
        nseq = tm // seg
        r = lax.broadcasted_iota(jnp.int32, (tm, hw * nseq), 0)
        c = lax.broadcasted_iota(jnp.int32, (tm, hw * nseq), 1)
        halo_u, halo_t = [], []
        for s in range(1, FFN_CONV):
            sel = (((r % seg) < s) & (c == hw * (r // seg) + hw - s + (r % seg))).astype(BF16)
            halo_u.append(_select_rows(sel, stu_ref[...]))
            halo_t.append(_select_rows(sel, stt_ref[...]))
        ro = lax.broadcasted_iota(jnp.int32, (hw * nseq, tm), 0)
        co = lax.broadcasted_iota(jnp.int32, (hw * nseq, tm), 1)
        sel_out = (co == (ro // hw) * seg + seg - hw + (ro % hw)).astype(BF16)
        cnu_ref[...] = _select_rows(sel_out, u_raw)
        cnt_ref[...] = _select_rows(sel_out, t_raw)
        rowmod = lax.broadcasted_iota(jnp.int32, (tm, tn), 0) % seg
        prev_u = prev_t = None
    else:
        start = (i % seq_tiles) == 0
        prev_u = jnp.where(start, 0.0, tailu_ref[...])
        prev_t = jnp.where(start, 0.0, tailt_ref[...])
        tailu_ref[...] = u_raw[tm - sl:tm, :]
        tailt_ref[...] = t_raw[tm - sl:tm, :]
        cnu_ref[...] = u_raw[tm - sl:tm, :]
        cnt_ref[...] = t_raw[tm - sl:tm, :]
        row8 = lax.broadcasted_iota(jnp.int32, (sl, tn), 0)
        halo_u = halo_t = rowmod = None

    def conv(x, prev8, cw_ref, cb_ref, halos):
        y = x * cw_ref[hw:hw + 1, :]
        for s in range(1, FFN_CONV):
            rolled = pltpu.roll(x, s, axis=0)
            if short:
                shifted = jnp.where(rowmod < s, halos[s - 1], rolled)
            else:
                head = jnp.where(row8 < s, pltpu.roll(prev8, s, axis=0), rolled[0:sl, :])
                shifted = jnp.concatenate([head, rolled[sl:, :]], axis=0)
            y = y + shifted * cw_ref[hw - s:hw - s + 1, :]
        return y + cb_ref[...]

    u = conv(u_raw, prev_u, cwu_ref, cbu_ref, halo_u)
    t = conv(t_raw, prev_t, cwt_ref, cbt_ref, halo_t)
    act_ref[...] = (_silu(t) * u).astype(act_ref.dtype)


def _ffn_up(xn, w, conv_w, conv_b, conv0, layer, *, L, tm, tn):
    m, k = xn.shape
    bt = m // L
    nj = D_FF // tn
    hw = FFN_CONV - 1
    short = L < tm
    seg = L if short else 0
    seq_tiles = 1 if short else L // tm
    assert (conv0 is not None) == short, "history rows are only supported for sequences shorter than a block"
    in_specs = [
        pl.BlockSpec((tm, k), lambda j, i: (i, 0)),
        pl.BlockSpec((None, k, tn), lambda j, i: (layer, 0, j)),
        pl.BlockSpec((None, k, tn), lambda j, i: (layer, 0, nj + j)),
        pl.BlockSpec((None, FFN_CONV, tn), lambda j, i: (layer, 0, j)),
        pl.BlockSpec((None, FFN_CONV, tn), lambda j, i: (layer, 0, nj + j)),
        pl.BlockSpec((None, 1, tn), lambda j, i: (layer, 0, j)),
        pl.BlockSpec((None, 1, tn), lambda j, i: (layer, 0, nj + j)),
    ]
    args = [xn, w, w, conv_w, conv_w, conv_b.reshape(DEPTH, 1, 2 * D_FF), conv_b.reshape(DEPTH, 1, 2 * D_FF)]
    scratch = [pltpu.VMEM((k, tn), BF16)] * 2
    if short:
        nst = (m // tm) * (tm // L) * hw
        st2d = conv0.reshape(DEPTH, nst, 2 * D_FF)
        rows = (tm // L) * hw
        in_specs += [pl.BlockSpec((None, rows, tn), lambda j, i: (layer, i, j)),
                     pl.BlockSpec((None, rows, tn), lambda j, i: (layer, i, nj + j))]
        args += [st2d, st2d]
        cn_shape = jax.ShapeDtypeStruct((nst, D_FF), F32)
        cn_spec = pl.BlockSpec((rows, tn), lambda j, i: (i, j))
    else:
        scratch += [pltpu.VMEM((SUBLANES, tn), F32)] * 2
        cn_shape = jax.ShapeDtypeStruct((m // tm, SUBLANES, D_FF), F32)
        cn_spec = pl.BlockSpec((None, SUBLANES, tn), lambda j, i: (i, 0, j))
    limit = _vmem_limit(_nbytes((tm, k), BF16), 2 * _nbytes((k, tn), F32), _nbytes((tm, tn), BF16),
                        scratch=2 * _nbytes((k, tn), BF16) + 6 * _nbytes((tm, tn), F32))
    act, cnu, cnt = pl.pallas_call(
        functools.partial(_ffn_up_kernel, tm=tm, tn=tn, seq_tiles=seq_tiles, seg=seg),
        grid=(nj, m // tm),
        in_specs=in_specs,
        out_specs=[pl.BlockSpec((tm, tn), lambda j, i: (i, j)), cn_spec, cn_spec],
        out_shape=[jax.ShapeDtypeStruct((m, D_FF), BF16), cn_shape, cn_shape],
        scratch_shapes=scratch,
        compiler_params=pltpu.CompilerParams(dimension_semantics=("arbitrary", "arbitrary"),
                                             vmem_limit_bytes=limit),
        name="ffn_up",
    )(*args)
    if short:
        conv_new = jnp.concatenate([cnu, cnt], axis=-1).reshape(bt, hw, 2 * D_FF)
    else:
        last = slice(seq_tiles - 1, None, seq_tiles)
        conv_new = jnp.concatenate([cnu[last, SUBLANES - hw:, :], cnt[last, SUBLANES - hw:, :]], axis=-1)
    return act, conv_new


def _causal_conv(x, tail_ref, w_ref, b, width):
    row = lax.broadcasted_iota(jnp.int32, x.shape, 0)
    y = x * w_ref[width - 1:width, :]
    for s in range(1, width):
        shifted = pltpu.roll(x, s, axis=0)
        for t in range(s):
            shifted = jnp.where(row == t, tail_ref[8 + t - s:9 + t - s, :], shifted)
        y = y + shifted * w_ref[width - 1 - s:width - s, :]
    return y + b


def _ssd_kernel(*refs, Q, zero_init, aliased):
    it = iter(refs)
    z_ref, xs_ref, bc_ref, sm_ref = (next(it) for _ in range(4))
    conv0_ref = None if zero_init else next(it)
    h0_ref = None if zero_init else next(it)
    cwx_ref, cwb_ref, cbx_ref, cbb_ref, dtb_ref, a_ref, dx_ref, nrm_ref, exp_ref = (next(it) for _ in range(9))
    if aliased:
        next(it), next(it)
    y_ref, h_out, conv_out, tailx_ref, tailb_ref = (next(it) for _ in range(5))
    h_ref = h_out if aliased else h_out.at[0]
    convout_ref = conv_out if aliased else conv_out.at[0]
    c = pl.program_id(1)
    hp = SSD_HEADDIM
    gw = SSD_HEADS // SSD_GROUPS * hp

    @pl.when(c == 0)
    def _():
        if not aliased:
            h_out[1:] = jnp.zeros((DEPTH - 1,) + h_ref.shape, F32)
            conv_out[1:] = jnp.zeros((DEPTH - 1,) + convout_ref.shape, F32)
        tailx_ref[...] = jnp.zeros_like(tailx_ref)
        tailb_ref[...] = jnp.zeros_like(tailb_ref)
        if zero_init:
            h_ref[...] = jnp.zeros_like(h_ref)
        else:
            h_ref[...] = h0_ref[...]
            tailx_ref[5:8, :] = conv0_ref[:, 0:D_MODEL]
            tailb_ref[5:8, :] = conv0_ref[:, D_MODEL:SSD_CONV_DIM]

    xs_raw = xs_ref[...].astype(F32)
    bc_raw = bc_ref[...].astype(F32)
    xs = _silu(_causal_conv(xs_raw, tailx_ref, cwx_ref, cbx_ref[...], SSD_CONV))
    bc = _silu(_causal_conv(bc_raw, tailb_ref, cwb_ref, cbb_ref[...], SSD_CONV))
    tailx_ref[...] = xs_raw[Q - 8:Q, :]
    tailb_ref[...] = bc_raw[Q - 8:Q, :]

    @pl.when(c == pl.num_programs(1) - 1)
    def _():
        convout_ref[:, 0:D_MODEL] = tailx_ref[5:8, :]
        convout_ref[:, D_MODEL:SSD_CONV_DIM] = tailb_ref[5:8, :]

    dt = _softplus(sm_ref[...] + dtb_ref[...])
    adt = dt * a_ref[...]
    ri = lax.broadcasted_iota(jnp.int32, (Q, Q), 0)
    ci = lax.broadcasted_iota(jnp.int32, (Q, Q), 1)
    causal = ci <= ri
    acum = jnp.dot(causal.astype(F32), adt, precision=HI, preferred_element_type=F32)
    eye = (lax.broadcasted_iota(jnp.int32, (LANES, LANES), 0)
           == lax.broadcasted_iota(jnp.int32, (LANES, LANES), 1)).astype(F32)
    acum_t = lax.dot_general(eye, acum, NT_DIMS, precision=HI, preferred_element_type=F32)
    dt_t = lax.dot_general(eye, dt, NT_DIMS, precision=HI, preferred_element_type=F32)
    a_last = acum[Q - 1:Q, :]
    ea = jnp.exp(acum)
    te = jnp.exp(a_last - acum) * dt
    dec_rows = jnp.broadcast_to(jnp.exp(acum_t[:, Q - 1:Q]), (LANES, LANES))
    ea_hi, ea_lo = _split2(ea)
    te_hi, te_lo = _split2(te)
    expand = exp_ref[...]
    ea_x = (jnp.dot(ea_hi, expand, preferred_element_type=F32)
            + jnp.dot(ea_lo, expand, preferred_element_type=F32))
    te_x = (jnp.dot(te_hi, expand, preferred_element_type=F32)
            + jnp.dot(te_lo, expand, preferred_element_type=F32))
    lane = lax.broadcasted_iota(jnp.int32, (Q, LANES), 1)

    for g in range(SSD_GROUPS):
        gl = slice(g * gw, (g + 1) * gw)
        b_g = bc[:, g * SSD_STATE:(g + 1) * SSD_STATE].astype(BF16)
        c_g = bc[:, (SSD_GROUPS + g) * SSD_STATE:(SSD_GROUPS + g + 1) * SSD_STATE].astype(BF16)
        cb = lax.dot_general(c_g, b_g, NT_DIMS, preferred_element_type=F32)
        h_g = h_ref[8 * g:8 * g + 8].reshape(gw, SSD_STATE)
        y_off = lax.dot_general(c_g, h_g.astype(BF16), NT_DIMS, preferred_element_type=F32)
        x_g = xs[:, gl]
        pairs = []
        for p in range(4):
            x_p = x_g[:, p * LANES:(p + 1) * LANES]
            acc = None
            for s in range(2):
                hh = g * 8 + p * 2 + s
                seg = jnp.broadcast_to(acum[:, hh:hh + 1], (Q, Q)) - jnp.broadcast_to(acum_t[hh:hh + 1, :], (Q, Q))
                decay = jnp.where(causal, jnp.exp(seg), 0.0)
                w_h = (cb * decay * dt_t[hh:hh + 1, :]).astype(BF16)
                x_m = jnp.where((lane // hp) == s, x_p, 0.0).astype(BF16)
                r = jnp.dot(w_h, x_m, preferred_element_type=F32)
                acc = r if acc is None else acc + r
            pairs.append(acc)
        y_g = jnp.concatenate(pairs, axis=1) + y_off * ea_x[:, gl] + dx_ref[:, gl] * x_g
        x_t = (x_g * te_x[:, gl]).astype(BF16)
        upd = lax.dot_general(x_t, b_g, TN_DIMS, preferred_element_type=F32)
        for h in range(8):
            hh = g * 8 + h
            h_ref[hh] = h_ref[hh] * dec_rows[hh:hh + 1, :] + upd[h * hp:(h + 1) * hp, :]
        z_g = z_ref[:, gl].astype(F32)
        y_ref[:, gl] = _rms(y_g * _silu(z_g), nrm_ref[:, gl]).astype(y_ref.dtype)


def _ssd(proj, small, conv0, h0, wl, layer, prev_h, prev_conv, *, bt, L, Q):
    zero_init = h0 is None
    aliased = prev_h is not None
    nc = L // Q
    T = bt * L
    full2 = lambda b, c: (0, 0)
    lay3 = lambda b, c: (layer, 0, 0)
    row = lambda col: (lambda b, c: (b * nc + c, col))
    hshape = (DEPTH, bt, SSD_HEADS, SSD_HEADDIM, SSD_STATE)
    cshape = (DEPTH, bt, SSD_CONV - 1, SSD_CONV_DIM)
    hspec = pl.BlockSpec((None, None, SSD_HEADS, SSD_HEADDIM, SSD_STATE), lambda b, c: (layer, b, 0, 0, 0))
    cspec = pl.BlockSpec((None, None, SSD_CONV - 1, SSD_CONV_DIM), lambda b, c: (layer, b, 0, 0))
    h_out_spec = hspec if aliased else pl.BlockSpec((DEPTH, None, SSD_HEADS, SSD_HEADDIM, SSD_STATE),
                                                    lambda b, c: (0, b, 0, 0, 0))
    c_out_spec = cspec if aliased else pl.BlockSpec((DEPTH, None, SSD_CONV - 1, SSD_CONV_DIM),
                                                    lambda b, c: (0, b, 0, 0))
    in_specs = [
        pl.BlockSpec((Q, D_MODEL), row(P_Z // D_MODEL)),
        pl.BlockSpec((Q, D_MODEL), row(P_XS // D_MODEL)),
        pl.BlockSpec((Q, 1024), row(P_BC // 1024)),
        pl.BlockSpec((Q, LANES), row(0)),
    ]
    args = [proj, proj, proj, small]
    if not zero_init:
        in_specs += [cspec, hspec]
        args += [conv0, h0]
    in_specs += [
        pl.BlockSpec((None, SSD_CONV, D_MODEL), lay3),
        pl.BlockSpec((None, SSD_CONV, 1024), lambda b, c: (layer, 0, D_MODEL // 1024)),
        pl.BlockSpec((None, 1, D_MODEL), lay3),
        pl.BlockSpec((None, 1, 1024), lambda b, c: (layer, 0, D_MODEL // 1024)),
        pl.BlockSpec((None, 1, LANES), lay3),
        pl.BlockSpec((None, 1, LANES), lay3),
        pl.BlockSpec((None, 1, D_MODEL), lay3),
        pl.BlockSpec((None, 1, D_MODEL), lay3),
        pl.BlockSpec((LANES, D_MODEL), full2),
    ]
    args += [wl["ssd_conv_w"], wl["ssd_conv_w"], wl["ssd_conv_b"], wl["ssd_conv_b"], wl["ssd_dtb"], wl["ssd_a"],
             wl["ssd_dx"], wl["ssd_norm"], wl["expand"]]
    aliases = {}
    if aliased:
        aliases = {len(args): 1, len(args) + 1: 2}
        in_specs += [pl.BlockSpec(memory_space=pl.ANY), pl.BlockSpec(memory_space=pl.ANY)]
        args += [prev_h, prev_conv]
    return pl.pallas_call(
        functools.partial(_ssd_kernel, Q=Q, zero_init=zero_init, aliased=aliased),
        grid=(bt, nc),
        in_specs=in_specs,
        out_specs=[pl.BlockSpec((Q, D_MODEL), row(0)), h_out_spec, c_out_spec],
        out_shape=[jax.ShapeDtypeStruct((T, D_MODEL), BF16), _stacked_out(prev_h, hshape, F32),
                   _stacked_out(prev_conv, cshape, F32)],
        scratch_shapes=[pltpu.VMEM((8, D_MODEL), F32), pltpu.VMEM((8, 1024), F32)],
        input_output_aliases=aliases,
        compiler_params=pltpu.CompilerParams(dimension_semantics=("parallel", "arbitrary"),
                                             vmem_limit_bytes=48 << 20),
        name="ssd",
    )(*args)


def _gla_kernel(*refs, TB, zero_init, aliased):
    it = iter(refs)
    q_ref, k_ref, v_ref, g_ref, sm_ref = (next(it) for _ in range(5))
    s0_ref = None if zero_init else next(it)
    wa_ref, ba_ref, wat_ref, bat_ref, gn_ref = (next(it) for _ in range(5))
    if aliased:
        next(it)
    y_ref, s_out = next(it), next(it)
    s_ref = s_out if aliased else s_out.at[0]
    c = pl.program_id(1)
    ck = GLA_CHUNK
    ns = TB // ck
    hk, hv = GLA_HEAD_K, GLA_HEAD_V

    @pl.when(c == 0)
    def _():
        if not aliased:
            s_out[1:] = jnp.zeros((DEPTH - 1,) + s_ref.shape, F32)
        if zero_init:
            s_ref[...] = jnp.zeros_like(s_ref)
        else:
            s_ref[...] = s0_ref[...]

    smb = sm_ref[...].astype(BF16)
    la = _log_sigmoid(jnp.dot(smb, wa_ref[...], preferred_element_type=F32) + ba_ref[...]) * (1.0 / GLA_GATE_NORM)
    rb = lax.broadcasted_iota(jnp.int32, (TB, TB), 0)
    cb = lax.broadcasted_iota(jnp.int32, (TB, TB), 1)
    same = (rb // ck) == (cb // ck)
    msel = jnp.concatenate([same & (cb <= rb), same, (cb // ck) < (rb // ck)], axis=0).astype(BF16)
    sums = _select_rows(msel, la)
    bcum, tot, bprev = sums[0:TB], sums[TB:2 * TB], sums[2 * TB:3 * TB]
    kf = k_ref[...].astype(F32)
    qd = q_ref[...].astype(F32) * (hk ** -0.5) * jnp.exp(bcum)
    qd_b = qd.astype(BF16)
    qs_b = (qd * jnp.exp(bprev)).astype(BF16)
    ki_b = (kf * jnp.exp(-bcum)).astype(BF16)
    ke_b = (kf * jnp.exp(tot - bcum)).astype(BF16)
    la_t = _log_sigmoid(lax.dot_general(wat_ref[...], smb, NT_DIMS, preferred_element_type=F32)
                        + bat_ref[:, 0:1]) * (1.0 / GLA_GATE_NORM)
    rs = lax.broadcasted_iota(jnp.int32, (TB, LANES), 0)
    cs = lax.broadcasted_iota(jnp.int32, (TB, LANES), 1)
    selc = (((rs // ck) == cs) | (cs == ns)).astype(BF16)
    la_hi = la_t.astype(BF16)
    la_r = la_t - la_hi.astype(F32)
    la_mid = la_r.astype(BF16)
    la_lo = (la_r - la_mid.astype(F32)).astype(BF16)
    dec_t = jnp.exp(jnp.dot(la_hi, selc, preferred_element_type=F32) + jnp.dot(la_mid, selc, preferred_element_type=F32)
                    + jnp.dot(la_lo, selc, preferred_element_type=F32))
    eye = (lax.broadcasted_iota(jnp.int32, (hk, hk), 0) == lax.broadcasted_iota(jnp.int32, (hk, hk), 1)).astype(BF16)
    slot = lax.broadcasted_iota(jnp.int32, (hk, TB), 1) // ck
    row16 = lax.broadcasted_iota(jnp.int32, (ck, TB), 0)
    col16 = lax.broadcasted_iota(jnp.int32, (ck, TB), 1)

    for h in range(GLA_HEADS):
        kl = slice(h * hk, (h + 1) * hk)
        vl = slice(h * hv, (h + 1) * hv)
        ke_t = lax.dot_general(eye, ke_b[:, kl], NT_DIMS, preferred_element_type=F32)
        ki_t = lax.dot_general(eye, ki_b[:, kl], NT_DIMS, preferred_element_type=F32)
        v_h = v_ref[:, vl]
        kbuf = jnp.zeros((hk, TB), F32)
        att = []
        for cc in range(ns):
            in_c = slot == cc
            kall = jnp.where(in_c, ki_t, kbuf).astype(BF16)
            a = jnp.dot(qd_b[cc * ck:(cc + 1) * ck, kl], kall, preferred_element_type=F32)
            att.append(jnp.where(col16 <= row16 + cc * ck, a, 0.0))
            kbuf = jnp.where(in_c, ke_t, kbuf * dec_t[kl, cc:cc + 1])
        a_full = jnp.concatenate(att, axis=0).astype(BF16)
        s_old = s_ref[h]
        o = (jnp.dot(a_full, v_h, preferred_element_type=F32)
             + jnp.dot(qs_b[:, kl], s_old.astype(BF16), preferred_element_type=F32))
        y_ref[:, vl] = (_rms(o, gn_ref[...]) * _silu(g_ref[:, vl].astype(F32))).astype(y_ref.dtype)
        s_ref[h] = s_old * dec_t[kl, ns:ns + 1] + jnp.dot(kbuf.astype(BF16), v_h, preferred_element_type=F32)


def _gla(proj, small, s0, wl, layer, prev_s, *, bt, L, TB):
    zero_init = s0 is None
    aliased = prev_s is not None
    nc = L // TB
    T = bt * L
    lay3 = lambda b, c: (layer, 0, 0)
    row = lambda col: (lambda b, c: (b * nc + c, col))
    kd = GLA_HEADS * GLA_HEAD_K
    sshape = (DEPTH, bt, GLA_HEADS, GLA_HEAD_K, GLA_HEAD_V)
    sspec = pl.BlockSpec((None, None, GLA_HEADS, GLA_HEAD_K, GLA_HEAD_V), lambda b, c: (layer, b, 0, 0, 0))
    s_out_spec = sspec if aliased else pl.BlockSpec((DEPTH, None, GLA_HEADS, GLA_HEAD_K, GLA_HEAD_V),
                                                     lambda b, c: (0, b, 0, 0, 0))
    in_specs = [
        pl.BlockSpec((TB, kd), row(P_Q // kd)),
        pl.BlockSpec((TB, kd), row(P_K // kd)),
        pl.BlockSpec((TB, D_MODEL), row(P_V // D_MODEL)),
        pl.BlockSpec((TB, D_MODEL), row(P_G // D_MODEL)),
        pl.BlockSpec((TB, LANES), row(0)),
    ]
    args = [proj, proj, proj, proj, small]
    if not zero_init:
        in_specs.append(sspec)
        args.append(s0)
    in_specs += [
        pl.BlockSpec((None, LANES, kd), lay3),
        pl.BlockSpec((None, 1, kd), lay3),
        pl.BlockSpec((None, kd, LANES), lay3),
        pl.BlockSpec((None, kd, LANES), lay3),
        pl.BlockSpec((None, 1, GLA_HEAD_V), lay3),
    ]
    args += [wl["gla_wa"], wl["gla_ba"], wl["gla_wa_t"], wl["gla_ba_t"], wl["gla_norm"]]
    aliases = {}
    if aliased:
        aliases = {len(args): 1}
        in_specs.append(pl.BlockSpec(memory_space=pl.ANY))
        args.append(prev_s)
    return pl.pallas_call(
        functools.partial(_gla_kernel, TB=TB, zero_init=zero_init, aliased=aliased),
        grid=(bt, nc),
        in_specs=in_specs,
        out_specs=[pl.BlockSpec((TB, D_MODEL), row(0)), s_out_spec],
        out_shape=[jax.ShapeDtypeStruct((T, D_MODEL), BF16), _stacked_out(prev_s, sshape, F32)],
        input_output_aliases=aliases,
        compiler_params=pltpu.CompilerParams(dimension_semantics=("parallel", "arbitrary"),
                                             vmem_limit_bytes=48 << 20),
        name="gla",
    )(*args)


def _attn_kernel(q_ref, k_ref, v_ref, o_ref, kb_ref, vb_ref):
    @pl.when(pl.program_id(1) == 0)
    def _():
        kb_ref[...] = k_ref[...].astype(BF16)
        vb_ref[...] = v_ref[...].astype(BF16)

    hd = MEM_HEAD_DIM
    for h in range(MEM_HEADS):
        hl = slice(h * hd, (h + 1) * hd)
        s = lax.dot_general(q_ref[:, hl], kb_ref[h], NT_DIMS, preferred_element_type=F32) * (hd ** -0.5)
        e = jnp.exp(s - jnp.max(s, axis=-1, keepdims=True))
        p = e / jnp.sum(e, axis=-1, keepdims=True)
        o_ref[:, hl] = jnp.dot(p.astype(BF16), vb_ref[h], preferred_element_type=F32).astype(o_ref.dtype)


def _attn(q, mem_k, mem_v, layer, *, bt, L, tl):
    nl = L // tl
    T = bt * L
    kvshape = (MEM_HEADS, MEM_TOKENS, MEM_HEAD_DIM)
    kvspec = pl.BlockSpec((None, None) + kvshape, lambda b, l: (layer, b, 0, 0, 0))
    return pl.pallas_call(
        _attn_kernel,
        grid=(bt, nl),
        in_specs=[pl.BlockSpec((tl, D_MODEL), lambda b, l: (b * nl + l, 0)), kvspec, kvspec],
        out_specs=pl.BlockSpec((tl, D_MODEL), lambda b, l: (b * nl + l, 0)),
        out_shape=jax.ShapeDtypeStruct((T, D_MODEL), BF16),
        scratch_shapes=[pltpu.VMEM(kvshape, BF16), pltpu.VMEM(kvshape, BF16)],
        compiler_params=pltpu.CompilerParams(dimension_semantics=("parallel", "arbitrary"),
                                             vmem_limit_bytes=40 << 20),
        name="mem_attn",
    )(q, mem_k, mem_v)


def _prep_weights(w_in, ssd_conv_w, ssd_conv_b, ssd_dt_bias, ssd_a_log, ssd_d, ssd_norm, gla_wa2, gla_ba, gla_norm):
    w_in_t = jnp.swapaxes(w_in, 1, 2)
    o_dt, o_alr = 5120, 11296
    pad_sm = LANES - SSD_HEADS - GLA_RANK
    w_small_t = jnp.concatenate(
        [w_in_t[:, o_dt:o_dt + SSD_HEADS, :], w_in_t[:, o_alr:o_alr + GLA_RANK, :],
         jnp.zeros((DEPTH, pad_sm, D_MODEL), F32)], axis=1)
    pad_h = LANES - SSD_HEADS
    kd = GLA_HEADS * GLA_HEAD_K
    wa = jnp.concatenate([jnp.zeros((DEPTH, SM_ALR, kd), F32), gla_wa2,
                          jnp.zeros((DEPTH, LANES - SM_ALR - GLA_RANK, kd), F32)], axis=1).astype(BF16)
    expand = (jnp.arange(D_MODEL)[None, :] // SSD_HEADDIM == jnp.arange(LANES)[:, None]).astype(BF16)
    return dict(
        w_in_t=w_in_t, w_small_t=w_small_t,
        ssd_conv_w=ssd_conv_w, ssd_conv_b=ssd_conv_b[:, None, :],
        ssd_dtb=jnp.pad(ssd_dt_bias, ((0, 0), (0, pad_h)))[:, None, :],
        ssd_a=jnp.pad(-jnp.exp(ssd_a_log), ((0, 0), (0, pad_h)))[:, None, :],
        ssd_dx=jnp.repeat(ssd_d, SSD_HEADDIM, axis=1)[:, None, :],
        ssd_norm=ssd_norm[:, None, :], expand=expand,
        gla_wa=wa, gla_ba=gla_ba[:, None, :], gla_wa_t=jnp.swapaxes(wa, 1, 2),
        gla_ba_t=jnp.broadcast_to(gla_ba[:, :, None], (DEPTH, kd, LANES)),
        gla_norm=gla_norm[:, None, :])


def _run_trunk(x, mem_k, mem_v, st_ssd, st_ssd_conv, st_gla, st_ffn_conv, wl, big, norm_final, *, bt, L, cfg):
    tm = cfg["tm"]
    n_h = n_c = n_s = None
    new_ffn = []
    for i in range(DEPTH):
        xn, small = _norm_small(x, big["norm_mix"], wl["w_small_t"], i, tm=cfg["tm_norm"])
        proj = _proj_in(xn, wl["w_in_t"], i, tm=tm)
        y_ssd, n_h, n_c = _ssd(proj, small, st_ssd_conv, st_ssd, wl, i, n_h, n_c, bt=bt, L=L, Q=cfg["ssd_q"])
        y_gla, n_s = _gla(proj, small, st_gla, wl, i, n_s, bt=bt, L=L, TB=cfg["gla_tb"])
        merged = _merge(y_ssd, y_gla, proj, big["w_branch"], i, tm=tm, tn=512)
        x = _mm_res(merged, big["w_out"], i, x, tm=tm, tn=1024)
        q = _norm_mm(x, big["norm_mem"], big["w_mq"], i, None, tm=tm, tn=512, out_dtype=BF16)
        o = _attn(q, mem_k, mem_v, i, bt=bt, L=L, tl=cfg["attn_tl"])
        x = _mm_res(o, big["w_mo"], i, x, tm=tm, tn=1024)
        xf = _norm_cast(x, big["norm_ffn"], i, tm=cfg["tm_norm"])
        act, f_c = _ffn_up(xf, big["w_ffn_in"], big["ffn_conv_w"], big["ffn_conv_b"], st_ffn_conv, i,
                           L=L, tm=tm, tn=512)
        x = _mm_res(act, big["w_ffn_out"], i, x, tm=cfg["tm_norm"], tn=512)
        new_ffn.append(f_c)
    y = _final_norm(x, norm_final, tm=cfg["tm_norm"])
    return y, n_h, n_c, n_s, jnp.stack(new_ffn)


def _group_cfg(bt, L):
    T = bt * L
    return dict(tm=min(T, 1024), tm_norm=min(T, 512), ssd_q=min(L, 128), gla_tb=min(L, 128), attn_tl=min(L, 512))


def kernel(x_prompt, x_sample, mem_prompt, state_ssd, state_ssd_conv, state_gla, state_ffn_conv, cache_mem_k, cache_mem_v, norm_mix, w_in, ssd_conv_w, ssd_conv_b, ssd_dt_bias, ssd_a_log, ssd_d, ssd_norm, gla_wa2, gla_ba, gla_norm, w_branch, w_out, norm_mem, w_mq, w_mk, w_mv, w_mo, norm_ffn, w_ffn_in, ffn_conv_w, ffn_conv_b, w_ffn_out, norm_final):
    wl = _prep_weights(w_in, ssd_conv_w, ssd_conv_b, ssd_dt_bias, ssd_a_log, ssd_d, ssd_norm, gla_wa2, gla_ba, gla_norm)
    big = dict(norm_mix=norm_mix, w_branch=w_branch, w_out=w_out, norm_mem=norm_mem, w_mq=w_mq, w_mo=w_mo,
               norm_ffn=norm_ffn, w_ffn_in=w_ffn_in, ffn_conv_w=ffn_conv_w, ffn_conv_b=ffn_conv_b, w_ffn_out=w_ffn_out)
    pb, pl_len, _ = x_prompt.shape
    sb, sl_len, _ = x_sample.shape
    mem2d = mem_prompt.reshape(pb * MEM_TOKENS, D_MODEL)
    p_mem_k = _mem_kv(mem2d, w_mk, nb=pb)
    p_mem_v = _mem_kv(mem2d, w_mv, nb=pb)
    head_major = (0, 1, 3, 2, 4)

    y_p, p_ssd, p_ssd_conv, p_gla, p_ffn = _run_trunk(
        x_prompt.reshape(pb * pl_len, D_MODEL), p_mem_k, p_mem_v, None, None, None, None, wl, big, norm_final,
        bt=pb, L=pl_len, cfg=_group_cfg(pb, pl_len))
    y_s, s_ssd, s_ssd_conv, s_gla, s_ffn = _run_trunk(
        x_sample.reshape(sb * sl_len, D_MODEL),
        jnp.transpose(cache_mem_k, head_major), jnp.transpose(cache_mem_v, head_major),
        state_ssd, state_ssd_conv, state_gla, state_ffn_conv, wl, big, norm_final,
        bt=sb, L=sl_len, cfg=_group_cfg(sb, sl_len))
    return (y_p.reshape(pb, pl_len, D_MODEL), y_s.reshape(sb, sl_len, D_MODEL),
            p_ssd, p_ssd_conv, p_gla, p_ffn, jnp.transpose(p_mem_k, head_major), jnp.transpose(p_mem_v, head_major),
            s_ssd, s_ssd_conv, s_gla, s_ffn)
```

```python
import functools

import jax
import jax.numpy as jnp
from jax import lax
from jax.experimental import pallas as pl
from jax.experimental.pallas import tpu as pltpu

F32 = jnp.float32
BF16 = jnp.bfloat16

D_MODEL = 2048
DEPTH = 2
EPS = 1e-6
SSD_HEADS = 32
SSD_HEADDIM = 64
SSD_GROUPS = 4
SSD_STATE = 128
SSD_CONV = 4
SSD_CONV_DIM = 3072
GLA_HEADS = 4
GLA_HEAD_K = 256
GLA_HEAD_V = 512
GLA_RANK = 16
GLA_GATE_NORM = 16.0
GLA_CHUNK = 16
MEM_TOKENS = 256
MEM_HEADS = 4
MEM_HEAD_DIM = 512
D_FF = 5632
FFN_CONV = 3

LANES = 128
MXU_COLS = 256
SUBLANES = 8
VMEM_CAP_BYTES = 56 * 2**20

P_Z, P_XS, P_V, P_G, P_GATE, P_BC, P_Q, P_K, P_N = 0, 2048, 4096, 6144, 8192, 12288, 13312, 14336, 15360
SM_DT, SM_ALR = 0, 32

NT_DIMS = (((1,), (1,)), ((), ()))
TN_DIMS = (((0,), (0,)), ((), ()))
HI = lax.Precision.HIGHEST


def _vmem_limit(*block_bytes, scratch=0):
    need = 2 * sum(block_bytes) + scratch + (4 << 20)
    return int(min(max(need, 16 << 20), VMEM_CAP_BYTES))


def _nbytes(shape, dtype):
    n = 1
    for s in shape:
        n *= s
    return n * jnp.dtype(dtype).itemsize


def _silu(x):
    return x * jax.nn.sigmoid(x)


def _softplus(x):
    return jnp.maximum(x, 0.0) + jnp.log1p(jnp.exp(-jnp.abs(x)))


def _log_sigmoid(x):
    return -_softplus(-x)


def _rms(x, gain):
    ms = jnp.mean(x * x, axis=-1, keepdims=True)
    return x * lax.rsqrt(ms + EPS) * gain


def _split2(x):
    hi = x.astype(BF16)
    lo = (x - hi.astype(F32)).astype(BF16)
    return hi, lo


def _select_rows(sel, x):
    hi = x.astype(BF16)
    r1 = x - hi.astype(F32)
    mid = r1.astype(BF16)
    lo = (r1 - mid.astype(F32)).astype(BF16)
    return (jnp.dot(sel, hi, preferred_element_type=F32) + jnp.dot(sel, mid, preferred_element_type=F32)
            + jnp.dot(sel, lo, preferred_element_type=F32))


def _stacked_out(prev, shape, dtype):
    return jax.ShapeDtypeStruct(shape, dtype) if prev is None else jax.ShapeDtypeStruct(prev.shape, prev.dtype)


def _norm_mm_kernel(*refs, has_gain, has_small):
    it = iter(refs)
    x_ref = next(it)
    g_ref = next(it) if has_gain else None
    w_ref = next(it)
    ws_ref = next(it) if has_small else None
    o_ref = next(it)
    os_ref = next(it) if has_small else None
    xn_ref = next(it)

    @pl.when(pl.program_id(1) == 0)
    def _():
        x = x_ref[...].astype(F32)
        if has_gain:
            x = _rms(x, g_ref[...])
        xn_ref[...] = x.astype(BF16)
        if has_small:
            os_ref[...] = jnp.dot(xn_ref[...], ws_ref[...], preferred_element_type=F32)

    o_ref[...] = jnp.dot(xn_ref[...], w_ref[...].astype(BF16), preferred_element_type=F32).astype(o_ref.dtype)


def _norm_mm(x, gain, w, layer, w_small, *, tm, tn, out_dtype):
    m, k = x.shape
    n = w.shape[2]
    has_gain = gain is not None
    has_small = w_small is not None
    in_specs = [pl.BlockSpec((tm, k), lambda i, j: (i, 0))]
    args = [x]
    if has_gain:
        in_specs.append(pl.BlockSpec((None, 1, k), lambda i, j: (layer, 0, 0)))
        args.append(gain.reshape(DEPTH, 1, k))
    in_specs.append(pl.BlockSpec((None, k, tn), lambda i, j: (layer, 0, j)))
    args.append(w)
    out_shape = [jax.ShapeDtypeStruct((m, n), out_dtype)]
    out_specs = [pl.BlockSpec((tm, tn), lambda i, j: (i, j))]
    if has_small:
        in_specs.append(pl.BlockSpec((None, k, LANES), lambda i, j: (layer, 0, 0)))
        args.append(w_small)
        out_shape.append(jax.ShapeDtypeStruct((m, LANES), F32))
        out_specs.append(pl.BlockSpec((tm, LANES), lambda i, j: (i, 0)))
    limit = _vmem_limit(_nbytes((tm, k), x.dtype), _nbytes((k, tn), w.dtype), _nbytes((tm, tn), out_dtype),
                        _nbytes((k, LANES), BF16), _nbytes((tm, LANES), F32), scratch=_nbytes((tm, k), BF16))
    res = pl.pallas_call(
        functools.partial(_norm_mm_kernel, has_gain=has_gain, has_small=has_small),
        grid=(m // tm, n // tn),
        in_specs=in_specs,
        out_specs=out_specs,
        out_shape=out_shape,
        scratch_shapes=[pltpu.VMEM((tm, k), BF16)],
        compiler_params=pltpu.CompilerParams(dimension_semantics=("parallel", "arbitrary"),
                                             vmem_limit_bytes=limit),
        name="norm_mm",
    )(*args)
    return res if has_small else res[0]


def _mem_kv_kernel(x_ref, w_ref, o_ref, xb_ref):
    @pl.when((pl.program_id(0) == 0) & (pl.program_id(1) == 0))
    def _():
        xb_ref[...] = x_ref[...].astype(BF16)

    res = jnp.dot(xb_ref[...], w_ref[...].astype(BF16), preferred_element_type=F32)
    o_ref[...] = res.reshape(o_ref.shape)


def _mem_kv(x, w, *, nb):
    m, k = x.shape
    tn = MEM_HEAD_DIM
    limit = _vmem_limit(_nbytes((m, k), F32), _nbytes((k, tn), F32), _nbytes((m, tn), F32),
                        scratch=_nbytes((m, k), BF16))
    return pl.pallas_call(
        _mem_kv_kernel,
        grid=(DEPTH, MEM_HEADS),
        in_specs=[pl.BlockSpec((m, k), lambda d, h: (0, 0)),
                  pl.BlockSpec((None, k, tn), lambda d, h: (d, 0, h))],
        out_specs=pl.BlockSpec((None, nb, None, m // nb, tn), lambda d, h: (d, 0, h, 0, 0)),
        out_shape=jax.ShapeDtypeStruct((DEPTH, nb, MEM_HEADS, m // nb, tn), F32),
        scratch_shapes=[pltpu.VMEM((m, k), BF16)],
        compiler_params=pltpu.CompilerParams(dimension_semantics=("arbitrary", "arbitrary"),
                                             vmem_limit_bytes=limit),
        name="mem_kv",
    )(x, w)


def _norm_cast_kernel(x_ref, g_ref, xn_ref):
    xn_ref[...] = _rms(x_ref[...], g_ref[...]).astype(BF16)


def _norm_cast(x, gain, layer, *, tm):
    m, k = x.shape
    return pl.pallas_call(
        _norm_cast_kernel,
        grid=(m // tm,),
        in_specs=[pl.BlockSpec((tm, k), lambda i: (i, 0)), pl.BlockSpec((None, 1, k), lambda i: (layer, 0, 0))],
        out_specs=pl.BlockSpec((tm, k), lambda i: (i, 0)),
        out_shape=jax.ShapeDtypeStruct((m, k), BF16),
        compiler_params=pltpu.CompilerParams(dimension_semantics=("parallel",)),
        name="norm_cast",
    )(x, gain.reshape(DEPTH, 1, k))


def _norm_small_kernel(x_ref, g_ref, wst_ref, xn_ref, os_ref):
    xn = _rms(x_ref[...], g_ref[...]).astype(BF16)
    xn_ref[...] = xn
    os_ref[...] = lax.dot_general(xn, wst_ref[...].astype(BF16), NT_DIMS, preferred_element_type=F32)


def _norm_small(x, gain, w_small_t, layer, *, tm):
    m, k = x.shape
    return pl.pallas_call(
        _norm_small_kernel,
        grid=(m // tm,),
        in_specs=[pl.BlockSpec((tm, k), lambda i: (i, 0)),
                  pl.BlockSpec((None, 1, k), lambda i: (layer, 0, 0)),
                  pl.BlockSpec((None, LANES, k), lambda i: (layer, 0, 0))],
        out_specs=[pl.BlockSpec((tm, k), lambda i: (i, 0)), pl.BlockSpec((tm, LANES), lambda i: (i, 0))],
        out_shape=[jax.ShapeDtypeStruct((m, k), BF16), jax.ShapeDtypeStruct((m, LANES), F32)],
        compiler_params=pltpu.CompilerParams(dimension_semantics=("parallel",)),
        name="norm_small",
    )(x, gain.reshape(DEPTH, 1, k), w_small_t)


PROJ_TN = 1024
PROJ_SRC = (0, 1024, 2048, 3072, 7200, 8224, 9248, 10272, 11312, 12336, 13360, 14384, 4096, 5152, 6176)


def _proj_kernel(src_ref, xn_ref, wt_ref, o_ref, wbf_ref):
    @pl.when(pl.program_id(1) == 0)
    def _():
        wbf_ref[...] = wt_ref[0].astype(BF16)

    o_ref[...] = lax.dot_general(xn_ref[...], wbf_ref[...], NT_DIMS, preferred_element_type=F32).astype(o_ref.dtype)


def _proj_in(xn, w_in_t, layer, *, tm):
    m, k = xn.shape
    tn = PROJ_TN
    nj = len(PROJ_SRC)
    assert all(s % SUBLANES == 0 for s in PROJ_SRC)
    src = jnp.array([s // SUBLANES for s in PROJ_SRC], jnp.int32)
    limit = _vmem_limit(_nbytes((tm, k), BF16), _nbytes((tn, k), F32), _nbytes((tm, tn), BF16),
                        scratch=_nbytes((tn, k), BF16))
    grid_spec = pltpu.PrefetchScalarGridSpec(
        num_scalar_prefetch=1,
        grid=(nj, m // tm),
        in_specs=[pl.BlockSpec((tm, k), lambda j, i, s: (i, 0)),
                  pl.BlockSpec((pl.Element(1), pl.Element(tn), pl.Element(k)),
                               lambda j, i, s: (layer, s[j] * SUBLANES, 0))],
        out_specs=pl.BlockSpec((tm, tn), lambda j, i, s: (i, j)),
        scratch_shapes=[pltpu.VMEM((tn, k), BF16)])
    return pl.pallas_call(
        _proj_kernel,
        grid_spec=grid_spec,
        out_shape=jax.ShapeDtypeStruct((m, nj * tn), BF16),
        compiler_params=pltpu.CompilerParams(dimension_semantics=("arbitrary", "arbitrary"),
                                             vmem_limit_bytes=limit),
        name="proj_in",
    )(src, xn, w_in_t)


def _mm_res_kernel(x_ref, w_ref, r_ref, o_ref, wbf_ref):
    @pl.when(pl.program_id(1) == 0)
    def _():
        wbf_ref[...] = w_ref[...].astype(BF16)

    o_ref[...] = r_ref[...] + jnp.dot(x_ref[...], wbf_ref[...], preferred_element_type=F32)


def _mm_res(x, w, layer, res, *, tm, tn):
    m, k = x.shape
    n = w.shape[2]
    limit = _vmem_limit(_nbytes((tm, k), BF16), _nbytes((k, tn), F32), 2 * _nbytes((tm, tn), F32),
                        scratch=_nbytes((k, tn), BF16))
    return pl.pallas_call(
        _mm_res_kernel,
        grid=(n // tn, m // tm),
        in_specs=[pl.BlockSpec((tm, k), lambda j, i: (i, 0)),
                  pl.BlockSpec((None, k, tn), lambda j, i: (layer, 0, j)),
                  pl.BlockSpec((tm, tn), lambda j, i: (i, j))],
        out_specs=pl.BlockSpec((tm, tn), lambda j, i: (i, j)),
        out_shape=jax.ShapeDtypeStruct((m, n), F32),
        scratch_shapes=[pltpu.VMEM((k, tn), BF16)],
        compiler_params=pltpu.CompilerParams(dimension_semantics=("parallel", "arbitrary"),
                                             vmem_limit_bytes=limit),
        name="mm_res",
    )(x, w, res)


def _merge_kernel(y0_ref, y1_ref, g0_ref, g1_ref, w_ref, o_ref, wbf_ref):
    @pl.when(pl.program_id(1) == 0)
    def _():
        wbf_ref[...] = w_ref[...].astype(BF16)

    b0 = jnp.dot(y0_ref[...], wbf_ref[0], preferred_element_type=F32)
    b1 = jnp.dot(y1_ref[...], wbf_ref[1], preferred_element_type=F32)
    g0 = jax.nn.sigmoid(g0_ref[...].astype(F32))
    g1 = jax.nn.sigmoid(g1_ref[...].astype(F32))
    o_ref[...] = (g0 * b0 + g1 * b1).astype(o_ref.dtype)


def _merge(y_ssd, y_gla, proj, w_branch, layer, *, tm, tn):
    m, k = y_ssd.shape
    n = D_MODEL
    gb = P_GATE // tn
    nb = n // tn
    limit = _vmem_limit(2 * _nbytes((tm, k), BF16), 2 * _nbytes((k, tn), F32), 3 * _nbytes((tm, tn), BF16),
                        scratch=2 * _nbytes((k, tn), BF16))
    return pl.pallas_call(
        _merge_kernel,
        grid=(nb, m // tm),
        in_specs=[pl.BlockSpec((tm, k), lambda j, i: (i, 0)),
                  pl.BlockSpec((tm, k), lambda j, i: (i, 0)),
                  pl.BlockSpec((tm, tn), lambda j, i: (i, gb + j)),
                  pl.BlockSpec((tm, tn), lambda j, i: (i, gb + nb + j)),
                  pl.BlockSpec((None, 2, k, tn), lambda j, i: (layer, 0, 0, j))],
        out_specs=pl.BlockSpec((tm, tn), lambda j, i: (i, j)),
        out_shape=jax.ShapeDtypeStruct((m, n), BF16),
        scratch_shapes=[pltpu.VMEM((2, k, tn), BF16)],
        compiler_params=pltpu.CompilerParams(dimension_semantics=("parallel", "arbitrary"),
                                             vmem_limit_bytes=limit),
        name="merge",
    )(y_ssd, y_gla, proj, proj, w_branch)


def _final_norm_kernel(x_ref, g_ref, o_ref):
    o_ref[...] = _rms(x_ref[...], g_ref[...])


def _final_norm(x, gain, *, tm):
    m, k = x.shape
    return pl.pallas_call(
        _final_norm_kernel,
        grid=(m // tm,),
        in_specs=[pl.BlockSpec((tm, k), lambda i: (i, 0)), pl.BlockSpec((1, k), lambda i: (0, 0))],
        out_specs=pl.BlockSpec((tm, k), lambda i: (i, 0)),
        out_shape=jax.ShapeDtypeStruct((m, k), F32),
        compiler_params=pltpu.CompilerParams(dimension_semantics=("parallel",)),
        name="final_norm",
    )(x, gain.reshape(1, k))


def _ffn_up_kernel(*refs, tm, tn, seq_tiles, seg):
    short = seg > 0
    it = iter(refs)
    xn_ref, wu_ref, wt_ref, cwu_ref, cwt_ref, cbu_ref, cbt_ref = (next(it) for _ in range(7))
    stu_ref = next(it) if short else None
    stt_ref = next(it) if short else None
    act_ref, cnu_ref, cnt_ref = next(it), next(it), next(it)
    wbu_ref, wbt_ref = next(it), next(it)
    tailu_ref = None if short else next(it)
    tailt_ref = None if short else next(it)
    i = pl.program_id(1)
    sl = SUBLANES
    hw = FFN_CONV - 1

    @pl.when(i == 0)
    def _():
        wbu_ref[...] = wu_ref[...].astype(BF16)
        wbt_ref[...] = wt_ref[...].astype(BF16)
        if not short:
            tailu_ref[...] = jnp.zeros_like(tailu_ref)
            tailt_ref[...] = jnp.zeros_like(tailt_ref)

    xn = xn_ref[...]
    u_raw = jnp.dot(xn, wbu_ref[...], preferred_element_type=F32)
    t_raw = jnp.dot(xn, wbt_ref[...], preferred_element_type=F32)

    if short:
        nseq = tm // seg
        r = lax.broadcasted_iota(jnp.int32, (tm, hw * nseq), 0)
        c = lax.broadcasted_iota(jnp.int32, (tm, hw * nseq), 1)
        halo_u, halo_t = [], []
        for s in range(1, FFN_CONV):
            sel = (((r % seg) < s) & (c == hw * (r // seg) + hw - s + (r % seg))).astype(BF16)
            halo_u.append(_select_rows(sel, stu_ref[...]))
            halo_t.append(_select_rows(sel, stt_ref[...]))
        ro = lax.broadcasted_iota(jnp.int32, (hw * nseq, tm), 0)
        co = lax.broadcasted_iota(jnp.int32, (hw * nseq, tm), 1)
        sel_out = (co == (ro // hw) * seg + seg - hw + (ro % hw)).astype(BF16)
        cnu_ref[...] = _select_rows(sel_out, u_raw)
        cnt_ref[...] = _select_rows(sel_out, t_raw)
        rowmod = lax.broadcasted_iota(jnp.int32, (tm, tn), 0) % seg
        prev_u = prev_t = None
    else:
        start = (i % seq_tiles) == 0
        prev_u = jnp.where(start, 0.0, tailu_ref[...])
        prev_t = jnp.where(start, 0.0, tailt_ref[...])
        tailu_ref[...] = u_raw[tm - sl:tm, :]
        tailt_ref[...] = t_raw[tm - sl:tm, :]
        cnu_ref[...] = u_raw[tm - sl:tm, :]
        cnt_ref[...] = t_raw[tm - sl:tm, :]
        row8 = lax.broadcasted_iota(jnp.int32, (sl, tn), 0)
        halo_u = halo_t = rowmod = None

    def conv(x, prev8, cw_ref, cb_ref, halos):
        y = x * cw_ref[hw:hw + 1, :]
        for s in range(1, FFN_CONV):
            rolled = pltpu.roll(x, s, axis=0)
            if short:
                shifted = jnp.where(rowmod < s, halos[s - 1], rolled)
            else:
                head = jnp.where(row8 < s, pltpu.roll(prev8, s, axis=0), rolled[0:sl, :])
                shifted = jnp.concatenate([head, rolled[sl:, :]], axis=0)
            y = y + shifted * cw_ref[hw - s:hw - s + 1, :]
        return y + cb_ref[...]

    u = conv(u_raw, prev_u, cwu_ref, cbu_ref, halo_u)
    t = conv(t_raw, prev_t, cwt_ref, cbt_ref, halo_t)
    act_ref[...] = (_silu(t) * u).astype(act_ref.dtype)


def _ffn_up(xn, w, conv_w, conv_b, conv0, layer, *, L, tm, tn):
    m, k = xn.shape
    bt = m // L
    nj = D_FF // tn
    hw = FFN_CONV - 1
    short = L < tm
    seg = L if short else 0
    seq_tiles = 1 if short else L // tm
    assert (conv0 is not None) == short, "history rows are only supported for sequences shorter than a block"
    in_specs = [
        pl.BlockSpec((tm, k), lambda j, i: (i, 0)),
        pl.BlockSpec((None, k, tn), lambda j, i: (layer, 0, j)),
        pl.BlockSpec((None, k, tn), lambda j, i: (layer, 0, nj + j)),
        pl.BlockSpec((None, FFN_CONV, tn), lambda j, i: (layer, 0, j)),
        pl.BlockSpec((None, FFN_CONV, tn), lambda j, i: (layer, 0, nj + j)),
        pl.BlockSpec((None, 1, tn), lambda j, i: (layer, 0, j)),
        pl.BlockSpec((None, 1, tn), lambda j, i: (layer, 0, nj + j)),
    ]
    args = [xn, w, w, conv_w, conv_w, conv_b.reshape(DEPTH, 1, 2 * D_FF), conv_b.reshape(DEPTH, 1, 2 * D_FF)]
    scratch = [pltpu.VMEM((k, tn), BF16)] * 2
    if short:
        nst = (m // tm) * (tm // L) * hw
        st2d = conv0.reshape(DEPTH, nst, 2 * D_FF)
        rows = (tm // L) * hw
        in_specs += [pl.BlockSpec((None, rows, tn), lambda j, i: (layer, i, j)),
                     pl.BlockSpec((None, rows, tn), lambda j, i: (layer, i, nj + j))]
        args += [st2d, st2d]
        cn_shape = jax.ShapeDtypeStruct((nst, D_FF), F32)
        cn_spec = pl.BlockSpec((rows, tn), lambda j, i: (i, j))
    else:
        scratch += [pltpu.VMEM((SUBLANES, tn), F32)] * 2
        cn_shape = jax.ShapeDtypeStruct((m // tm, SUBLANES, D_FF), F32)
        cn_spec = pl.BlockSpec((None, SUBLANES, tn), lambda j, i: (i, 0, j))
    limit = _vmem_limit(_nbytes((tm, k), BF16), 2 * _nbytes((k, tn), F32), _nbytes((tm, tn), BF16),
                        scratch=2 * _nbytes((k, tn), BF16) + 6 * _nbytes((tm, tn), F32))
    act, cnu, cnt = pl.pallas_call(
        functools.partial(_ffn_up_kernel, tm=tm, tn=tn, seq_tiles=seq_tiles, seg=seg),
        grid=(nj, m // tm),
        in_specs=in_specs,
        out_specs=[pl.BlockSpec((tm, tn), lambda j, i: (i, j)), cn_spec, cn_spec],
        out_shape=[jax.ShapeDtypeStruct((m, D_FF), BF16), cn_shape, cn_shape],
        scratch_shapes=scratch,
        compiler_params=pltpu.CompilerParams(dimension_semantics=("arbitrary", "arbitrary"),
                                             vmem_limit_bytes=limit),
        name="ffn_up",
    )(*args)
    if short:
        conv_new = jnp.concatenate([cnu, cnt], axis=-1).reshape(bt, hw, 2 * D_FF)
    else:
        last = slice(seq_tiles - 1, None, seq_tiles)
        conv_new = jnp.concatenate([cnu[last, SUBLANES - hw:, :], cnt[last, SUBLANES - hw:, :]], axis=-1)
    return act, conv_new


def _causal_conv(x, tail_ref, w_ref, b, width):
    sl = SUBLANES
    row = lax.broadcasted_iota(jnp.int32, (sl, x.shape[1]), 0)
    prev = tail_ref[...]
    y = x * w_ref[width - 1:width, :]
    for s in range(1, width):
        rolled = pltpu.roll(x, s, axis=0)
        head = jnp.where(row < s, pltpu.roll(prev, s, axis=0), rolled[0:sl, :])
        shifted = jnp.concatenate([head, rolled[sl:, :]], axis=0)
        y = y + shifted * w_ref[width - 1 - s:width - s, :]
    return y + b


def _ssd_kernel(*refs, Q, zero_init, aliased):
    it = iter(refs)
    z_ref, xs_ref, bc_ref, sm_ref = (next(it) for _ in range(4))
    conv0_ref = None if zero_init else next(it)
    h0_ref = None if zero_init else next(it)
    cwx_ref, cwb_ref, cbx_ref, cbb_ref, dtb_ref, a_ref, dx_ref, nrm_ref, exp_ref = (next(it) for _ in range(9))
    if aliased:
        next(it), next(it)
    y_ref, h_out, conv_out, tailx_ref, tailb_ref = (next(it) for _ in range(5))
    h_ref = h_out if aliased else h_out.at[0]
    convout_ref = conv_out if aliased else conv_out.at[0]
    c = pl.program_id(1)
    hp = SSD_HEADDIM
    gw = SSD_HEADS // SSD_GROUPS * hp

    @pl.when(c == 0)
    def _():
        if not aliased:
            h_out[1:] = jnp.zeros((DEPTH - 1,) + h_ref.shape, F32)
            conv_out[1:] = jnp.zeros((DEPTH - 1,) + convout_ref.shape, F32)
        tailx_ref[...] = jnp.zeros_like(tailx_ref)
        tailb_ref[...] = jnp.zeros_like(tailb_ref)
        if zero_init:
            h_ref[...] = jnp.zeros_like(h_ref)
        else:
            h_ref[...] = h0_ref[...]
            tailx_ref[5:8, :] = conv0_ref[:, 0:D_MODEL]
            tailb_ref[5:8, :] = conv0_ref[:, D_MODEL:SSD_CONV_DIM]

    xs_raw = xs_ref[...].astype(F32)
    bc_raw = bc_ref[...].astype(F32)
    xs = _silu(_causal_conv(xs_raw, tailx_ref, cwx_ref, cbx_ref[...], SSD_CONV))
    bc = _silu(_causal_conv(bc_raw, tailb_ref, cwb_ref, cbb_ref[...], SSD_CONV))
    tailx_ref[...] = xs_raw[Q - 8:Q, :]
    tailb_ref[...] = bc_raw[Q - 8:Q, :]

    @pl.when(c == pl.num_programs(1) - 1)
    def _():
        convout_ref[:, 0:D_MODEL] = tailx_ref[5:8, :]
        convout_ref[:, D_MODEL:SSD_CONV_DIM] = tailb_ref[5:8, :]

    dt = _softplus(sm_ref[...] + dtb_ref[...])
    adt = dt * a_ref[...]
    ri = lax.broadcasted_iota(jnp.int32, (Q, Q), 0)
    ci = lax.broadcasted_iota(jnp.int32, (Q, Q), 1)
    causal = ci <= ri
    acum = jnp.dot(causal.astype(F32), adt, precision=HI, preferred_element_type=F32)
    eye = (lax.broadcasted_iota(jnp.int32, (LANES, LANES), 0)
           == lax.broadcasted_iota(jnp.int32, (LANES, LANES), 1)).astype(F32)
    acum_t = lax.dot_general(eye, acum, NT_DIMS, precision=HI, preferred_element_type=F32)
    dt_t = lax.dot_general(eye, dt, NT_DIMS, precision=HI, preferred_element_type=F32)
    a_last = acum[Q - 1:Q, :]
    ea = jnp.exp(acum)
    te = jnp.exp(a_last - acum) * dt
    dec_rows = jnp.broadcast_to(jnp.exp(acum_t[:, Q - 1:Q]), (LANES, LANES))
    ea_hi, ea_lo = _split2(ea)
    te_hi, te_lo = _split2(te)
    expand = exp_ref[...]
    ea_x = (jnp.dot(ea_hi, expand, preferred_element_type=F32)
            + jnp.dot(ea_lo, expand, preferred_element_type=F32))
    te_x = (jnp.dot(te_hi, expand, preferred_element_type=F32)
            + jnp.dot(te_lo, expand, preferred_element_type=F32))
    lane = lax.broadcasted_iota(jnp.int32, (Q, LANES), 1)

    for g in range(SSD_GROUPS):
        gl = slice(g * gw, (g + 1) * gw)
        b_g = bc[:, g * SSD_STATE:(g + 1) * SSD_STATE].astype(BF16)
        c_g = bc[:, (SSD_GROUPS + g) * SSD_STATE:(SSD_GROUPS + g + 1) * SSD_STATE].astype(BF16)
        cb = lax.dot_general(c_g, b_g, NT_DIMS, preferred_element_type=F32)
        h_g = h_ref[8 * g:8 * g + 8].reshape(gw, SSD_STATE)
        y_off = lax.dot_general(c_g, h_g.astype(BF16), NT_DIMS, preferred_element_type=F32)
        x_g = xs[:, gl]
        pairs = []
        for p in range(4):
            x_p = x_g[:, p * LANES:(p + 1) * LANES]
            acc = None
            for s in range(2):
                hh = g * 8 + p * 2 + s
                seg = jnp.broadcast_to(acum[:, hh:hh + 1], (Q, Q)) - jnp.broadcast_to(acum_t[hh:hh + 1, :], (Q, Q))
                decay = jnp.where(causal, jnp.exp(seg), 0.0)
                w_h = (cb * decay * dt_t[hh:hh + 1, :]).astype(BF16)
                x_m = jnp.where((lane // hp) == s, x_p, 0.0).astype(BF16)
                r = jnp.dot(w_h, x_m, preferred_element_type=F32)
                acc = r if acc is None else acc + r
            pairs.append(acc)
        y_g = jnp.concatenate(pairs, axis=1) + y_off * ea_x[:, gl] + dx_ref[:, gl] * x_g
        x_t = (x_g * te_x[:, gl]).astype(BF16)
        upd = lax.dot_general(x_t, b_g, TN_DIMS, preferred_element_type=F32)
        for h in range(8):
            hh = g * 8 + h
            h_ref[hh] = h_ref[hh] * dec_rows[hh:hh + 1, :] + upd[h * hp:(h + 1) * hp, :]
        z_g = z_ref[:, gl].astype(F32)
        y_ref[:, gl] = _rms(y_g * _silu(z_g), nrm_ref[:, gl]).astype(y_ref.dtype)


def _ssd(proj, small, conv0, h0, wl, layer, prev_h, prev_conv, *, bt, L, Q):
    zero_init = h0 is None
    aliased = prev_h is not None
    nc = L // Q
    T = bt * L
    full2 = lambda b, c: (0, 0)
    lay3 = lambda b, c: (layer, 0, 0)
    row = lambda col: (lambda b, c: (b * nc + c, col))
    hshape = (DEPTH, bt, SSD_HEADS, SSD_HEADDIM, SSD_STATE)
    cshape = (DEPTH, bt, SSD_CONV - 1, SSD_CONV_DIM)
    hspec = pl.BlockSpec((None, None, SSD_HEADS, SSD_HEADDIM, SSD_STATE), lambda b, c: (layer, b, 0, 0, 0))
    cspec = pl.BlockSpec((None, None, SSD_CONV - 1, SSD_CONV_DIM), lambda b, c: (layer, b, 0, 0))
    h_out_spec = hspec if aliased else pl.BlockSpec((DEPTH, None, SSD_HEADS, SSD_HEADDIM, SSD_STATE),
                                                    lambda b, c: (0, b, 0, 0, 0))
    c_out_spec = cspec if aliased else pl.BlockSpec((DEPTH, None, SSD_CONV - 1, SSD_CONV_DIM),
                                                    lambda b, c: (0, b, 0, 0))
    in_specs = [
        pl.BlockSpec((Q, D_MODEL), row(P_Z // D_MODEL)),
        pl.BlockSpec((Q, D_MODEL), row(P_XS // D_MODEL)),
        pl.BlockSpec((Q, 1024), row(P_BC // 1024)),
        pl.BlockSpec((Q, LANES), row(0)),
    ]
    args = [proj, proj, proj, small]
    if not zero_init:
        in_specs += [cspec, hspec]
        args += [conv0, h0]
    in_specs += [
        pl.BlockSpec((None, SSD_CONV, D_MODEL), lay3),
        pl.BlockSpec((None, SSD_CONV, 1024), lambda b, c: (layer, 0, D_MODEL // 1024)),
        pl.BlockSpec((None, 1, D_MODEL), lay3),
        pl.BlockSpec((None, 1, 1024), lambda b, c: (layer, 0, D_MODEL // 1024)),
        pl.BlockSpec((None, 1, LANES), lay3),
        pl.BlockSpec((None, 1, LANES), lay3),
        pl.BlockSpec((None, 1, D_MODEL), lay3),
        pl.BlockSpec((None, 1, D_MODEL), lay3),
        pl.BlockSpec((LANES, D_MODEL), full2),
    ]
    args += [wl["ssd_conv_w"], wl["ssd_conv_w"], wl["ssd_conv_b"], wl["ssd_conv_b"], wl["ssd_dtb"], wl["ssd_a"],
             wl["ssd_dx"], wl["ssd_norm"], wl["expand"]]
    aliases = {}
    if aliased:
        aliases = {len(args): 1, len(args) + 1: 2}
        in_specs += [pl.BlockSpec(memory_space=pl.ANY), pl.BlockSpec(memory_space=pl.ANY)]
        args += [prev_h, prev_conv]
    return pl.pallas_call(
        functools.partial(_ssd_kernel, Q=Q, zero_init=zero_init, aliased=aliased),
        grid=(bt, nc),
        in_specs=in_specs,
        out_specs=[pl.BlockSpec((Q, D_MODEL), row(0)), h_out_spec, c_out_spec],
        out_shape=[jax.ShapeDtypeStruct((T, D_MODEL), BF16), _stacked_out(prev_h, hshape, F32),
                   _stacked_out(prev_conv, cshape, F32)],
        scratch_shapes=[pltpu.VMEM((8, D_MODEL), F32), pltpu.VMEM((8, 1024), F32)],
        input_output_aliases=aliases,
        compiler_params=pltpu.CompilerParams(dimension_semantics=("parallel", "arbitrary"),
                                             vmem_limit_bytes=48 << 20),
        name="ssd",
    )(*args)


def _gla_kernel(*refs, TB, zero_init, aliased):
    it = iter(refs)
    q_ref, k_ref, v_ref, g_ref, sm_ref = (next(it) for _ in range(5))
    s0_ref = None if zero_init else next(it)
    wa_ref, ba_ref, wat_ref, bat_ref, gn_ref = (next(it) for _ in range(5))
    if aliased:
        next(it)
    y_ref, s_out = next(it), next(it)
    s_ref = s_out if aliased else s_out.at[0]
    c = pl.program_id(1)
    ck = GLA_CHUNK
    ns = TB // ck
    hk, hv = GLA_HEAD_K, GLA_HEAD_V

    @pl.when(c == 0)
    def _():
        if not aliased:
            s_out[1:] = jnp.zeros((DEPTH - 1,) + s_ref.shape, F32)
        if zero_init:
            s_ref[...] = jnp.zeros_like(s_ref)
        else:
            s_ref[...] = s0_ref[...]

    smb = sm_ref[...].astype(BF16)
    la = _log_sigmoid(jnp.dot(smb, wa_ref[...], preferred_element_type=F32) + ba_ref[...]) * (1.0 / GLA_GATE_NORM)
    rb = lax.broadcasted_iota(jnp.int32, (TB, TB), 0)
    cb = lax.broadcasted_iota(jnp.int32, (TB, TB), 1)
    same = (rb // ck) == (cb // ck)
    msel = jnp.concatenate([same & (cb <= rb), same, (cb // ck) < (rb // ck)], axis=0).astype(BF16)
    sums = _select_rows(msel, la)
    bcum, tot, bprev = sums[0:TB], sums[TB:2 * TB], sums[2 * TB:3 * TB]
    kf = k_ref[...].astype(F32)
    qd = q_ref[...].astype(F32) * (hk ** -0.5) * jnp.exp(bcum)
    qd_b = qd.astype(BF16)
    qs_b = (qd * jnp.exp(bprev)).astype(BF16)
    ki = kf * jnp.exp(-bcum)
    ke = kf * jnp.exp(tot - bcum)
    dtot = jnp.exp(tot)
    la_t = _log_sigmoid(lax.dot_general(wat_ref[...], smb, NT_DIMS, preferred_element_type=F32)
                        + bat_ref[:, 0:1]) * (1.0 / GLA_GATE_NORM)
    ones = jnp.ones((TB, LANES), BF16)
    la_hi = la_t.astype(BF16)
    la_r = la_t - la_hi.astype(F32)
    la_mid = la_r.astype(BF16)
    la_lo = (la_r - la_mid.astype(F32)).astype(BF16)
    dec_t = jnp.exp(jnp.dot(la_hi, ones, preferred_element_type=F32) + jnp.dot(la_mid, ones, preferred_element_type=F32)
                    + jnp.dot(la_lo, ones, preferred_element_type=F32))
    rchunk = lax.broadcasted_iota(jnp.int32, (TB, hk), 0) // ck
    row16 = lax.broadcasted_iota(jnp.int32, (ck, TB), 0)
    col16 = lax.broadcasted_iota(jnp.int32, (ck, TB), 1)

    for h in range(GLA_HEADS):
        kl = slice(h * hk, (h + 1) * hk)
        vl = slice(h * hv, (h + 1) * hv)
        ke_h, ki_h = ke[:, kl], ki[:, kl]
        v_h = v_ref[:, vl]
        kbuf = jnp.zeros((TB, hk), F32)
        att = []
        for cc in range(ns):
            in_c = rchunk == cc
            kall = jnp.where(in_c, ki_h, kbuf).astype(BF16)
            a = lax.dot_general(qd_b[cc * ck:(cc + 1) * ck, kl], kall, NT_DIMS, preferred_element_type=F32)
            att.append(jnp.where(col16 <= row16 + cc * ck, a, 0.0))
            kbuf = jnp.where(in_c, ke_h, kbuf * dtot[cc * ck:cc * ck + 1, kl])
        a_full = jnp.concatenate(att, axis=0).astype(BF16)
        s_old = s_ref[h]
        o = (jnp.dot(a_full, v_h, preferred_element_type=F32)
             + jnp.dot(qs_b[:, kl], s_old.astype(BF16), preferred_element_type=F32))
        y_ref[:, vl] = (_rms(o, gn_ref[...]) * _silu(g_ref[:, vl].astype(F32))).astype(y_ref.dtype)
        dcol = jnp.concatenate([dec_t[kl, :]] * (hv // LANES), axis=1)
        s_ref[h] = s_old * dcol + lax.dot_general(kbuf.astype(BF16), v_h, TN_DIMS, preferred_element_type=F32)


def _gla(proj, small, s0, wl, layer, prev_s, *, bt, L, TB):
    zero_init = s0 is None
    aliased = prev_s is not None
    nc = L // TB
    T = bt * L
    lay3 = lambda b, c: (layer, 0, 0)
    row = lambda col: (lambda b, c: (b * nc + c, col))
    kd = GLA_HEADS * GLA_HEAD_K
    sshape = (DEPTH, bt, GLA_HEADS, GLA_HEAD_K, GLA_HEAD_V)
    sspec = pl.BlockSpec((None, None, GLA_HEADS, GLA_HEAD_K, GLA_HEAD_V), lambda b, c: (layer, b, 0, 0, 0))
    s_out_spec = sspec if aliased else pl.BlockSpec((DEPTH, None, GLA_HEADS, GLA_HEAD_K, GLA_HEAD_V),
                                                     lambda b, c: (0, b, 0, 0, 0))
    in_specs = [
        pl.BlockSpec((TB, kd), row(P_Q // kd)),
        pl.BlockSpec((TB, kd), row(P_K // kd)),
        pl.BlockSpec((TB, D_MODEL), row(P_V // D_MODEL)),
        pl.BlockSpec((TB, D_MODEL), row(P_G // D_MODEL)),
        pl.BlockSpec((TB, LANES), row(0)),
    ]
    args = [proj, proj, proj, proj, small]
    if not zero_init:
        in_specs.append(sspec)
        args.append(s0)
    in_specs += [
        pl.BlockSpec((None, LANES, kd), lay3),
        pl.BlockSpec((None, 1, kd), lay3),
        pl.BlockSpec((None, kd, LANES), lay3),
        pl.BlockSpec((None, kd, LANES), lay3),
        pl.BlockSpec((None, 1, GLA_HEAD_V), lay3),
    ]
    args += [wl["gla_wa"], wl["gla_ba"], wl["gla_wa_t"], wl["gla_ba_t"], wl["gla_norm"]]
    aliases = {}
    if aliased:
        aliases = {len(args): 1}
        in_specs.append(pl.BlockSpec(memory_space=pl.ANY))
        args.append(prev_s)
    return pl.pallas_call(
        functools.partial(_gla_kernel, TB=TB, zero_init=zero_init, aliased=aliased),
        grid=(bt, nc),
        in_specs=in_specs,
        out_specs=[pl.BlockSpec((TB, D_MODEL), row(0)), s_out_spec],
        out_shape=[jax.ShapeDtypeStruct((T, D_MODEL), BF16), _stacked_out(prev_s, sshape, F32)],
        input_output_aliases=aliases,
        compiler_params=pltpu.CompilerParams(dimension_semantics=("parallel", "arbitrary"),
                                             vmem_limit_bytes=48 << 20),
        name="gla",
    )(*args)


def _attn_kernel(q_ref, k_ref, v_ref, o_ref, kb_ref, vb_ref):
    @pl.when(pl.program_id(1) == 0)
    def _():
        kb_ref[...] = k_ref[...].astype(BF16)
        vb_ref[...] = v_ref[...].astype(BF16)

    hd = MEM_HEAD_DIM
    for h in range(MEM_HEADS):
        hl = slice(h * hd, (h + 1) * hd)
        s = lax.dot_general(q_ref[:, hl], kb_ref[h], NT_DIMS, preferred_element_type=F32) * (hd ** -0.5)
        e = jnp.exp(s - jnp.max(s, axis=-1, keepdims=True))
        p = e / jnp.sum(e, axis=-1, keepdims=True)
        o_ref[:, hl] = jnp.dot(p.astype(BF16), vb_ref[h], preferred_element_type=F32).astype(o_ref.dtype)


def _attn(q, mem_k, mem_v, layer, *, bt, L, tl):
    nl = L // tl
    T = bt * L
    kvshape = (MEM_HEADS, MEM_TOKENS, MEM_HEAD_DIM)
    kvspec = pl.BlockSpec((None, None) + kvshape, lambda b, l: (layer, b, 0, 0, 0))
    return pl.pallas_call(
        _attn_kernel,
        grid=(bt, nl),
        in_specs=[pl.BlockSpec((tl, D_MODEL), lambda b, l: (b * nl + l, 0)), kvspec, kvspec],
        out_specs=pl.BlockSpec((tl, D_MODEL), lambda b, l: (b * nl + l, 0)),
        out_shape=jax.ShapeDtypeStruct((T, D_MODEL), BF16),
        scratch_shapes=[pltpu.VMEM(kvshape, BF16), pltpu.VMEM(kvshape, BF16)],
        compiler_params=pltpu.CompilerParams(dimension_semantics=("parallel", "arbitrary"),
                                             vmem_limit_bytes=40 << 20),
        name="mem_attn",
    )(q, mem_k, mem_v)


def _prep_weights(w_in, ssd_conv_w, ssd_conv_b, ssd_dt_bias, ssd_a_log, ssd_d, ssd_norm, gla_wa2, gla_ba, gla_norm):
    w_in_t = jnp.swapaxes(w_in, 1, 2)
    o_dt, o_alr = 5120, 11296
    pad_sm = LANES - SSD_HEADS - GLA_RANK
    w_small_t = jnp.concatenate(
        [w_in_t[:, o_dt:o_dt + SSD_HEADS, :], w_in_t[:, o_alr:o_alr + GLA_RANK, :],
         jnp.zeros((DEPTH, pad_sm, D_MODEL), F32)], axis=1)
    pad_h = LANES - SSD_HEADS
    kd = GLA_HEADS * GLA_HEAD_K
    wa = jnp.concatenate([jnp.zeros((DEPTH, SM_ALR, kd), F32), gla_wa2,
                          jnp.zeros((DEPTH, LANES - SM_ALR - GLA_RANK, kd), F32)], axis=1).astype(BF16)
    expand = (jnp.arange(D_MODEL)[None, :] // SSD_HEADDIM == jnp.arange(LANES)[:, None]).astype(BF16)
    return dict(
        w_in_t=w_in_t, w_small_t=w_small_t,
        ssd_conv_w=ssd_conv_w, ssd_conv_b=ssd_conv_b[:, None, :],
        ssd_dtb=jnp.pad(ssd_dt_bias, ((0, 0), (0, pad_h)))[:, None, :],
        ssd_a=jnp.pad(-jnp.exp(ssd_a_log), ((0, 0), (0, pad_h)))[:, None, :],
        ssd_dx=jnp.repeat(ssd_d, SSD_HEADDIM, axis=1)[:, None, :],
        ssd_norm=ssd_norm[:, None, :], expand=expand,
        gla_wa=wa, gla_ba=gla_ba[:, None, :], gla_wa_t=jnp.swapaxes(wa, 1, 2),
        gla_ba_t=jnp.broadcast_to(gla_ba[:, :, None], (DEPTH, kd, LANES)),
        gla_norm=gla_norm[:, None, :])


def _run_trunk(x, mem_k, mem_v, st_ssd, st_ssd_conv, st_gla, st_ffn_conv, wl, big, norm_final, *, bt, L, cfg):
    tm = cfg["tm"]
    n_h = n_c = n_s = None
    new_ffn = []
    for i in range(DEPTH):
        xn, small = _norm_small(x, big["norm_mix"], wl["w_small_t"], i, tm=cfg["tm_norm"])
        proj = _proj_in(xn, wl["w_in_t"], i, tm=cfg["tm_proj"])
        y_ssd, n_h, n_c = _ssd(proj, small, st_ssd_conv, st_ssd, wl, i, n_h, n_c, bt=bt, L=L, Q=cfg["ssd_q"])
        y_gla, n_s = _gla(proj, small, st_gla, wl, i, n_s, bt=bt, L=L, TB=cfg["gla_tb"])
        merged = _merge(y_ssd, y_gla, proj, big["w_branch"], i, tm=tm, tn=512)
        x = _mm_res(merged, big["w_out"], i, x, tm=tm, tn=1024)
        q = _norm_mm(x, big["norm_mem"], big["w_mq"], i, None, tm=tm, tn=512, out_dtype=BF16)
        o = _attn(q, mem_k, mem_v, i, bt=bt, L=L, tl=cfg["attn_tl"])
        x = _mm_res(o, big["w_mo"], i, x, tm=tm, tn=1024)
        xf = _norm_cast(x, big["norm_ffn"], i, tm=cfg["tm_norm"])
        act, f_c = _ffn_up(xf, big["w_ffn_in"], big["ffn_conv_w"], big["ffn_conv_b"], st_ffn_conv, i,
                           L=L, tm=tm, tn=512)
        x = _mm_res(act, big["w_ffn_out"], i, x, tm=cfg["tm_norm"], tn=512)
        new_ffn.append(f_c)
    y = _final_norm(x, norm_final, tm=cfg["tm_norm"])
    return y, n_h, n_c, n_s, jnp.stack(new_ffn)


def _group_cfg(bt, L):
    T = bt * L
    return dict(tm=min(T, 1024), tm_proj=min(T, 1024), tm_norm=min(T, 512), ssd_q=min(L, 128), gla_tb=min(L, 128), attn_tl=min(L, 512))


def kernel(x_prompt, x_sample, mem_prompt, state_ssd, state_ssd_conv, state_gla, state_ffn_conv, cache_mem_k, cache_mem_v, norm_mix, w_in, ssd_conv_w, ssd_conv_b, ssd_dt_bias, ssd_a_log, ssd_d, ssd_norm, gla_wa2, gla_ba, gla_norm, w_branch, w_out, norm_mem, w_mq, w_mk, w_mv, w_mo, norm_ffn, w_ffn_in, ffn_conv_w, ffn_conv_b, w_ffn_out, norm_final):
    wl = _prep_weights(w_in, ssd_conv_w, ssd_conv_b, ssd_dt_bias, ssd_a_log, ssd_d, ssd_norm, gla_wa2, gla_ba, gla_norm)
    big = dict(norm_mix=norm_mix, w_branch=w_branch, w_out=w_out, norm_mem=norm_mem, w_mq=w_mq, w_mo=w_mo,
               norm_ffn=norm_ffn, w_ffn_in=w_ffn_in, ffn_conv_w=ffn_conv_w, ffn_conv_b=ffn_conv_b, w_ffn_out=w_ffn_out)
    pb, pl_len, _ = x_prompt.shape
    sb, sl_len, _ = x_sample.shape
    mem2d = mem_prompt.reshape(pb * MEM_TOKENS, D_MODEL)
    p_mem_k = _mem_kv(mem2d, w_mk, nb=pb)
    p_mem_v = _mem_kv(mem2d, w_mv, nb=pb)
    head_major = (0, 1, 3, 2, 4)

    y_p, p_ssd, p_ssd_conv, p_gla, p_ffn = _run_trunk(
        x_prompt.reshape(pb * pl_len, D_MODEL), p_mem_k, p_mem_v, None, None, None, None, wl, big, norm_final,
        bt=pb, L=pl_len, cfg=_group_cfg(pb, pl_len))
    y_s, s_ssd, s_ssd_conv, s_gla, s_ffn = _run_trunk(
        x_sample.reshape(sb * sl_len, D_MODEL),
        jnp.transpose(cache_mem_k, head_major), jnp.transpose(cache_mem_v, head_major),
        state_ssd, state_ssd_conv, state_gla, state_ffn_conv, wl, big, norm_final,
        bt=sb, L=sl_len, cfg=_group_cfg(sb, sl_len))
    return (y_p.reshape(pb, pl_len, D_MODEL), y_s.reshape(sb, sl_len, D_MODEL),
            p_ssd, p_ssd_conv, p_gla, p_ffn, jnp.transpose(p_mem_k, head_major), jnp.transpose(p_mem_v, head_major),
            s_ssd, s_ssd_conv, s_gla, s_ffn)
```

```python
import functools

import jax
import jax.numpy as jnp
from jax import lax
from jax.experimental import pallas as pl
from jax.experimental.pallas import tpu as pltpu

F32 = jnp.float32
BF16 = jnp.bfloat16

D_MODEL = 2048
DEPTH = 2
EPS = 1e-6
SSD_HEADS = 32
SSD_HEADDIM = 64
SSD_GROUPS = 4
SSD_STATE = 128
SSD_CONV = 4
SSD_CONV_DIM = 3072
GLA_HEADS = 4
GLA_HEAD_K = 256
GLA_HEAD_V = 512
GLA_RANK = 16
GLA_GATE_NORM = 16.0
GLA_CHUNK = 16
MEM_TOKENS = 256
MEM_HEADS = 4
MEM_HEAD_DIM = 512
D_FF = 5632
FFN_CONV = 3

LANES = 128
MXU_COLS = 256
SUBLANES = 8
VMEM_CAP_BYTES = 56 * 2**20

P_Z, P_XS, P_V, P_G, P_GATE, P_BC, P_Q, P_K, P_N = 0, 2048, 4096, 6144, 8192, 12288, 13312, 14336, 15360
SM_DT, SM_ALR = 0, 32

NT_DIMS = (((1,), (1,)), ((), ()))
TN_DIMS = (((0,), (0,)), ((), ()))
HI = lax.Precision.HIGHEST


def _vmem_limit(*block_bytes, scratch=0):
    need = 2 * sum(block_bytes) + scratch + (4 << 20)
    return int(min(max(need, 16 << 20), VMEM_CAP_BYTES))


def _nbytes(shape, dtype):
    n = 1
    for s in shape:
        n *= s
    return n * jnp.dtype(dtype).itemsize


def _silu(x):
    return x * jax.nn.sigmoid(x)


def _softplus(x):
    return jnp.maximum(x, 0.0) + jnp.log1p(jnp.exp(-jnp.abs(x)))


def _log_sigmoid(x):
    return -_softplus(-x)


def _rms(x, gain):
    ms = jnp.mean(x * x, axis=-1, keepdims=True)
    return x * lax.rsqrt(ms + EPS) * gain


def _split2(x):
    hi = x.astype(BF16)
    lo = (x - hi.astype(F32)).astype(BF16)
    return hi, lo


def _select_rows(sel, x):
    hi = x.astype(BF16)
    r1 = x - hi.astype(F32)
    mid = r1.astype(BF16)
    lo = (r1 - mid.astype(F32)).astype(BF16)
    return (jnp.dot(sel, hi, preferred_element_type=F32) + jnp.dot(sel, mid, preferred_element_type=F32)
            + jnp.dot(sel, lo, preferred_element_type=F32))


def _stacked_out(prev, shape, dtype):
    return jax.ShapeDtypeStruct(shape, dtype) if prev is None else jax.ShapeDtypeStruct(prev.shape, prev.dtype)


def _norm_mm_kernel(*refs, has_gain, has_small):
    it = iter(refs)
    x_ref = next(it)
    g_ref = next(it) if has_gain else None
    w_ref = next(it)
    ws_ref = next(it) if has_small else None
    o_ref = next(it)
    os_ref = next(it) if has_small else None
    xn_ref = next(it)

    @pl.when(pl.program_id(1) == 0)
    def _():
        x = x_ref[...].astype(F32)
        if has_gain:
            x = _rms(x, g_ref[...])
        xn_ref[...] = x.astype(BF16)
        if has_small:
            os_ref[...] = jnp.dot(xn_ref[...], ws_ref[...], preferred_element_type=F32)

    o_ref[...] = jnp.dot(xn_ref[...], w_ref[...].astype(BF16), preferred_element_type=F32).astype(o_ref.dtype)


def _norm_mm(x, gain, w, layer, w_small, *, tm, tn, out_dtype):
    m, k = x.shape
    n = w.shape[2]
    has_gain = gain is not None
    has_small = w_small is not None
    in_specs = [pl.BlockSpec((tm, k), lambda i, j: (i, 0))]
    args = [x]
    if has_gain:
        in_specs.append(pl.BlockSpec((None, 1, k), lambda i, j: (layer, 0, 0)))
        args.append(gain.reshape(DEPTH, 1, k))
    in_specs.append(pl.BlockSpec((None, k, tn), lambda i, j: (layer, 0, j)))
    args.append(w)
    out_shape = [jax.ShapeDtypeStruct((m, n), out_dtype)]
    out_specs = [pl.BlockSpec((tm, tn), lambda i, j: (i, j))]
    if has_small:
        in_specs.append(pl.BlockSpec((None, k, LANES), lambda i, j: (layer, 0, 0)))
        args.append(w_small)
        out_shape.append(jax.ShapeDtypeStruct((m, LANES), F32))
        out_specs.append(pl.BlockSpec((tm, LANES), lambda i, j: (i, 0)))
    limit = _vmem_limit(_nbytes((tm, k), x.dtype), _nbytes((k, tn), w.dtype), _nbytes((tm, tn), out_dtype),
                        _nbytes((k, LANES), BF16), _nbytes((tm, LANES), F32), scratch=_nbytes((tm, k), BF16))
    res = pl.pallas_call(
        functools.partial(_norm_mm_kernel, has_gain=has_gain, has_small=has_small),
        grid=(m // tm, n // tn),
        in_specs=in_specs,
        out_specs=out_specs,
        out_shape=out_shape,
        scratch_shapes=[pltpu.VMEM((tm, k), BF16)],
        compiler_params=pltpu.CompilerParams(dimension_semantics=("parallel", "arbitrary"),
                                             vmem_limit_bytes=limit),
        name="norm_mm",
    )(*args)
    return res if has_small else res[0]


def _mem_kv_kernel(x_ref, w_ref, o_ref, xb_ref):
    @pl.when((pl.program_id(0) == 0) & (pl.program_id(1) == 0))
    def _():
        xb_ref[...] = x_ref[...].astype(BF16)

    res = jnp.dot(xb_ref[...], w_ref[...].astype(BF16), preferred_element_type=F32)
    o_ref[...] = res.reshape(o_ref.shape)


def _mem_kv(x, w, *, nb):
    m, k = x.shape
    tn = MEM_HEAD_DIM
    limit = _vmem_limit(_nbytes((m, k), F32), _nbytes((k, tn), F32), _nbytes((m, tn), F32),
                        scratch=_nbytes((m, k), BF16))
    return pl.pallas_call(
        _mem_kv_kernel,
        grid=(DEPTH, MEM_HEADS),
        in_specs=[pl.BlockSpec((m, k), lambda d, h: (0, 0)),
                  pl.BlockSpec((None, k, tn), lambda d, h: (d, 0, h))],
        out_specs=pl.BlockSpec((None, nb, None, m // nb, tn), lambda d, h: (d, 0, h, 0, 0)),
        out_shape=jax.ShapeDtypeStruct((DEPTH, nb, MEM_HEADS, m // nb, tn), F32),
        scratch_shapes=[pltpu.VMEM((m, k), BF16)],
        compiler_params=pltpu.CompilerParams(dimension_semantics=("arbitrary", "arbitrary"),
                                             vmem_limit_bytes=limit),
        name="mem_kv",
    )(x, w)


def _norm_cast_kernel(x_ref, g_ref, xn_ref):
    xn_ref[...] = _rms(x_ref[...], g_ref[...]).astype(BF16)


def _norm_cast(x, gain, layer, *, tm):
    m, k = x.shape
    return pl.pallas_call(
        _norm_cast_kernel,
        grid=(m // tm,),
        in_specs=[pl.BlockSpec((tm, k), lambda i: (i, 0)), pl.BlockSpec((None, 1, k), lambda i: (layer, 0, 0))],
        out_specs=pl.BlockSpec((tm, k), lambda i: (i, 0)),
        out_shape=jax.ShapeDtypeStruct((m, k), BF16),
        compiler_params=pltpu.CompilerParams(dimension_semantics=("parallel",)),
        name="norm_cast",
    )(x, gain.reshape(DEPTH, 1, k))


def _norm_small_kernel(x_ref, g_ref, wst_ref, xn_ref, os_ref):
    xn = _rms(x_ref[...], g_ref[...]).astype(BF16)
    xn_ref[...] = xn
    os_ref[...] = lax.dot_general(xn, wst_ref[...].astype(BF16), NT_DIMS, preferred_element_type=F32)


def _norm_small(x, gain, w_small_t, layer, *, tm):
    m, k = x.shape
    return pl.pallas_call(
        _norm_small_kernel,
        grid=(m // tm,),
        in_specs=[pl.BlockSpec((tm, k), lambda i: (i, 0)),
                  pl.BlockSpec((None, 1, k), lambda i: (layer, 0, 0)),
                  pl.BlockSpec((None, LANES, k), lambda i: (layer, 0, 0))],
        out_specs=[pl.BlockSpec((tm, k), lambda i: (i, 0)), pl.BlockSpec((tm, LANES), lambda i: (i, 0))],
        out_shape=[jax.ShapeDtypeStruct((m, k), BF16), jax.ShapeDtypeStruct((m, LANES), F32)],
        compiler_params=pltpu.CompilerParams(dimension_semantics=("parallel",)),
        name="norm_small",
    )(x, gain.reshape(DEPTH, 1, k), w_small_t)


PROJ_TN = 512
_PROJ_SEGMENTS = ((0, 2048), (2048, 2048), (7200, 2048), (9248, 2048), (11312, 4096), (4096, 1024), (5152, 1024),
                  (6176, 1024))
PROJ_SRC = tuple(s0 + c for s0, width in _PROJ_SEGMENTS for c in range(0, width, PROJ_TN))


def _proj_kernel(src_ref, xn_ref, wt_ref, o_ref, wbf_ref):
    @pl.when(pl.program_id(1) == 0)
    def _():
        wbf_ref[...] = wt_ref[0].astype(BF16)

    o_ref[...] = lax.dot_general(xn_ref[...], wbf_ref[...], NT_DIMS, preferred_element_type=F32).astype(o_ref.dtype)


def _proj_in(xn, w_in_t, layer, *, tm):
    m, k = xn.shape
    tn = PROJ_TN
    nj = len(PROJ_SRC)
    assert all(s % SUBLANES == 0 for s in PROJ_SRC)
    src = jnp.array([s // SUBLANES for s in PROJ_SRC], jnp.int32)
    limit = _vmem_limit(_nbytes((tm, k), BF16), _nbytes((tn, k), F32), _nbytes((tm, tn), BF16),
                        scratch=_nbytes((tn, k), BF16) + _nbytes((tm, tn), F32))
    grid_spec = pltpu.PrefetchScalarGridSpec(
        num_scalar_prefetch=1,
        grid=(nj, m // tm),
        in_specs=[pl.BlockSpec((tm, k), lambda j, i, s: (i, 0)),
                  pl.BlockSpec((pl.Element(1), pl.Element(tn), pl.Element(k)),
                               lambda j, i, s: (layer, s[j] * SUBLANES, 0))],
        out_specs=pl.BlockSpec((tm, tn), lambda j, i, s: (i, j)),
        scratch_shapes=[pltpu.VMEM((tn, k), BF16)])
    return pl.pallas_call(
        _proj_kernel,
        grid_spec=grid_spec,
        out_shape=jax.ShapeDtypeStruct((m, nj * tn), BF16),
        compiler_params=pltpu.CompilerParams(dimension_semantics=("arbitrary", "arbitrary"),
                                             vmem_limit_bytes=limit),
        name="proj_in",
    )(src, xn, w_in_t)


def _mm_res_kernel(x_ref, w_ref, r_ref, o_ref, wbf_ref):
    @pl.when(pl.program_id(1) == 0)
    def _():
        wbf_ref[...] = w_ref[...].astype(BF16)

    o_ref[...] = r_ref[...] + jnp.dot(x_ref[...], wbf_ref[...], preferred_element_type=F32)


def _mm_res(x, w, layer, res, *, tm, tn):
    m, k = x.shape
    n = w.shape[2]
    limit = _vmem_limit(_nbytes((tm, k), BF16), _nbytes((k, tn), F32), 2 * _nbytes((tm, tn), F32),
                        scratch=_nbytes((k, tn), BF16) + _nbytes((tm, tn), F32))
    return pl.pallas_call(
        _mm_res_kernel,
        grid=(n // tn, m // tm),
        in_specs=[pl.BlockSpec((tm, k), lambda j, i: (i, 0)),
                  pl.BlockSpec((None, k, tn), lambda j, i: (layer, 0, j)),
                  pl.BlockSpec((tm, tn), lambda j, i: (i, j))],
        out_specs=pl.BlockSpec((tm, tn), lambda j, i: (i, j)),
        out_shape=jax.ShapeDtypeStruct((m, n), F32),
        scratch_shapes=[pltpu.VMEM((k, tn), BF16)],
        compiler_params=pltpu.CompilerParams(dimension_semantics=("parallel", "arbitrary"),
                                             vmem_limit_bytes=limit),
        name="mm_res",
    )(x, w, res)


def _merge_kernel(y0_ref, y1_ref, g0_ref, g1_ref, w_ref, o_ref, wbf_ref):
    @pl.when(pl.program_id(1) == 0)
    def _():
        wbf_ref[...] = w_ref[...].astype(BF16)

    b0 = jnp.dot(y0_ref[...], wbf_ref[0], preferred_element_type=F32)
    b1 = jnp.dot(y1_ref[...], wbf_ref[1], preferred_element_type=F32)
    g0 = jax.nn.sigmoid(g0_ref[...].astype(F32))
    g1 = jax.nn.sigmoid(g1_ref[...].astype(F32))
    o_ref[...] = (g0 * b0 + g1 * b1).astype(o_ref.dtype)


def _merge(y_ssd, y_gla, proj, w_branch, layer, *, tm, tn):
    m, k = y_ssd.shape
    n = D_MODEL
    gb = P_GATE // tn
    nb = n // tn
    limit = _vmem_limit(2 * _nbytes((tm, k), BF16), 2 * _nbytes((k, tn), F32), 3 * _nbytes((tm, tn), BF16),
                        scratch=2 * _nbytes((k, tn), BF16))
    return pl.pallas_call(
        _merge_kernel,
        grid=(nb, m // tm),
        in_specs=[pl.BlockSpec((tm, k), lambda j, i: (i, 0)),
                  pl.BlockSpec((tm, k), lambda j, i: (i, 0)),
                  pl.BlockSpec((tm, tn), lambda j, i: (i, gb + j)),
                  pl.BlockSpec((tm, tn), lambda j, i: (i, gb + nb + j)),
                  pl.BlockSpec((None, 2, k, tn), lambda j, i: (layer, 0, 0, j))],
        out_specs=pl.BlockSpec((tm, tn), lambda j, i: (i, j)),
        out_shape=jax.ShapeDtypeStruct((m, n), BF16),
        scratch_shapes=[pltpu.VMEM((2, k, tn), BF16)],
        compiler_params=pltpu.CompilerParams(dimension_semantics=("parallel", "arbitrary"),
                                             vmem_limit_bytes=limit),
        name="merge",
    )(y_ssd, y_gla, proj, proj, w_branch)


def _final_norm_kernel(x_ref, g_ref, o_ref):
    o_ref[...] = _rms(x_ref[...], g_ref[...])


def _final_norm(x, gain, *, tm):
    m, k = x.shape
    return pl.pallas_call(
        _final_norm_kernel,
        grid=(m // tm,),
        in_specs=[pl.BlockSpec((tm, k), lambda i: (i, 0)), pl.BlockSpec((1, k), lambda i: (0, 0))],
        out_specs=pl.BlockSpec((tm, k), lambda i: (i, 0)),
        out_shape=jax.ShapeDtypeStruct((m, k), F32),
        compiler_params=pltpu.CompilerParams(dimension_semantics=("parallel",)),
        name="final_norm",
    )(x, gain.reshape(1, k))


def _ffn_up_kernel(*refs, tm, tn, seq_tiles, seg):
    short = seg > 0
    it = iter(refs)
    xn_ref, wu_ref, wt_ref, cwu_ref, cwt_ref, cbu_ref, cbt_ref = (next(it) for _ in range(7))
    stu_ref = next(it) if short else None
    stt_ref = next(it) if short else None
    act_ref, cnu_ref, cnt_ref = next(it), next(it), next(it)
    wbu_ref, wbt_ref = next(it), next(it)
    tailu_ref = None if short else next(it)
    tailt_ref = None if short else next(it)
    i = pl.program_id(1)
    sl = SUBLANES
    hw = FFN_CONV - 1

    @pl.when(i == 0)
    def _():
        wbu_ref[...] = wu_ref[...].astype(BF16)
        wbt_ref[...] = wt_ref[...].astype(BF16)
        if not short:
            tailu_ref[...] = jnp.zeros_like(tailu_ref)
            tailt_ref[...] = jnp.zeros_like(tailt_ref)

    xn = xn_ref[...]
    u_raw = jnp.dot(xn, wbu_ref[...], preferred_element_type=F32)
    t_raw = jnp.dot(xn, wbt_ref[...], preferred_element_type=F32)

    if short:
        nseq = tm // seg
        r = lax.broadcasted_iota(jnp.int32, (tm, hw * nseq), 0)
        c = lax.broadcasted_iota(jnp.int32, (tm, hw * nseq), 1)
        halo_u, halo_t = [], []
        for s in range(1, FFN_CONV):
            sel = (((r % seg) < s) & (c == hw * (r // seg) + hw - s + (r % seg))).astype(BF16)
            halo_u.append(_select_rows(sel, stu_ref[...]))
            halo_t.append(_select_rows(sel, stt_ref[...]))
        ro = lax.broadcasted_iota(jnp.int32, (hw * nseq, tm), 0)
        co = lax.broadcasted_iota(jnp.int32, (hw * nseq, tm), 1)
        sel_out = (co == (ro // hw) * seg + seg - hw + (ro % hw)).astype(BF16)
        cnu_ref[...] = _select_rows(sel_out, u_raw)
        cnt_ref[...] = _select_rows(sel_out, t_raw)
        rowmod = lax.broadcasted_iota(jnp.int32, (tm, tn), 0) % seg
        prev_u = prev_t = None
    else:
        start = (i % seq_tiles) == 0
        prev_u = jnp.where(start, 0.0, tailu_ref[...])
        prev_t = jnp.where(start, 0.0, tailt_ref[...])
        tailu_ref[...] = u_raw[tm - sl:tm, :]
        tailt_ref[...] = t_raw[tm - sl:tm, :]
        cnu_ref[...] = u_raw[tm - sl:tm, :]
        cnt_ref[...] = t_raw[tm - sl:tm, :]
        row8 = lax.broadcasted_iota(jnp.int32, (sl, tn), 0)
        halo_u = halo_t = rowmod = None

    def conv(x, prev8, cw_ref, cb_ref, halos):
        y = x * cw_ref[hw:hw + 1, :]
        for s in range(1, FFN_CONV):
            rolled = pltpu.roll(x, s, axis=0)
            if short:
                shifted = jnp.where(rowmod < s, halos[s - 1], rolled)
            else:
                head = jnp.where(row8 < s, pltpu.roll(prev8, s, axis=0), rolled[0:sl, :])
                shifted = jnp.concatenate([head, rolled[sl:, :]], axis=0)
            y = y + shifted * cw_ref[hw - s:hw - s + 1, :]
        return y + cb_ref[...]

    u = conv(u_raw, prev_u, cwu_ref, cbu_ref, halo_u)
    t = conv(t_raw, prev_t, cwt_ref, cbt_ref, halo_t)
    act_ref[...] = (_silu(t) * u).astype(act_ref.dtype)


def _ffn_up(xn, w, conv_w, conv_b, conv0, layer, *, L, tm, tn):
    m, k = xn.shape
    bt = m // L
    nj = D_FF // tn
    hw = FFN_CONV - 1
    short = L < tm
    seg = L if short else 0
    seq_tiles = 1 if short else L // tm
    assert (conv0 is not None) == short, "history rows are only supported for sequences shorter than a block"
    in_specs = [
        pl.BlockSpec((tm, k), lambda j, i: (i, 0)),
        pl.BlockSpec((None, k, tn), lambda j, i: (layer, 0, j)),
        pl.BlockSpec((None, k, tn), lambda j, i: (layer, 0, nj + j)),
        pl.BlockSpec((None, FFN_CONV, tn), lambda j, i: (layer, 0, j)),
        pl.BlockSpec((None, FFN_CONV, tn), lambda j, i: (layer, 0, nj + j)),
        pl.BlockSpec((None, 1, tn), lambda j, i: (layer, 0, j)),
        pl.BlockSpec((None, 1, tn), lambda j, i: (layer, 0, nj + j)),
    ]
    args = [xn, w, w, conv_w, conv_w, conv_b.reshape(DEPTH, 1, 2 * D_FF), conv_b.reshape(DEPTH, 1, 2 * D_FF)]
    scratch = [pltpu.VMEM((k, tn), BF16)] * 2
    if short:
        nst = (m // tm) * (tm // L) * hw
        st2d = conv0.reshape(DEPTH, nst, 2 * D_FF)
        rows = (tm // L) * hw
        in_specs += [pl.BlockSpec((None, rows, tn), lambda j, i: (layer, i, j)),
                     pl.BlockSpec((None, rows, tn), lambda j, i: (layer, i, nj + j))]
        args += [st2d, st2d]
        cn_shape = jax.ShapeDtypeStruct((nst, D_FF), F32)
        cn_spec = pl.BlockSpec((rows, tn), lambda j, i: (i, j))
    else:
        scratch += [pltpu.VMEM((SUBLANES, tn), F32)] * 2
        cn_shape = jax.ShapeDtypeStruct((m // tm, SUBLANES, D_FF), F32)
        cn_spec = pl.BlockSpec((None, SUBLANES, tn), lambda j, i: (i, 0, j))
    limit = _vmem_limit(_nbytes((tm, k), BF16), 2 * _nbytes((k, tn), F32), _nbytes((tm, tn), BF16),
                        scratch=2 * _nbytes((k, tn), BF16) + 6 * _nbytes((tm, tn), F32))
    act, cnu, cnt = pl.pallas_call(
        functools.partial(_ffn_up_kernel, tm=tm, tn=tn, seq_tiles=seq_tiles, seg=seg),
        grid=(nj, m // tm),
        in_specs=in_specs,
        out_specs=[pl.BlockSpec((tm, tn), lambda j, i: (i, j)), cn_spec, cn_spec],
        out_shape=[jax.ShapeDtypeStruct((m, D_FF), BF16), cn_shape, cn_shape],
        scratch_shapes=scratch,
        compiler_params=pltpu.CompilerParams(dimension_semantics=("arbitrary", "arbitrary"),
                                             vmem_limit_bytes=limit),
        name="ffn_up",
    )(*args)
    if short:
        conv_new = jnp.concatenate([cnu, cnt], axis=-1).reshape(bt, hw, 2 * D_FF)
    else:
        last = slice(seq_tiles - 1, None, seq_tiles)
        conv_new = jnp.concatenate([cnu[last, SUBLANES - hw:, :], cnt[last, SUBLANES - hw:, :]], axis=-1)
    return act, conv_new


def _causal_conv(x, tail_ref, w_ref, b, width):
    sl = SUBLANES
    row = lax.broadcasted_iota(jnp.int32, (sl, x.shape[1]), 0)
    prev = tail_ref[...]
    y = x * w_ref[width - 1:width, :]
    for s in range(1, width):
        rolled = pltpu.roll(x, s, axis=0)
        head = jnp.where(row < s, pltpu.roll(prev, s, axis=0), rolled[0:sl, :])
        shifted = jnp.concatenate([head, rolled[sl:, :]], axis=0)
        y = y + shifted * w_ref[width - 1 - s:width - s, :]
    return y + b


def _ssd_kernel(*refs, Q, zero_init, aliased):
    it = iter(refs)
    z_ref, xs_ref, bc_ref, sm_ref = (next(it) for _ in range(4))
    conv0_ref = None if zero_init else next(it)
    h0_ref = None if zero_init else next(it)
    cwx_ref, cwb_ref, cbx_ref, cbb_ref, dtb_ref, a_ref, dx_ref, nrm_ref, exp_ref = (next(it) for _ in range(9))
    if aliased:
        next(it), next(it)
    y_ref, h_out, conv_out, tailx_ref, tailb_ref = (next(it) for _ in range(5))
    h_ref = h_out if aliased else h_out.at[0]
    convout_ref = conv_out if aliased else conv_out.at[0]
    c = pl.program_id(1)
    hp = SSD_HEADDIM
    gw = SSD_HEADS // SSD_GROUPS * hp

    @pl.when(c == 0)
    def _():
        if not aliased:
            h_out[1:] = jnp.zeros((DEPTH - 1,) + h_ref.shape, F32)
            conv_out[1:] = jnp.zeros((DEPTH - 1,) + convout_ref.shape, F32)
        tailx_ref[...] = jnp.zeros_like(tailx_ref)
        tailb_ref[...] = jnp.zeros_like(tailb_ref)
        if zero_init:
            h_ref[...] = jnp.zeros_like(h_ref)
        else:
            h_ref[...] = h0_ref[...]
            tailx_ref[5:8, :] = conv0_ref[:, 0:D_MODEL]
            tailb_ref[5:8, :] = conv0_ref[:, D_MODEL:SSD_CONV_DIM]

    xs_raw = xs_ref[...].astype(F32)
    bc_raw = bc_ref[...].astype(F32)
    xs = _silu(_causal_conv(xs_raw, tailx_ref, cwx_ref, cbx_ref[...], SSD_CONV))
    bc = _silu(_causal_conv(bc_raw, tailb_ref, cwb_ref, cbb_ref[...], SSD_CONV))
    tailx_ref[...] = xs_raw[Q - 8:Q, :]
    tailb_ref[...] = bc_raw[Q - 8:Q, :]

    @pl.when(c == pl.num_programs(1) - 1)
    def _():
        convout_ref[:, 0:D_MODEL] = tailx_ref[5:8, :]
        convout_ref[:, D_MODEL:SSD_CONV_DIM] = tailb_ref[5:8, :]

    dt = _softplus(sm_ref[...] + dtb_ref[...])
    adt = dt * a_ref[...]
    ri = lax.broadcasted_iota(jnp.int32, (Q, Q), 0)
    ci = lax.broadcasted_iota(jnp.int32, (Q, Q), 1)
    causal = ci <= ri
    acum = jnp.dot(causal.astype(F32), adt, precision=HI, preferred_element_type=F32)
    eye = (lax.broadcasted_iota(jnp.int32, (LANES, LANES), 0)
           == lax.broadcasted_iota(jnp.int32, (LANES, LANES), 1)).astype(F32)
    acum_t = lax.dot_general(eye, acum, NT_DIMS, precision=HI, preferred_element_type=F32)
    dt_t = lax.dot_general(eye, dt, NT_DIMS, precision=HI, preferred_element_type=F32)
    a_last = acum[Q - 1:Q, :]
    ea = jnp.exp(acum)
    te = jnp.exp(a_last - acum) * dt
    dec_rows = jnp.broadcast_to(jnp.exp(acum_t[:, Q - 1:Q]), (LANES, LANES))
    ea_hi, ea_lo = _split2(ea)
    te_hi, te_lo = _split2(te)
    expand = exp_ref[...]
    ea_x = (jnp.dot(ea_hi, expand, preferred_element_type=F32)
            + jnp.dot(ea_lo, expand, preferred_element_type=F32))
    te_x = (jnp.dot(te_hi, expand, preferred_element_type=F32)
            + jnp.dot(te_lo, expand, preferred_element_type=F32))
    lane = lax.broadcasted_iota(jnp.int32, (Q, LANES), 1)

    for g in range(SSD_GROUPS):
        gl = slice(g * gw, (g + 1) * gw)
        b_g = bc[:, g * SSD_STATE:(g + 1) * SSD_STATE].astype(BF16)
        c_g = bc[:, (SSD_GROUPS + g) * SSD_STATE:(SSD_GROUPS + g + 1) * SSD_STATE].astype(BF16)
        cb = lax.dot_general(c_g, b_g, NT_DIMS, preferred_element_type=F32)
        h_g = h_ref[8 * g:8 * g + 8].reshape(gw, SSD_STATE)
        y_off = lax.dot_general(c_g, h_g.astype(BF16), NT_DIMS, preferred_element_type=F32)
        x_g = xs[:, gl]
        pairs = []
        for p in range(4):
            x_p = x_g[:, p * LANES:(p + 1) * LANES]
            acc = None
            for s in range(2):
                hh = g * 8 + p * 2 + s
                seg = jnp.broadcast_to(acum[:, hh:hh + 1], (Q, Q)) - jnp.broadcast_to(acum_t[hh:hh + 1, :], (Q, Q))
                decay = jnp.where(causal, jnp.exp(seg), 0.0)
                w_h = (cb * decay * dt_t[hh:hh + 1, :]).astype(BF16)
                x_m = jnp.where((lane // hp) == s, x_p, 0.0).astype(BF16)
                r = jnp.dot(w_h, x_m, preferred_element_type=F32)
                acc = r if acc is None else acc + r
            pairs.append(acc)
        y_g = jnp.concatenate(pairs, axis=1) + y_off * ea_x[:, gl] + dx_ref[:, gl] * x_g
        x_t = (x_g * te_x[:, gl]).astype(BF16)
        upd = lax.dot_general(x_t, b_g, TN_DIMS, preferred_element_type=F32)
        for h in range(8):
            hh = g * 8 + h
            h_ref[hh] = h_ref[hh] * dec_rows[hh:hh + 1, :] + upd[h * hp:(h + 1) * hp, :]
        z_g = z_ref[:, gl].astype(F32)
        y_ref[:, gl] = _rms(y_g * _silu(z_g), nrm_ref[:, gl]).astype(y_ref.dtype)


def _ssd(proj, small, conv0, h0, wl, layer, prev_h, prev_conv, *, bt, L, Q):
    zero_init = h0 is None
    aliased = prev_h is not None
    nc = L // Q
    T = bt * L
    full2 = lambda b, c: (0, 0)
    lay3 = lambda b, c: (layer, 0, 0)
    row = lambda col: (lambda b, c: (b * nc + c, col))
    hshape = (DEPTH, bt, SSD_HEADS, SSD_HEADDIM, SSD_STATE)
    cshape = (DEPTH, bt, SSD_CONV - 1, SSD_CONV_DIM)
    hspec = pl.BlockSpec((None, None, SSD_HEADS, SSD_HEADDIM, SSD_STATE), lambda b, c: (layer, b, 0, 0, 0))
    cspec = pl.BlockSpec((None, None, SSD_CONV - 1, SSD_CONV_DIM), lambda b, c: (layer, b, 0, 0))
    h_out_spec = hspec if aliased else pl.BlockSpec((DEPTH, None, SSD_HEADS, SSD_HEADDIM, SSD_STATE),
                                                    lambda b, c: (0, b, 0, 0, 0))
    c_out_spec = cspec if aliased else pl.BlockSpec((DEPTH, None, SSD_CONV - 1, SSD_CONV_DIM),
                                                    lambda b, c: (0, b, 0, 0))
    in_specs = [
        pl.BlockSpec((Q, D_MODEL), row(P_Z // D_MODEL)),
        pl.BlockSpec((Q, D_MODEL), row(P_XS // D_MODEL)),
        pl.BlockSpec((Q, 1024), row(P_BC // 1024)),
        pl.BlockSpec((Q, LANES), row(0)),
    ]
    args = [proj, proj, proj, small]
    if not zero_init:
        in_specs += [cspec, hspec]
        args += [conv0, h0]
    in_specs += [
        pl.BlockSpec((None, SSD_CONV, D_MODEL), lay3),
        pl.BlockSpec((None, SSD_CONV, 1024), lambda b, c: (layer, 0, D_MODEL // 1024)),
        pl.BlockSpec((None, 1, D_MODEL), lay3),
        pl.BlockSpec((None, 1, 1024), lambda b, c: (layer, 0, D_MODEL // 1024)),
        pl.BlockSpec((None, 1, LANES), lay3),
        pl.BlockSpec((None, 1, LANES), lay3),
        pl.BlockSpec((None, 1, D_MODEL), lay3),
        pl.BlockSpec((None, 1, D_MODEL), lay3),
        pl.BlockSpec((LANES, D_MODEL), full2),
    ]
    args += [wl["ssd_conv_w"], wl["ssd_conv_w"], wl["ssd_conv_b"], wl["ssd_conv_b"], wl["ssd_dtb"], wl["ssd_a"],
             wl["ssd_dx"], wl["ssd_norm"], wl["expand"]]
    aliases = {}
    if aliased:
        aliases = {len(args): 1, len(args) + 1: 2}
        in_specs += [pl.BlockSpec(memory_space=pl.ANY), pl.BlockSpec(memory_space=pl.ANY)]
        args += [prev_h, prev_conv]
    return pl.pallas_call(
        functools.partial(_ssd_kernel, Q=Q, zero_init=zero_init, aliased=aliased),
        grid=(bt, nc),
        in_specs=in_specs,
        out_specs=[pl.BlockSpec((Q, D_MODEL), row(0)), h_out_spec, c_out_spec],
        out_shape=[jax.ShapeDtypeStruct((T, D_MODEL), BF16), _stacked_out(prev_h, hshape, F32),
                   _stacked_out(prev_conv, cshape, F32)],
        scratch_shapes=[pltpu.VMEM((8, D_MODEL), F32), pltpu.VMEM((8, 1024), F32)],
        input_output_aliases=aliases,
        compiler_params=pltpu.CompilerParams(dimension_semantics=("parallel", "arbitrary"),
                                             vmem_limit_bytes=48 << 20),
        name="ssd",
    )(*args)


def _gla_kernel(*refs, TB, zero_init, aliased):
    it = iter(refs)
    q_ref, k_ref, v_ref, g_ref, sm_ref = (next(it) for _ in range(5))
    s0_ref = None if zero_init else next(it)
    wa_ref, ba_ref, wat_ref, bat_ref, gn_ref = (next(it) for _ in range(5))
    if aliased:
        next(it)
    y_ref, s_out = next(it), next(it)
    s_ref = s_out if aliased else s_out.at[0]
    c = pl.program_id(1)
    ck = GLA_CHUNK
    ns = TB // ck
    hk, hv = GLA_HEAD_K, GLA_HEAD_V

    @pl.when(c == 0)
    def _():
        if not aliased:
            s_out[1:] = jnp.zeros((DEPTH - 1,) + s_ref.shape, F32)
        if zero_init:
            s_ref[...] = jnp.zeros_like(s_ref)
        else:
            s_ref[...] = s0_ref[...]

    smb = sm_ref[...].astype(BF16)
    la = _log_sigmoid(jnp.dot(smb, wa_ref[...], preferred_element_type=F32) + ba_ref[...]) * (1.0 / GLA_GATE_NORM)
    rb = lax.broadcasted_iota(jnp.int32, (TB, TB), 0)
    cb = lax.broadcasted_iota(jnp.int32, (TB, TB), 1)
    same = (rb // ck) == (cb // ck)
    msel = jnp.concatenate([same & (cb <= rb), same, (cb // ck) < (rb // ck)], axis=0).astype(BF16)
    sums = _select_rows(msel, la)
    bcum, tot, bprev = sums[0:TB], sums[TB:2 * TB], sums[2 * TB:3 * TB]
    kf = k_ref[...].astype(F32)
    qd = q_ref[...].astype(F32) * (hk ** -0.5) * jnp.exp(bcum)
    qd_b = qd.astype(BF16)
    qs_b = (qd * jnp.exp(bprev)).astype(BF16)
    ki = kf * jnp.exp(-bcum)
    ke = kf * jnp.exp(tot - bcum)
    dtot = jnp.exp(tot)
    la_t = _log_sigmoid(lax.dot_general(wat_ref[...], smb, NT_DIMS, preferred_element_type=F32)
                        + bat_ref[:, 0:1]) * (1.0 / GLA_GATE_NORM)
    ones = jnp.ones((TB, LANES), BF16)
    la_hi = la_t.astype(BF16)
    la_r = la_t - la_hi.astype(F32)
    la_mid = la_r.astype(BF16)
    la_lo = (la_r - la_mid.astype(F32)).astype(BF16)
    dec_t = jnp.exp(jnp.dot(la_hi, ones, preferred_element_type=F32) + jnp.dot(la_mid, ones, preferred_element_type=F32)
                    + jnp.dot(la_lo, ones, preferred_element_type=F32))
    rchunk = lax.broadcasted_iota(jnp.int32, (TB, hk), 0) // ck
    row16 = lax.broadcasted_iota(jnp.int32, (ck, TB), 0)
    col16 = lax.broadcasted_iota(jnp.int32, (ck, TB), 1)

    for h in range(GLA_HEADS):
        kl = slice(h * hk, (h + 1) * hk)
        vl = slice(h * hv, (h + 1) * hv)
        ke_h, ki_h = ke[:, kl], ki[:, kl]
        v_h = v_ref[:, vl]
        kbuf = jnp.zeros((TB, hk), F32)
        att = []
        for cc in range(ns):
            in_c = rchunk == cc
            kall = jnp.where(in_c, ki_h, kbuf).astype(BF16)
            a = lax.dot_general(qd_b[cc * ck:(cc + 1) * ck, kl], kall, NT_DIMS, preferred_element_type=F32)
            att.append(jnp.where(col16 <= row16 + cc * ck, a, 0.0))
            kbuf = jnp.where(in_c, ke_h, kbuf * dtot[cc * ck:cc * ck + 1, kl])
        a_full = jnp.concatenate(att, axis=0).astype(BF16)
        s_old = s_ref[h]
        o = (jnp.dot(a_full, v_h, preferred_element_type=F32)
             + jnp.dot(qs_b[:, kl], s_old.astype(BF16), preferred_element_type=F32))
        y_ref[:, vl] = (_rms(o, gn_ref[...]) * _silu(g_ref[:, vl].astype(F32))).astype(y_ref.dtype)
        dcol = jnp.concatenate([dec_t[kl, :]] * (hv // LANES), axis=1)
        s_ref[h] = s_old * dcol + lax.dot_general(kbuf.astype(BF16), v_h, TN_DIMS, preferred_element_type=F32)


def _gla(proj, small, s0, wl, layer, prev_s, *, bt, L, TB):
    zero_init = s0 is None
    aliased = prev_s is not None
    nc = L // TB
    T = bt * L
    lay3 = lambda b, c: (layer, 0, 0)
    row = lambda col: (lambda b, c: (b * nc + c, col))
    kd = GLA_HEADS * GLA_HEAD_K
    sshape = (DEPTH, bt, GLA_HEADS, GLA_HEAD_K, GLA_HEAD_V)
    sspec = pl.BlockSpec((None, None, GLA_HEADS, GLA_HEAD_K, GLA_HEAD_V), lambda b, c: (layer, b, 0, 0, 0))
    s_out_spec = sspec if aliased else pl.BlockSpec((DEPTH, None, GLA_HEADS, GLA_HEAD_K, GLA_HEAD_V),
                                                     lambda b, c: (0, b, 0, 0, 0))
    in_specs = [
        pl.BlockSpec((TB, kd), row(P_Q // kd)),
        pl.BlockSpec((TB, kd), row(P_K // kd)),
        pl.BlockSpec((TB, D_MODEL), row(P_V // D_MODEL)),
        pl.BlockSpec((TB, D_MODEL), row(P_G // D_MODEL)),
        pl.BlockSpec((TB, LANES), row(0)),
    ]
    args = [proj, proj, proj, proj, small]
    if not zero_init:
        in_specs.append(sspec)
        args.append(s0)
    in_specs += [
        pl.BlockSpec((None, LANES, kd), lay3),
        pl.BlockSpec((None, 1, kd), lay3),
        pl.BlockSpec((None, kd, LANES), lay3),
        pl.BlockSpec((None, kd, LANES), lay3),
        pl.BlockSpec((None, 1, GLA_HEAD_V), lay3),
    ]
    args += [wl["gla_wa"], wl["gla_ba"], wl["gla_wa_t"], wl["gla_ba_t"], wl["gla_norm"]]
    aliases = {}
    if aliased:
        aliases = {len(args): 1}
        in_specs.append(pl.BlockSpec(memory_space=pl.ANY))
        args.append(prev_s)
    return pl.pallas_call(
        functools.partial(_gla_kernel, TB=TB, zero_init=zero_init, aliased=aliased),
        grid=(bt, nc),
        in_specs=in_specs,
        out_specs=[pl.BlockSpec((TB, D_MODEL), row(0)), s_out_spec],
        out_shape=[jax.ShapeDtypeStruct((T, D_MODEL), BF16), _stacked_out(prev_s, sshape, F32)],
        input_output_aliases=aliases,
        compiler_params=pltpu.CompilerParams(dimension_semantics=("parallel", "arbitrary"),
                                             vmem_limit_bytes=48 << 20),
        name="gla",
    )(*args)


def _attn_kernel(q_ref, k_ref, v_ref, o_ref, kb_ref, vb_ref):
    @pl.when(pl.program_id(1) == 0)
    def _():
        kb_ref[...] = k_ref[...].astype(BF16)
        vb_ref[...] = v_ref[...].astype(BF16)

    hd = MEM_HEAD_DIM
    for h in range(MEM_HEADS):
        hl = slice(h * hd, (h + 1) * hd)
        s = lax.dot_general(q_ref[:, hl], kb_ref[h], NT_DIMS, preferred_element_type=F32) * (hd ** -0.5)
        e = jnp.exp(s - jnp.max(s, axis=-1, keepdims=True))
        p = e / jnp.sum(e, axis=-1, keepdims=True)
        o_ref[:, hl] = jnp.dot(p.astype(BF16), vb_ref[h], preferred_element_type=F32).astype(o_ref.dtype)


def _attn(q, mem_k, mem_v, layer, *, bt, L, tl):
    nl = L // tl
    T = bt * L
    kvshape = (MEM_HEADS, MEM_TOKENS, MEM_HEAD_DIM)
    kvspec = pl.BlockSpec((None, None) + kvshape, lambda b, l: (layer, b, 0, 0, 0))
    return pl.pallas_call(
        _attn_kernel,
        grid=(bt, nl),
        in_specs=[pl.BlockSpec((tl, D_MODEL), lambda b, l: (b * nl + l, 0)), kvspec, kvspec],
        out_specs=pl.BlockSpec((tl, D_MODEL), lambda b, l: (b * nl + l, 0)),
        out_shape=jax.ShapeDtypeStruct((T, D_MODEL), BF16),
        scratch_shapes=[pltpu.VMEM(kvshape, BF16), pltpu.VMEM(kvshape, BF16)],
        compiler_params=pltpu.CompilerParams(dimension_semantics=("parallel", "arbitrary"),
                                             vmem_limit_bytes=40 << 20),
        name="mem_attn",
    )(q, mem_k, mem_v)


def _prep_weights(w_in, ssd_conv_w, ssd_conv_b, ssd_dt_bias, ssd_a_log, ssd_d, ssd_norm, gla_wa2, gla_ba, gla_norm):
    w_in_t = jnp.swapaxes(w_in, 1, 2)
    o_dt, o_alr = 5120, 11296
    pad_sm = LANES - SSD_HEADS - GLA_RANK
    w_small_t = jnp.concatenate(
        [w_in_t[:, o_dt:o_dt + SSD_HEADS, :], w_in_t[:, o_alr:o_alr + GLA_RANK, :],
         jnp.zeros((DEPTH, pad_sm, D_MODEL), F32)], axis=1)
    pad_h = LANES - SSD_HEADS
    kd = GLA_HEADS * GLA_HEAD_K
    wa = jnp.concatenate([jnp.zeros((DEPTH, SM_ALR, kd), F32), gla_wa2,
                          jnp.zeros((DEPTH, LANES - SM_ALR - GLA_RANK, kd), F32)], axis=1).astype(BF16)
    expand = (jnp.arange(D_MODEL)[None, :] // SSD_HEADDIM == jnp.arange(LANES)[:, None]).astype(BF16)
    return dict(
        w_in_t=w_in_t, w_small_t=w_small_t,
        ssd_conv_w=ssd_conv_w, ssd_conv_b=ssd_conv_b[:, None, :],
        ssd_dtb=jnp.pad(ssd_dt_bias, ((0, 0), (0, pad_h)))[:, None, :],
        ssd_a=jnp.pad(-jnp.exp(ssd_a_log), ((0, 0), (0, pad_h)))[:, None, :],
        ssd_dx=jnp.repeat(ssd_d, SSD_HEADDIM, axis=1)[:, None, :],
        ssd_norm=ssd_norm[:, None, :], expand=expand,
        gla_wa=wa, gla_ba=gla_ba[:, None, :], gla_wa_t=jnp.swapaxes(wa, 1, 2),
        gla_ba_t=jnp.broadcast_to(gla_ba[:, :, None], (DEPTH, kd, LANES)),
        gla_norm=gla_norm[:, None, :])


def _run_trunk(x, mem_k, mem_v, st_ssd, st_ssd_conv, st_gla, st_ffn_conv, wl, big, norm_final, *, bt, L, cfg):
    tm = cfg["tm"]
    n_h = n_c = n_s = None
    new_ffn = []
    for i in range(DEPTH):
        xn, small = _norm_small(x, big["norm_mix"], wl["w_small_t"], i, tm=cfg["tm_norm"])
        proj = _proj_in(xn, wl["w_in_t"], i, tm=cfg["tm_proj"])
        y_ssd, n_h, n_c = _ssd(proj, small, st_ssd_conv, st_ssd, wl, i, n_h, n_c, bt=bt, L=L, Q=cfg["ssd_q"])
        y_gla, n_s = _gla(proj, small, st_gla, wl, i, n_s, bt=bt, L=L, TB=cfg["gla_tb"])
        merged = _merge(y_ssd, y_gla, proj, big["w_branch"], i, tm=tm, tn=512)
        x = _mm_res(merged, big["w_out"], i, x, tm=cfg["tm_proj"], tn=512)
        q = _norm_mm(x, big["norm_mem"], big["w_mq"], i, None, tm=tm, tn=512, out_dtype=BF16)
        o = _attn(q, mem_k, mem_v, i, bt=bt, L=L, tl=cfg["attn_tl"])
        x = _mm_res(o, big["w_mo"], i, x, tm=cfg["tm_proj"], tn=512)
        xf = _norm_cast(x, big["norm_ffn"], i, tm=cfg["tm_norm"])
        act, f_c = _ffn_up(xf, big["w_ffn_in"], big["ffn_conv_w"], big["ffn_conv_b"], st_ffn_conv, i,
                           L=L, tm=tm, tn=512)
        x = _mm_res(act, big["w_ffn_out"], i, x, tm=cfg["tm_norm"], tn=512)
        new_ffn.append(f_c)
    y = _final_norm(x, norm_final, tm=cfg["tm_norm"])
    return y, n_h, n_c, n_s, jnp.stack(new_ffn)


def _group_cfg(bt, L):
    T = bt * L
    return dict(tm=min(T, 1024), tm_proj=min(T, 2048), tm_norm=min(T, 512), ssd_q=min(L, 128), gla_tb=min(L, 128), attn_tl=min(L, 512))


def kernel(x_prompt, x_sample, mem_prompt, state_ssd, state_ssd_conv, state_gla, state_ffn_conv, cache_mem_k, cache_mem_v, norm_mix, w_in, ssd_conv_w, ssd_conv_b, ssd_dt_bias, ssd_a_log, ssd_d, ssd_norm, gla_wa2, gla_ba, gla_norm, w_branch, w_out, norm_mem, w_mq, w_mk, w_mv, w_mo, norm_ffn, w_ffn_in, ffn_conv_w, ffn_conv_b, w_ffn_out, norm_final):
    wl = _prep_weights(w_in, ssd_conv_w, ssd_conv_b, ssd_dt_bias, ssd_a_log, ssd_d, ssd_norm, gla_wa2, gla_ba, gla_norm)
    big = dict(norm_mix=norm_mix, w_branch=w_branch, w_out=w_out, norm_mem=norm_mem, w_mq=w_mq, w_mo=w_mo,
               norm_ffn=norm_ffn, w_ffn_in=w_ffn_in, ffn_conv_w=ffn_conv_w, ffn_conv_b=ffn_conv_b, w_ffn_out=w_ffn_out)
    pb, pl_len, _ = x_prompt.shape
    sb, sl_len, _ = x_sample.shape
    mem2d = mem_prompt.reshape(pb * MEM_TOKENS, D_MODEL)
    p_mem_k = _mem_kv(mem2d, w_mk, nb=pb)
    p_mem_v = _mem_kv(mem2d, w_mv, nb=pb)
    head_major = (0, 1, 3, 2, 4)

    y_p, p_ssd, p_ssd_conv, p_gla, p_ffn = _run_trunk(
        x_prompt.reshape(pb * pl_len, D_MODEL), p_mem_k, p_mem_v, None, None, None, None, wl, big, norm_final,
        bt=pb, L=pl_len, cfg=_group_cfg(pb, pl_len))
    y_s, s_ssd, s_ssd_conv, s_gla, s_ffn = _run_trunk(
        x_sample.reshape(sb * sl_len, D_MODEL),
        jnp.transpose(cache_mem_k, head_major), jnp.transpose(cache_mem_v, head_major),
        state_ssd, state_ssd_conv, state_gla, state_ffn_conv, wl, big, norm_final,
        bt=sb, L=sl_len, cfg=_group_cfg(sb, sl_len))
    return (y_p.reshape(pb, pl_len, D_MODEL), y_s.reshape(sb, sl_len, D_MODEL),
            p_ssd, p_ssd_conv, p_gla, p_ffn, jnp.transpose(p_mem_k, head_major), jnp.transpose(p_mem_v, head_major),
            s_ssd, s_ssd_conv, s_gla, s_ffn)
```

```python
import functools

import jax
import jax.numpy as jnp
from jax import lax
from jax.experimental import pallas as pl
from jax.experimental.pallas import tpu as pltpu

F32 = jnp.float32
BF16 = jnp.bfloat16

D_MODEL = 2048
DEPTH = 2
EPS = 1e-6
SSD_HEADS = 32
SSD_HEADDIM = 64
SSD_GROUPS = 4
SSD_STATE = 128
SSD_CONV = 4
SSD_CONV_DIM = 3072
GLA_HEADS = 4
GLA_HEAD_K = 256
GLA_HEAD_V = 512
GLA_RANK = 16
GLA_GATE_NORM = 16.0
GLA_CHUNK = 16
MEM_TOKENS = 256
MEM_HEADS = 4
MEM_HEAD_DIM = 512
D_FF = 5632
FFN_CONV = 3

LANES = 128
MXU_COLS = 256
SUBLANES = 8
VMEM_CAP_BYTES = 56 * 2**20

P_Z, P_XS, P_V, P_G, P_GATE, P_BC, P_Q, P_K, P_N = 0, 2048, 4096, 6144, 8192, 12288, 13312, 14336, 15360
SM_DT, SM_ALR = 0, 32

NT_DIMS = (((1,), (1,)), ((), ()))
TN_DIMS = (((0,), (0,)), ((), ()))
HI = lax.Precision.HIGHEST


def _vmem_limit(*block_bytes, scratch=0):
    need = 2 * sum(block_bytes) + scratch + (4 << 20)
    return int(min(max(need, 16 << 20), VMEM_CAP_BYTES))


def _nbytes(shape, dtype):
    n = 1
    for s in shape:
        n *= s
    return n * jnp.dtype(dtype).itemsize


def _silu(x):
    return x * jax.nn.sigmoid(x)


def _softplus(x):
    return jnp.maximum(x, 0.0) + jnp.log1p(jnp.exp(-jnp.abs(x)))


def _log_sigmoid(x):
    return -_softplus(-x)


def _rms(x, gain):
    ms = jnp.mean(x * x, axis=-1, keepdims=True)
    return x * lax.rsqrt(ms + EPS) * gain


def _split2(x):
    hi = x.astype(BF16)
    lo = (x - hi.astype(F32)).astype(BF16)
    return hi, lo


def _select_rows(sel, x):
    hi = x.astype(BF16)
    r1 = x - hi.astype(F32)
    mid = r1.astype(BF16)
    lo = (r1 - mid.astype(F32)).astype(BF16)
    return (jnp.dot(sel, hi, preferred_element_type=F32) + jnp.dot(sel, mid, preferred_element_type=F32)
            + jnp.dot(sel, lo, preferred_element_type=F32))


def _stacked_out(prev, shape, dtype):
    return jax.ShapeDtypeStruct(shape, dtype) if prev is None else jax.ShapeDtypeStruct(prev.shape, prev.dtype)


def _norm_mm_kernel(*refs, has_gain, has_small):
    it = iter(refs)
    x_ref = next(it)
    g_ref = next(it) if has_gain else None
    w_ref = next(it)
    ws_ref = next(it) if has_small else None
    o_ref = next(it)
    os_ref = next(it) if has_small else None
    xn_ref = next(it)

    @pl.when(pl.program_id(1) == 0)
    def _():
        x = x_ref[...].astype(F32)
        if has_gain:
            x = _rms(x, g_ref[...])
        xn_ref[...] = x.astype(BF16)
        if has_small:
            os_ref[...] = jnp.dot(xn_ref[...], ws_ref[...], preferred_element_type=F32)

    o_ref[...] = jnp.dot(xn_ref[...], w_ref[...].astype(BF16), preferred_element_type=F32).astype(o_ref.dtype)


def _norm_mm(x, gain, w, layer, w_small, *, tm, tn, out_dtype):
    m, k = x.shape
    n = w.shape[2]
    has_gain = gain is not None
    has_small = w_small is not None
    in_specs = [pl.BlockSpec((tm, k), lambda i, j: (i, 0))]
    args = [x]
    if has_gain:
        in_specs.append(pl.BlockSpec((None, 1, k), lambda i, j: (layer, 0, 0)))
        args.append(gain.reshape(DEPTH, 1, k))
    in_specs.append(pl.BlockSpec((None, k, tn), lambda i, j: (layer, 0, j)))
    args.append(w)
    out_shape = [jax.ShapeDtypeStruct((m, n), out_dtype)]
    out_specs = [pl.BlockSpec((tm, tn), lambda i, j: (i, j))]
    if has_small:
        in_specs.append(pl.BlockSpec((None, k, LANES), lambda i, j: (layer, 0, 0)))
        args.append(w_small)
        out_shape.append(jax.ShapeDtypeStruct((m, LANES), F32))
        out_specs.append(pl.BlockSpec((tm, LANES), lambda i, j: (i, 0)))
    limit = _vmem_limit(_nbytes((tm, k), x.dtype), _nbytes((k, tn), w.dtype), _nbytes((tm, tn), out_dtype),
                        _nbytes((k, LANES), BF16), _nbytes((tm, LANES), F32), scratch=_nbytes((tm, k), BF16))
    res = pl.pallas_call(
        functools.partial(_norm_mm_kernel, has_gain=has_gain, has_small=has_small),
        grid=(m // tm, n // tn),
        in_specs=in_specs,
        out_specs=out_specs,
        out_shape=out_shape,
        scratch_shapes=[pltpu.VMEM((tm, k), BF16)],
        compiler_params=pltpu.CompilerParams(dimension_semantics=("parallel", "arbitrary"),
                                             vmem_limit_bytes=limit),
        name="norm_mm",
    )(*args)
    return res if has_small else res[0]


def _mem_kv_kernel(x_ref, w_ref, o_ref, xb_ref):
    @pl.when((pl.program_id(0) == 0) & (pl.program_id(1) == 0))
    def _():
        xb_ref[...] = x_ref[...].astype(BF16)

    res = jnp.dot(xb_ref[...], w_ref[...].astype(BF16), preferred_element_type=F32)
    o_ref[...] = res.reshape(o_ref.shape)


def _mem_kv(x, w, *, nb):
    m, k = x.shape
    tn = MEM_HEAD_DIM
    limit = _vmem_limit(_nbytes((m, k), F32), _nbytes((k, tn), F32), _nbytes((m, tn), F32),
                        scratch=_nbytes((m, k), BF16))
    return pl.pallas_call(
        _mem_kv_kernel,
        grid=(DEPTH, MEM_HEADS),
        in_specs=[pl.BlockSpec((m, k), lambda d, h: (0, 0)),
                  pl.BlockSpec((None, k, tn), lambda d, h: (d, 0, h))],
        out_specs=pl.BlockSpec((None, nb, None, m // nb, tn), lambda d, h: (d, 0, h, 0, 0)),
        out_shape=jax.ShapeDtypeStruct((DEPTH, nb, MEM_HEADS, m // nb, tn), F32),
        scratch_shapes=[pltpu.VMEM((m, k), BF16)],
        compiler_params=pltpu.CompilerParams(dimension_semantics=("arbitrary", "arbitrary"),
                                             vmem_limit_bytes=limit),
        name="mem_kv",
    )(x, w)


def _norm_cast_kernel(x_ref, g_ref, xn_ref):
    xn_ref[...] = _rms(x_ref[...], g_ref[...]).astype(BF16)


def _norm_cast(x, gain, layer, *, tm):
    m, k = x.shape
    return pl.pallas_call(
        _norm_cast_kernel,
        grid=(m // tm,),
        in_specs=[pl.BlockSpec((tm, k), lambda i: (i, 0)), pl.BlockSpec((None, 1, k), lambda i: (layer, 0, 0))],
        out_specs=pl.BlockSpec((tm, k), lambda i: (i, 0)),
        out_shape=jax.ShapeDtypeStruct((m, k), BF16),
        compiler_params=pltpu.CompilerParams(dimension_semantics=("parallel",)),
        name="norm_cast",
    )(x, gain.reshape(DEPTH, 1, k))


def _norm_small_kernel(x_ref, g_ref, wst_ref, xn_ref, os_ref):
    xn = _rms(x_ref[...], g_ref[...]).astype(BF16)
    xn_ref[...] = xn
    os_ref[...] = lax.dot_general(xn, wst_ref[...].astype(BF16), NT_DIMS, preferred_element_type=F32)


def _norm_small(x, gain, w_small_t, layer, *, tm):
    m, k = x.shape
    return pl.pallas_call(
        _norm_small_kernel,
        grid=(m // tm,),
        in_specs=[pl.BlockSpec((tm, k), lambda i: (i, 0)),
                  pl.BlockSpec((None, 1, k), lambda i: (layer, 0, 0)),
                  pl.BlockSpec((None, LANES, k), lambda i: (layer, 0, 0))],
        out_specs=[pl.BlockSpec((tm, k), lambda i: (i, 0)), pl.BlockSpec((tm, LANES), lambda i: (i, 0))],
        out_shape=[jax.ShapeDtypeStruct((m, k), BF16), jax.ShapeDtypeStruct((m, LANES), F32)],
        compiler_params=pltpu.CompilerParams(dimension_semantics=("parallel",)),
        name="norm_small",
    )(x, gain.reshape(DEPTH, 1, k), w_small_t)


PROJ_TN = 1024
PROJ_SRC = (0, 1024, 2048, 3072, 7200, 8224, 9248, 10272, 11312, 12336, 13360, 14384, 4096, 5152, 6176)


def _rider_steps(n_main, rider, main_fn, rider_fn):
    if not rider:
        main_fn()
        return
    i = pl.program_id(1)
    pl.when(i < n_main)(main_fn)
    pl.when(i == n_main)(rider_fn)


def _proj_kernel(*refs, n_main, rider):
    it = iter(refs)
    _src_ref, xn_ref = next(it), next(it)
    xn2_ref = next(it) if rider else None
    wt_ref, o_ref = next(it), next(it)
    o2_ref = next(it) if rider else None
    wbf_ref = next(it)

    @pl.when(pl.program_id(1) == 0)
    def _():
        wbf_ref[...] = wt_ref[0].astype(BF16)

    def mm(x_ref, out_ref):
        out_ref[...] = lax.dot_general(x_ref[...], wbf_ref[...], NT_DIMS,
                                       preferred_element_type=F32).astype(out_ref.dtype)

    _rider_steps(n_main, rider, lambda: mm(xn_ref, o_ref), lambda: mm(xn2_ref, o2_ref))


def _proj_in(xn, w_in_t, layer, *, tm, rider=None):
    m, k = xn.shape
    tn = PROJ_TN
    nj = len(PROJ_SRC)
    n_main = m // tm
    assert all(s % SUBLANES == 0 for s in PROJ_SRC)
    src = jnp.array([s // SUBLANES for s in PROJ_SRC], jnp.int32)
    im = lambda i: jnp.minimum(i, n_main - 1)
    in_specs = [pl.BlockSpec((tm, k), lambda j, i, s: (im(i), 0))]
    out_specs = [pl.BlockSpec((tm, tn), lambda j, i, s: (im(i), j))]
    out_shape = [jax.ShapeDtypeStruct((m, nj * tn), BF16)]
    args = [src, xn]
    m2 = 0
    if rider is not None:
        m2 = rider.shape[0]
        in_specs.append(pl.BlockSpec((m2, k), lambda j, i, s: (0, 0)))
        out_specs.append(pl.BlockSpec((m2, tn), lambda j, i, s: (0, j)))
        out_shape.append(jax.ShapeDtypeStruct((m2, nj * tn), BF16))
        args.append(rider)
    in_specs.append(pl.BlockSpec((pl.Element(1), pl.Element(tn), pl.Element(k)),
                                 lambda j, i, s: (layer, s[j] * SUBLANES, 0)))
    args.append(w_in_t)
    limit = _vmem_limit(_nbytes((tm + m2, k), BF16), _nbytes((tn, k), F32), _nbytes((tm + m2, tn), BF16),
                        scratch=_nbytes((tn, k), BF16))
    grid_spec = pltpu.PrefetchScalarGridSpec(
        num_scalar_prefetch=1,
        grid=(nj, n_main + (rider is not None)),
        in_specs=in_specs,
        out_specs=out_specs,
        scratch_shapes=[pltpu.VMEM((tn, k), BF16)])
    return pl.pallas_call(
        functools.partial(_proj_kernel, n_main=n_main, rider=rider is not None),
        grid_spec=grid_spec,
        out_shape=out_shape,
        compiler_params=pltpu.CompilerParams(dimension_semantics=("arbitrary", "arbitrary"),
                                             vmem_limit_bytes=limit),
        name="proj_in",
    )(*args)


def _mm_res_kernel(*refs, n_main, rider):
    it = iter(refs)
    x_ref, r_ref = next(it), next(it)
    x2_ref, r2_ref = (next(it), next(it)) if rider else (None, None)
    w_ref, o_ref = next(it), next(it)
    o2_ref = next(it) if rider else None
    wbf_ref = next(it)

    @pl.when(pl.program_id(1) == 0)
    def _():
        wbf_ref[...] = w_ref[...].astype(BF16)

    def mm(xr, rr, outr):
        outr[...] = rr[...] + jnp.dot(xr[...], wbf_ref[...], preferred_element_type=F32)

    _rider_steps(n_main, rider, lambda: mm(x_ref, r_ref, o_ref), lambda: mm(x2_ref, r2_ref, o2_ref))


def _mm_res(x, w, layer, res, *, tm, tn, rider=None):
    m, k = x.shape
    n = w.shape[2]
    n_main = m // tm
    im = lambda i: jnp.minimum(i, n_main - 1)
    in_specs = [pl.BlockSpec((tm, k), lambda j, i: (im(i), 0)), pl.BlockSpec((tm, tn), lambda j, i: (im(i), j))]
    out_specs = [pl.BlockSpec((tm, tn), lambda j, i: (im(i), j))]
    out_shape = [jax.ShapeDtypeStruct((m, n), F32)]
    args = [x, res]
    m2 = 0
    if rider is not None:
        m2 = rider[0].shape[0]
        in_specs += [pl.BlockSpec((m2, k), lambda j, i: (0, 0)), pl.BlockSpec((m2, tn), lambda j, i: (0, j))]
        out_specs.append(pl.BlockSpec((m2, tn), lambda j, i: (0, j)))
        out_shape.append(jax.ShapeDtypeStruct((m2, n), F32))
        args += list(rider)
    in_specs.append(pl.BlockSpec((None, k, tn), lambda j, i: (layer, 0, j)))
    args.append(w)
    limit = _vmem_limit(_nbytes((tm + m2, k), BF16), _nbytes((k, tn), F32), 2 * _nbytes((tm + m2, tn), F32),
                        scratch=_nbytes((k, tn), BF16))
    return pl.pallas_call(
        functools.partial(_mm_res_kernel, n_main=n_main, rider=rider is not None),
        grid=(n // tn, n_main + (rider is not None)),
        in_specs=in_specs,
        out_specs=out_specs,
        out_shape=out_shape,
        scratch_shapes=[pltpu.VMEM((k, tn), BF16)],
        compiler_params=pltpu.CompilerParams(dimension_semantics=("parallel", "arbitrary"),
                                             vmem_limit_bytes=limit),
        name="mm_res",
    )(*args)


def _merge_kernel(*refs, n_main, rider):
    it = iter(refs)
    main_in = [next(it) for _ in range(4)]
    rider_in = [next(it) for _ in range(4)] if rider else None
    w_ref, o_ref = next(it), next(it)
    o2_ref = next(it) if rider else None
    wbf_ref = next(it)

    @pl.when(pl.program_id(1) == 0)
    def _():
        wbf_ref[...] = w_ref[...].astype(BF16)

    def mm(ins, outr):
        y0_ref, y1_ref, g0_ref, g1_ref = ins
        b0 = jnp.dot(y0_ref[...], wbf_ref[0], preferred_element_type=F32)
        b1 = jnp.dot(y1_ref[...], wbf_ref[1], preferred_element_type=F32)
        g0 = jax.nn.sigmoid(g0_ref[...].astype(F32))
        g1 = jax.nn.sigmoid(g1_ref[...].astype(F32))
        outr[...] = (g0 * b0 + g1 * b1).astype(outr.dtype)

    _rider_steps(n_main, rider, lambda: mm(main_in, o_ref), lambda: mm(rider_in, o2_ref))


def _merge(y_ssd, y_gla, proj, w_branch, layer, *, tm, tn, rider=None):
    m, k = y_ssd.shape
    n = D_MODEL
    gb = P_GATE // tn
    nb = n // tn
    n_main = m // tm
    im = lambda i: jnp.minimum(i, n_main - 1)
    in_specs = [pl.BlockSpec((tm, k), lambda j, i: (im(i), 0)),
                pl.BlockSpec((tm, k), lambda j, i: (im(i), 0)),
                pl.BlockSpec((tm, tn), lambda j, i: (im(i), gb + j)),
                pl.BlockSpec((tm, tn), lambda j, i: (im(i), gb + nb + j))]
    out_specs = [pl.BlockSpec((tm, tn), lambda j, i: (im(i), j))]
    out_shape = [jax.ShapeDtypeStruct((m, n), BF16)]
    args = [y_ssd, y_gla, proj, proj]
    m2 = 0
    if rider is not None:
        y0_2, y1_2, proj2 = rider
        m2 = y0_2.shape[0]
        in_specs += [pl.BlockSpec((m2, k), lambda j, i: (0, 0)),
                     pl.BlockSpec((m2, k), lambda j, i: (0, 0)),
                     pl.BlockSpec((m2, tn), lambda j, i: (0, gb + j)),
                     pl.BlockSpec((m2, tn), lambda j, i: (0, gb + nb + j))]
        out_specs.append(pl.BlockSpec((m2, tn), lambda j, i: (0, j)))
        out_shape.append(jax.ShapeDtypeStruct((m2, n), BF16))
        args += [y0_2, y1_2, proj2, proj2]
    in_specs.append(pl.BlockSpec((None, 2, k, tn), lambda j, i: (layer, 0, 0, j)))
    args.append(w_branch)
    limit = _vmem_limit(2 * _nbytes((tm + m2, k), BF16), 2 * _nbytes((k, tn), F32),
                        3 * _nbytes((tm + m2, tn), BF16), scratch=2 * _nbytes((k, tn), BF16))
    return pl.pallas_call(
        functools.partial(_merge_kernel, n_main=n_main, rider=rider is not None),
        grid=(nb, n_main + (rider is not None)),
        in_specs=in_specs,
        out_specs=out_specs,
        out_shape=out_shape,
        scratch_shapes=[pltpu.VMEM((2, k, tn), BF16)],
        compiler_params=pltpu.CompilerParams(dimension_semantics=("parallel", "arbitrary"),
                                             vmem_limit_bytes=limit),
        name="merge",
    )(*args)


def _final_norm_kernel(x_ref, g_ref, o_ref):
    o_ref[...] = _rms(x_ref[...], g_ref[...])


def _final_norm(x, gain, *, tm):
    m, k = x.shape
    return pl.pallas_call(
        _final_norm_kernel,
        grid=(m // tm,),
        in_specs=[pl.BlockSpec((tm, k), lambda i: (i, 0)), pl.BlockSpec((1, k), lambda i: (0, 0))],
        out_specs=pl.BlockSpec((tm, k), lambda i: (i, 0)),
        out_shape=jax.ShapeDtypeStruct((m, k), F32),
        compiler_params=pltpu.CompilerParams(dimension_semantics=("parallel",)),
        name="final_norm",
    )(x, gain.reshape(1, k))


def _ffn_up_kernel(*refs, tm, tn, seq_tiles, seg):
    short = seg > 0
    it = iter(refs)
    xn_ref, wu_ref, wt_ref, cwu_ref, cwt_ref, cbu_ref, cbt_ref = (next(it) for _ in range(7))
    stu_ref = next(it) if short else None
    stt_ref = next(it) if short else None
    act_ref, cnu_ref, cnt_ref = next(it), next(it), next(it)
    wbu_ref, wbt_ref = next(it), next(it)
    tailu_ref = None if short else next(it)
    tailt_ref = None if short else next(it)
    i = pl.program_id(1)
    sl = SUBLANES
    hw = FFN_CONV - 1

    @pl.when(i == 0)
    def _():
        wbu_ref[...] = wu_ref[...].astype(BF16)
        wbt_ref[...] = wt_ref[...].astype(BF16)
        if not short:
            tailu_ref[...] = jnp.zeros_like(tailu_ref)
            tailt_ref[...] = jnp.zeros_like(tailt_ref)

    xn = xn_ref[...]
    u_raw = jnp.dot(xn, wbu_ref[...], preferred_element_type=F32)
    t_raw = jnp.dot(xn, wbt_ref[...], preferred_element_type=F32)

    if short:
        nseq = tm // seg
        r = lax.broadcasted_iota(jnp.int32, (tm, hw * nseq), 0)
        c = lax.broadcasted_iota(jnp.int32, (tm, hw * nseq), 1)
        halo_u, halo_t = [], []
        for s in range(1, FFN_CONV):
            sel = (((r % seg) < s) & (c == hw * (r // seg) + hw - s + (r % seg))).astype(BF16)
            halo_u.append(_select_rows(sel, stu_ref[...]))
            halo_t.append(_select_rows(sel, stt_ref[...]))
        ro = lax.broadcasted_iota(jnp.int32, (hw * nseq, tm), 0)
        co = lax.broadcasted_iota(jnp.int32, (hw * nseq, tm), 1)
        sel_out = (co == (ro // hw) * seg + seg - hw + (ro % hw)).astype(BF16)
        cnu_ref[...] = _select_rows(sel_out, u_raw)
        cnt_ref[...] = _select_rows(sel_out, t_raw)
        rowmod = lax.broadcasted_iota(jnp.int32, (tm, tn), 0) % seg
        prev_u = prev_t = None
    else:
        start = (i % seq_tiles) == 0
        prev_u = jnp.where(start, 0.0, tailu_ref[...])
        prev_t = jnp.where(start, 0.0, tailt_ref[...])
        tailu_ref[...] = u_raw[tm - sl:tm, :]
        tailt_ref[...] = t_raw[tm - sl:tm, :]
        cnu_ref[...] = u_raw[tm - sl:tm, :]
        cnt_ref[...] = t_raw[tm - sl:tm, :]
        row8 = lax.broadcasted_iota(jnp.int32, (sl, tn), 0)
        halo_u = halo_t = rowmod = None

    def conv(x, prev8, cw_ref, cb_ref, halos):
        y = x * cw_ref[hw:hw + 1, :]
        for s in range(1, FFN_CONV):
            rolled = pltpu.roll(x, s, axis=0)
            if short:
                shifted = jnp.where(rowmod < s, halos[s - 1], rolled)
            else:
                head = jnp.where(row8 < s, pltpu.roll(prev8, s, axis=0), rolled[0:sl, :])
                shifted = jnp.concatenate([head, rolled[sl:, :]], axis=0)
            y = y + shifted * cw_ref[hw - s:hw - s + 1, :]
        return y + cb_ref[...]

    u = conv(u_raw, prev_u, cwu_ref, cbu_ref, halo_u)
    t = conv(t_raw, prev_t, cwt_ref, cbt_ref, halo_t)
    act_ref[...] = (_silu(t) * u).astype(act_ref.dtype)


def _ffn_up(xn, w, conv_w, conv_b, conv0, layer, *, L, tm, tn):
    m, k = xn.shape
    bt = m // L
    nj = D_FF // tn
    hw = FFN_CONV - 1
    short = L < tm
    seg = L if short else 0
    seq_tiles = 1 if short else L // tm
    assert (conv0 is not None) == short, "history rows are only supported for sequences shorter than a block"
    in_specs = [
        pl.BlockSpec((tm, k), lambda j, i: (i, 0)),
        pl.BlockSpec((None, k, tn), lambda j, i: (layer, 0, j)),
        pl.BlockSpec((None, k, tn), lambda j, i: (layer, 0, nj + j)),
        pl.BlockSpec((None, FFN_CONV, tn), lambda j, i: (layer, 0, j)),
        pl.BlockSpec((None, FFN_CONV, tn), lambda j, i: (layer, 0, nj + j)),
        pl.BlockSpec((None, 1, tn), lambda j, i: (layer, 0, j)),
        pl.BlockSpec((None, 1, tn), lambda j, i: (layer, 0, nj + j)),
    ]
    args = [xn, w, w, conv_w, conv_w, conv_b.reshape(DEPTH, 1, 2 * D_FF), conv_b.reshape(DEPTH, 1, 2 * D_FF)]
    scratch = [pltpu.VMEM((k, tn), BF16)] * 2
    if short:
        nst = (m // tm) * (tm // L) * hw
        st2d = conv0.reshape(DEPTH, nst, 2 * D_FF)
        rows = (tm // L) * hw
        in_specs += [pl.BlockSpec((None, rows, tn), lambda j, i: (layer, i, j)),
                     pl.BlockSpec((None, rows, tn), lambda j, i: (layer, i, nj + j))]
        args += [st2d, st2d]
        cn_shape = jax.ShapeDtypeStruct((nst, D_FF), F32)
        cn_spec = pl.BlockSpec((rows, tn), lambda j, i: (i, j))
    else:
        scratch += [pltpu.VMEM((SUBLANES, tn), F32)] * 2
        cn_shape = jax.ShapeDtypeStruct((m // tm, SUBLANES, D_FF), F32)
        cn_spec = pl.BlockSpec((None, SUBLANES, tn), lambda j, i: (i, 0, j))
    limit = _vmem_limit(_nbytes((tm, k), BF16), 2 * _nbytes((k, tn), F32), _nbytes((tm, tn), BF16),
                        scratch=2 * _nbytes((k, tn), BF16) + 6 * _nbytes((tm, tn), F32))
    act, cnu, cnt = pl.pallas_call(
        functools.partial(_ffn_up_kernel, tm=tm, tn=tn, seq_tiles=seq_tiles, seg=seg),
        grid=(nj, m // tm),
        in_specs=in_specs,
        out_specs=[pl.BlockSpec((tm, tn), lambda j, i: (i, j)), cn_spec, cn_spec],
        out_shape=[jax.ShapeDtypeStruct((m, D_FF), BF16), cn_shape, cn_shape],
        scratch_shapes=scratch,
        compiler_params=pltpu.CompilerParams(dimension_semantics=("arbitrary", "arbitrary"),
                                             vmem_limit_bytes=limit),
        name="ffn_up",
    )(*args)
    if short:
        conv_new = jnp.concatenate([cnu, cnt], axis=-1).reshape(bt, hw, 2 * D_FF)
    else:
        last = slice(seq_tiles - 1, None, seq_tiles)
        conv_new = jnp.concatenate([cnu[last, SUBLANES - hw:, :], cnt[last, SUBLANES - hw:, :]], axis=-1)
    return act, conv_new


def _causal_conv(x, tail_ref, w_ref, b, width):
    sl = SUBLANES
    row = lax.broadcasted_iota(jnp.int32, (sl, x.shape[1]), 0)
    prev = tail_ref[...]
    y = x * w_ref[width - 1:width, :]
    for s in range(1, width):
        rolled = pltpu.roll(x, s, axis=0)
        head = jnp.where(row < s, pltpu.roll(prev, s, axis=0), rolled[0:sl, :])
        shifted = jnp.concatenate([head, rolled[sl:, :]], axis=0)
        y = y + shifted * w_ref[width - 1 - s:width - s, :]
    return y + b


def _ssd_kernel(*refs, Q, zero_init, aliased):
    it = iter(refs)
    z_ref, xs_ref, bc_ref, sm_ref = (next(it) for _ in range(4))
    conv0_ref = None if zero_init else next(it)
    h0_ref = None if zero_init else next(it)
    cwx_ref, cwb_ref, cbx_ref, cbb_ref, dtb_ref, a_ref, dx_ref, nrm_ref, exp_ref = (next(it) for _ in range(9))
    if aliased:
        next(it), next(it)
    y_ref, h_out, conv_out, tailx_ref, tailb_ref = (next(it) for _ in range(5))
    h_ref = h_out if aliased else h_out.at[0]
    convout_ref = conv_out if aliased else conv_out.at[0]
    c = pl.program_id(1)
    hp = SSD_HEADDIM
    gw = SSD_HEADS // SSD_GROUPS * hp

    @pl.when(c == 0)
    def _():
        if not aliased:
            h_out[1:] = jnp.zeros((DEPTH - 1,) + h_ref.shape, F32)
            conv_out[1:] = jnp.zeros((DEPTH - 1,) + convout_ref.shape, F32)
        tailx_ref[...] = jnp.zeros_like(tailx_ref)
        tailb_ref[...] = jnp.zeros_like(tailb_ref)
        if zero_init:
            h_ref[...] = jnp.zeros_like(h_ref)
        else:
            h_ref[...] = h0_ref[...]
            tailx_ref[5:8, :] = conv0_ref[:, 0:D_MODEL]
            tailb_ref[5:8, :] = conv0_ref[:, D_MODEL:SSD_CONV_DIM]

    xs_raw = xs_ref[...].astype(F32)
    bc_raw = bc_ref[...].astype(F32)
    xs = _silu(_causal_conv(xs_raw, tailx_ref, cwx_ref, cbx_ref[...], SSD_CONV))
    bc = _silu(_causal_conv(bc_raw, tailb_ref, cwb_ref, cbb_ref[...], SSD_CONV))
    tailx_ref[...] = xs_raw[Q - 8:Q, :]
    tailb_ref[...] = bc_raw[Q - 8:Q, :]

    @pl.when(c == pl.num_programs(1) - 1)
    def _():
        convout_ref[:, 0:D_MODEL] = tailx_ref[5:8, :]
        convout_ref[:, D_MODEL:SSD_CONV_DIM] = tailb_ref[5:8, :]

    dt = _softplus(sm_ref[...] + dtb_ref[...])
    adt = dt * a_ref[...]
    ri = lax.broadcasted_iota(jnp.int32, (Q, Q), 0)
    ci = lax.broadcasted_iota(jnp.int32, (Q, Q), 1)
    causal = ci <= ri
    acum = jnp.dot(causal.astype(F32), adt, precision=HI, preferred_element_type=F32)
    eye = (lax.broadcasted_iota(jnp.int32, (LANES, LANES), 0)
           == lax.broadcasted_iota(jnp.int32, (LANES, LANES), 1)).astype(F32)
    acum_t = lax.dot_general(eye, acum, NT_DIMS, precision=HI, preferred_element_type=F32)
    dt_t = lax.dot_general(eye, dt, NT_DIMS, precision=HI, preferred_element_type=F32)
    a_last = acum[Q - 1:Q, :]
    ea = jnp.exp(acum)
    te = jnp.exp(a_last - acum) * dt
    dec_rows = jnp.broadcast_to(jnp.exp(acum_t[:, Q - 1:Q]), (LANES, LANES))
    ea_hi, ea_lo = _split2(ea)
    te_hi, te_lo = _split2(te)
    expand = exp_ref[...]
    ea_x = (jnp.dot(ea_hi, expand, preferred_element_type=F32)
            + jnp.dot(ea_lo, expand, preferred_element_type=F32))
    te_x = (jnp.dot(te_hi, expand, preferred_element_type=F32)
            + jnp.dot(te_lo, expand, preferred_element_type=F32))
    lane = lax.broadcasted_iota(jnp.int32, (Q, LANES), 1)

    for g in range(SSD_GROUPS):
        gl = slice(g * gw, (g + 1) * gw)
        b_g = bc[:, g * SSD_STATE:(g + 1) * SSD_STATE].astype(BF16)
        c_g = bc[:, (SSD_GROUPS + g) * SSD_STATE:(SSD_GROUPS + g + 1) * SSD_STATE].astype(BF16)
        cb = lax.dot_general(c_g, b_g, NT_DIMS, preferred_element_type=F32)
        h_g = h_ref[8 * g:8 * g + 8].reshape(gw, SSD_STATE)
        y_off = lax.dot_general(c_g, h_g.astype(BF16), NT_DIMS, preferred_element_type=F32)
        x_g = xs[:, gl]
        pairs = []
        for p in range(4):
            x_p = x_g[:, p * LANES:(p + 1) * LANES]
            acc = None
            for s in range(2):
                hh = g * 8 + p * 2 + s
                seg = jnp.broadcast_to(acum[:, hh:hh + 1], (Q, Q)) - jnp.broadcast_to(acum_t[hh:hh + 1, :], (Q, Q))
                decay = jnp.where(causal, jnp.exp(seg), 0.0)
                w_h = (cb * decay * dt_t[hh:hh + 1, :]).astype(BF16)
                x_m = jnp.where((lane // hp) == s, x_p, 0.0).astype(BF16)
                r = jnp.dot(w_h, x_m, preferred_element_type=F32)
                acc = r if acc is None else acc + r
            pairs.append(acc)
        y_g = jnp.concatenate(pairs, axis=1) + y_off * ea_x[:, gl] + dx_ref[:, gl] * x_g
        x_t = (x_g * te_x[:, gl]).astype(BF16)
        upd = lax.dot_general(x_t, b_g, TN_DIMS, preferred_element_type=F32)
        for h in range(8):
            hh = g * 8 + h
            h_ref[hh] = h_ref[hh] * dec_rows[hh:hh + 1, :] + upd[h * hp:(h + 1) * hp, :]
        z_g = z_ref[:, gl].astype(F32)
        y_ref[:, gl] = _rms(y_g * _silu(z_g), nrm_ref[:, gl]).astype(y_ref.dtype)


def _ssd(proj, small, conv0, h0, wl, layer, prev_h, prev_conv, *, bt, L, Q):
    zero_init = h0 is None
    aliased = prev_h is not None
    nc = L // Q
    T = bt * L
    full2 = lambda b, c: (0, 0)
    lay3 = lambda b, c: (layer, 0, 0)
    row = lambda col: (lambda b, c: (b * nc + c, col))
    hshape = (DEPTH, bt, SSD_HEADS, SSD_HEADDIM, SSD_STATE)
    cshape = (DEPTH, bt, SSD_CONV - 1, SSD_CONV_DIM)
    hspec = pl.BlockSpec((None, None, SSD_HEADS, SSD_HEADDIM, SSD_STATE), lambda b, c: (layer, b, 0, 0, 0))
    cspec = pl.BlockSpec((None, None, SSD_CONV - 1, SSD_CONV_DIM), lambda b, c: (layer, b, 0, 0))
    h_out_spec = hspec if aliased else pl.BlockSpec((DEPTH, None, SSD_HEADS, SSD_HEADDIM, SSD_STATE),
                                                    lambda b, c: (0, b, 0, 0, 0))
    c_out_spec = cspec if aliased else pl.BlockSpec((DEPTH, None, SSD_CONV - 1, SSD_CONV_DIM),
                                                    lambda b, c: (0, b, 0, 0))
    in_specs = [
        pl.BlockSpec((Q, D_MODEL), row(P_Z // D_MODEL)),
        pl.BlockSpec((Q, D_MODEL), row(P_XS // D_MODEL)),
        pl.BlockSpec((Q, 1024), row(P_BC // 1024)),
        pl.BlockSpec((Q, LANES), row(0)),
    ]
    args = [proj, proj, proj, small]
    if not zero_init:
        in_specs += [cspec, hspec]
        args += [conv0, h0]
    in_specs += [
        pl.BlockSpec((None, SSD_CONV, D_MODEL), lay3),
        pl.BlockSpec((None, SSD_CONV, 1024), lambda b, c: (layer, 0, D_MODEL // 1024)),
        pl.BlockSpec((None, 1, D_MODEL), lay3),
        pl.BlockSpec((None, 1, 1024), lambda b, c: (layer, 0, D_MODEL // 1024)),
        pl.BlockSpec((None, 1, LANES), lay3),
        pl.BlockSpec((None, 1, LANES), lay3),
        pl.BlockSpec((None, 1, D_MODEL), lay3),
        pl.BlockSpec((None, 1, D_MODEL), lay3),
        pl.BlockSpec((LANES, D_MODEL), full2),
    ]
    args += [wl["ssd_conv_w"], wl["ssd_conv_w"], wl["ssd_conv_b"], wl["ssd_conv_b"], wl["ssd_dtb"], wl["ssd_a"],
             wl["ssd_dx"], wl["ssd_norm"], wl["expand"]]
    aliases = {}
    if aliased:
        aliases = {len(args): 1, len(args) + 1: 2}
        in_specs += [pl.BlockSpec(memory_space=pl.ANY), pl.BlockSpec(memory_space=pl.ANY)]
        args += [prev_h, prev_conv]
    return pl.pallas_call(
        functools.partial(_ssd_kernel, Q=Q, zero_init=zero_init, aliased=aliased),
        grid=(bt, nc),
        in_specs=in_specs,
        out_specs=[pl.BlockSpec((Q, D_MODEL), row(0)), h_out_spec, c_out_spec],
        out_shape=[jax.ShapeDtypeStruct((T, D_MODEL), BF16), _stacked_out(prev_h, hshape, F32),
                   _stacked_out(prev_conv, cshape, F32)],
        scratch_shapes=[pltpu.VMEM((8, D_MODEL), F32), pltpu.VMEM((8, 1024), F32)],
        input_output_aliases=aliases,
        compiler_params=pltpu.CompilerParams(dimension_semantics=("parallel", "arbitrary"),
                                             vmem_limit_bytes=48 << 20),
        name="ssd",
    )(*args)


def _gla_kernel(*refs, TB, zero_init, aliased):
    it = iter(refs)
    q_ref, k_ref, v_ref, g_ref, sm_ref = (next(it) for _ in range(5))
    s0_ref = None if zero_init else next(it)
    wa_ref, ba_ref, wat_ref, bat_ref, gn_ref = (next(it) for _ in range(5))
    if aliased:
        next(it)
    y_ref, s_out = next(it), next(it)
    s_ref = s_out if aliased else s_out.at[0]
    c = pl.program_id(1)
    ck = GLA_CHUNK
    ns = TB // ck
    hk, hv = GLA_HEAD_K, GLA_HEAD_V

    @pl.when(c == 0)
    def _():
        if not aliased:
            s_out[1:] = jnp.zeros((DEPTH - 1,) + s_ref.shape, F32)
        if zero_init:
            s_ref[...] = jnp.zeros_like(s_ref)
        else:
            s_ref[...] = s0_ref[...]

    smb = sm_ref[...].astype(BF16)
    la = _log_sigmoid(jnp.dot(smb, wa_ref[...], preferred_element_type=F32) + ba_ref[...]) * (1.0 / GLA_GATE_NORM)
    rb = lax.broadcasted_iota(jnp.int32, (TB, TB), 0)
    cb = lax.broadcasted_iota(jnp.int32, (TB, TB), 1)
    same = (rb // ck) == (cb // ck)
    msel = jnp.concatenate([same & (cb <= rb), same, (cb // ck) < (rb // ck)], axis=0).astype(BF16)
    sums = _select_rows(msel, la)
    bcum, tot, bprev = sums[0:TB], sums[TB:2 * TB], sums[2 * TB:3 * TB]
    kf = k_ref[...].astype(F32)
    qd = q_ref[...].astype(F32) * (hk ** -0.5) * jnp.exp(bcum)
    qd_b = qd.astype(BF16)
    qs_b = (qd * jnp.exp(bprev)).astype(BF16)
    ki = kf * jnp.exp(-bcum)
    ke = kf * jnp.exp(tot - bcum)
    dtot = jnp.exp(tot)
    la_t = _log_sigmoid(lax.dot_general(wat_ref[...], smb, NT_DIMS, preferred_element_type=F32)
                        + bat_ref[:, 0:1]) * (1.0 / GLA_GATE_NORM)
    ones = jnp.ones((TB, LANES), BF16)
    la_hi = la_t.astype(BF16)
    la_r = la_t - la_hi.astype(F32)
    la_mid = la_r.astype(BF16)
    la_lo = (la_r - la_mid.astype(F32)).astype(BF16)
    dec_t = jnp.exp(jnp.dot(la_hi, ones, preferred_element_type=F32) + jnp.dot(la_mid, ones, preferred_element_type=F32)
                    + jnp.dot(la_lo, ones, preferred_element_type=F32))
    rchunk = lax.broadcasted_iota(jnp.int32, (TB, hk), 0) // ck
    row16 = lax.broadcasted_iota(jnp.int32, (ck, TB), 0)
    col16 = lax.broadcasted_iota(jnp.int32, (ck, TB), 1)

    for h in range(GLA_HEADS):
        kl = slice(h * hk, (h + 1) * hk)
        vl = slice(h * hv, (h + 1) * hv)
        ke_h, ki_h = ke[:, kl], ki[:, kl]
        v_h = v_ref[:, vl]
        kbuf = jnp.zeros((TB, hk), F32)
        att = []
        for cc in range(ns):
            in_c = rchunk == cc
            kall = jnp.where(in_c, ki_h, kbuf).astype(BF16)
            a = lax.dot_general(qd_b[cc * ck:(cc + 1) * ck, kl], kall, NT_DIMS, preferred_element_type=F32)
            att.append(jnp.where(col16 <= row16 + cc * ck, a, 0.0))
            kbuf = jnp.where(in_c, ke_h, kbuf * dtot[cc * ck:cc * ck + 1, kl])
        a_full = jnp.concatenate(att, axis=0).astype(BF16)
        s_old = s_ref[h]
        o = (jnp.dot(a_full, v_h, preferred_element_type=F32)
             + jnp.dot(qs_b[:, kl], s_old.astype(BF16), preferred_element_type=F32))
        y_ref[:, vl] = (_rms(o, gn_ref[...]) * _silu(g_ref[:, vl].astype(F32))).astype(y_ref.dtype)
        dcol = jnp.concatenate([dec_t[kl, :]] * (hv // LANES), axis=1)
        s_ref[h] = s_old * dcol + lax.dot_general(kbuf.astype(BF16), v_h, TN_DIMS, preferred_element_type=F32)


def _gla(proj, small, s0, wl, layer, prev_s, *, bt, L, TB):
    zero_init = s0 is None
    aliased = prev_s is not None
    nc = L // TB
    T = bt * L
    lay3 = lambda b, c: (layer, 0, 0)
    row = lambda col: (lambda b, c: (b * nc + c, col))
    kd = GLA_HEADS * GLA_HEAD_K
    sshape = (DEPTH, bt, GLA_HEADS, GLA_HEAD_K, GLA_HEAD_V)
    sspec = pl.BlockSpec((None, None, GLA_HEADS, GLA_HEAD_K, GLA_HEAD_V), lambda b, c: (layer, b, 0, 0, 0))
    s_out_spec = sspec if aliased else pl.BlockSpec((DEPTH, None, GLA_HEADS, GLA_HEAD_K, GLA_HEAD_V),
                                                     lambda b, c: (0, b, 0, 0, 0))
    in_specs = [
        pl.BlockSpec((TB, kd), row(P_Q // kd)),
        pl.BlockSpec((TB, kd), row(P_K // kd)),
        pl.BlockSpec((TB, D_MODEL), row(P_V // D_MODEL)),
        pl.BlockSpec((TB, D_MODEL), row(P_G // D_MODEL)),
        pl.BlockSpec((TB, LANES), row(0)),
    ]
    args = [proj, proj, proj, proj, small]
    if not zero_init:
        in_specs.append(sspec)
        args.append(s0)
    in_specs += [
        pl.BlockSpec((None, LANES, kd), lay3),
        pl.BlockSpec((None, 1, kd), lay3),
        pl.BlockSpec((None, kd, LANES), lay3),
        pl.BlockSpec((None, kd, LANES), lay3),
        pl.BlockSpec((None, 1, GLA_HEAD_V), lay3),
    ]
    args += [wl["gla_wa"], wl["gla_ba"], wl["gla_wa_t"], wl["gla_ba_t"], wl["gla_norm"]]
    aliases = {}
    if aliased:
        aliases = {len(args): 1}
        in_specs.append(pl.BlockSpec(memory_space=pl.ANY))
        args.append(prev_s)
    return pl.pallas_call(
        functools.partial(_gla_kernel, TB=TB, zero_init=zero_init, aliased=aliased),
        grid=(bt, nc),
        in_specs=in_specs,
        out_specs=[pl.BlockSpec((TB, D_MODEL), row(0)), s_out_spec],
        out_shape=[jax.ShapeDtypeStruct((T, D_MODEL), BF16), _stacked_out(prev_s, sshape, F32)],
        input_output_aliases=aliases,
        compiler_params=pltpu.CompilerParams(dimension_semantics=("parallel", "arbitrary"),
                                             vmem_limit_bytes=48 << 20),
        name="gla",
    )(*args)


def _attn_kernel(q_ref, k_ref, v_ref, o_ref, kb_ref, vb_ref):
    @pl.when(pl.program_id(1) == 0)
    def _():
        kb_ref[...] = k_ref[...].astype(BF16)
        vb_ref[...] = v_ref[...].astype(BF16)

    hd = MEM_HEAD_DIM
    for h in range(MEM_HEADS):
        hl = slice(h * hd, (h + 1) * hd)
        s = lax.dot_general(q_ref[:, hl], kb_ref[h], NT_DIMS, preferred_element_type=F32) * (hd ** -0.5)
        e = jnp.exp(s - jnp.max(s, axis=-1, keepdims=True))
        p = e / jnp.sum(e, axis=-1, keepdims=True)
        o_ref[:, hl] = jnp.dot(p.astype(BF16), vb_ref[h], preferred_element_type=F32).astype(o_ref.dtype)


def _attn(q, mem_k, mem_v, layer, *, bt, L, tl):
    nl = L // tl
    T = bt * L
    kvshape = (MEM_HEADS, MEM_TOKENS, MEM_HEAD_DIM)
    kvspec = pl.BlockSpec((None, None) + kvshape, lambda b, l: (layer, b, 0, 0, 0))
    return pl.pallas_call(
        _attn_kernel,
        grid=(bt, nl),
        in_specs=[pl.BlockSpec((tl, D_MODEL), lambda b, l: (b * nl + l, 0)), kvspec, kvspec],
        out_specs=pl.BlockSpec((tl, D_MODEL), lambda b, l: (b * nl + l, 0)),
        out_shape=jax.ShapeDtypeStruct((T, D_MODEL), BF16),
        scratch_shapes=[pltpu.VMEM(kvshape, BF16), pltpu.VMEM(kvshape, BF16)],
        compiler_params=pltpu.CompilerParams(dimension_semantics=("parallel", "arbitrary"),
                                             vmem_limit_bytes=40 << 20),
        name="mem_attn",
    )(q, mem_k, mem_v)


def _prep_weights(w_in, ssd_conv_w, ssd_conv_b, ssd_dt_bias, ssd_a_log, ssd_d, ssd_norm, gla_wa2, gla_ba, gla_norm):
    w_in_t = jnp.swapaxes(w_in, 1, 2)
    o_dt, o_alr = 5120, 11296
    pad_sm = LANES - SSD_HEADS - GLA_RANK
    w_small_t = jnp.concatenate(
        [w_in_t[:, o_dt:o_dt + SSD_HEADS, :], w_in_t[:, o_alr:o_alr + GLA_RANK, :],
         jnp.zeros((DEPTH, pad_sm, D_MODEL), F32)], axis=1)
    pad_h = LANES - SSD_HEADS
    kd = GLA_HEADS * GLA_HEAD_K
    wa = jnp.concatenate([jnp.zeros((DEPTH, SM_ALR, kd), F32), gla_wa2,
                          jnp.zeros((DEPTH, LANES - SM_ALR - GLA_RANK, kd), F32)], axis=1).astype(BF16)
    expand = (jnp.arange(D_MODEL)[None, :] // SSD_HEADDIM == jnp.arange(LANES)[:, None]).astype(BF16)
    return dict(
        w_in_t=w_in_t, w_small_t=w_small_t,
        ssd_conv_w=ssd_conv_w, ssd_conv_b=ssd_conv_b[:, None, :],
        ssd_dtb=jnp.pad(ssd_dt_bias, ((0, 0), (0, pad_h)))[:, None, :],
        ssd_a=jnp.pad(-jnp.exp(ssd_a_log), ((0, 0), (0, pad_h)))[:, None, :],
        ssd_dx=jnp.repeat(ssd_d, SSD_HEADDIM, axis=1)[:, None, :],
        ssd_norm=ssd_norm[:, None, :], expand=expand,
        gla_wa=wa, gla_ba=gla_ba[:, None, :], gla_wa_t=jnp.swapaxes(wa, 1, 2),
        gla_ba_t=jnp.broadcast_to(gla_ba[:, :, None], (DEPTH, kd, LANES)),
        gla_norm=gla_norm[:, None, :])


def _group(x, bt, L, mem_k, mem_v, st_ssd=None, st_ssd_conv=None, st_gla=None, st_ffn_conv=None):
    return dict(x=x, bt=bt, L=L, mem_k=mem_k, mem_v=mem_v, st_ssd=st_ssd, st_ssd_conv=st_ssd_conv, st_gla=st_gla,
                st_ffn_conv=st_ffn_conv, cfg=_group_cfg(bt, L))


def _run_trunk(groups, wl, big, norm_final):
    main = groups[0]
    cfg = main["cfg"]
    tm = cfg["tm"]
    xs = [g["x"] for g in groups]
    acc = [dict(h=None, c=None, s=None, ffn=[]) for _ in groups]
    rest = lambda vals: vals[1] if len(vals) > 1 else None
    for i in range(DEPTH):
        normed = [_norm_small(x, big["norm_mix"], wl["w_small_t"], i, tm=g["cfg"]["tm_norm"])
                  for x, g in zip(xs, groups)]
        xn, small = [n[0] for n in normed], [n[1] for n in normed]
        proj = _proj_in(xn[0], wl["w_in_t"], i, tm=cfg["tm_proj"], rider=rest(xn))
        y_ssd, y_gla = [], []
        for g, a, p, sm in zip(groups, acc, proj, small):
            y, a["h"], a["c"] = _ssd(p, sm, g["st_ssd_conv"], g["st_ssd"], wl, i, a["h"], a["c"],
                                     bt=g["bt"], L=g["L"], Q=g["cfg"]["ssd_q"])
            y_ssd.append(y)
            y, a["s"] = _gla(p, sm, g["st_gla"], wl, i, a["s"], bt=g["bt"], L=g["L"], TB=g["cfg"]["gla_tb"])
            y_gla.append(y)
        merged = _merge(y_ssd[0], y_gla[0], proj[0], big["w_branch"], i, tm=tm, tn=512,
                        rider=rest(list(zip(y_ssd, y_gla, proj))))
        xs = _mm_res(merged[0], big["w_out"], i, xs[0], tm=tm, tn=1024, rider=rest(list(zip(merged, xs))))
        o = []
        for g, x in zip(groups, xs):
            q = _norm_mm(x, big["norm_mem"], big["w_mq"], i, None, tm=g["cfg"]["tm"], tn=512, out_dtype=BF16)
            o.append(_attn(q, g["mem_k"], g["mem_v"], i, bt=g["bt"], L=g["L"], tl=g["cfg"]["attn_tl"]))
        xs = _mm_res(o[0], big["w_mo"], i, xs[0], tm=tm, tn=1024, rider=rest(list(zip(o, xs))))
        act = []
        for g, a, x in zip(groups, acc, xs):
            xf = _norm_cast(x, big["norm_ffn"], i, tm=g["cfg"]["tm_norm"])
            y, f_c = _ffn_up(xf, big["w_ffn_in"], big["ffn_conv_w"], big["ffn_conv_b"], g["st_ffn_conv"], i,
                             L=g["L"], tm=g["cfg"]["tm"], tn=512)
            act.append(y)
            a["ffn"].append(f_c)
        xs = _mm_res(act[0], big["w_ffn_out"], i, xs[0], tm=cfg["tm_norm"], tn=512, rider=rest(list(zip(act, xs))))
    return [(_final_norm(x, norm_final, tm=g["cfg"]["tm_norm"]), a["h"], a["c"], a["s"], jnp.stack(a["ffn"]))
            for g, a, x in zip(groups, acc, xs)]


def _group_cfg(bt, L):
    T = bt * L
    return dict(tm=min(T, 1024), tm_proj=min(T, 1024), tm_norm=min(T, 512), ssd_q=min(L, 128), gla_tb=min(L, 128), attn_tl=min(L, 512))


def kernel(x_prompt, x_sample, mem_prompt, state_ssd, state_ssd_conv, state_gla, state_ffn_conv, cache_mem_k, cache_mem_v, norm_mix, w_in, ssd_conv_w, ssd_conv_b, ssd_dt_bias, ssd_a_log, ssd_d, ssd_norm, gla_wa2, gla_ba, gla_norm, w_branch, w_out, norm_mem, w_mq, w_mk, w_mv, w_mo, norm_ffn, w_ffn_in, ffn_conv_w, ffn_conv_b, w_ffn_out, norm_final):
    wl = _prep_weights(w_in, ssd_conv_w, ssd_conv_b, ssd_dt_bias, ssd_a_log, ssd_d, ssd_norm, gla_wa2, gla_ba, gla_norm)
    big = dict(norm_mix=norm_mix, w_branch=w_branch, w_out=w_out, norm_mem=norm_mem, w_mq=w_mq, w_mo=w_mo,
               norm_ffn=norm_ffn, w_ffn_in=w_ffn_in, ffn_conv_w=ffn_conv_w, ffn_conv_b=ffn_conv_b, w_ffn_out=w_ffn_out)
    pb, pl_len, _ = x_prompt.shape
    sb, sl_len, _ = x_sample.shape
    mem2d = mem_prompt.reshape(pb * MEM_TOKENS, D_MODEL)
    p_mem_k = _mem_kv(mem2d, w_mk, nb=pb)
    p_mem_v = _mem_kv(mem2d, w_mv, nb=pb)
    head_major = (0, 1, 3, 2, 4)

    prompt = _group(x_prompt.reshape(pb * pl_len, D_MODEL), pb, pl_len, p_mem_k, p_mem_v)
    sample = _group(x_sample.reshape(sb * sl_len, D_MODEL), sb, sl_len,
                    jnp.transpose(cache_mem_k, head_major), jnp.transpose(cache_mem_v, head_major),
                    state_ssd, state_ssd_conv, state_gla, state_ffn_conv)
    (y_p, p_ssd, p_ssd_conv, p_gla, p_ffn), (y_s, s_ssd, s_ssd_conv, s_gla, s_ffn) = _run_trunk(
        [prompt, sample], wl, big, norm_final)
    return (y_p.reshape(pb, pl_len, D_MODEL), y_s.reshape(sb, sl_len, D_MODEL),
            p_ssd, p_ssd_conv, p_gla, p_ffn, jnp.transpose(p_mem_k, head_major), jnp.transpose(p_mem_v, head_major),
            s_ssd, s_ssd_conv, s_gla, s_ffn)
```

```python
import functools

import jax
import jax.numpy as jnp
from jax import lax
from jax.experimental import pallas as pl
from jax.experimental.pallas import tpu as pltpu

F32 = jnp.float32
BF16 = jnp.bfloat16

D_MODEL = 2048
DEPTH = 2
EPS = 1e-6
SSD_HEADS = 32
SSD_HEADDIM = 64
SSD_GROUPS = 4
SSD_STATE = 128
SSD_CONV = 4
SSD_CONV_DIM = 3072
GLA_HEADS = 4
GLA_HEAD_K = 256
GLA_HEAD_V = 512
GLA_RANK = 16
GLA_GATE_NORM = 16.0
GLA_CHUNK = 16
MEM_TOKENS = 256
MEM_HEADS = 4
MEM_HEAD_DIM = 512
D_FF = 5632
FFN_CONV = 3

LANES = 128
MXU_COLS = 256
SUBLANES = 8
VMEM_CAP_BYTES = 56 * 2**20

P_Z, P_XS, P_V, P_G, P_GATE, P_BC, P_Q, P_K, P_N = 0, 2048, 4096, 6144, 8192, 12288, 13312, 14336, 15360
SM_DT, SM_ALR = 0, 32

NT_DIMS = (((1,), (1,)), ((), ()))
TN_DIMS = (((0,), (0,)), ((), ()))
HI = lax.Precision.HIGHEST


def _vmem_limit(*block_bytes, scratch=0):
    need = 2 * sum(block_bytes) + scratch + (4 << 20)
    return int(min(max(need, 16 << 20), VMEM_CAP_BYTES))


def _nbytes(shape, dtype):
    n = 1
    for s in shape:
        n *= s
    return n * jnp.dtype(dtype).itemsize


def _silu(x):
    return x * jax.nn.sigmoid(x)


def _softplus(x):
    return jnp.maximum(x, 0.0) + jnp.log1p(jnp.exp(-jnp.abs(x)))


def _log_sigmoid(x):
    return -_softplus(-x)


def _rms(x, gain):
    ms = jnp.mean(x * x, axis=-1, keepdims=True)
    return x * lax.rsqrt(ms + EPS) * gain


def _split2(x):
    hi = x.astype(BF16)
    lo = (x - hi.astype(F32)).astype(BF16)
    return hi, lo


def _select_rows(sel, x):
    hi = x.astype(BF16)
    r1 = x - hi.astype(F32)
    mid = r1.astype(BF16)
    lo = (r1 - mid.astype(F32)).astype(BF16)
    return (jnp.dot(sel, hi, preferred_element_type=F32) + jnp.dot(sel, mid, preferred_element_type=F32)
            + jnp.dot(sel, lo, preferred_element_type=F32))


def _stacked_out(prev, shape, dtype):
    return jax.ShapeDtypeStruct(shape, dtype) if prev is None else jax.ShapeDtypeStruct(prev.shape, prev.dtype)


def _norm_mm_kernel(*refs, has_gain, has_small):
    it = iter(refs)
    x_ref = next(it)
    g_ref = next(it) if has_gain else None
    w_ref = next(it)
    ws_ref = next(it) if has_small else None
    o_ref = next(it)
    os_ref = next(it) if has_small else None
    xn_ref = next(it)

    @pl.when(pl.program_id(1) == 0)
    def _():
        x = x_ref[...].astype(F32)
        if has_gain:
            x = _rms(x, g_ref[...])
        xn_ref[...] = x.astype(BF16)
        if has_small:
            os_ref[...] = jnp.dot(xn_ref[...], ws_ref[...], preferred_element_type=F32)

    o_ref[...] = jnp.dot(xn_ref[...], w_ref[...].astype(BF16), preferred_element_type=F32).astype(o_ref.dtype)


def _norm_mm(x, gain, w, layer, w_small, *, tm, tn, out_dtype):
    m, k = x.shape
    n = w.shape[2]
    has_gain = gain is not None
    has_small = w_small is not None
    in_specs = [pl.BlockSpec((tm, k), lambda i, j: (i, 0))]
    args = [x]
    if has_gain:
        in_specs.append(pl.BlockSpec((None, 1, k), lambda i, j: (layer, 0, 0)))
        args.append(gain.reshape(DEPTH, 1, k))
    in_specs.append(pl.BlockSpec((None, k, tn), lambda i, j: (layer, 0, j)))
    args.append(w)
    out_shape = [jax.ShapeDtypeStruct((m, n), out_dtype)]
    out_specs = [pl.BlockSpec((tm, tn), lambda i, j: (i, j))]
    if has_small:
        in_specs.append(pl.BlockSpec((None, k, LANES), lambda i, j: (layer, 0, 0)))
        args.append(w_small)
        out_shape.append(jax.ShapeDtypeStruct((m, LANES), F32))
        out_specs.append(pl.BlockSpec((tm, LANES), lambda i, j: (i, 0)))
    limit = _vmem_limit(_nbytes((tm, k), x.dtype), _nbytes((k, tn), w.dtype), _nbytes((tm, tn), out_dtype),
                        _nbytes((k, LANES), BF16), _nbytes((tm, LANES), F32), scratch=_nbytes((tm, k), BF16))
    res = pl.pallas_call(
        functools.partial(_norm_mm_kernel, has_gain=has_gain, has_small=has_small),
        grid=(m // tm, n // tn),
        in_specs=in_specs,
        out_specs=out_specs,
        out_shape=out_shape,
        scratch_shapes=[pltpu.VMEM((tm, k), BF16)],
        compiler_params=pltpu.CompilerParams(dimension_semantics=("parallel", "arbitrary"),
                                             vmem_limit_bytes=limit),
        name="norm_mm",
    )(*args)
    return res if has_small else res[0]


def _mem_kv_kernel(x_ref, w_ref, o_ref, xb_ref):
    @pl.when((pl.program_id(0) == 0) & (pl.program_id(1) == 0))
    def _():
        xb_ref[...] = x_ref[...].astype(BF16)

    res = jnp.dot(xb_ref[...], w_ref[...].astype(BF16), preferred_element_type=F32)
    o_ref[...] = res.reshape(o_ref.shape)


def _mem_kv(x, w, *, nb):
    m, k = x.shape
    tn = MEM_HEAD_DIM
    limit = _vmem_limit(_nbytes((m, k), F32), _nbytes((k, tn), F32), _nbytes((m, tn), F32),
                        scratch=_nbytes((m, k), BF16))
    return pl.pallas_call(
        _mem_kv_kernel,
        grid=(DEPTH, MEM_HEADS),
        in_specs=[pl.BlockSpec((m, k), lambda d, h: (0, 0)),
                  pl.BlockSpec((None, k, tn), lambda d, h: (d, 0, h))],
        out_specs=pl.BlockSpec((None, nb, None, m // nb, tn), lambda d, h: (d, 0, h, 0, 0)),
        out_shape=jax.ShapeDtypeStruct((DEPTH, nb, MEM_HEADS, m // nb, tn), F32),
        scratch_shapes=[pltpu.VMEM((m, k), BF16)],
        compiler_params=pltpu.CompilerParams(dimension_semantics=("arbitrary", "arbitrary"),
                                             vmem_limit_bytes=limit),
        name="mem_kv",
    )(x, w)


def _norm_cast_kernel(x_ref, g_ref, xn_ref):
    xn_ref[...] = _rms(x_ref[...], g_ref[...]).astype(BF16)


def _norm_cast(x, gain, layer, *, tm):
    m, k = x.shape
    return pl.pallas_call(
        _norm_cast_kernel,
        grid=(m // tm,),
        in_specs=[pl.BlockSpec((tm, k), lambda i: (i, 0)), pl.BlockSpec((None, 1, k), lambda i: (layer, 0, 0))],
        out_specs=pl.BlockSpec((tm, k), lambda i: (i, 0)),
        out_shape=jax.ShapeDtypeStruct((m, k), BF16),
        compiler_params=pltpu.CompilerParams(dimension_semantics=("parallel",)),
        name="norm_cast",
    )(x, gain.reshape(DEPTH, 1, k))


def _norm_small_kernel(x_ref, g_ref, wst_ref, xn_ref, os_ref):
    xn = _rms(x_ref[...], g_ref[...]).astype(BF16)
    xn_ref[...] = xn
    os_ref[...] = lax.dot_general(xn, wst_ref[...].astype(BF16), NT_DIMS, preferred_element_type=F32)


def _norm_small(x, gain, w_small_t, layer, *, tm):
    m, k = x.shape
    return pl.pallas_call(
        _norm_small_kernel,
        grid=(m // tm,),
        in_specs=[pl.BlockSpec((tm, k), lambda i: (i, 0)),
                  pl.BlockSpec((None, 1, k), lambda i: (layer, 0, 0)),
                  pl.BlockSpec((None, LANES, k), lambda i: (layer, 0, 0))],
        out_specs=[pl.BlockSpec((tm, k), lambda i: (i, 0)), pl.BlockSpec((tm, LANES), lambda i: (i, 0))],
        out_shape=[jax.ShapeDtypeStruct((m, k), BF16), jax.ShapeDtypeStruct((m, LANES), F32)],
        compiler_params=pltpu.CompilerParams(dimension_semantics=("parallel",)),
        name="norm_small",
    )(x, gain.reshape(DEPTH, 1, k), w_small_t)


PROJ_TN = 1024
PROJ_SRC = (0, 1024, 2048, 3072, 7200, 8224, 9248, 10272, 11312, 12336, 13360, 14384, 4096, 5152, 6176)


def _rider_steps(n_main, rider, main_fn, rider_fn):
    if not rider:
        main_fn()
        return
    i = pl.program_id(1)
    pl.when(i == 0)(rider_fn)
    pl.when(i > 0)(main_fn)


def _proj_kernel(*refs, n_main, rider):
    it = iter(refs)
    _src_ref, xn_ref = next(it), next(it)
    xn2_ref = next(it) if rider else None
    wt_ref, o_ref = next(it), next(it)
    o2_ref = next(it) if rider else None
    wbf_ref = next(it)

    @pl.when(pl.program_id(1) == 0)
    def _():
        wbf_ref[...] = wt_ref[0].astype(BF16)

    def mm(x_ref, out_ref):
        out_ref[...] = lax.dot_general(x_ref[...], wbf_ref[...], NT_DIMS,
                                       preferred_element_type=F32).astype(out_ref.dtype)

    _rider_steps(n_main, rider, lambda: mm(xn_ref, o_ref), lambda: mm(xn2_ref, o2_ref))


def _proj_in(xn, w_in_t, layer, *, tm, rider=None):
    m, k = xn.shape
    tn = PROJ_TN
    nj = len(PROJ_SRC)
    n_main = m // tm
    assert all(s % SUBLANES == 0 for s in PROJ_SRC)
    src = jnp.array([s // SUBLANES for s in PROJ_SRC], jnp.int32)
    im = (lambda i: i) if rider is None else (lambda i: jnp.maximum(i - 1, 0))
    in_specs = [pl.BlockSpec((tm, k), lambda j, i, s: (im(i), 0))]
    out_specs = [pl.BlockSpec((tm, tn), lambda j, i, s: (im(i), j))]
    out_shape = [jax.ShapeDtypeStruct((m, nj * tn), BF16)]
    args = [src, xn]
    m2 = 0
    if rider is not None:
        m2 = rider.shape[0]
        in_specs.append(pl.BlockSpec((m2, k), lambda j, i, s: (0, 0)))
        out_specs.append(pl.BlockSpec((m2, tn), lambda j, i, s: (0, j)))
        out_shape.append(jax.ShapeDtypeStruct((m2, nj * tn), BF16))
        args.append(rider)
    in_specs.append(pl.BlockSpec((pl.Element(1), pl.Element(tn), pl.Element(k)),
                                 lambda j, i, s: (layer, s[j] * SUBLANES, 0)))
    args.append(w_in_t)
    limit = _vmem_limit(_nbytes((tm + m2, k), BF16), _nbytes((tn, k), F32), _nbytes((tm + m2, tn), BF16),
                        scratch=_nbytes((tn, k), BF16))
    grid_spec = pltpu.PrefetchScalarGridSpec(
        num_scalar_prefetch=1,
        grid=(nj, n_main + (rider is not None)),
        in_specs=in_specs,
        out_specs=out_specs,
        scratch_shapes=[pltpu.VMEM((tn, k), BF16)])
    return pl.pallas_call(
        functools.partial(_proj_kernel, n_main=n_main, rider=rider is not None),
        grid_spec=grid_spec,
        out_shape=out_shape,
        compiler_params=pltpu.CompilerParams(dimension_semantics=("arbitrary", "arbitrary"),
                                             vmem_limit_bytes=limit),
        name="proj_in",
    )(*args)


def _mm_res_kernel(*refs, n_main, rider):
    it = iter(refs)
    x_ref, r_ref = next(it), next(it)
    x2_ref, r2_ref = (next(it), next(it)) if rider else (None, None)
    w_ref, o_ref = next(it), next(it)
    o2_ref = next(it) if rider else None
    wbf_ref = next(it)

    @pl.when(pl.program_id(1) == 0)
    def _():
        wbf_ref[...] = w_ref[...].astype(BF16)

    def mm(xr, rr, outr):
        outr[...] = rr[...] + jnp.dot(xr[...], wbf_ref[...], preferred_element_type=F32)

    _rider_steps(n_main, rider, lambda: mm(x_ref, r_ref, o_ref), lambda: mm(x2_ref, r2_ref, o2_ref))


def _mm_res(x, w, layer, res, *, tm, tn, rider=None):
    m, k = x.shape
    n = w.shape[2]
    n_main = m // tm
    im = (lambda i: i) if rider is None else (lambda i: jnp.maximum(i - 1, 0))
    in_specs = [pl.BlockSpec((tm, k), lambda j, i: (im(i), 0)), pl.BlockSpec((tm, tn), lambda j, i: (im(i), j))]
    out_specs = [pl.BlockSpec((tm, tn), lambda j, i: (im(i), j))]
    out_shape = [jax.ShapeDtypeStruct((m, n), F32)]
    args = [x, res]
    m2 = 0
    if rider is not None:
        m2 = rider[0].shape[0]
        in_specs += [pl.BlockSpec((m2, k), lambda j, i: (0, 0)), pl.BlockSpec((m2, tn), lambda j, i: (0, j))]
        out_specs.append(pl.BlockSpec((m2, tn), lambda j, i: (0, j)))
        out_shape.append(jax.ShapeDtypeStruct((m2, n), F32))
        args += list(rider)
    in_specs.append(pl.BlockSpec((None, k, tn), lambda j, i: (layer, 0, j)))
    args.append(w)
    limit = _vmem_limit(_nbytes((tm + m2, k), BF16), _nbytes((k, tn), F32), 2 * _nbytes((tm + m2, tn), F32),
                        scratch=_nbytes((k, tn), BF16))
    return pl.pallas_call(
        functools.partial(_mm_res_kernel, n_main=n_main, rider=rider is not None),
        grid=(n // tn, n_main + (rider is not None)),
        in_specs=in_specs,
        out_specs=out_specs,
        out_shape=out_shape,
        scratch_shapes=[pltpu.VMEM((k, tn), BF16)],
        compiler_params=pltpu.CompilerParams(dimension_semantics=("parallel", "arbitrary"),
                                             vmem_limit_bytes=limit),
        name="mm_res",
    )(*args)


def _merge_kernel(*refs, n_main, rider):
    it = iter(refs)
    main_in = [next(it) for _ in range(4)]
    rider_in = [next(it) for _ in range(4)] if rider else None
    w_ref, o_ref = next(it), next(it)
    o2_ref = next(it) if rider else None
    wbf_ref = next(it)

    @pl.when(pl.program_id(1) == 0)
    def _():
        wbf_ref[...] = w_ref[...].astype(BF16)

    def mm(ins, outr):
        y0_ref, y1_ref, g0_ref, g1_ref = ins
        b0 = jnp.dot(y0_ref[...], wbf_ref[0], preferred_element_type=F32)
        b1 = jnp.dot(y1_ref[...], wbf_ref[1], preferred_element_type=F32)
        g0 = jax.nn.sigmoid(g0_ref[...].astype(F32))
        g1 = jax.nn.sigmoid(g1_ref[...].astype(F32))
        outr[...] = (g0 * b0 + g1 * b1).astype(outr.dtype)

    _rider_steps(n_main, rider, lambda: mm(main_in, o_ref), lambda: mm(rider_in, o2_ref))


def _merge(y_ssd, y_gla, proj, w_branch, layer, *, tm, tn, rider=None):
    m, k = y_ssd.shape
    n = D_MODEL
    gb = P_GATE // tn
    nb = n // tn
    n_main = m // tm
    im = (lambda i: i) if rider is None else (lambda i: jnp.maximum(i - 1, 0))
    in_specs = [pl.BlockSpec((tm, k), lambda j, i: (im(i), 0)),
                pl.BlockSpec((tm, k), lambda j, i: (im(i), 0)),
                pl.BlockSpec((tm, tn), lambda j, i: (im(i), gb + j)),
                pl.BlockSpec((tm, tn), lambda j, i: (im(i), gb + nb + j))]
    out_specs = [pl.BlockSpec((tm, tn), lambda j, i: (im(i), j))]
    out_shape = [jax.ShapeDtypeStruct((m, n), BF16)]
    args = [y_ssd, y_gla, proj, proj]
    m2 = 0
    if rider is not None:
        y0_2, y1_2, proj2 = rider
        m2 = y0_2.shape[0]
        in_specs += [pl.BlockSpec((m2, k), lambda j, i: (0, 0)),
                     pl.BlockSpec((m2, k), lambda j, i: (0, 0)),
                     pl.BlockSpec((m2, tn), lambda j, i: (0, gb + j)),
                     pl.BlockSpec((m2, tn), lambda j, i: (0, gb + nb + j))]
        out_specs.append(pl.BlockSpec((m2, tn), lambda j, i: (0, j)))
        out_shape.append(jax.ShapeDtypeStruct((m2, n), BF16))
        args += [y0_2, y1_2, proj2, proj2]
    in_specs.append(pl.BlockSpec((None, 2, k, tn), lambda j, i: (layer, 0, 0, j)))
    args.append(w_branch)
    limit = _vmem_limit(2 * _nbytes((tm + m2, k), BF16), 2 * _nbytes((k, tn), F32),
                        3 * _nbytes((tm + m2, tn), BF16), scratch=2 * _nbytes((k, tn), BF16))
    return pl.pallas_call(
        functools.partial(_merge_kernel, n_main=n_main, rider=rider is not None),
        grid=(nb, n_main + (rider is not None)),
        in_specs=in_specs,
        out_specs=out_specs,
        out_shape=out_shape,
        scratch_shapes=[pltpu.VMEM((2, k, tn), BF16)],
        compiler_params=pltpu.CompilerParams(dimension_semantics=("parallel", "arbitrary"),
                                             vmem_limit_bytes=limit),
        name="merge",
    )(*args)


def _final_norm_kernel(x_ref, g_ref, o_ref):
    o_ref[...] = _rms(x_ref[...], g_ref[...])


def _final_norm(x, gain, *, tm):
    m, k = x.shape
    return pl.pallas_call(
        _final_norm_kernel,
        grid=(m // tm,),
        in_specs=[pl.BlockSpec((tm, k), lambda i: (i, 0)), pl.BlockSpec((1, k), lambda i: (0, 0))],
        out_specs=pl.BlockSpec((tm, k), lambda i: (i, 0)),
        out_shape=jax.ShapeDtypeStruct((m, k), F32),
        compiler_params=pltpu.CompilerParams(dimension_semantics=("parallel",)),
        name="final_norm",
    )(x, gain.reshape(1, k))


def _ffn_up_kernel(*refs, tm, tn, seq_tiles, seg):
    short = seg > 0
    it = iter(refs)
    xn_ref, wu_ref, wt_ref, cwu_ref, cwt_ref, cbu_ref, cbt_ref = (next(it) for _ in range(7))
    stu_ref = next(it) if short else None
    stt_ref = next(it) if short else None
    act_ref, cnu_ref, cnt_ref = next(it), next(it), next(it)
    wbu_ref, wbt_ref = next(it), next(it)
    tailu_ref = None if short else next(it)
    tailt_ref = None if short else next(it)
    i = pl.program_id(1)
    sl = SUBLANES
    hw = FFN_CONV - 1

    @pl.when(i == 0)
    def _():
        wbu_ref[...] = wu_ref[...].astype(BF16)
        wbt_ref[...] = wt_ref[...].astype(BF16)
        if not short:
            tailu_ref[...] = jnp.zeros_like(tailu_ref)
            tailt_ref[...] = jnp.zeros_like(tailt_ref)

    xn = xn_ref[...]
    u_raw = jnp.dot(xn, wbu_ref[...], preferred_element_type=F32)
    t_raw = jnp.dot(xn, wbt_ref[...], preferred_element_type=F32)

    if short:
        nseq = tm // seg
        r = lax.broadcasted_iota(jnp.int32, (tm, hw * nseq), 0)
        c = lax.broadcasted_iota(jnp.int32, (tm, hw * nseq), 1)
        halo_u, halo_t = [], []
        for s in range(1, FFN_CONV):
            sel = (((r % seg) < s) & (c == hw * (r // seg) + hw - s + (r % seg))).astype(BF16)
            halo_u.append(_select_rows(sel, stu_ref[...]))
            halo_t.append(_select_rows(sel, stt_ref[...]))
        ro = lax.broadcasted_iota(jnp.int32, (hw * nseq, tm), 0)
        co = lax.broadcasted_iota(jnp.int32, (hw * nseq, tm), 1)
        sel_out = (co == (ro // hw) * seg + seg - hw + (ro % hw)).astype(BF16)
        cnu_ref[...] = _select_rows(sel_out, u_raw)
        cnt_ref[...] = _select_rows(sel_out, t_raw)
        rowmod = lax.broadcasted_iota(jnp.int32, (tm, tn), 0) % seg
        prev_u = prev_t = None
    else:
        start = (i % seq_tiles) == 0
        prev_u = jnp.where(start, 0.0, tailu_ref[...])
        prev_t = jnp.where(start, 0.0, tailt_ref[...])
        tailu_ref[...] = u_raw[tm - sl:tm, :]
        tailt_ref[...] = t_raw[tm - sl:tm, :]
        cnu_ref[...] = u_raw[tm - sl:tm, :]
        cnt_ref[...] = t_raw[tm - sl:tm, :]
        row8 = lax.broadcasted_iota(jnp.int32, (sl, tn), 0)
        halo_u = halo_t = rowmod = None

    def conv(x, prev8, cw_ref, cb_ref, halos):
        y = x * cw_ref[hw:hw + 1, :]
        for s in range(1, FFN_CONV):
            rolled = pltpu.roll(x, s, axis=0)
            if short:
                shifted = jnp.where(rowmod < s, halos[s - 1], rolled)
            else:
                head = jnp.where(row8 < s, pltpu.roll(prev8, s, axis=0), rolled[0:sl, :])
                shifted = jnp.concatenate([head, rolled[sl:, :]], axis=0)
            y = y + shifted * cw_ref[hw - s:hw - s + 1, :]
        return y + cb_ref[...]

    u = conv(u_raw, prev_u, cwu_ref, cbu_ref, halo_u)
    t = conv(t_raw, prev_t, cwt_ref, cbt_ref, halo_t)
    act_ref[...] = (_silu(t) * u).astype(act_ref.dtype)


def _ffn_up(xn, w, conv_w, conv_b, conv0, layer, *, L, tm, tn):
    m, k = xn.shape
    bt = m // L
    nj = D_FF // tn
    hw = FFN_CONV - 1
    short = L < tm
    seg = L if short else 0
    seq_tiles = 1 if short else L // tm
    assert (conv0 is not None) == short, "history rows are only supported for sequences shorter than a block"
    in_specs = [
        pl.BlockSpec((tm, k), lambda j, i: (i, 0)),
        pl.BlockSpec((None, k, tn), lambda j, i: (layer, 0, j)),
        pl.BlockSpec((None, k, tn), lambda j, i: (layer, 0, nj + j)),
        pl.BlockSpec((None, FFN_CONV, tn), lambda j, i: (layer, 0, j)),
        pl.BlockSpec((None, FFN_CONV, tn), lambda j, i: (layer, 0, nj + j)),
        pl.BlockSpec((None, 1, tn), lambda j, i: (layer, 0, j)),
        pl.BlockSpec((None, 1, tn), lambda j, i: (layer, 0, nj + j)),
    ]
    args = [xn, w, w, conv_w, conv_w, conv_b.reshape(DEPTH, 1, 2 * D_FF), conv_b.reshape(DEPTH, 1, 2 * D_FF)]
    scratch = [pltpu.VMEM((k, tn), BF16)] * 2
    if short:
        nst = (m // tm) * (tm // L) * hw
        st2d = conv0.reshape(DEPTH, nst, 2 * D_FF)
        rows = (tm // L) * hw
        in_specs += [pl.BlockSpec((None, rows, tn), lambda j, i: (layer, i, j)),
                     pl.BlockSpec((None, rows, tn), lambda j, i: (layer, i, nj + j))]
        args += [st2d, st2d]
        cn_shape = jax.ShapeDtypeStruct((nst, D_FF), F32)
        cn_spec = pl.BlockSpec((rows, tn), lambda j, i: (i, j))
    else:
        scratch += [pltpu.VMEM((SUBLANES, tn), F32)] * 2
        cn_shape = jax.ShapeDtypeStruct((m // tm, SUBLANES, D_FF), F32)
        cn_spec = pl.BlockSpec((None, SUBLANES, tn), lambda j, i: (i, 0, j))
    limit = _vmem_limit(_nbytes((tm, k), BF16), 2 * _nbytes((k, tn), F32), _nbytes((tm, tn), BF16),
                        scratch=2 * _nbytes((k, tn), BF16) + 6 * _nbytes((tm, tn), F32))
    act, cnu, cnt = pl.pallas_call(
        functools.partial(_ffn_up_kernel, tm=tm, tn=tn, seq_tiles=seq_tiles, seg=seg),
        grid=(nj, m // tm),
        in_specs=in_specs,
        out_specs=[pl.BlockSpec((tm, tn), lambda j, i: (i, j)), cn_spec, cn_spec],
        out_shape=[jax.ShapeDtypeStruct((m, D_FF), BF16), cn_shape, cn_shape],
        scratch_shapes=scratch,
        compiler_params=pltpu.CompilerParams(dimension_semantics=("arbitrary", "arbitrary"),
                                             vmem_limit_bytes=limit),
        name="ffn_up",
    )(*args)
    if short:
        conv_new = jnp.concatenate([cnu, cnt], axis=-1).reshape(bt, hw, 2 * D_FF)
    else:
        last = slice(seq_tiles - 1, None, seq_tiles)
        conv_new = jnp.concatenate([cnu[last, SUBLANES - hw:, :], cnt[last, SUBLANES - hw:, :]], axis=-1)
    return act, conv_new


def _causal_conv(x, tail_ref, w_ref, b, width):
    sl = SUBLANES
    row = lax.broadcasted_iota(jnp.int32, (sl, x.shape[1]), 0)
    prev = tail_ref[...]
    y = x * w_ref[width - 1:width, :]
    for s in range(1, width):
        rolled = pltpu.roll(x, s, axis=0)
        head = jnp.where(row < s, pltpu.roll(prev, s, axis=0), rolled[0:sl, :])
        shifted = jnp.concatenate([head, rolled[sl:, :]], axis=0)
        y = y + shifted * w_ref[width - 1 - s:width - s, :]
    return y + b


def _ssd_kernel(*refs, Q, zero_init, aliased):
    it = iter(refs)
    z_ref, xs_ref, bc_ref, sm_ref = (next(it) for _ in range(4))
    conv0_ref = None if zero_init else next(it)
    h0_ref = None if zero_init else next(it)
    cwx_ref, cwb_ref, cbx_ref, cbb_ref, dtb_ref, a_ref, dx_ref, nrm_ref, exp_ref = (next(it) for _ in range(9))
    if aliased:
        next(it), next(it)
    y_ref, h_out, conv_out, tailx_ref, tailb_ref = (next(it) for _ in range(5))
    h_ref = h_out if aliased else h_out.at[0]
    convout_ref = conv_out if aliased else conv_out.at[0]
    c = pl.program_id(1)
    hp = SSD_HEADDIM
    gw = SSD_HEADS // SSD_GROUPS * hp

    @pl.when(c == 0)
    def _():
        if not aliased:
            h_out[1:] = jnp.zeros((DEPTH - 1,) + h_ref.shape, F32)
            conv_out[1:] = jnp.zeros((DEPTH - 1,) + convout_ref.shape, F32)
        tailx_ref[...] = jnp.zeros_like(tailx_ref)
        tailb_ref[...] = jnp.zeros_like(tailb_ref)
        if zero_init:
            h_ref[...] = jnp.zeros_like(h_ref)
        else:
            h_ref[...] = h0_ref[...]
            tailx_ref[5:8, :] = conv0_ref[:, 0:D_MODEL]
            tailb_ref[5:8, :] = conv0_ref[:, D_MODEL:SSD_CONV_DIM]

    xs_raw = xs_ref[...].astype(F32)
    bc_raw = bc_ref[...].astype(F32)
    xs = _silu(_causal_conv(xs_raw, tailx_ref, cwx_ref, cbx_ref[...], SSD_CONV))
    bc = _silu(_causal_conv(bc_raw, tailb_ref, cwb_ref, cbb_ref[...], SSD_CONV))
    tailx_ref[...] = xs_raw[Q - 8:Q, :]
    tailb_ref[...] = bc_raw[Q - 8:Q, :]

    @pl.when(c == pl.num_programs(1) - 1)
    def _():
        convout_ref[:, 0:D_MODEL] = tailx_ref[5:8, :]
        convout_ref[:, D_MODEL:SSD_CONV_DIM] = tailb_ref[5:8, :]

    dt = _softplus(sm_ref[...] + dtb_ref[...])
    adt = dt * a_ref[...]
    ri = lax.broadcasted_iota(jnp.int32, (Q, Q), 0)
    ci = lax.broadcasted_iota(jnp.int32, (Q, Q), 1)
    causal = ci <= ri
    acum = jnp.dot(causal.astype(F32), adt, precision=HI, preferred_element_type=F32)
    eye = (lax.broadcasted_iota(jnp.int32, (LANES, LANES), 0)
           == lax.broadcasted_iota(jnp.int32, (LANES, LANES), 1)).astype(F32)
    acum_t = lax.dot_general(eye, acum, NT_DIMS, precision=HI, preferred_element_type=F32)
    dt_t = lax.dot_general(eye, dt, NT_DIMS, precision=HI, preferred_element_type=F32)
    a_last = acum[Q - 1:Q, :]
    ea = jnp.exp(acum)
    te = jnp.exp(a_last - acum) * dt
    dec_rows = jnp.broadcast_to(jnp.exp(acum_t[:, Q - 1:Q]), (LANES, LANES))
    ea_hi, ea_lo = _split2(ea)
    te_hi, te_lo = _split2(te)
    expand = exp_ref[...]
    ea_x = (jnp.dot(ea_hi, expand, preferred_element_type=F32)
            + jnp.dot(ea_lo, expand, preferred_element_type=F32))
    te_x = (jnp.dot(te_hi, expand, preferred_element_type=F32)
            + jnp.dot(te_lo, expand, preferred_element_type=F32))
    lane = lax.broadcasted_iota(jnp.int32, (Q, LANES), 1)

    for g in range(SSD_GROUPS):
        gl = slice(g * gw, (g + 1) * gw)
        b_g = bc[:, g * SSD_STATE:(g + 1) * SSD_STATE].astype(BF16)
        c_g = bc[:, (SSD_GROUPS + g) * SSD_STATE:(SSD_GROUPS + g + 1) * SSD_STATE].astype(BF16)
        cb = lax.dot_general(c_g, b_g, NT_DIMS, preferred_element_type=F32)
        h_g = h_ref[8 * g:8 * g + 8].reshape(gw, SSD_STATE)
        y_off = lax.dot_general(c_g, h_g.astype(BF16), NT_DIMS, preferred_element_type=F32)
        x_g = xs[:, gl]
        pairs = []
        for p in range(4):
            x_p = x_g[:, p * LANES:(p + 1) * LANES]
            acc = None
            for s in range(2):
                hh = g * 8 + p * 2 + s
                seg = jnp.broadcast_to(acum[:, hh:hh + 1], (Q, Q)) - jnp.broadcast_to(acum_t[hh:hh + 1, :], (Q, Q))
                decay = jnp.where(causal, jnp.exp(seg), 0.0)
                w_h = (cb * decay * dt_t[hh:hh + 1, :]).astype(BF16)
                x_m = jnp.where((lane // hp) == s, x_p, 0.0).astype(BF16)
                r = jnp.dot(w_h, x_m, preferred_element_type=F32)
                acc = r if acc is None else acc + r
            pairs.append(acc)
        y_g = jnp.concatenate(pairs, axis=1) + y_off * ea_x[:, gl] + dx_ref[:, gl] * x_g
        x_t = (x_g * te_x[:, gl]).astype(BF16)
        upd = lax.dot_general(x_t, b_g, TN_DIMS, preferred_element_type=F32)
        for h in range(8):
            hh = g * 8 + h
            h_ref[hh] = h_ref[hh] * dec_rows[hh:hh + 1, :] + upd[h * hp:(h + 1) * hp, :]
        z_g = z_ref[:, gl].astype(F32)
        y_ref[:, gl] = _rms(y_g * _silu(z_g), nrm_ref[:, gl]).astype(y_ref.dtype)


def _ssd(proj, small, conv0, h0, wl, layer, prev_h, prev_conv, *, bt, L, Q):
    zero_init = h0 is None
    aliased = prev_h is not None
    nc = L // Q
    T = bt * L
    full2 = lambda b, c: (0, 0)
    lay3 = lambda b, c: (layer, 0, 0)
    row = lambda col: (lambda b, c: (b * nc + c, col))
    hshape = (DEPTH, bt, SSD_HEADS, SSD_HEADDIM, SSD_STATE)
    cshape = (DEPTH, bt, SSD_CONV - 1, SSD_CONV_DIM)
    hspec = pl.BlockSpec((None, None, SSD_HEADS, SSD_HEADDIM, SSD_STATE), lambda b, c: (layer, b, 0, 0, 0))
    cspec = pl.BlockSpec((None, None, SSD_CONV - 1, SSD_CONV_DIM), lambda b, c: (layer, b, 0, 0))
    h_out_spec = hspec if aliased else pl.BlockSpec((DEPTH, None, SSD_HEADS, SSD_HEADDIM, SSD_STATE),
                                                    lambda b, c: (0, b, 0, 0, 0))
    c_out_spec = cspec if aliased else pl.BlockSpec((DEPTH, None, SSD_CONV - 1, SSD_CONV_DIM),
                                                    lambda b, c: (0, b, 0, 0))
    in_specs = [
        pl.BlockSpec((Q, D_MODEL), row(P_Z // D_MODEL)),
        pl.BlockSpec((Q, D_MODEL), row(P_XS // D_MODEL)),
        pl.BlockSpec((Q, 1024), row(P_BC // 1024)),
        pl.BlockSpec((Q, LANES), row(0)),
    ]
    args = [proj, proj, proj, small]
    if not zero_init:
        in_specs += [cspec, hspec]
        args += [conv0, h0]
    in_specs += [
        pl.BlockSpec((None, SSD_CONV, D_MODEL), lay3),
        pl.BlockSpec((None, SSD_CONV, 1024), lambda b, c: (layer, 0, D_MODEL // 1024)),
        pl.BlockSpec((None, 1, D_MODEL), lay3),
        pl.BlockSpec((None, 1, 1024), lambda b, c: (layer, 0, D_MODEL // 1024)),
        pl.BlockSpec((None, 1, LANES), lay3),
        pl.BlockSpec((None, 1, LANES), lay3),
        pl.BlockSpec((None, 1, D_MODEL), lay3),
        pl.BlockSpec((None, 1, D_MODEL), lay3),
        pl.BlockSpec((LANES, D_MODEL), full2),
    ]
    args += [wl["ssd_conv_w"], wl["ssd_conv_w"], wl["ssd_conv_b"], wl["ssd_conv_b"], wl["ssd_dtb"], wl["ssd_a"],
             wl["ssd_dx"], wl["ssd_norm"], wl["expand"]]
    aliases = {}
    if aliased:
        aliases = {len(args): 1, len(args) + 1: 2}
        in_specs += [pl.BlockSpec(memory_space=pl.ANY), pl.BlockSpec(memory_space=pl.ANY)]
        args += [prev_h, prev_conv]
    return pl.pallas_call(
        functools.partial(_ssd_kernel, Q=Q, zero_init=zero_init, aliased=aliased),
        grid=(bt, nc),
        in_specs=in_specs,
        out_specs=[pl.BlockSpec((Q, D_MODEL), row(0)), h_out_spec, c_out_spec],
        out_shape=[jax.ShapeDtypeStruct((T, D_MODEL), BF16), _stacked_out(prev_h, hshape, F32),
                   _stacked_out(prev_conv, cshape, F32)],
        scratch_shapes=[pltpu.VMEM((8, D_MODEL), F32), pltpu.VMEM((8, 1024), F32)],
        input_output_aliases=aliases,
        compiler_params=pltpu.CompilerParams(dimension_semantics=("parallel", "arbitrary"),
                                             vmem_limit_bytes=48 << 20),
        name="ssd",
    )(*args)


def _gla_kernel(*refs, TB, zero_init, aliased):
    it = iter(refs)
    q_ref, k_ref, v_ref, g_ref, sm_ref = (next(it) for _ in range(5))
    s0_ref = None if zero_init else next(it)
    wa_ref, ba_ref, wat_ref, bat_ref, gn_ref = (next(it) for _ in range(5))
    if aliased:
        next(it)
    y_ref, s_out = next(it), next(it)
    s_ref = s_out if aliased else s_out.at[0]
    c = pl.program_id(1)
    ck = GLA_CHUNK
    ns = TB // ck
    hk, hv = GLA_HEAD_K, GLA_HEAD_V

    @pl.when(c == 0)
    def _():
        if not aliased:
            s_out[1:] = jnp.zeros((DEPTH - 1,) + s_ref.shape, F32)
        if zero_init:
            s_ref[...] = jnp.zeros_like(s_ref)
        else:
            s_ref[...] = s0_ref[...]

    smb = sm_ref[...].astype(BF16)
    la = _log_sigmoid(jnp.dot(smb, wa_ref[...], preferred_element_type=F32) + ba_ref[...]) * (1.0 / GLA_GATE_NORM)
    rb = lax.broadcasted_iota(jnp.int32, (TB, TB), 0)
    cb = lax.broadcasted_iota(jnp.int32, (TB, TB), 1)
    same = (rb // ck) == (cb // ck)
    msel = jnp.concatenate([same & (cb <= rb), same, (cb // ck) < (rb // ck)], axis=0).astype(BF16)
    sums = _select_rows(msel, la)
    bcum, tot, bprev = sums[0:TB], sums[TB:2 * TB], sums[2 * TB:3 * TB]
    kf = k_ref[...].astype(F32)
    qd = q_ref[...].astype(F32) * (hk ** -0.5) * jnp.exp(bcum)
    qd_b = qd.astype(BF16)
    qs_b = (qd * jnp.exp(bprev)).astype(BF16)
    ki = kf * jnp.exp(-bcum)
    ke = kf * jnp.exp(tot - bcum)
    dtot = jnp.exp(tot)
    la_t = _log_sigmoid(lax.dot_general(wat_ref[...], smb, NT_DIMS, preferred_element_type=F32)
                        + bat_ref[:, 0:1]) * (1.0 / GLA_GATE_NORM)
    ones = jnp.ones((TB, LANES), BF16)
    la_hi = la_t.astype(BF16)
    la_r = la_t - la_hi.astype(F32)
    la_mid = la_r.astype(BF16)
    la_lo = (la_r - la_mid.astype(F32)).astype(BF16)
    dec_t = jnp.exp(jnp.dot(la_hi, ones, preferred_element_type=F32) + jnp.dot(la_mid, ones, preferred_element_type=F32)
                    + jnp.dot(la_lo, ones, preferred_element_type=F32))
    rchunk = lax.broadcasted_iota(jnp.int32, (TB, hk), 0) // ck
    row16 = lax.broadcasted_iota(jnp.int32, (ck, TB), 0)
    col16 = lax.broadcasted_iota(jnp.int32, (ck, TB), 1)

    for h in range(GLA_HEADS):
        kl = slice(h * hk, (h + 1) * hk)
        vl = slice(h * hv, (h + 1) * hv)
        ke_h, ki_h = ke[:, kl], ki[:, kl]
        v_h = v_ref[:, vl]
        kbuf = jnp.zeros((TB, hk), F32)
        att = []
        for cc in range(ns):
            in_c = rchunk == cc
            kall = jnp.where(in_c, ki_h, kbuf).astype(BF16)
            a = lax.dot_general(qd_b[cc * ck:(cc + 1) * ck, kl], kall, NT_DIMS, preferred_element_type=F32)
            att.append(jnp.where(col16 <= row16 + cc * ck, a, 0.0))
            kbuf = jnp.where(in_c, ke_h, kbuf * dtot[cc * ck:cc * ck + 1, kl])
        a_full = jnp.concatenate(att, axis=0).astype(BF16)
        s_old = s_ref[h]
        o = (jnp.dot(a_full, v_h, preferred_element_type=F32)
             + jnp.dot(qs_b[:, kl], s_old.astype(BF16), preferred_element_type=F32))
        y_ref[:, vl] = (_rms(o, gn_ref[...]) * _silu(g_ref[:, vl].astype(F32))).astype(y_ref.dtype)
        dcol = jnp.concatenate([dec_t[kl, :]] * (hv // LANES), axis=1)
        s_ref[h] = s_old * dcol + lax.dot_general(kbuf.astype(BF16), v_h, TN_DIMS, preferred_element_type=F32)


def _gla(proj, small, s0, wl, layer, prev_s, *, bt, L, TB):
    zero_init = s0 is None
    aliased = prev_s is not None
    nc = L // TB
    T = bt * L
    lay3 = lambda b, c: (layer, 0, 0)
    row = lambda col: (lambda b, c: (b * nc + c, col))
    kd = GLA_HEADS * GLA_HEAD_K
    sshape = (DEPTH, bt, GLA_HEADS, GLA_HEAD_K, GLA_HEAD_V)
    sspec = pl.BlockSpec((None, None, GLA_HEADS, GLA_HEAD_K, GLA_HEAD_V), lambda b, c: (layer, b, 0, 0, 0))
    s_out_spec = sspec if aliased else pl.BlockSpec((DEPTH, None, GLA_HEADS, GLA_HEAD_K, GLA_HEAD_V),
                                                     lambda b, c: (0, b, 0, 0, 0))
    in_specs = [
        pl.BlockSpec((TB, kd), row(P_Q // kd)),
        pl.BlockSpec((TB, kd), row(P_K // kd)),
        pl.BlockSpec((TB, D_MODEL), row(P_V // D_MODEL)),
        pl.BlockSpec((TB, D_MODEL), row(P_G // D_MODEL)),
        pl.BlockSpec((TB, LANES), row(0)),
    ]
    args = [proj, proj, proj, proj, small]
    if not zero_init:
        in_specs.append(sspec)
        args.append(s0)
    in_specs += [
        pl.BlockSpec((None, LANES, kd), lay3),
        pl.BlockSpec((None, 1, kd), lay3),
        pl.BlockSpec((None, kd, LANES), lay3),
        pl.BlockSpec((None, kd, LANES), lay3),
        pl.BlockSpec((None, 1, GLA_HEAD_V), lay3),
    ]
    args += [wl["gla_wa"], wl["gla_ba"], wl["gla_wa_t"], wl["gla_ba_t"], wl["gla_norm"]]
    aliases = {}
    if aliased:
        aliases = {len(args): 1}
        in_specs.append(pl.BlockSpec(memory_space=pl.ANY))
        args.append(prev_s)
    return pl.pallas_call(
        functools.partial(_gla_kernel, TB=TB, zero_init=zero_init, aliased=aliased),
        grid=(bt, nc),
        in_specs=in_specs,
        out_specs=[pl.BlockSpec((TB, D_MODEL), row(0)), s_out_spec],
        out_shape=[jax.ShapeDtypeStruct((T, D_MODEL), BF16), _stacked_out(prev_s, sshape, F32)],
        input_output_aliases=aliases,
        compiler_params=pltpu.CompilerParams(dimension_semantics=("parallel", "arbitrary"),
                                             vmem_limit_bytes=48 << 20),
        name="gla",
    )(*args)


def _attn_kernel(q_ref, k_ref, v_ref, o_ref, kb_ref, vb_ref):
    @pl.when(pl.program_id(1) == 0)
    def _():
        kb_ref[...] = k_ref[...].astype(BF16)
        vb_ref[...] = v_ref[...].astype(BF16)

    hd = MEM_HEAD_DIM
    for h in range(MEM_HEADS):
        hl = slice(h * hd, (h + 1) * hd)
        s = lax.dot_general(q_ref[:, hl], kb_ref[h], NT_DIMS, preferred_element_type=F32) * (hd ** -0.5)
        e = jnp.exp(s - jnp.max(s, axis=-1, keepdims=True))
        p = e / jnp.sum(e, axis=-1, keepdims=True)
        o_ref[:, hl] = jnp.dot(p.astype(BF16), vb_ref[h], preferred_element_type=F32).astype(o_ref.dtype)


def _attn(q, mem_k, mem_v, layer, *, bt, L, tl):
    nl = L // tl
    T = bt * L
    kvshape = (MEM_HEADS, MEM_TOKENS, MEM_HEAD_DIM)
    kvspec = pl.BlockSpec((None, None) + kvshape, lambda b, l: (layer, b, 0, 0, 0))
    return pl.pallas_call(
        _attn_kernel,
        grid=(bt, nl),
        in_specs=[pl.BlockSpec((tl, D_MODEL), lambda b, l: (b * nl + l, 0)), kvspec, kvspec],
        out_specs=pl.BlockSpec((tl, D_MODEL), lambda b, l: (b * nl + l, 0)),
        out_shape=jax.ShapeDtypeStruct((T, D_MODEL), BF16),
        scratch_shapes=[pltpu.VMEM(kvshape, BF16), pltpu.VMEM(kvshape, BF16)],
        compiler_params=pltpu.CompilerParams(dimension_semantics=("parallel", "arbitrary"),
                                             vmem_limit_bytes=40 << 20),
        name="mem_attn",
    )(q, mem_k, mem_v)


def _prep_weights(w_in, ssd_conv_w, ssd_conv_b, ssd_dt_bias, ssd_a_log, ssd_d, ssd_norm, gla_wa2, gla_ba, gla_norm):
    w_in_t = jnp.swapaxes(w_in, 1, 2)
    o_dt, o_alr = 5120, 11296
    pad_sm = LANES - SSD_HEADS - GLA_RANK
    w_small_t = jnp.concatenate(
        [w_in_t[:, o_dt:o_dt + SSD_HEADS, :], w_in_t[:, o_alr:o_alr + GLA_RANK, :],
         jnp.zeros((DEPTH, pad_sm, D_MODEL), F32)], axis=1)
    pad_h = LANES - SSD_HEADS
    kd = GLA_HEADS * GLA_HEAD_K
    wa = jnp.concatenate([jnp.zeros((DEPTH, SM_ALR, kd), F32), gla_wa2,
                          jnp.zeros((DEPTH, LANES - SM_ALR - GLA_RANK, kd), F32)], axis=1).astype(BF16)
    expand = (jnp.arange(D_MODEL)[None, :] // SSD_HEADDIM == jnp.arange(LANES)[:, None]).astype(BF16)
    return dict(
        w_in_t=w_in_t, w_small_t=w_small_t,
        ssd_conv_w=ssd_conv_w, ssd_conv_b=ssd_conv_b[:, None, :],
        ssd_dtb=jnp.pad(ssd_dt_bias, ((0, 0), (0, pad_h)))[:, None, :],
        ssd_a=jnp.pad(-jnp.exp(ssd_a_log), ((0, 0), (0, pad_h)))[:, None, :],
        ssd_dx=jnp.repeat(ssd_d, SSD_HEADDIM, axis=1)[:, None, :],
        ssd_norm=ssd_norm[:, None, :], expand=expand,
        gla_wa=wa, gla_ba=gla_ba[:, None, :], gla_wa_t=jnp.swapaxes(wa, 1, 2),
        gla_ba_t=jnp.broadcast_to(gla_ba[:, :, None], (DEPTH, kd, LANES)),
        gla_norm=gla_norm[:, None, :])


def _group(x, bt, L, mem_k, mem_v, st_ssd=None, st_ssd_conv=None, st_gla=None, st_ffn_conv=None):
    return dict(x=x, bt=bt, L=L, mem_k=mem_k, mem_v=mem_v, st_ssd=st_ssd, st_ssd_conv=st_ssd_conv, st_gla=st_gla,
                st_ffn_conv=st_ffn_conv, cfg=_group_cfg(bt, L))


def _run_trunk(groups, wl, big, norm_final):
    main = groups[0]
    cfg = main["cfg"]
    tm = cfg["tm"]
    xs = [g["x"] for g in groups]
    acc = [dict(h=None, c=None, s=None, ffn=[]) for _ in groups]
    rest = lambda vals: vals[1] if len(vals) > 1 else None
    for i in range(DEPTH):
        normed = [_norm_small(x, big["norm_mix"], wl["w_small_t"], i, tm=g["cfg"]["tm_norm"])
                  for x, g in zip(xs, groups)]
        xn, small = [n[0] for n in normed], [n[1] for n in normed]
        proj = _proj_in(xn[0], wl["w_in_t"], i, tm=cfg["tm_proj"], rider=rest(xn))
        y_ssd, y_gla = [], []
        for g, a, p, sm in zip(groups, acc, proj, small):
            y, a["h"], a["c"] = _ssd(p, sm, g["st_ssd_conv"], g["st_ssd"], wl, i, a["h"], a["c"],
                                     bt=g["bt"], L=g["L"], Q=g["cfg"]["ssd_q"])
            y_ssd.append(y)
            y, a["s"] = _gla(p, sm, g["st_gla"], wl, i, a["s"], bt=g["bt"], L=g["L"], TB=g["cfg"]["gla_tb"])
            y_gla.append(y)
        merged = _merge(y_ssd[0], y_gla[0], proj[0], big["w_branch"], i, tm=tm, tn=512,
                        rider=rest(list(zip(y_ssd, y_gla, proj))))
        xs = _mm_res(merged[0], big["w_out"], i, xs[0], tm=tm, tn=1024, rider=rest(list(zip(merged, xs))))
        o = []
        for g, x in zip(groups, xs):
            q = _norm_mm(x, big["norm_mem"], big["w_mq"], i, None, tm=g["cfg"]["tm"], tn=512, out_dtype=BF16)
            o.append(_attn(q, g["mem_k"], g["mem_v"], i, bt=g["bt"], L=g["L"], tl=g["cfg"]["attn_tl"]))
        xs = _mm_res(o[0], big["w_mo"], i, xs[0], tm=tm, tn=1024, rider=rest(list(zip(o, xs))))
        act = []
        for g, a, x in zip(groups, acc, xs):
            xf = _norm_cast(x, big["norm_ffn"], i, tm=g["cfg"]["tm_norm"])
            y, f_c = _ffn_up(xf, big["w_ffn_in"], big["ffn_conv_w"], big["ffn_conv_b"], g["st_ffn_conv"], i,
                             L=g["L"], tm=g["cfg"]["tm"], tn=512)
            act.append(y)
            a["ffn"].append(f_c)
        xs = _mm_res(act[0], big["w_ffn_out"], i, xs[0], tm=cfg["tm_norm"], tn=512, rider=rest(list(zip(act, xs))))
    return [(_final_norm(x, norm_final, tm=g["cfg"]["tm_norm"]), a["h"], a["c"], a["s"], jnp.stack(a["ffn"]))
            for g, a, x in zip(groups, acc, xs)]


def _group_cfg(bt, L):
    T = bt * L
    return dict(tm=min(T, 1024), tm_proj=min(T, 1024), tm_norm=min(T, 512), ssd_q=min(L, 128), gla_tb=min(L, 128), attn_tl=min(L, 512))


def kernel(x_prompt, x_sample, mem_prompt, state_ssd, state_ssd_conv, state_gla, state_ffn_conv, cache_mem_k, cache_mem_v, norm_mix, w_in, ssd_conv_w, ssd_conv_b, ssd_dt_bias, ssd_a_log, ssd_d, ssd_norm, gla_wa2, gla_ba, gla_norm, w_branch, w_out, norm_mem, w_mq, w_mk, w_mv, w_mo, norm_ffn, w_ffn_in, ffn_conv_w, ffn_conv_b, w_ffn_out, norm_final):
    wl = _prep_weights(w_in, ssd_conv_w, ssd_conv_b, ssd_dt_bias, ssd_a_log, ssd_d, ssd_norm, gla_wa2, gla_ba, gla_norm)
    big = dict(norm_mix=norm_mix, w_branch=w_branch, w_out=w_out, norm_mem=norm_mem, w_mq=w_mq, w_mo=w_mo,
               norm_ffn=norm_ffn, w_ffn_in=w_ffn_in, ffn_conv_w=ffn_conv_w, ffn_conv_b=ffn_conv_b, w_ffn_out=w_ffn_out)
    pb, pl_len, _ = x_prompt.shape
    sb, sl_len, _ = x_sample.shape
    mem2d = mem_prompt.reshape(pb * MEM_TOKENS, D_MODEL)
    p_mem_k = _mem_kv(mem2d, w_mk, nb=pb)
    p_mem_v = _mem_kv(mem2d, w_mv, nb=pb)
    head_major = (0, 1, 3, 2, 4)

    prompt = _group(x_prompt.reshape(pb * pl_len, D_MODEL), pb, pl_len, p_mem_k, p_mem_v)
    sample = _group(x_sample.reshape(sb * sl_len, D_MODEL), sb, sl_len,
                    jnp.transpose(cache_mem_k, head_major), jnp.transpose(cache_mem_v, head_major),
                    state_ssd, state_ssd_conv, state_gla, state_ffn_conv)
    (y_p, p_ssd, p_ssd_conv, p_gla, p_ffn), (y_s, s_ssd, s_ssd_conv, s_gla, s_ffn) = _run_trunk(
        [prompt, sample], wl, big, norm_final)
    return (y_p.reshape(pb, pl_len, D_MODEL), y_s.reshape(sb, sl_len, D_MODEL),
            p_ssd, p_ssd_conv, p_gla, p_ffn, jnp.transpose(p_mem_k, head_major), jnp.transpose(p_mem_v, head_major),
            s_ssd, s_ssd_conv, s_gla, s_ffn)
```

```python
import functools

import jax
import jax.numpy as jnp
from jax import lax
from jax.experimental import pallas as pl
from jax.experimental.pallas import tpu as pltpu

F32 = jnp.float32
BF16 = jnp.bfloat16

D_MODEL = 2048
DEPTH = 2
EPS = 1e-6
SSD_HEADS = 32
SSD_HEADDIM = 64
SSD_GROUPS = 4
SSD_STATE = 128
SSD_CONV = 4
SSD_CONV_DIM = 3072
GLA_HEADS = 4
GLA_HEAD_K = 256
GLA_HEAD_V = 512
GLA_RANK = 16
GLA_GATE_NORM = 16.0
GLA_CHUNK = 16
MEM_TOKENS = 256
MEM_HEADS = 4
MEM_HEAD_DIM = 512
D_FF = 5632
FFN_CONV = 3

LANES = 128
MXU_COLS = 256
SUBLANES = 8
VMEM_CAP_BYTES = 56 * 2**20

P_Z, P_XS, P_V, P_G, P_GATE, P_BC, P_Q, P_K, P_N = 0, 2048, 4096, 6144, 8192, 12288, 13312, 14336, 15360
SM_DT, SM_ALR = 0, 32

NT_DIMS = (((1,), (1,)), ((), ()))
TN_DIMS = (((0,), (0,)), ((), ()))
HI = lax.Precision.HIGHEST


def _vmem_limit(*block_bytes, scratch=0):
    need = 2 * sum(block_bytes) + scratch + (4 << 20)
    return int(min(max(need, 16 << 20), VMEM_CAP_BYTES))


def _nbytes(shape, dtype):
    n = 1
    for s in shape:
        n *= s
    return n * jnp.dtype(dtype).itemsize


def _silu(x):
    return x * jax.nn.sigmoid(x)


def _softplus(x):
    return jnp.maximum(x, 0.0) + jnp.log1p(jnp.exp(-jnp.abs(x)))


def _log_sigmoid(x):
    return -_softplus(-x)


def _rms(x, gain):
    ms = jnp.mean(x * x, axis=-1, keepdims=True)
    return x * lax.rsqrt(ms + EPS) * gain


def _split2(x):
    hi = x.astype(BF16)
    lo = (x - hi.astype(F32)).astype(BF16)
    return hi, lo


def _select_rows(sel, x):
    hi = x.astype(BF16)
    r1 = x - hi.astype(F32)
    mid = r1.astype(BF16)
    lo = (r1 - mid.astype(F32)).astype(BF16)
    return (jnp.dot(sel, hi, preferred_element_type=F32) + jnp.dot(sel, mid, preferred_element_type=F32)
            + jnp.dot(sel, lo, preferred_element_type=F32))


def _stacked_out(prev, shape, dtype):
    return jax.ShapeDtypeStruct(shape, dtype) if prev is None else jax.ShapeDtypeStruct(prev.shape, prev.dtype)


def _norm_mm_kernel(*refs, has_gain, has_small):
    it = iter(refs)
    x_ref = next(it)
    g_ref = next(it) if has_gain else None
    w_ref = next(it)
    ws_ref = next(it) if has_small else None
    o_ref = next(it)
    os_ref = next(it) if has_small else None
    xn_ref = next(it)

    @pl.when(pl.program_id(1) == 0)
    def _():
        x = x_ref[...].astype(F32)
        if has_gain:
            x = _rms(x, g_ref[...])
        xn_ref[...] = x.astype(BF16)
        if has_small:
            os_ref[...] = jnp.dot(xn_ref[...], ws_ref[...], preferred_element_type=F32)

    o_ref[...] = jnp.dot(xn_ref[...], w_ref[...].astype(BF16), preferred_element_type=F32).astype(o_ref.dtype)


def _norm_mm(x, gain, w, layer, w_small, *, tm, tn, out_dtype):
    m, k = x.shape
    n = w.shape[2]
    has_gain = gain is not None
    has_small = w_small is not None
    in_specs = [pl.BlockSpec((tm, k), lambda i, j: (i, 0))]
    args = [x]
    if has_gain:
        in_specs.append(pl.BlockSpec((None, 1, k), lambda i, j: (layer, 0, 0)))
        args.append(gain.reshape(DEPTH, 1, k))
    in_specs.append(pl.BlockSpec((None, k, tn), lambda i, j: (layer, 0, j)))
    args.append(w)
    out_shape = [jax.ShapeDtypeStruct((m, n), out_dtype)]
    out_specs = [pl.BlockSpec((tm, tn), lambda i, j: (i, j))]
    if has_small:
        in_specs.append(pl.BlockSpec((None, k, LANES), lambda i, j: (layer, 0, 0)))
        args.append(w_small)
        out_shape.append(jax.ShapeDtypeStruct((m, LANES), F32))
        out_specs.append(pl.BlockSpec((tm, LANES), lambda i, j: (i, 0)))
    limit = _vmem_limit(_nbytes((tm, k), x.dtype), _nbytes((k, tn), w.dtype), _nbytes((tm, tn), out_dtype),
                        _nbytes((k, LANES), BF16), _nbytes((tm, LANES), F32), scratch=_nbytes((tm, k), BF16))
    res = pl.pallas_call(
        functools.partial(_norm_mm_kernel, has_gain=has_gain, has_small=has_small),
        grid=(m // tm, n // tn),
        in_specs=in_specs,
        out_specs=out_specs,
        out_shape=out_shape,
        scratch_shapes=[pltpu.VMEM((tm, k), BF16)],
        compiler_params=pltpu.CompilerParams(dimension_semantics=("parallel", "arbitrary"),
                                             vmem_limit_bytes=limit),
        name="norm_mm",
    )(*args)
    return res if has_small else res[0]


def _mem_kv_kernel(x_ref, w_ref, o_ref, xb_ref):
    @pl.when((pl.program_id(0) == 0) & (pl.program_id(1) == 0))
    def _():
        xb_ref[...] = x_ref[...].astype(BF16)

    res = jnp.dot(xb_ref[...], w_ref[...].astype(BF16), preferred_element_type=F32)
    o_ref[...] = res.reshape(o_ref.shape)


def _mem_kv(x, w, *, nb):
    m, k = x.shape
    tn = MEM_HEAD_DIM
    limit = _vmem_limit(_nbytes((m, k), F32), _nbytes((k, tn), F32), _nbytes((m, tn), F32),
                        scratch=_nbytes((m, k), BF16))
    return pl.pallas_call(
        _mem_kv_kernel,
        grid=(DEPTH, MEM_HEADS),
        in_specs=[pl.BlockSpec((m, k), lambda d, h: (0, 0)),
                  pl.BlockSpec((None, k, tn), lambda d, h: (d, 0, h))],
        out_specs=pl.BlockSpec((None, nb, None, m // nb, tn), lambda d, h: (d, 0, h, 0, 0)),
        out_shape=jax.ShapeDtypeStruct((DEPTH, nb, MEM_HEADS, m // nb, tn), F32),
        scratch_shapes=[pltpu.VMEM((m, k), BF16)],
        compiler_params=pltpu.CompilerParams(dimension_semantics=("arbitrary", "arbitrary"),
                                             vmem_limit_bytes=limit),
        name="mem_kv",
    )(x, w)


def _cast_kernel(w_ref, o_ref):
    o_ref[...] = w_ref[...].astype(o_ref.dtype)


def _cast_bf16(w, *, tn):
    d, k, n = w.shape
    return pl.pallas_call(
        _cast_kernel,
        grid=(d, n // tn),
        in_specs=[pl.BlockSpec((None, k, tn), lambda l, j: (l, 0, j))],
        out_specs=pl.BlockSpec((None, k, tn), lambda l, j: (l, 0, j)),
        out_shape=jax.ShapeDtypeStruct(w.shape, BF16),
        compiler_params=pltpu.CompilerParams(dimension_semantics=("parallel", "parallel")),
        name="cast_bf16",
    )(w)


def _norm_cast_kernel(x_ref, g_ref, xn_ref):
    xn_ref[...] = _rms(x_ref[...], g_ref[...]).astype(BF16)


def _norm_cast(x, gain, layer, *, tm):
    m, k = x.shape
    return pl.pallas_call(
        _norm_cast_kernel,
        grid=(m // tm,),
        in_specs=[pl.BlockSpec((tm, k), lambda i: (i, 0)), pl.BlockSpec((None, 1, k), lambda i: (layer, 0, 0))],
        out_specs=pl.BlockSpec((tm, k), lambda i: (i, 0)),
        out_shape=jax.ShapeDtypeStruct((m, k), BF16),
        compiler_params=pltpu.CompilerParams(dimension_semantics=("parallel",)),
        name="norm_cast",
    )(x, gain.reshape(DEPTH, 1, k))


def _norm_small_kernel(x_ref, g_ref, wst_ref, xn_ref, os_ref):
    xn = _rms(x_ref[...], g_ref[...]).astype(BF16)
    xn_ref[...] = xn
    os_ref[...] = lax.dot_general(xn, wst_ref[...].astype(BF16), NT_DIMS, preferred_element_type=F32)


def _norm_small(x, gain, w_small_t, layer, *, tm):
    m, k = x.shape
    return pl.pallas_call(
        _norm_small_kernel,
        grid=(m // tm,),
        in_specs=[pl.BlockSpec((tm, k), lambda i: (i, 0)),
                  pl.BlockSpec((None, 1, k), lambda i: (layer, 0, 0)),
                  pl.BlockSpec((None, LANES, k), lambda i: (layer, 0, 0))],
        out_specs=[pl.BlockSpec((tm, k), lambda i: (i, 0)), pl.BlockSpec((tm, LANES), lambda i: (i, 0))],
        out_shape=[jax.ShapeDtypeStruct((m, k), BF16), jax.ShapeDtypeStruct((m, LANES), F32)],
        compiler_params=pltpu.CompilerParams(dimension_semantics=("parallel",)),
        name="norm_small",
    )(x, gain.reshape(DEPTH, 1, k), w_small_t)


PROJ_TN = 1024
PROJ_SRC = (0, 1024, 2048, 3072, 7200, 8224, 9248, 10272, 11312, 12336, 13360, 14384, 4096, 5152, 6176)


def _rider_steps(n_main, rider, main_fn, rider_fn):
    if not rider:
        main_fn()
        return
    i = pl.program_id(1)
    pl.when(i == 0)(rider_fn)
    pl.when(i > 0)(main_fn)


def _proj_kernel(*refs, n_main, rider):
    it = iter(refs)
    _src_ref, xn_ref = next(it), next(it)
    xn2_ref = next(it) if rider else None
    wt_ref, o_ref = next(it), next(it)
    o2_ref = next(it) if rider else None
    wbf_ref = next(it)

    @pl.when(pl.program_id(1) == 0)
    def _():
        wbf_ref[...] = wt_ref[0].astype(BF16)

    def mm(x_ref, out_ref):
        out_ref[...] = lax.dot_general(x_ref[...], wbf_ref[...], NT_DIMS,
                                       preferred_element_type=F32).astype(out_ref.dtype)

    _rider_steps(n_main, rider, lambda: mm(xn_ref, o_ref), lambda: mm(xn2_ref, o2_ref))


def _proj_in(xn, w_in_t, layer, *, tm, rider=None):
    m, k = xn.shape
    tn = PROJ_TN
    nj = len(PROJ_SRC)
    n_main = m // tm
    assert all(s % SUBLANES == 0 for s in PROJ_SRC)
    src = jnp.array([s // SUBLANES for s in PROJ_SRC], jnp.int32)
    im = (lambda i: i) if rider is None else (lambda i: jnp.maximum(i - 1, 0))
    in_specs = [pl.BlockSpec((tm, k), lambda j, i, s: (im(i), 0))]
    out_specs = [pl.BlockSpec((tm, tn), lambda j, i, s: (im(i), j))]
    out_shape = [jax.ShapeDtypeStruct((m, nj * tn), BF16)]
    args = [src, xn]
    m2 = 0
    if rider is not None:
        m2 = rider.shape[0]
        in_specs.append(pl.BlockSpec((m2, k), lambda j, i, s: (0, 0)))
        out_specs.append(pl.BlockSpec((m2, tn), lambda j, i, s: (0, j)))
        out_shape.append(jax.ShapeDtypeStruct((m2, nj * tn), BF16))
        args.append(rider)
    in_specs.append(pl.BlockSpec((pl.Element(1), pl.Element(tn), pl.Element(k)),
                                 lambda j, i, s: (layer, s[j] * SUBLANES, 0)))
    args.append(w_in_t)
    limit = _vmem_limit(_nbytes((tm + m2, k), BF16), _nbytes((tn, k), F32), _nbytes((tm + m2, tn), BF16),
                        scratch=_nbytes((tn, k), BF16))
    grid_spec = pltpu.PrefetchScalarGridSpec(
        num_scalar_prefetch=1,
        grid=(nj, n_main + (rider is not None)),
        in_specs=in_specs,
        out_specs=out_specs,
        scratch_shapes=[pltpu.VMEM((tn, k), BF16)])
    return pl.pallas_call(
        functools.partial(_proj_kernel, n_main=n_main, rider=rider is not None),
        grid_spec=grid_spec,
        out_shape=out_shape,
        compiler_params=pltpu.CompilerParams(dimension_semantics=("arbitrary", "arbitrary"),
                                             vmem_limit_bytes=limit),
        name="proj_in",
    )(*args)


def _mm_res_kernel(*refs, n_main, rider):
    it = iter(refs)
    x_ref, r_ref = next(it), next(it)
    x2_ref, r2_ref = (next(it), next(it)) if rider else (None, None)
    w_ref, o_ref = next(it), next(it)
    o2_ref = next(it) if rider else None
    wbf_ref = next(it)

    @pl.when(pl.program_id(1) == 0)
    def _():
        wbf_ref[...] = w_ref[...].astype(BF16)

    def mm(xr, rr, outr):
        outr[...] = rr[...] + jnp.dot(xr[...], wbf_ref[...], preferred_element_type=F32)

    _rider_steps(n_main, rider, lambda: mm(x_ref, r_ref, o_ref), lambda: mm(x2_ref, r2_ref, o2_ref))


def _mm_res(x, w, layer, res, *, tm, tn, rider=None):
    m, k = x.shape
    n = w.shape[2]
    n_main = m // tm
    im = (lambda i: i) if rider is None else (lambda i: jnp.maximum(i - 1, 0))
    in_specs = [pl.BlockSpec((tm, k), lambda j, i: (im(i), 0)), pl.BlockSpec((tm, tn), lambda j, i: (im(i), j))]
    out_specs = [pl.BlockSpec((tm, tn), lambda j, i: (im(i), j))]
    out_shape = [jax.ShapeDtypeStruct((m, n), F32)]
    args = [x, res]
    m2 = 0
    if rider is not None:
        m2 = rider[0].shape[0]
        in_specs += [pl.BlockSpec((m2, k), lambda j, i: (0, 0)), pl.BlockSpec((m2, tn), lambda j, i: (0, j))]
        out_specs.append(pl.BlockSpec((m2, tn), lambda j, i: (0, j)))
        out_shape.append(jax.ShapeDtypeStruct((m2, n), F32))
        args += list(rider)
    in_specs.append(pl.BlockSpec((None, k, tn), lambda j, i: (layer, 0, j)))
    args.append(w)
    limit = _vmem_limit(_nbytes((tm + m2, k), BF16), _nbytes((k, tn), F32), 2 * _nbytes((tm + m2, tn), F32),
                        scratch=_nbytes((k, tn), BF16))
    return pl.pallas_call(
        functools.partial(_mm_res_kernel, n_main=n_main, rider=rider is not None),
        grid=(n // tn, n_main + (rider is not None)),
        in_specs=in_specs,
        out_specs=out_specs,
        out_shape=out_shape,
        scratch_shapes=[pltpu.VMEM((k, tn), BF16)],
        compiler_params=pltpu.CompilerParams(dimension_semantics=("parallel", "arbitrary"),
                                             vmem_limit_bytes=limit),
        name="mm_res",
    )(*args)


def _merge_kernel(*refs, n_main, rider):
    it = iter(refs)
    main_in = [next(it) for _ in range(4)]
    rider_in = [next(it) for _ in range(4)] if rider else None
    w_ref, o_ref = next(it), next(it)
    o2_ref = next(it) if rider else None
    wbf_ref = next(it)

    @pl.when(pl.program_id(1) == 0)
    def _():
        wbf_ref[...] = w_ref[...].astype(BF16)

    def mm(ins, outr):
        y0_ref, y1_ref, g0_ref, g1_ref = ins
        b0 = jnp.dot(y0_ref[...], wbf_ref[0], preferred_element_type=F32)
        b1 = jnp.dot(y1_ref[...], wbf_ref[1], preferred_element_type=F32)
        g0 = jax.nn.sigmoid(g0_ref[...].astype(F32))
        g1 = jax.nn.sigmoid(g1_ref[...].astype(F32))
        outr[...] = (g0 * b0 + g1 * b1).astype(outr.dtype)

    _rider_steps(n_main, rider, lambda: mm(main_in, o_ref), lambda: mm(rider_in, o2_ref))


def _merge(y_ssd, y_gla, proj, w_branch, layer, *, tm, tn, rider=None):
    m, k = y_ssd.shape
    n = D_MODEL
    gb = P_GATE // tn
    nb = n // tn
    n_main = m // tm
    im = (lambda i: i) if rider is None else (lambda i: jnp.maximum(i - 1, 0))
    in_specs = [pl.BlockSpec((tm, k), lambda j, i: (im(i), 0)),
                pl.BlockSpec((tm, k), lambda j, i: (im(i), 0)),
                pl.BlockSpec((tm, tn), lambda j, i: (im(i), gb + j)),
                pl.BlockSpec((tm, tn), lambda j, i: (im(i), gb + nb + j))]
    out_specs = [pl.BlockSpec((tm, tn), lambda j, i: (im(i), j))]
    out_shape = [jax.ShapeDtypeStruct((m, n), BF16)]
    args = [y_ssd, y_gla, proj, proj]
    m2 = 0
    if rider is not None:
        y0_2, y1_2, proj2 = rider
        m2 = y0_2.shape[0]
        in_specs += [pl.BlockSpec((m2, k), lambda j, i: (0, 0)),
                     pl.BlockSpec((m2, k), lambda j, i: (0, 0)),
                     pl.BlockSpec((m2, tn), lambda j, i: (0, gb + j)),
                     pl.BlockSpec((m2, tn), lambda j, i: (0, gb + nb + j))]
        out_specs.append(pl.BlockSpec((m2, tn), lambda j, i: (0, j)))
        out_shape.append(jax.ShapeDtypeStruct((m2, n), BF16))
        args += [y0_2, y1_2, proj2, proj2]
    in_specs.append(pl.BlockSpec((None, 2, k, tn), lambda j, i: (layer, 0, 0, j)))
    args.append(w_branch)
    limit = _vmem_limit(2 * _nbytes((tm + m2, k), BF16), 2 * _nbytes((k, tn), F32),
                        3 * _nbytes((tm + m2, tn), BF16), scratch=2 * _nbytes((k, tn), BF16))
    return pl.pallas_call(
        functools.partial(_merge_kernel, n_main=n_main, rider=rider is not None),
        grid=(nb, n_main + (rider is not None)),
        in_specs=in_specs,
        out_specs=out_specs,
        out_shape=out_shape,
        scratch_shapes=[pltpu.VMEM((2, k, tn), BF16)],
        compiler_params=pltpu.CompilerParams(dimension_semantics=("parallel", "arbitrary"),
                                             vmem_limit_bytes=limit),
        name="merge",
    )(*args)


def _final_norm_kernel(x_ref, g_ref, o_ref):
    o_ref[...] = _rms(x_ref[...], g_ref[...])


def _final_norm(x, gain, *, tm):
    m, k = x.shape
    return pl.pallas_call(
        _final_norm_kernel,
        grid=(m // tm,),
        in_specs=[pl.BlockSpec((tm, k), lambda i: (i, 0)), pl.BlockSpec((1, k), lambda i: (0, 0))],
        out_specs=pl.BlockSpec((tm, k), lambda i: (i, 0)),
        out_shape=jax.ShapeDtypeStruct((m, k), F32),
        compiler_params=pltpu.CompilerParams(dimension_semantics=("parallel",)),
        name="final_norm",
    )(x, gain.reshape(1, k))


def _ffn_up_kernel(*refs, tm, tn, seq_tiles, seg):
    short = seg > 0
    it = iter(refs)
    xn_ref, wu_ref, wt_ref, cwu_ref, cwt_ref, cbu_ref, cbt_ref = (next(it) for _ in range(7))
    stu_ref = next(it) if short else None
    stt_ref = next(it) if short else None
    act_ref, cnu_ref, cnt_ref = next(it), next(it), next(it)
    wbu_ref, wbt_ref = next(it), next(it)
    tailu_ref = None if short else next(it)
    tailt_ref = None if short else next(it)
    i = pl.program_id(1)
    sl = SUBLANES
    hw = FFN_CONV - 1

    @pl.when(i == 0)
    def _():
        wbu_ref[...] = wu_ref[...].astype(BF16)
        wbt_ref[...] = wt_ref[...].astype(BF16)
        if not short:
            tailu_ref[...] = jnp.zeros_like(tailu_ref)
            tailt_ref[...] = jnp.zeros_like(tailt_ref)

    xn = xn_ref[...]
    u_raw = jnp.dot(xn, wbu_ref[...], preferred_element_type=F32)
    t_raw = jnp.dot(xn, wbt_ref[...], preferred_element_type=F32)

    if short:
        nseq = tm // seg
        r = lax.broadcasted_iota(jnp.int32, (tm, hw * nseq), 0)
        c = lax.broadcasted_iota(jnp.int32, (tm, hw * nseq), 1)
        halo_u, halo_t = [], []
        for s in range(1, FFN_CONV):
            sel = (((r % seg) < s) & (c == hw * (r // seg) + hw - s + (r % seg))).astype(BF16)
            halo_u.append(_select_rows(sel, stu_ref[...]))
            halo_t.append(_select_rows(sel, stt_ref[...]))
        ro = lax.broadcasted_iota(jnp.int32, (hw * nseq, tm), 0)
        co = lax.broadcasted_iota(jnp.int32, (hw * nseq, tm), 1)
        sel_out = (co == (ro // hw) * seg + seg - hw + (ro % hw)).astype(BF16)
        cnu_ref[...] = _select_rows(sel_out, u_raw)
        cnt_ref[...] = _select_rows(sel_out, t_raw)
        rowmod = lax.broadcasted_iota(jnp.int32, (tm, tn), 0) % seg
        prev_u = prev_t = None
    else:
        start = (i % seq_tiles) == 0
        prev_u = jnp.where(start, 0.0, tailu_ref[...])
        prev_t = jnp.where(start, 0.0, tailt_ref[...])
        tailu_ref[...] = u_raw[tm - sl:tm, :]
        tailt_ref[...] = t_raw[tm - sl:tm, :]
        cnu_ref[...] = u_raw[tm - sl:tm, :]
        cnt_ref[...] = t_raw[tm - sl:tm, :]
        row8 = lax.broadcasted_iota(jnp.int32, (sl, tn), 0)
        halo_u = halo_t = rowmod = None

    def conv(x, prev8, cw_ref, cb_ref, halos):
        y = x * cw_ref[hw:hw + 1, :]
        for s in range(1, FFN_CONV):
            rolled = pltpu.roll(x, s, axis=0)
            if short:
                shifted = jnp.where(rowmod < s, halos[s - 1], rolled)
            else:
                head = jnp.where(row8 < s, pltpu.roll(prev8, s, axis=0), rolled[0:sl, :])
                shifted = jnp.concatenate([head, rolled[sl:, :]], axis=0)
            y = y + shifted * cw_ref[hw - s:hw - s + 1, :]
        return y + cb_ref[...]

    u = conv(u_raw, prev_u, cwu_ref, cbu_ref, halo_u)
    t = conv(t_raw, prev_t, cwt_ref, cbt_ref, halo_t)
    act_ref[...] = (_silu(t) * u).astype(act_ref.dtype)


def _ffn_up(xn, w, conv_w, conv_b, conv0, layer, *, L, tm, tn):
    m, k = xn.shape
    bt = m // L
    nj = D_FF // tn
    hw = FFN_CONV - 1
    short = L < tm
    seg = L if short else 0
    seq_tiles = 1 if short else L // tm
    assert (conv0 is not None) == short, "history rows are only supported for sequences shorter than a block"
    in_specs = [
        pl.BlockSpec((tm, k), lambda j, i: (i, 0)),
        pl.BlockSpec((None, k, tn), lambda j, i: (layer, 0, j)),
        pl.BlockSpec((None, k, tn), lambda j, i: (layer, 0, nj + j)),
        pl.BlockSpec((None, FFN_CONV, tn), lambda j, i: (layer, 0, j)),
        pl.BlockSpec((None, FFN_CONV, tn), lambda j, i: (layer, 0, nj + j)),
        pl.BlockSpec((None, 1, tn), lambda j, i: (layer, 0, j)),
        pl.BlockSpec((None, 1, tn), lambda j, i: (layer, 0, nj + j)),
    ]
    args = [xn, w, w, conv_w, conv_w, conv_b.reshape(DEPTH, 1, 2 * D_FF), conv_b.reshape(DEPTH, 1, 2 * D_FF)]
    scratch = [pltpu.VMEM((k, tn), BF16)] * 2
    if short:
        nst = (m // tm) * (tm // L) * hw
        st2d = conv0.reshape(DEPTH, nst, 2 * D_FF)
        rows = (tm // L) * hw
        in_specs += [pl.BlockSpec((None, rows, tn), lambda j, i: (layer, i, j)),
                     pl.BlockSpec((None, rows, tn), lambda j, i: (layer, i, nj + j))]
        args += [st2d, st2d]
        cn_shape = jax.ShapeDtypeStruct((nst, D_FF), F32)
        cn_spec = pl.BlockSpec((rows, tn), lambda j, i: (i, j))
    else:
        scratch += [pltpu.VMEM((SUBLANES, tn), F32)] * 2
        cn_shape = jax.ShapeDtypeStruct((m // tm, SUBLANES, D_FF), F32)
        cn_spec = pl.BlockSpec((None, SUBLANES, tn), lambda j, i: (i, 0, j))
    limit = _vmem_limit(_nbytes((tm, k), BF16), 2 * _nbytes((k, tn), F32), _nbytes((tm, tn), BF16),
                        scratch=2 * _nbytes((k, tn), BF16) + 6 * _nbytes((tm, tn), F32))
    act, cnu, cnt = pl.pallas_call(
        functools.partial(_ffn_up_kernel, tm=tm, tn=tn, seq_tiles=seq_tiles, seg=seg),
        grid=(nj, m // tm),
        in_specs=in_specs,
        out_specs=[pl.BlockSpec((tm, tn), lambda j, i: (i, j)), cn_spec, cn_spec],
        out_shape=[jax.ShapeDtypeStruct((m, D_FF), BF16), cn_shape, cn_shape],
        scratch_shapes=scratch,
        compiler_params=pltpu.CompilerParams(dimension_semantics=("arbitrary", "arbitrary"),
                                             vmem_limit_bytes=limit),
        name="ffn_up",
    )(*args)
    if short:
        conv_new = jnp.concatenate([cnu, cnt], axis=-1).reshape(bt, hw, 2 * D_FF)
    else:
        last = slice(seq_tiles - 1, None, seq_tiles)
        conv_new = jnp.concatenate([cnu[last, SUBLANES - hw:, :], cnt[last, SUBLANES - hw:, :]], axis=-1)
    return act, conv_new


def _causal_conv(x, tail_ref, w_ref, b, width):
    sl = SUBLANES
    row = lax.broadcasted_iota(jnp.int32, (sl, x.shape[1]), 0)
    prev = tail_ref[...]
    y = x * w_ref[width - 1:width, :]
    for s in range(1, width):
        rolled = pltpu.roll(x, s, axis=0)
        head = jnp.where(row < s, pltpu.roll(prev, s, axis=0), rolled[0:sl, :])
        shifted = jnp.concatenate([head, rolled[sl:, :]], axis=0)
        y = y + shifted * w_ref[width - 1 - s:width - s, :]
    return y + b


def _ssd_kernel(*refs, Q, zero_init, aliased):
    it = iter(refs)
    z_ref, xs_ref, bc_ref, sm_ref = (next(it) for _ in range(4))
    conv0_ref = None if zero_init else next(it)
    h0_ref = None if zero_init else next(it)
    cwx_ref, cwb_ref, cbx_ref, cbb_ref, dtb_ref, a_ref, dx_ref, nrm_ref, exp_ref = (next(it) for _ in range(9))
    if aliased:
        next(it), next(it)
    y_ref, h_out, conv_out, tailx_ref, tailb_ref = (next(it) for _ in range(5))
    h_ref = h_out if aliased else h_out.at[0]
    convout_ref = conv_out if aliased else conv_out.at[0]
    c = pl.program_id(1)
    hp = SSD_HEADDIM
    gw = SSD_HEADS // SSD_GROUPS * hp

    @pl.when(c == 0)
    def _():
        if not aliased:
            h_out[1:] = jnp.zeros((DEPTH - 1,) + h_ref.shape, F32)
            conv_out[1:] = jnp.zeros((DEPTH - 1,) + convout_ref.shape, F32)
        tailx_ref[...] = jnp.zeros_like(tailx_ref)
        tailb_ref[...] = jnp.zeros_like(tailb_ref)
        if zero_init:
            h_ref[...] = jnp.zeros_like(h_ref)
        else:
            h_ref[...] = h0_ref[...]
            tailx_ref[5:8, :] = conv0_ref[:, 0:D_MODEL]
            tailb_ref[5:8, :] = conv0_ref[:, D_MODEL:SSD_CONV_DIM]

    xs_raw = xs_ref[...].astype(F32)
    bc_raw = bc_ref[...].astype(F32)
    xs = _silu(_causal_conv(xs_raw, tailx_ref, cwx_ref, cbx_ref[...], SSD_CONV))
    bc = _silu(_causal_conv(bc_raw, tailb_ref, cwb_ref, cbb_ref[...], SSD_CONV))
    tailx_ref[...] = xs_raw[Q - 8:Q, :]
    tailb_ref[...] = bc_raw[Q - 8:Q, :]

    @pl.when(c == pl.num_programs(1) - 1)
    def _():
        convout_ref[:, 0:D_MODEL] = tailx_ref[5:8, :]
        convout_ref[:, D_MODEL:SSD_CONV_DIM] = tailb_ref[5:8, :]

    dt = _softplus(sm_ref[...] + dtb_ref[...])
    adt = dt * a_ref[...]
    ri = lax.broadcasted_iota(jnp.int32, (Q, Q), 0)
    ci = lax.broadcasted_iota(jnp.int32, (Q, Q), 1)
    causal = ci <= ri
    acum = jnp.dot(causal.astype(F32), adt, precision=HI, preferred_element_type=F32)
    eye = (lax.broadcasted_iota(jnp.int32, (LANES, LANES), 0)
           == lax.broadcasted_iota(jnp.int32, (LANES, LANES), 1)).astype(F32)
    acum_t = lax.dot_general(eye, acum, NT_DIMS, precision=HI, preferred_element_type=F32)
    dt_t = lax.dot_general(eye, dt, NT_DIMS, precision=HI, preferred_element_type=F32)
    a_last = acum[Q - 1:Q, :]
    ea = jnp.exp(acum)
    te = jnp.exp(a_last - acum) * dt
    dec_rows = jnp.broadcast_to(jnp.exp(acum_t[:, Q - 1:Q]), (LANES, LANES))
    ea_hi, ea_lo = _split2(ea)
    te_hi, te_lo = _split2(te)
    expand = exp_ref[...]
    ea_x = (jnp.dot(ea_hi, expand, preferred_element_type=F32)
            + jnp.dot(ea_lo, expand, preferred_element_type=F32))
    te_x = (jnp.dot(te_hi, expand, preferred_element_type=F32)
            + jnp.dot(te_lo, expand, preferred_element_type=F32))
    lane = lax.broadcasted_iota(jnp.int32, (Q, LANES), 1)

    for g in range(SSD_GROUPS):
        gl = slice(g * gw, (g + 1) * gw)
        b_g = bc[:, g * SSD_STATE:(g + 1) * SSD_STATE].astype(BF16)
        c_g = bc[:, (SSD_GROUPS + g) * SSD_STATE:(SSD_GROUPS + g + 1) * SSD_STATE].astype(BF16)
        cb = lax.dot_general(c_g, b_g, NT_DIMS, preferred_element_type=F32)
        h_g = h_ref[8 * g:8 * g + 8].reshape(gw, SSD_STATE)
        y_off = lax.dot_general(c_g, h_g.astype(BF16), NT_DIMS, preferred_element_type=F32)
        x_g = xs[:, gl]
        pairs = []
        for p in range(4):
            x_p = x_g[:, p * LANES:(p + 1) * LANES]
            acc = None
            for s in range(2):
                hh = g * 8 + p * 2 + s
                seg = jnp.broadcast_to(acum[:, hh:hh + 1], (Q, Q)) - jnp.broadcast_to(acum_t[hh:hh + 1, :], (Q, Q))
                decay = jnp.where(causal, jnp.exp(seg), 0.0)
                w_h = (cb * decay * dt_t[hh:hh + 1, :]).astype(BF16)
                x_m = jnp.where((lane // hp) == s, x_p, 0.0).astype(BF16)
                r = jnp.dot(w_h, x_m, preferred_element_type=F32)
                acc = r if acc is None else acc + r
            pairs.append(acc)
        y_g = jnp.concatenate(pairs, axis=1) + y_off * ea_x[:, gl] + dx_ref[:, gl] * x_g
        x_t = (x_g * te_x[:, gl]).astype(BF16)
        upd = lax.dot_general(x_t, b_g, TN_DIMS, preferred_element_type=F32)
        for h in range(8):
            hh = g * 8 + h
            h_ref[hh] = h_ref[hh] * dec_rows[hh:hh + 1, :] + upd[h * hp:(h + 1) * hp, :]
        z_g = z_ref[:, gl].astype(F32)
        y_ref[:, gl] = _rms(y_g * _silu(z_g), nrm_ref[:, gl]).astype(y_ref.dtype)


def _ssd(proj, small, conv0, h0, wl, layer, prev_h, prev_conv, *, bt, L, Q):
    zero_init = h0 is None
    aliased = prev_h is not None
    nc = L // Q
    T = bt * L
    full2 = lambda b, c: (0, 0)
    lay3 = lambda b, c: (layer, 0, 0)
    row = lambda col: (lambda b, c: (b * nc + c, col))
    hshape = (DEPTH, bt, SSD_HEADS, SSD_HEADDIM, SSD_STATE)
    cshape = (DEPTH, bt, SSD_CONV - 1, SSD_CONV_DIM)
    hspec = pl.BlockSpec((None, None, SSD_HEADS, SSD_HEADDIM, SSD_STATE), lambda b, c: (layer, b, 0, 0, 0))
    cspec = pl.BlockSpec((None, None, SSD_CONV - 1, SSD_CONV_DIM), lambda b, c: (layer, b, 0, 0))
    h_out_spec = hspec if aliased else pl.BlockSpec((DEPTH, None, SSD_HEADS, SSD_HEADDIM, SSD_STATE),
                                                    lambda b, c: (0, b, 0, 0, 0))
    c_out_spec = cspec if aliased else pl.BlockSpec((DEPTH, None, SSD_CONV - 1, SSD_CONV_DIM),
                                                    lambda b, c: (0, b, 0, 0))
    in_specs = [
        pl.BlockSpec((Q, D_MODEL), row(P_Z // D_MODEL)),
        pl.BlockSpec((Q, D_MODEL), row(P_XS // D_MODEL)),
        pl.BlockSpec((Q, 1024), row(P_BC // 1024)),
        pl.BlockSpec((Q, LANES), row(0)),
    ]
    args = [proj, proj, proj, small]
    if not zero_init:
        in_specs += [cspec, hspec]
        args += [conv0, h0]
    in_specs += [
        pl.BlockSpec((None, SSD_CONV, D_MODEL), lay3),
        pl.BlockSpec((None, SSD_CONV, 1024), lambda b, c: (layer, 0, D_MODEL // 1024)),
        pl.BlockSpec((None, 1, D_MODEL), lay3),
        pl.BlockSpec((None, 1, 1024), lambda b, c: (layer, 0, D_MODEL // 1024)),
        pl.BlockSpec((None, 1, LANES), lay3),
        pl.BlockSpec((None, 1, LANES), lay3),
        pl.BlockSpec((None, 1, D_MODEL), lay3),
        pl.BlockSpec((None, 1, D_MODEL), lay3),
        pl.BlockSpec((LANES, D_MODEL), full2),
    ]
    args += [wl["ssd_conv_w"], wl["ssd_conv_w"], wl["ssd_conv_b"], wl["ssd_conv_b"], wl["ssd_dtb"], wl["ssd_a"],
             wl["ssd_dx"], wl["ssd_norm"], wl["expand"]]
    aliases = {}
    if aliased:
        aliases = {len(args): 1, len(args) + 1: 2}
        in_specs += [pl.BlockSpec(memory_space=pl.ANY), pl.BlockSpec(memory_space=pl.ANY)]
        args += [prev_h, prev_conv]
    return pl.pallas_call(
        functools.partial(_ssd_kernel, Q=Q, zero_init=zero_init, aliased=aliased),
        grid=(bt, nc),
        in_specs=in_specs,
        out_specs=[pl.BlockSpec((Q, D_MODEL), row(0)), h_out_spec, c_out_spec],
        out_shape=[jax.ShapeDtypeStruct((T, D_MODEL), BF16), _stacked_out(prev_h, hshape, F32),
                   _stacked_out(prev_conv, cshape, F32)],
        scratch_shapes=[pltpu.VMEM((8, D_MODEL), F32), pltpu.VMEM((8, 1024), F32)],
        input_output_aliases=aliases,
        compiler_params=pltpu.CompilerParams(dimension_semantics=("parallel", "arbitrary"),
                                             vmem_limit_bytes=48 << 20),
        name="ssd",
    )(*args)


def _gla_kernel(*refs, TB, zero_init, aliased):
    it = iter(refs)
    q_ref, k_ref, v_ref, g_ref, sm_ref = (next(it) for _ in range(5))
    s0_ref = None if zero_init else next(it)
    wa_ref, ba_ref, wat_ref, bat_ref, gn_ref = (next(it) for _ in range(5))
    if aliased:
        next(it)
    y_ref, s_out = next(it), next(it)
    s_ref = s_out if aliased else s_out.at[0]
    c = pl.program_id(1)
    ck = GLA_CHUNK
    ns = TB // ck
    hk, hv = GLA_HEAD_K, GLA_HEAD_V

    @pl.when(c == 0)
    def _():
        if not aliased:
            s_out[1:] = jnp.zeros((DEPTH - 1,) + s_ref.shape, F32)
        if zero_init:
            s_ref[...] = jnp.zeros_like(s_ref)
        else:
            s_ref[...] = s0_ref[...]

    smb = sm_ref[...].astype(BF16)
    la = _log_sigmoid(jnp.dot(smb, wa_ref[...], preferred_element_type=F32) + ba_ref[...]) * (1.0 / GLA_GATE_NORM)
    rb = lax.broadcasted_iota(jnp.int32, (TB, TB), 0)
    cb = lax.broadcasted_iota(jnp.int32, (TB, TB), 1)
    same = (rb // ck) == (cb // ck)
    msel = jnp.concatenate([same & (cb <= rb), same, (cb // ck) < (rb // ck)], axis=0).astype(BF16)
    sums = _select_rows(msel, la)
    bcum, tot, bprev = sums[0:TB], sums[TB:2 * TB], sums[2 * TB:3 * TB]
    kf = k_ref[...].astype(F32)
    qd = q_ref[...].astype(F32) * (hk ** -0.5) * jnp.exp(bcum)
    qd_b = qd.astype(BF16)
    qs_b = (qd * jnp.exp(bprev)).astype(BF16)
    ki = kf * jnp.exp(-bcum)
    ke = kf * jnp.exp(tot - bcum)
    dtot = jnp.exp(tot)
    la_t = _log_sigmoid(lax.dot_general(wat_ref[...], smb, NT_DIMS, preferred_element_type=F32)
                        + bat_ref[:, 0:1]) * (1.0 / GLA_GATE_NORM)
    ones = jnp.ones((TB, LANES), BF16)
    la_hi = la_t.astype(BF16)
    la_r = la_t - la_hi.astype(F32)
    la_mid = la_r.astype(BF16)
    la_lo = (la_r - la_mid.astype(F32)).astype(BF16)
    dec_t = jnp.exp(jnp.dot(la_hi, ones, preferred_element_type=F32) + jnp.dot(la_mid, ones, preferred_element_type=F32)
                    + jnp.dot(la_lo, ones, preferred_element_type=F32))
    rchunk = lax.broadcasted_iota(jnp.int32, (TB, hk), 0) // ck
    row16 = lax.broadcasted_iota(jnp.int32, (ck, TB), 0)
    col16 = lax.broadcasted_iota(jnp.int32, (ck, TB), 1)

    for h in range(GLA_HEADS):
        kl = slice(h * hk, (h + 1) * hk)
        vl = slice(h * hv, (h + 1) * hv)
        ke_h, ki_h = ke[:, kl], ki[:, kl]
        v_h = v_ref[:, vl]
        kbuf = jnp.zeros((TB, hk), F32)
        att = []
        for cc in range(ns):
            in_c = rchunk == cc
            kall = jnp.where(in_c, ki_h, kbuf).astype(BF16)
            a = lax.dot_general(qd_b[cc * ck:(cc + 1) * ck, kl], kall, NT_DIMS, preferred_element_type=F32)
            att.append(jnp.where(col16 <= row16 + cc * ck, a, 0.0))
            kbuf = jnp.where(in_c, ke_h, kbuf * dtot[cc * ck:cc * ck + 1, kl])
        a_full = jnp.concatenate(att, axis=0).astype(BF16)
        s_old = s_ref[h]
        o = (jnp.dot(a_full, v_h, preferred_element_type=F32)
             + jnp.dot(qs_b[:, kl], s_old.astype(BF16), preferred_element_type=F32))
        y_ref[:, vl] = (_rms(o, gn_ref[...]) * _silu(g_ref[:, vl].astype(F32))).astype(y_ref.dtype)
        dcol = jnp.concatenate([dec_t[kl, :]] * (hv // LANES), axis=1)
        s_ref[h] = s_old * dcol + lax.dot_general(kbuf.astype(BF16), v_h, TN_DIMS, preferred_element_type=F32)


def _gla(proj, small, s0, wl, layer, prev_s, *, bt, L, TB):
    zero_init = s0 is None
    aliased = prev_s is not None
    nc = L // TB
    T = bt * L
    lay3 = lambda b, c: (layer, 0, 0)
    row = lambda col: (lambda b, c: (b * nc + c, col))
    kd = GLA_HEADS * GLA_HEAD_K
    sshape = (DEPTH, bt, GLA_HEADS, GLA_HEAD_K, GLA_HEAD_V)
    sspec = pl.BlockSpec((None, None, GLA_HEADS, GLA_HEAD_K, GLA_HEAD_V), lambda b, c: (layer, b, 0, 0, 0))
    s_out_spec = sspec if aliased else pl.BlockSpec((DEPTH, None, GLA_HEADS, GLA_HEAD_K, GLA_HEAD_V),
                                                     lambda b, c: (0, b, 0, 0, 0))
    in_specs = [
        pl.BlockSpec((TB, kd), row(P_Q // kd)),
        pl.BlockSpec((TB, kd), row(P_K // kd)),
        pl.BlockSpec((TB, D_MODEL), row(P_V // D_MODEL)),
        pl.BlockSpec((TB, D_MODEL), row(P_G // D_MODEL)),
        pl.BlockSpec((TB, LANES), row(0)),
    ]
    args = [proj, proj, proj, proj, small]
    if not zero_init:
        in_specs.append(sspec)
        args.append(s0)
    in_specs += [
        pl.BlockSpec((None, LANES, kd), lay3),
        pl.BlockSpec((None, 1, kd), lay3),
        pl.BlockSpec((None, kd, LANES), lay3),
        pl.BlockSpec((None, kd, LANES), lay3),
        pl.BlockSpec((None, 1, GLA_HEAD_V), lay3),
    ]
    args += [wl["gla_wa"], wl["gla_ba"], wl["gla_wa_t"], wl["gla_ba_t"], wl["gla_norm"]]
    aliases = {}
    if aliased:
        aliases = {len(args): 1}
        in_specs.append(pl.BlockSpec(memory_space=pl.ANY))
        args.append(prev_s)
    return pl.pallas_call(
        functools.partial(_gla_kernel, TB=TB, zero_init=zero_init, aliased=aliased),
        grid=(bt, nc),
        in_specs=in_specs,
        out_specs=[pl.BlockSpec((TB, D_MODEL), row(0)), s_out_spec],
        out_shape=[jax.ShapeDtypeStruct((T, D_MODEL), BF16), _stacked_out(prev_s, sshape, F32)],
        input_output_aliases=aliases,
        compiler_params=pltpu.CompilerParams(dimension_semantics=("parallel", "arbitrary"),
                                             vmem_limit_bytes=48 << 20),
        name="gla",
    )(*args)


def _attn_kernel(q_ref, k_ref, v_ref, o_ref, kb_ref, vb_ref):
    @pl.when(pl.program_id(1) == 0)
    def _():
        kb_ref[...] = k_ref[...].astype(BF16)
        vb_ref[...] = v_ref[...].astype(BF16)

    hd = MEM_HEAD_DIM
    for h in range(MEM_HEADS):
        hl = slice(h * hd, (h + 1) * hd)
        s = lax.dot_general(q_ref[:, hl], kb_ref[h], NT_DIMS, preferred_element_type=F32) * (hd ** -0.5)
        e = jnp.exp(s - jnp.max(s, axis=-1, keepdims=True))
        p = e / jnp.sum(e, axis=-1, keepdims=True)
        o_ref[:, hl] = jnp.dot(p.astype(BF16), vb_ref[h], preferred_element_type=F32).astype(o_ref.dtype)


def _attn(q, mem_k, mem_v, layer, *, bt, L, tl):
    nl = L // tl
    T = bt * L
    kvshape = (MEM_HEADS, MEM_TOKENS, MEM_HEAD_DIM)
    kvspec = pl.BlockSpec((None, None) + kvshape, lambda b, l: (layer, b, 0, 0, 0))
    return pl.pallas_call(
        _attn_kernel,
        grid=(bt, nl),
        in_specs=[pl.BlockSpec((tl, D_MODEL), lambda b, l: (b * nl + l, 0)), kvspec, kvspec],
        out_specs=pl.BlockSpec((tl, D_MODEL), lambda b, l: (b * nl + l, 0)),
        out_shape=jax.ShapeDtypeStruct((T, D_MODEL), BF16),
        scratch_shapes=[pltpu.VMEM(kvshape, BF16), pltpu.VMEM(kvshape, BF16)],
        compiler_params=pltpu.CompilerParams(dimension_semantics=("parallel", "arbitrary"),
                                             vmem_limit_bytes=40 << 20),
        name="mem_attn",
    )(q, mem_k, mem_v)


def _prep_weights(w_in, ssd_conv_w, ssd_conv_b, ssd_dt_bias, ssd_a_log, ssd_d, ssd_norm, gla_wa2, gla_ba, gla_norm):
    w_in_t = jnp.swapaxes(w_in, 1, 2)
    o_dt, o_alr = 5120, 11296
    pad_sm = LANES - SSD_HEADS - GLA_RANK
    w_small_t = jnp.concatenate(
        [w_in_t[:, o_dt:o_dt + SSD_HEADS, :], w_in_t[:, o_alr:o_alr + GLA_RANK, :],
         jnp.zeros((DEPTH, pad_sm, D_MODEL), F32)], axis=1)
    pad_h = LANES - SSD_HEADS
    kd = GLA_HEADS * GLA_HEAD_K
    wa = jnp.concatenate([jnp.zeros((DEPTH, SM_ALR, kd), F32), gla_wa2,
                          jnp.zeros((DEPTH, LANES - SM_ALR - GLA_RANK, kd), F32)], axis=1).astype(BF16)
    expand = (jnp.arange(D_MODEL)[None, :] // SSD_HEADDIM == jnp.arange(LANES)[:, None]).astype(BF16)
    return dict(
        w_in_t=w_in_t, w_small_t=w_small_t,
        ssd_conv_w=ssd_conv_w, ssd_conv_b=ssd_conv_b[:, None, :],
        ssd_dtb=jnp.pad(ssd_dt_bias, ((0, 0), (0, pad_h)))[:, None, :],
        ssd_a=jnp.pad(-jnp.exp(ssd_a_log), ((0, 0), (0, pad_h)))[:, None, :],
        ssd_dx=jnp.repeat(ssd_d, SSD_HEADDIM, axis=1)[:, None, :],
        ssd_norm=ssd_norm[:, None, :], expand=expand,
        gla_wa=wa, gla_ba=gla_ba[:, None, :], gla_wa_t=jnp.swapaxes(wa, 1, 2),
        gla_ba_t=jnp.broadcast_to(gla_ba[:, :, None], (DEPTH, kd, LANES)),
        gla_norm=gla_norm[:, None, :])


def _group(x, bt, L, mem_k, mem_v, st_ssd=None, st_ssd_conv=None, st_gla=None, st_ffn_conv=None):
    return dict(x=x, bt=bt, L=L, mem_k=mem_k, mem_v=mem_v, st_ssd=st_ssd, st_ssd_conv=st_ssd_conv, st_gla=st_gla,
                st_ffn_conv=st_ffn_conv, cfg=_group_cfg(bt, L))


def _run_trunk(groups, wl, big, norm_final):
    main = groups[0]
    cfg = main["cfg"]
    tm = cfg["tm"]
    xs = [g["x"] for g in groups]
    acc = [dict(h=None, c=None, s=None, ffn=[]) for _ in groups]
    rest = lambda vals: vals[1] if len(vals) > 1 else None
    w_mq_b = _cast_bf16(big["w_mq"], tn=1024)
    for i in range(DEPTH):
        normed = [_norm_small(x, big["norm_mix"], wl["w_small_t"], i, tm=g["cfg"]["tm_norm"])
                  for x, g in zip(xs, groups)]
        xn, small = [n[0] for n in normed], [n[1] for n in normed]
        proj = _proj_in(xn[0], wl["w_in_t"], i, tm=cfg["tm_proj"], rider=rest(xn))
        y_ssd, y_gla = [], []
        for g, a, p, sm in zip(groups, acc, proj, small):
            y, a["h"], a["c"] = _ssd(p, sm, g["st_ssd_conv"], g["st_ssd"], wl, i, a["h"], a["c"],
                                     bt=g["bt"], L=g["L"], Q=g["cfg"]["ssd_q"])
            y_ssd.append(y)
            y, a["s"] = _gla(p, sm, g["st_gla"], wl, i, a["s"], bt=g["bt"], L=g["L"], TB=g["cfg"]["gla_tb"])
            y_gla.append(y)
        merged = _merge(y_ssd[0], y_gla[0], proj[0], big["w_branch"], i, tm=tm, tn=512,
                        rider=rest(list(zip(y_ssd, y_gla, proj))))
        xs = _mm_res(merged[0], big["w_out"], i, xs[0], tm=tm, tn=1024, rider=rest(list(zip(merged, xs))))
        o = []
        for g, x in zip(groups, xs):
            q = _norm_mm(x, big["norm_mem"], w_mq_b, i, None, tm=g["cfg"]["tm"], tn=1024, out_dtype=BF16)
            o.append(_attn(q, g["mem_k"], g["mem_v"], i, bt=g["bt"], L=g["L"], tl=g["cfg"]["attn_tl"]))
        xs = _mm_res(o[0], big["w_mo"], i, xs[0], tm=tm, tn=1024, rider=rest(list(zip(o, xs))))
        act = []
        for g, a, x in zip(groups, acc, xs):
            xf = _norm_cast(x, big["norm_ffn"], i, tm=g["cfg"]["tm_norm"])
            y, f_c = _ffn_up(xf, big["w_ffn_in"], big["ffn_conv_w"], big["ffn_conv_b"], g["st_ffn_conv"], i,
                             L=g["L"], tm=g["cfg"]["tm"], tn=512)
            act.append(y)
            a["ffn"].append(f_c)
        xs = _mm_res(act[0], big["w_ffn_out"], i, xs[0], tm=cfg["tm_norm"], tn=512, rider=rest(list(zip(act, xs))))
    return [(_final_norm(x, norm_final, tm=g["cfg"]["tm_norm"]), a["h"], a["c"], a["s"], jnp.stack(a["ffn"]))
            for g, a, x in zip(groups, acc, xs)]


def _group_cfg(bt, L):
    T = bt * L
    return dict(tm=min(T, 1024), tm_proj=min(T, 1024), tm_norm=min(T, 512), ssd_q=min(L, 128), gla_tb=min(L, 128), attn_tl=min(L, 512))


def kernel(x_prompt, x_sample, mem_prompt, state_ssd, state_ssd_conv, state_gla, state_ffn_conv, cache_mem_k, cache_mem_v, norm_mix, w_in, ssd_conv_w, ssd_conv_b, ssd_dt_bias, ssd_a_log, ssd_d, ssd_norm, gla_wa2, gla_ba, gla_norm, w_branch, w_out, norm_mem, w_mq, w_mk, w_mv, w_mo, norm_ffn, w_ffn_in, ffn_conv_w, ffn_conv_b, w_ffn_out, norm_final):
    wl = _prep_weights(w_in, ssd_conv_w, ssd_conv_b, ssd_dt_bias, ssd_a_log, ssd_d, ssd_norm, gla_wa2, gla_ba, gla_norm)
    big = dict(norm_mix=norm_mix, w_branch=w_branch, w_out=w_out, norm_mem=norm_mem, w_mq=w_mq, w_mo=w_mo,
               norm_ffn=norm_ffn, w_ffn_in=w_ffn_in, ffn_conv_w=ffn_conv_w, ffn_conv_b=ffn_conv_b, w_ffn_out=w_ffn_out)
    pb, pl_len, _ = x_prompt.shape
    sb, sl_len, _ = x_sample.shape
    mem2d = mem_prompt.reshape(pb * MEM_TOKENS, D_MODEL)
    p_mem_k = _mem_kv(mem2d, w_mk, nb=pb)
    p_mem_v = _mem_kv(mem2d, w_mv, nb=pb)
    head_major = (0, 1, 3, 2, 4)

    prompt = _group(x_prompt.reshape(pb * pl_len, D_MODEL), pb, pl_len, p_mem_k, p_mem_v)
    sample = _group(x_sample.reshape(sb * sl_len, D_MODEL), sb, sl_len,
                    jnp.transpose(cache_mem_k, head_major), jnp.transpose(cache_mem_v, head_major),
                    state_ssd, state_ssd_conv, state_gla, state_ffn_conv)
    (y_p, p_ssd, p_ssd_conv, p_gla, p_ffn), (y_s, s_ssd, s_ssd_conv, s_gla, s_ffn) = _run_trunk(
        [prompt, sample], wl, big, norm_final)
    return (y_p.reshape(pb, pl_len, D_MODEL), y_s.reshape(sb, sl_len, D_MODEL),
            p_ssd, p_ssd_conv, p_gla, p_ffn, jnp.transpose(p_mem_k, head_major), jnp.transpose(p_mem_v, head_major),
            s_ssd, s_ssd_conv, s_gla, s_ffn)
```

```python
import functools

import jax
import jax.numpy as jnp
from jax import lax
from jax.experimental import pallas as pl
from jax.experimental.pallas import tpu as pltpu

F32 = jnp.float32
BF16 = jnp.bfloat16

D_MODEL = 2048
DEPTH = 2
EPS = 1e-6
SSD_HEADS = 32
SSD_HEADDIM = 64
SSD_GROUPS = 4
SSD_STATE = 128
SSD_CONV = 4
SSD_CONV_DIM = 3072
GLA_HEADS = 4
GLA_HEAD_K = 256
GLA_HEAD_V = 512
GLA_RANK = 16
GLA_GATE_NORM = 16.0
GLA_CHUNK = 16
MEM_TOKENS = 256
MEM_HEADS = 4
MEM_HEAD_DIM = 512
D_FF = 5632
FFN_CONV = 3

LANES = 128
MXU_COLS = 256
SUBLANES = 8
VMEM_CAP_BYTES = 56 * 2**20

P_Z, P_XS, P_V, P_G, P_GATE, P_BC, P_Q, P_K, P_N = 0, 2048, 4096, 6144, 8192, 12288, 13312, 14336, 15360
SM_DT, SM_ALR = 0, 32

NT_DIMS = (((1,), (1,)), ((), ()))
TN_DIMS = (((0,), (0,)), ((), ()))
HI = lax.Precision.HIGHEST


def _vmem_limit(*block_bytes, scratch=0):
    need = 2 * sum(block_bytes) + scratch + (4 << 20)
    return int(min(max(need, 16 << 20), VMEM_CAP_BYTES))


def _nbytes(shape, dtype):
    n = 1
    for s in shape:
        n *= s
    return n * jnp.dtype(dtype).itemsize


def _silu(x):
    return x * jax.nn.sigmoid(x)


def _softplus(x):
    return jnp.maximum(x, 0.0) + jnp.log1p(jnp.exp(-jnp.abs(x)))


def _log_sigmoid(x):
    return jnp.minimum(x, 0.0) - jnp.log(1.0 + jnp.exp(-jnp.abs(x)))


def _rms(x, gain):
    ms = jnp.mean(x * x, axis=-1, keepdims=True)
    return x * lax.rsqrt(ms + EPS) * gain


def _split2(x):
    hi = x.astype(BF16)
    lo = (x - hi.astype(F32)).astype(BF16)
    return hi, lo


def _select_rows(sel, x):
    hi = x.astype(BF16)
    r1 = x - hi.astype(F32)
    mid = r1.astype(BF16)
    lo = (r1 - mid.astype(F32)).astype(BF16)
    return (jnp.dot(sel, hi, preferred_element_type=F32) + jnp.dot(sel, mid, preferred_element_type=F32)
            + jnp.dot(sel, lo, preferred_element_type=F32))


def _stacked_out(prev, shape, dtype):
    return jax.ShapeDtypeStruct(shape, dtype) if prev is None else jax.ShapeDtypeStruct(prev.shape, prev.dtype)


def _norm_mm_kernel(*refs, has_gain, has_small):
    it = iter(refs)
    x_ref = next(it)
    g_ref = next(it) if has_gain else None
    w_ref = next(it)
    ws_ref = next(it) if has_small else None
    o_ref = next(it)
    os_ref = next(it) if has_small else None
    xn_ref = next(it)

    @pl.when(pl.program_id(1) == 0)
    def _():
        x = x_ref[...].astype(F32)
        if has_gain:
            x = _rms(x, g_ref[...])
        xn_ref[...] = x.astype(BF16)
        if has_small:
            os_ref[...] = jnp.dot(xn_ref[...], ws_ref[...], preferred_element_type=F32)

    o_ref[...] = jnp.dot(xn_ref[...], w_ref[...].astype(BF16), preferred_element_type=F32).astype(o_ref.dtype)


def _norm_mm(x, gain, w, layer, w_small, *, tm, tn, out_dtype):
    m, k = x.shape
    n = w.shape[2]
    has_gain = gain is not None
    has_small = w_small is not None
    in_specs = [pl.BlockSpec((tm, k), lambda i, j: (i, 0))]
    args = [x]
    if has_gain:
        in_specs.append(pl.BlockSpec((None, 1, k), lambda i, j: (layer, 0, 0)))
        args.append(gain.reshape(DEPTH, 1, k))
    in_specs.append(pl.BlockSpec((None, k, tn), lambda i, j: (layer, 0, j)))
    args.append(w)
    out_shape = [jax.ShapeDtypeStruct((m, n), out_dtype)]
    out_specs = [pl.BlockSpec((tm, tn), lambda i, j: (i, j))]
    if has_small:
        in_specs.append(pl.BlockSpec((None, k, LANES), lambda i, j: (layer, 0, 0)))
        args.append(w_small)
        out_shape.append(jax.ShapeDtypeStruct((m, LANES), F32))
        out_specs.append(pl.BlockSpec((tm, LANES), lambda i, j: (i, 0)))
    limit = _vmem_limit(_nbytes((tm, k), x.dtype), _nbytes((k, tn), w.dtype), _nbytes((tm, tn), out_dtype),
                        _nbytes((k, LANES), BF16), _nbytes((tm, LANES), F32), scratch=_nbytes((tm, k), BF16))
    res = pl.pallas_call(
        functools.partial(_norm_mm_kernel, has_gain=has_gain, has_small=has_small),
        grid=(m // tm, n // tn),
        in_specs=in_specs,
        out_specs=out_specs,
        out_shape=out_shape,
        scratch_shapes=[pltpu.VMEM((tm, k), BF16)],
        compiler_params=pltpu.CompilerParams(dimension_semantics=("parallel", "arbitrary"),
                                             vmem_limit_bytes=limit),
        name="norm_mm",
    )(*args)
    return res if has_small else res[0]


def _mem_kv_kernel(x_ref, w_ref, o_ref, xb_ref):
    @pl.when((pl.program_id(0) == 0) & (pl.program_id(1) == 0))
    def _():
        xb_ref[...] = x_ref[...].astype(BF16)

    res = jnp.dot(xb_ref[...], w_ref[...].astype(BF16), preferred_element_type=F32)
    o_ref[...] = res.reshape(o_ref.shape)


def _mem_kv(x, w, *, nb):
    m, k = x.shape
    tn = MEM_HEAD_DIM
    limit = _vmem_limit(_nbytes((m, k), F32), _nbytes((k, tn), F32), _nbytes((m, tn), F32),
                        scratch=_nbytes((m, k), BF16))
    return pl.pallas_call(
        _mem_kv_kernel,
        grid=(DEPTH, MEM_HEADS),
        in_specs=[pl.BlockSpec((m, k), lambda d, h: (0, 0)),
                  pl.BlockSpec((None, k, tn), lambda d, h: (d, 0, h))],
        out_specs=pl.BlockSpec((None, nb, None, m // nb, tn), lambda d, h: (d, 0, h, 0, 0)),
        out_shape=jax.ShapeDtypeStruct((DEPTH, nb, MEM_HEADS, m // nb, tn), F32),
        scratch_shapes=[pltpu.VMEM((m, k), BF16)],
        compiler_params=pltpu.CompilerParams(dimension_semantics=("arbitrary", "arbitrary"),
                                             vmem_limit_bytes=limit),
        name="mem_kv",
    )(x, w)


def _cast_kernel(w_ref, o_ref):
    o_ref[...] = w_ref[...].astype(o_ref.dtype)


def _cast_bf16(w, *, tn):
    d, k, n = w.shape
    return pl.pallas_call(
        _cast_kernel,
        grid=(d, n // tn),
        in_specs=[pl.BlockSpec((None, k, tn), lambda l, j: (l, 0, j))],
        out_specs=pl.BlockSpec((None, k, tn), lambda l, j: (l, 0, j)),
        out_shape=jax.ShapeDtypeStruct(w.shape, BF16),
        compiler_params=pltpu.CompilerParams(dimension_semantics=("parallel", "parallel")),
        name="cast_bf16",
    )(w)


def _norm_cast_kernel(x_ref, g_ref, xn_ref):
    xn_ref[...] = _rms(x_ref[...], g_ref[...]).astype(BF16)


def _norm_cast(x, gain, layer, *, tm):
    m, k = x.shape
    return pl.pallas_call(
        _norm_cast_kernel,
        grid=(m // tm,),
        in_specs=[pl.BlockSpec((tm, k), lambda i: (i, 0)), pl.BlockSpec((None, 1, k), lambda i: (layer, 0, 0))],
        out_specs=pl.BlockSpec((tm, k), lambda i: (i, 0)),
        out_shape=jax.ShapeDtypeStruct((m, k), BF16),
        compiler_params=pltpu.CompilerParams(dimension_semantics=("parallel",)),
        name="norm_cast",
    )(x, gain.reshape(DEPTH, 1, k))


def _norm_small_kernel(x_ref, g_ref, wst_ref, xn_ref, os_ref):
    xn = _rms(x_ref[...], g_ref[...]).astype(BF16)
    xn_ref[...] = xn
    os_ref[...] = lax.dot_general(xn, wst_ref[...].astype(BF16), NT_DIMS, preferred_element_type=F32)


def _norm_small(x, gain, w_small_t, layer, *, tm):
    m, k = x.shape
    return pl.pallas_call(
        _norm_small_kernel,
        grid=(m // tm,),
        in_specs=[pl.BlockSpec((tm, k), lambda i: (i, 0)),
                  pl.BlockSpec((None, 1, k), lambda i: (layer, 0, 0)),
                  pl.BlockSpec((None, LANES, k), lambda i: (layer, 0, 0))],
        out_specs=[pl.BlockSpec((tm, k), lambda i: (i, 0)), pl.BlockSpec((tm, LANES), lambda i: (i, 0))],
        out_shape=[jax.ShapeDtypeStruct((m, k), BF16), jax.ShapeDtypeStruct((m, LANES), F32)],
        compiler_params=pltpu.CompilerParams(dimension_semantics=("parallel",)),
        name="norm_small",
    )(x, gain.reshape(DEPTH, 1, k), w_small_t)


PROJ_TN = 1024
PROJ_SRC = (0, 1024, 2048, 3072, 7200, 8224, 9248, 10272, 11312, 12336, 13360, 14384, 4096, 5152, 6176)


def _rider_steps(n_main, rider, main_fn, rider_fn):
    if not rider:
        main_fn()
        return
    i = pl.program_id(1)
    pl.when(i == 0)(rider_fn)
    pl.when(i > 0)(main_fn)


def _proj_kernel(*refs, n_main, rider):
    it = iter(refs)
    _src_ref, xn_ref = next(it), next(it)
    xn2_ref = next(it) if rider else None
    wt_ref, o_ref = next(it), next(it)
    o2_ref = next(it) if rider else None
    wbf_ref = next(it)

    @pl.when(pl.program_id(1) == 0)
    def _():
        wbf_ref[...] = wt_ref[0].astype(BF16)

    def mm(x_ref, out_ref):
        out_ref[...] = lax.dot_general(x_ref[...], wbf_ref[...], NT_DIMS,
                                       preferred_element_type=F32).astype(out_ref.dtype)

    _rider_steps(n_main, rider, lambda: mm(xn_ref, o_ref), lambda: mm(xn2_ref, o2_ref))


def _proj_in(xn, w_in_t, layer, *, tm, rider=None):
    m, k = xn.shape
    tn = PROJ_TN
    nj = len(PROJ_SRC)
    n_main = m // tm
    assert all(s % SUBLANES == 0 for s in PROJ_SRC)
    src = jnp.array([s // SUBLANES for s in PROJ_SRC], jnp.int32)
    im = (lambda i: i) if rider is None else (lambda i: jnp.maximum(i - 1, 0))
    in_specs = [pl.BlockSpec((tm, k), lambda j, i, s: (im(i), 0))]
    out_specs = [pl.BlockSpec((tm, tn), lambda j, i, s: (im(i), j))]
    out_shape = [jax.ShapeDtypeStruct((m, nj * tn), BF16)]
    args = [src, xn]
    m2 = 0
    if rider is not None:
        m2 = rider.shape[0]
        in_specs.append(pl.BlockSpec((m2, k), lambda j, i, s: (0, 0)))
        out_specs.append(pl.BlockSpec((m2, tn), lambda j, i, s: (0, j)))
        out_shape.append(jax.ShapeDtypeStruct((m2, nj * tn), BF16))
        args.append(rider)
    in_specs.append(pl.BlockSpec((pl.Element(1), pl.Element(tn), pl.Element(k)),
                                 lambda j, i, s: (layer, s[j] * SUBLANES, 0)))
    args.append(w_in_t)
    limit = _vmem_limit(_nbytes((tm + m2, k), BF16), _nbytes((tn, k), F32), _nbytes((tm + m2, tn), BF16),
                        scratch=_nbytes((tn, k), BF16))
    grid_spec = pltpu.PrefetchScalarGridSpec(
        num_scalar_prefetch=1,
        grid=(nj, n_main + (rider is not None)),
        in_specs=in_specs,
        out_specs=out_specs,
        scratch_shapes=[pltpu.VMEM((tn, k), BF16)])
    return pl.pallas_call(
        functools.partial(_proj_kernel, n_main=n_main, rider=rider is not None),
        grid_spec=grid_spec,
        out_shape=out_shape,
        compiler_params=pltpu.CompilerParams(dimension_semantics=("arbitrary", "arbitrary"),
                                             vmem_limit_bytes=limit),
        name="proj_in",
    )(*args)


def _mm_res_kernel(*refs, n_main, rider):
    it = iter(refs)
    x_ref, r_ref = next(it), next(it)
    x2_ref, r2_ref = (next(it), next(it)) if rider else (None, None)
    w_ref, o_ref = next(it), next(it)
    o2_ref = next(it) if rider else None
    wbf_ref = next(it)

    @pl.when(pl.program_id(1) == 0)
    def _():
        wbf_ref[...] = w_ref[...].astype(BF16)

    def mm(xr, rr, outr):
        outr[...] = rr[...] + jnp.dot(xr[...], wbf_ref[...], preferred_element_type=F32)

    _rider_steps(n_main, rider, lambda: mm(x_ref, r_ref, o_ref), lambda: mm(x2_ref, r2_ref, o2_ref))


def _mm_res(x, w, layer, res, *, tm, tn, rider=None):
    m, k = x.shape
    n = w.shape[2]
    n_main = m // tm
    im = (lambda i: i) if rider is None else (lambda i: jnp.maximum(i - 1, 0))
    in_specs = [pl.BlockSpec((tm, k), lambda j, i: (im(i), 0)), pl.BlockSpec((tm, tn), lambda j, i: (im(i), j))]
    out_specs = [pl.BlockSpec((tm, tn), lambda j, i: (im(i), j))]
    out_shape = [jax.ShapeDtypeStruct((m, n), F32)]
    args = [x, res]
    m2 = 0
    if rider is not None:
        m2 = rider[0].shape[0]
        in_specs += [pl.BlockSpec((m2, k), lambda j, i: (0, 0)), pl.BlockSpec((m2, tn), lambda j, i: (0, j))]
        out_specs.append(pl.BlockSpec((m2, tn), lambda j, i: (0, j)))
        out_shape.append(jax.ShapeDtypeStruct((m2, n), F32))
        args += list(rider)
    in_specs.append(pl.BlockSpec((None, k, tn), lambda j, i: (layer, 0, j)))
    args.append(w)
    limit = _vmem_limit(_nbytes((tm + m2, k), BF16), _nbytes((k, tn), F32), 2 * _nbytes((tm + m2, tn), F32),
                        scratch=_nbytes((k, tn), BF16))
    return pl.pallas_call(
        functools.partial(_mm_res_kernel, n_main=n_main, rider=rider is not None),
        grid=(n // tn, n_main + (rider is not None)),
        in_specs=in_specs,
        out_specs=out_specs,
        out_shape=out_shape,
        scratch_shapes=[pltpu.VMEM((k, tn), BF16)],
        compiler_params=pltpu.CompilerParams(dimension_semantics=("parallel", "arbitrary"),
                                             vmem_limit_bytes=limit),
        name="mm_res",
    )(*args)


def _merge_kernel(*refs, n_main, rider):
    it = iter(refs)
    main_in = [next(it) for _ in range(4)]
    rider_in = [next(it) for _ in range(4)] if rider else None
    w_ref, o_ref = next(it), next(it)
    o2_ref = next(it) if rider else None
    wbf_ref = next(it)

    @pl.when(pl.program_id(1) == 0)
    def _():
        wbf_ref[...] = w_ref[...].astype(BF16)

    def mm(ins, outr):
        y0_ref, y1_ref, g0_ref, g1_ref = ins
        b0 = jnp.dot(y0_ref[...], wbf_ref[0], preferred_element_type=F32)
        b1 = jnp.dot(y1_ref[...], wbf_ref[1], preferred_element_type=F32)
        g0 = jax.nn.sigmoid(g0_ref[...].astype(F32))
        g1 = jax.nn.sigmoid(g1_ref[...].astype(F32))
        outr[...] = (g0 * b0 + g1 * b1).astype(outr.dtype)

    _rider_steps(n_main, rider, lambda: mm(main_in, o_ref), lambda: mm(rider_in, o2_ref))


def _merge(y_ssd, y_gla, proj, w_branch, layer, *, tm, tn, rider=None):
    m, k = y_ssd.shape
    n = D_MODEL
    gb = P_GATE // tn
    nb = n // tn
    n_main = m // tm
    im = (lambda i: i) if rider is None else (lambda i: jnp.maximum(i - 1, 0))
    in_specs = [pl.BlockSpec((tm, k), lambda j, i: (im(i), 0)),
                pl.BlockSpec((tm, k), lambda j, i: (im(i), 0)),
                pl.BlockSpec((tm, tn), lambda j, i: (im(i), gb + j)),
                pl.BlockSpec((tm, tn), lambda j, i: (im(i), gb + nb + j))]
    out_specs = [pl.BlockSpec((tm, tn), lambda j, i: (im(i), j))]
    out_shape = [jax.ShapeDtypeStruct((m, n), BF16)]
    args = [y_ssd, y_gla, proj, proj]
    m2 = 0
    if rider is not None:
        y0_2, y1_2, proj2 = rider
        m2 = y0_2.shape[0]
        in_specs += [pl.BlockSpec((m2, k), lambda j, i: (0, 0)),
                     pl.BlockSpec((m2, k), lambda j, i: (0, 0)),
                     pl.BlockSpec((m2, tn), lambda j, i: (0, gb + j)),
                     pl.BlockSpec((m2, tn), lambda j, i: (0, gb + nb + j))]
        out_specs.append(pl.BlockSpec((m2, tn), lambda j, i: (0, j)))
        out_shape.append(jax.ShapeDtypeStruct((m2, n), BF16))
        args += [y0_2, y1_2, proj2, proj2]
    in_specs.append(pl.BlockSpec((None, 2, k, tn), lambda j, i: (layer, 0, 0, j)))
    args.append(w_branch)
    limit = _vmem_limit(2 * _nbytes((tm + m2, k), BF16), 2 * _nbytes((k, tn), F32),
                        3 * _nbytes((tm + m2, tn), BF16), scratch=2 * _nbytes((k, tn), BF16))
    return pl.pallas_call(
        functools.partial(_merge_kernel, n_main=n_main, rider=rider is not None),
        grid=(nb, n_main + (rider is not None)),
        in_specs=in_specs,
        out_specs=out_specs,
        out_shape=out_shape,
        scratch_shapes=[pltpu.VMEM((2, k, tn), BF16)],
        compiler_params=pltpu.CompilerParams(dimension_semantics=("parallel", "arbitrary"),
                                             vmem_limit_bytes=limit),
        name="merge",
    )(*args)


def _final_norm_kernel(x_ref, g_ref, o_ref):
    o_ref[...] = _rms(x_ref[...], g_ref[...])


def _final_norm(x, gain, *, tm):
    m, k = x.shape
    return pl.pallas_call(
        _final_norm_kernel,
        grid=(m // tm,),
        in_specs=[pl.BlockSpec((tm, k), lambda i: (i, 0)), pl.BlockSpec((1, k), lambda i: (0, 0))],
        out_specs=pl.BlockSpec((tm, k), lambda i: (i, 0)),
        out_shape=jax.ShapeDtypeStruct((m, k), F32),
        compiler_params=pltpu.CompilerParams(dimension_semantics=("parallel",)),
        name="final_norm",
    )(x, gain.reshape(1, k))


def _ffn_up_kernel(*refs, tm, tn, seq_tiles, seg):
    short = seg > 0
    it = iter(refs)
    xn_ref, wu_ref, wt_ref, cwu_ref, cwt_ref, cbu_ref, cbt_ref = (next(it) for _ in range(7))
    stu_ref = next(it) if short else None
    stt_ref = next(it) if short else None
    act_ref, cnu_ref, cnt_ref = next(it), next(it), next(it)
    wbu_ref, wbt_ref = next(it), next(it)
    tailu_ref = None if short else next(it)
    tailt_ref = None if short else next(it)
    i = pl.program_id(1)
    sl = SUBLANES
    hw = FFN_CONV - 1

    @pl.when(i == 0)
    def _():
        wbu_ref[...] = wu_ref[...].astype(BF16)
        wbt_ref[...] = wt_ref[...].astype(BF16)
        if not short:
            tailu_ref[...] = jnp.zeros_like(tailu_ref)
            tailt_ref[...] = jnp.zeros_like(tailt_ref)

    xn = xn_ref[...]
    u_raw = jnp.dot(xn, wbu_ref[...], preferred_element_type=F32)
    t_raw = jnp.dot(xn, wbt_ref[...], preferred_element_type=F32)

    if short:
        nseq = tm // seg
        r = lax.broadcasted_iota(jnp.int32, (tm, hw * nseq), 0)
        c = lax.broadcasted_iota(jnp.int32, (tm, hw * nseq), 1)
        halo_u, halo_t = [], []
        for s in range(1, FFN_CONV):
            sel = (((r % seg) < s) & (c == hw * (r // seg) + hw - s + (r % seg))).astype(BF16)
            halo_u.append(_select_rows(sel, stu_ref[...]))
            halo_t.append(_select_rows(sel, stt_ref[...]))
        ro = lax.broadcasted_iota(jnp.int32, (hw * nseq, tm), 0)
        co = lax.broadcasted_iota(jnp.int32, (hw * nseq, tm), 1)
        sel_out = (co == (ro // hw) * seg + seg - hw + (ro % hw)).astype(BF16)
        cnu_ref[...] = _select_rows(sel_out, u_raw)
        cnt_ref[...] = _select_rows(sel_out, t_raw)
        rowmod = lax.broadcasted_iota(jnp.int32, (tm, tn), 0) % seg
        prev_u = prev_t = None
    else:
        start = (i % seq_tiles) == 0
        prev_u = jnp.where(start, 0.0, tailu_ref[...])
        prev_t = jnp.where(start, 0.0, tailt_ref[...])
        tailu_ref[...] = u_raw[tm - sl:tm, :]
        tailt_ref[...] = t_raw[tm - sl:tm, :]
        cnu_ref[...] = u_raw[tm - sl:tm, :]
        cnt_ref[...] = t_raw[tm - sl:tm, :]
        row8 = lax.broadcasted_iota(jnp.int32, (sl, tn), 0)
        halo_u = halo_t = rowmod = None

    def conv(x, prev8, cw_ref, cb_ref, halos):
        y = x * cw_ref[hw:hw + 1, :]
        for s in range(1, FFN_CONV):
            rolled = pltpu.roll(x, s, axis=0)
            if short:
                shifted = jnp.where(rowmod < s, halos[s - 1], rolled)
            else:
                head = jnp.where(row8 < s, pltpu.roll(prev8, s, axis=0), rolled[0:sl, :])
                shifted = jnp.concatenate([head, rolled[sl:, :]], axis=0)
            y = y + shifted * cw_ref[hw - s:hw - s + 1, :]
        return y + cb_ref[...]

    u = conv(u_raw, prev_u, cwu_ref, cbu_ref, halo_u)
    t = conv(t_raw, prev_t, cwt_ref, cbt_ref, halo_t)
    act_ref[...] = (_silu(t) * u).astype(act_ref.dtype)


def _ffn_up(xn, w, conv_w, conv_b, conv0, layer, *, L, tm, tn):
    m, k = xn.shape
    bt = m // L
    nj = D_FF // tn
    hw = FFN_CONV - 1
    short = L < tm
    seg = L if short else 0
    seq_tiles = 1 if short else L // tm
    assert (conv0 is not None) == short, "history rows are only supported for sequences shorter than a block"
    in_specs = [
        pl.BlockSpec((tm, k), lambda j, i: (i, 0)),
        pl.BlockSpec((None, k, tn), lambda j, i: (layer, 0, j)),
        pl.BlockSpec((None, k, tn), lambda j, i: (layer, 0, nj + j)),
        pl.BlockSpec((None, FFN_CONV, tn), lambda j, i: (layer, 0, j)),
        pl.BlockSpec((None, FFN_CONV, tn), lambda j, i: (layer, 0, nj + j)),
        pl.BlockSpec((None, 1, tn), lambda j, i: (layer, 0, j)),
        pl.BlockSpec((None, 1, tn), lambda j, i: (layer, 0, nj + j)),
    ]
    args = [xn, w, w, conv_w, conv_w, conv_b.reshape(DEPTH, 1, 2 * D_FF), conv_b.reshape(DEPTH, 1, 2 * D_FF)]
    scratch = [pltpu.VMEM((k, tn), BF16)] * 2
    if short:
        nst = (m // tm) * (tm // L) * hw
        st2d = conv0.reshape(DEPTH, nst, 2 * D_FF)
        rows = (tm // L) * hw
        in_specs += [pl.BlockSpec((None, rows, tn), lambda j, i: (layer, i, j)),
                     pl.BlockSpec((None, rows, tn), lambda j, i: (layer, i, nj + j))]
        args += [st2d, st2d]
        cn_shape = jax.ShapeDtypeStruct((nst, D_FF), F32)
        cn_spec = pl.BlockSpec((rows, tn), lambda j, i: (i, j))
    else:
        scratch += [pltpu.VMEM((SUBLANES, tn), F32)] * 2
        cn_shape = jax.ShapeDtypeStruct((m // tm, SUBLANES, D_FF), F32)
        cn_spec = pl.BlockSpec((None, SUBLANES, tn), lambda j, i: (i, 0, j))
    limit = _vmem_limit(_nbytes((tm, k), BF16), 2 * _nbytes((k, tn), F32), _nbytes((tm, tn), BF16),
                        scratch=2 * _nbytes((k, tn), BF16) + 6 * _nbytes((tm, tn), F32))
    act, cnu, cnt = pl.pallas_call(
        functools.partial(_ffn_up_kernel, tm=tm, tn=tn, seq_tiles=seq_tiles, seg=seg),
        grid=(nj, m // tm),
        in_specs=in_specs,
        out_specs=[pl.BlockSpec((tm, tn), lambda j, i: (i, j)), cn_spec, cn_spec],
        out_shape=[jax.ShapeDtypeStruct((m, D_FF), BF16), cn_shape, cn_shape],
        scratch_shapes=scratch,
        compiler_params=pltpu.CompilerParams(dimension_semantics=("arbitrary", "arbitrary"),
                                             vmem_limit_bytes=limit),
        name="ffn_up",
    )(*args)
    if short:
        conv_new = jnp.concatenate([cnu, cnt], axis=-1).reshape(bt, hw, 2 * D_FF)
    else:
        last = slice(seq_tiles - 1, None, seq_tiles)
        conv_new = jnp.concatenate([cnu[last, SUBLANES - hw:, :], cnt[last, SUBLANES - hw:, :]], axis=-1)
    return act, conv_new


def _causal_conv(x, tail_ref, w_ref, b, width):
    sl = SUBLANES
    row = lax.broadcasted_iota(jnp.int32, (sl, x.shape[1]), 0)
    prev = tail_ref[...]
    y = x * w_ref[width - 1:width, :]
    for s in range(1, width):
        rolled = pltpu.roll(x, s, axis=0)
        head = jnp.where(row < s, pltpu.roll(prev, s, axis=0), rolled[0:sl, :])
        shifted = jnp.concatenate([head, rolled[sl:, :]], axis=0)
        y = y + shifted * w_ref[width - 1 - s:width - s, :]
    return y + b


def _ssd_kernel(*refs, Q, zero_init, aliased):
    it = iter(refs)
    z_ref, xs_ref, bc_ref, sm_ref = (next(it) for _ in range(4))
    conv0_ref = None if zero_init else next(it)
    h0_ref = None if zero_init else next(it)
    cwx_ref, cwb_ref, cbx_ref, cbb_ref, dtb_ref, a_ref, dx_ref, nrm_ref, exp_ref = (next(it) for _ in range(9))
    if aliased:
        next(it), next(it)
    y_ref, h_out, conv_out, tailx_ref, tailb_ref = (next(it) for _ in range(5))
    h_ref = h_out if aliased else h_out.at[0]
    convout_ref = conv_out if aliased else conv_out.at[0]
    c = pl.program_id(1)
    hp = SSD_HEADDIM
    gw = SSD_HEADS // SSD_GROUPS * hp

    @pl.when(c == 0)
    def _():
        if not aliased:
            h_out[1:] = jnp.zeros((DEPTH - 1,) + h_ref.shape, F32)
            conv_out[1:] = jnp.zeros((DEPTH - 1,) + convout_ref.shape, F32)
        tailx_ref[...] = jnp.zeros_like(tailx_ref)
        tailb_ref[...] = jnp.zeros_like(tailb_ref)
        if zero_init:
            h_ref[...] = jnp.zeros_like(h_ref)
        else:
            h_ref[...] = h0_ref[...]
            tailx_ref[5:8, :] = conv0_ref[:, 0:D_MODEL]
            tailb_ref[5:8, :] = conv0_ref[:, D_MODEL:SSD_CONV_DIM]

    xs_raw = xs_ref[...].astype(F32)
    bc_raw = bc_ref[...].astype(F32)
    xs = _silu(_causal_conv(xs_raw, tailx_ref, cwx_ref, cbx_ref[...], SSD_CONV))
    bc = _silu(_causal_conv(bc_raw, tailb_ref, cwb_ref, cbb_ref[...], SSD_CONV))
    tailx_ref[...] = xs_raw[Q - 8:Q, :]
    tailb_ref[...] = bc_raw[Q - 8:Q, :]

    @pl.when(c == pl.num_programs(1) - 1)
    def _():
        convout_ref[:, 0:D_MODEL] = tailx_ref[5:8, :]
        convout_ref[:, D_MODEL:SSD_CONV_DIM] = tailb_ref[5:8, :]

    dt = _softplus(sm_ref[...] + dtb_ref[...])
    adt = dt * a_ref[...]
    ri = lax.broadcasted_iota(jnp.int32, (Q, Q), 0)
    ci = lax.broadcasted_iota(jnp.int32, (Q, Q), 1)
    causal = ci <= ri
    acum = jnp.dot(causal.astype(F32), adt, precision=HI, preferred_element_type=F32)
    eye = (lax.broadcasted_iota(jnp.int32, (LANES, LANES), 0)
           == lax.broadcasted_iota(jnp.int32, (LANES, LANES), 1)).astype(F32)
    acum_t = lax.dot_general(eye, acum, NT_DIMS, precision=HI, preferred_element_type=F32)
    dt_t = lax.dot_general(eye, dt, NT_DIMS, precision=HI, preferred_element_type=F32)
    a_last = acum[Q - 1:Q, :]
    ea = jnp.exp(acum)
    te = jnp.exp(a_last - acum) * dt
    dec_rows = jnp.broadcast_to(jnp.exp(acum_t[:, Q - 1:Q]), (LANES, LANES))
    ea_hi, ea_lo = _split2(ea)
    te_hi, te_lo = _split2(te)
    expand = exp_ref[...]
    ea_x = (jnp.dot(ea_hi, expand, preferred_element_type=F32)
            + jnp.dot(ea_lo, expand, preferred_element_type=F32))
    te_x = (jnp.dot(te_hi, expand, preferred_element_type=F32)
            + jnp.dot(te_lo, expand, preferred_element_type=F32))
    lane = lax.broadcasted_iota(jnp.int32, (Q, LANES), 1)

    for g in range(SSD_GROUPS):
        gl = slice(g * gw, (g + 1) * gw)
        b_g = bc[:, g * SSD_STATE:(g + 1) * SSD_STATE].astype(BF16)
        c_g = bc[:, (SSD_GROUPS + g) * SSD_STATE:(SSD_GROUPS + g + 1) * SSD_STATE].astype(BF16)
        cb = lax.dot_general(c_g, b_g, NT_DIMS, preferred_element_type=F32)
        h_g = h_ref[8 * g:8 * g + 8].reshape(gw, SSD_STATE)
        y_off = lax.dot_general(c_g, h_g.astype(BF16), NT_DIMS, preferred_element_type=F32)
        x_g = xs[:, gl]
        pairs = []
        for p in range(4):
            x_p = x_g[:, p * LANES:(p + 1) * LANES]
            acc = None
            for s in range(2):
                hh = g * 8 + p * 2 + s
                seg = jnp.broadcast_to(acum[:, hh:hh + 1], (Q, Q)) - jnp.broadcast_to(acum_t[hh:hh + 1, :], (Q, Q))
                decay = jnp.where(causal, jnp.exp(seg), 0.0)
                w_h = (cb * decay * dt_t[hh:hh + 1, :]).astype(BF16)
                x_m = jnp.where((lane // hp) == s, x_p, 0.0).astype(BF16)
                r = jnp.dot(w_h, x_m, preferred_element_type=F32)
                acc = r if acc is None else acc + r
            pairs.append(acc)
        y_g = jnp.concatenate(pairs, axis=1) + y_off * ea_x[:, gl] + dx_ref[:, gl] * x_g
        x_t = (x_g * te_x[:, gl]).astype(BF16)
        upd = lax.dot_general(x_t, b_g, TN_DIMS, preferred_element_type=F32)
        for h in range(8):
            hh = g * 8 + h
            h_ref[hh] = h_ref[hh] * dec_rows[hh:hh + 1, :] + upd[h * hp:(h + 1) * hp, :]
        z_g = z_ref[:, gl].astype(F32)
        y_ref[:, gl] = _rms(y_g * _silu(z_g), nrm_ref[:, gl]).astype(y_ref.dtype)


def _ssd(proj, small, conv0, h0, wl, layer, prev_h, prev_conv, *, bt, L, Q):
    zero_init = h0 is None
    aliased = prev_h is not None
    nc = L // Q
    T = bt * L
    full2 = lambda b, c: (0, 0)
    lay3 = lambda b, c: (layer, 0, 0)
    row = lambda col: (lambda b, c: (b * nc + c, col))
    hshape = (DEPTH, bt, SSD_HEADS, SSD_HEADDIM, SSD_STATE)
    cshape = (DEPTH, bt, SSD_CONV - 1, SSD_CONV_DIM)
    hspec = pl.BlockSpec((None, None, SSD_HEADS, SSD_HEADDIM, SSD_STATE), lambda b, c: (layer, b, 0, 0, 0))
    cspec = pl.BlockSpec((None, None, SSD_CONV - 1, SSD_CONV_DIM), lambda b, c: (layer, b, 0, 0))
    h_out_spec = hspec if aliased else pl.BlockSpec((DEPTH, None, SSD_HEADS, SSD_HEADDIM, SSD_STATE),
                                                    lambda b, c: (0, b, 0, 0, 0))
    c_out_spec = cspec if aliased else pl.BlockSpec((DEPTH, None, SSD_CONV - 1, SSD_CONV_DIM),
                                                    lambda b, c: (0, b, 0, 0))
    in_specs = [
        pl.BlockSpec((Q, D_MODEL), row(P_Z // D_MODEL)),
        pl.BlockSpec((Q, D_MODEL), row(P_XS // D_MODEL)),
        pl.BlockSpec((Q, 1024), row(P_BC // 1024)),
        pl.BlockSpec((Q, LANES), row(0)),
    ]
    args = [proj, proj, proj, small]
    if not zero_init:
        in_specs += [cspec, hspec]
        args += [conv0, h0]
    in_specs += [
        pl.BlockSpec((None, SSD_CONV, D_MODEL), lay3),
        pl.BlockSpec((None, SSD_CONV, 1024), lambda b, c: (layer, 0, D_MODEL // 1024)),
        pl.BlockSpec((None, 1, D_MODEL), lay3),
        pl.BlockSpec((None, 1, 1024), lambda b, c: (layer, 0, D_MODEL // 1024)),
        pl.BlockSpec((None, 1, LANES), lay3),
        pl.BlockSpec((None, 1, LANES), lay3),
        pl.BlockSpec((None, 1, D_MODEL), lay3),
        pl.BlockSpec((None, 1, D_MODEL), lay3),
        pl.BlockSpec((LANES, D_MODEL), full2),
    ]
    args += [wl["ssd_conv_w"], wl["ssd_conv_w"], wl["ssd_conv_b"], wl["ssd_conv_b"], wl["ssd_dtb"], wl["ssd_a"],
             wl["ssd_dx"], wl["ssd_norm"], wl["expand"]]
    aliases = {}
    if aliased:
        aliases = {len(args): 1, len(args) + 1: 2}
        in_specs += [pl.BlockSpec(memory_space=pl.ANY), pl.BlockSpec(memory_space=pl.ANY)]
        args += [prev_h, prev_conv]
    return pl.pallas_call(
        functools.partial(_ssd_kernel, Q=Q, zero_init=zero_init, aliased=aliased),
        grid=(bt, nc),
        in_specs=in_specs,
        out_specs=[pl.BlockSpec((Q, D_MODEL), row(0)), h_out_spec, c_out_spec],
        out_shape=[jax.ShapeDtypeStruct((T, D_MODEL), BF16), _stacked_out(prev_h, hshape, F32),
                   _stacked_out(prev_conv, cshape, F32)],
        scratch_shapes=[pltpu.VMEM((8, D_MODEL), F32), pltpu.VMEM((8, 1024), F32)],
        input_output_aliases=aliases,
        compiler_params=pltpu.CompilerParams(dimension_semantics=("parallel", "arbitrary"),
                                             vmem_limit_bytes=48 << 20),
        name="ssd",
    )(*args)


def _gla_kernel(*refs, TB, zero_init, aliased):
    it = iter(refs)
    q_ref, k_ref, v_ref, g_ref, sm_ref = (next(it) for _ in range(5))
    s0_ref = None if zero_init else next(it)
    wa_ref, ba_ref, gn_ref = (next(it) for _ in range(3))
    if aliased:
        next(it)
    y_ref, s_out = next(it), next(it)
    s_ref = s_out if aliased else s_out.at[0]
    c = pl.program_id(1)
    ck = GLA_CHUNK
    ns = TB // ck
    hk, hv = GLA_HEAD_K, GLA_HEAD_V

    @pl.when(c == 0)
    def _():
        if not aliased:
            s_out[1:] = jnp.zeros((DEPTH - 1,) + s_ref.shape, F32)
        if zero_init:
            s_ref[...] = jnp.zeros_like(s_ref)
        else:
            s_ref[...] = s0_ref[...]

    smb = sm_ref[...].astype(BF16)
    la = _log_sigmoid(jnp.dot(smb, wa_ref[...], preferred_element_type=F32) + ba_ref[...]) * (1.0 / GLA_GATE_NORM)
    rb = lax.broadcasted_iota(jnp.int32, (TB, TB), 0)
    cb = lax.broadcasted_iota(jnp.int32, (TB, TB), 1)
    same = (rb // ck) == (cb // ck)
    msel = jnp.concatenate([same & (cb <= rb), same, (cb // ck) < (rb // ck)], axis=0).astype(BF16)
    sums = _select_rows(msel, la)
    bcum, tot, bprev = sums[0:TB], sums[TB:2 * TB], sums[2 * TB:3 * TB]
    kf = k_ref[...].astype(F32)
    qd = q_ref[...].astype(F32) * (hk ** -0.5) * jnp.exp(bcum)
    qd_b = qd.astype(BF16)
    qs_b = (qd * jnp.exp(bprev)).astype(BF16)
    ki = kf * jnp.exp(-bcum)
    ke = kf * jnp.exp(tot - bcum)
    dtot = jnp.exp(tot)
    ones = jnp.ones((TB, LANES), BF16)
    la_hi = la.astype(BF16)
    la_r = la - la_hi.astype(F32)
    la_mid = la_r.astype(BF16)
    la_lo = (la_r - la_mid.astype(F32)).astype(BF16)
    dec_t = jnp.exp(lax.dot_general(la_hi, ones, TN_DIMS, preferred_element_type=F32)
                    + lax.dot_general(la_mid, ones, TN_DIMS, preferred_element_type=F32)
                    + lax.dot_general(la_lo, ones, TN_DIMS, preferred_element_type=F32))
    rchunk = lax.broadcasted_iota(jnp.int32, (TB, hk), 0) // ck
    row16 = lax.broadcasted_iota(jnp.int32, (ck, TB), 0)
    col16 = lax.broadcasted_iota(jnp.int32, (ck, TB), 1)

    for h in range(GLA_HEADS):
        kl = slice(h * hk, (h + 1) * hk)
        vl = slice(h * hv, (h + 1) * hv)
        ke_h, ki_h = ke[:, kl], ki[:, kl]
        v_h = v_ref[:, vl]
        kbuf = jnp.zeros((TB, hk), F32)
        att = []
        for cc in range(ns):
            in_c = rchunk == cc
            kall = jnp.where(in_c, ki_h, kbuf).astype(BF16)
            a = lax.dot_general(qd_b[cc * ck:(cc + 1) * ck, kl], kall, NT_DIMS, preferred_element_type=F32)
            att.append(jnp.where(col16 <= row16 + cc * ck, a, 0.0))
            kbuf = jnp.where(in_c, ke_h, kbuf * dtot[cc * ck:cc * ck + 1, kl])
        a_full = jnp.concatenate(att, axis=0).astype(BF16)
        s_old = s_ref[h]
        o = (jnp.dot(a_full, v_h, preferred_element_type=F32)
             + jnp.dot(qs_b[:, kl], s_old.astype(BF16), preferred_element_type=F32))
        y_ref[:, vl] = (_rms(o, gn_ref[...]) * _silu(g_ref[:, vl].astype(F32))).astype(y_ref.dtype)
        dcol = jnp.concatenate([dec_t[kl, :]] * (hv // LANES), axis=1)
        s_ref[h] = s_old * dcol + lax.dot_general(kbuf.astype(BF16), v_h, TN_DIMS, preferred_element_type=F32)


def _gla(proj, small, s0, wl, layer, prev_s, *, bt, L, TB):
    zero_init = s0 is None
    aliased = prev_s is not None
    nc = L // TB
    T = bt * L
    lay3 = lambda b, c: (layer, 0, 0)
    row = lambda col: (lambda b, c: (b * nc + c, col))
    kd = GLA_HEADS * GLA_HEAD_K
    sshape = (DEPTH, bt, GLA_HEADS, GLA_HEAD_K, GLA_HEAD_V)
    sspec = pl.BlockSpec((None, None, GLA_HEADS, GLA_HEAD_K, GLA_HEAD_V), lambda b, c: (layer, b, 0, 0, 0))
    s_out_spec = sspec if aliased else pl.BlockSpec((DEPTH, None, GLA_HEADS, GLA_HEAD_K, GLA_HEAD_V),
                                                     lambda b, c: (0, b, 0, 0, 0))
    in_specs = [
        pl.BlockSpec((TB, kd), row(P_Q // kd)),
        pl.BlockSpec((TB, kd), row(P_K // kd)),
        pl.BlockSpec((TB, D_MODEL), row(P_V // D_MODEL)),
        pl.BlockSpec((TB, D_MODEL), row(P_G // D_MODEL)),
        pl.BlockSpec((TB, LANES), row(0)),
    ]
    args = [proj, proj, proj, proj, small]
    if not zero_init:
        in_specs.append(sspec)
        args.append(s0)
    in_specs += [
        pl.BlockSpec((None, LANES, kd), lay3),
        pl.BlockSpec((None, 1, kd), lay3),
        pl.BlockSpec((None, 1, GLA_HEAD_V), lay3),
    ]
    args += [wl["gla_wa"], wl["gla_ba"], wl["gla_norm"]]
    aliases = {}
    if aliased:
        aliases = {len(args): 1}
        in_specs.append(pl.BlockSpec(memory_space=pl.ANY))
        args.append(prev_s)
    return pl.pallas_call(
        functools.partial(_gla_kernel, TB=TB, zero_init=zero_init, aliased=aliased),
        grid=(bt, nc),
        in_specs=in_specs,
        out_specs=[pl.BlockSpec((TB, D_MODEL), row(0)), s_out_spec],
        out_shape=[jax.ShapeDtypeStruct((T, D_MODEL), BF16), _stacked_out(prev_s, sshape, F32)],
        input_output_aliases=aliases,
        compiler_params=pltpu.CompilerParams(dimension_semantics=("parallel", "arbitrary"),
                                             vmem_limit_bytes=48 << 20),
        name="gla",
    )(*args)


def _attn_kernel(q_ref, k_ref, v_ref, o_ref, kb_ref, vb_ref):
    @pl.when(pl.program_id(1) == 0)
    def _():
        kb_ref[...] = k_ref[...].astype(BF16)
        vb_ref[...] = v_ref[...].astype(BF16)

    hd = MEM_HEAD_DIM
    for h in range(MEM_HEADS):
        hl = slice(h * hd, (h + 1) * hd)
        s = lax.dot_general(q_ref[:, hl], kb_ref[h], NT_DIMS, preferred_element_type=F32) * (hd ** -0.5)
        e = jnp.exp(s - jnp.max(s, axis=-1, keepdims=True))
        p = e / jnp.sum(e, axis=-1, keepdims=True)
        o_ref[:, hl] = jnp.dot(p.astype(BF16), vb_ref[h], preferred_element_type=F32).astype(o_ref.dtype)


def _attn(q, mem_k, mem_v, layer, *, bt, L, tl):
    nl = L // tl
    T = bt * L
    kvshape = (MEM_HEADS, MEM_TOKENS, MEM_HEAD_DIM)
    kvspec = pl.BlockSpec((None, None) + kvshape, lambda b, l: (layer, b, 0, 0, 0))
    return pl.pallas_call(
        _attn_kernel,
        grid=(bt, nl),
        in_specs=[pl.BlockSpec((tl, D_MODEL), lambda b, l: (b * nl + l, 0)), kvspec, kvspec],
        out_specs=pl.BlockSpec((tl, D_MODEL), lambda b, l: (b * nl + l, 0)),
        out_shape=jax.ShapeDtypeStruct((T, D_MODEL), BF16),
        scratch_shapes=[pltpu.VMEM(kvshape, BF16), pltpu.VMEM(kvshape, BF16)],
        compiler_params=pltpu.CompilerParams(dimension_semantics=("parallel", "arbitrary"),
                                             vmem_limit_bytes=40 << 20),
        name="mem_attn",
    )(q, mem_k, mem_v)


def _prep_weights(w_in, ssd_conv_w, ssd_conv_b, ssd_dt_bias, ssd_a_log, ssd_d, ssd_norm, gla_wa2, gla_ba, gla_norm):
    w_in_t = jnp.swapaxes(w_in, 1, 2)
    o_dt, o_alr = 5120, 11296
    pad_sm = LANES - SSD_HEADS - GLA_RANK
    w_small_t = jnp.concatenate(
        [w_in_t[:, o_dt:o_dt + SSD_HEADS, :], w_in_t[:, o_alr:o_alr + GLA_RANK, :],
         jnp.zeros((DEPTH, pad_sm, D_MODEL), F32)], axis=1)
    pad_h = LANES - SSD_HEADS
    kd = GLA_HEADS * GLA_HEAD_K
    wa = jnp.concatenate([jnp.zeros((DEPTH, SM_ALR, kd), F32), gla_wa2,
                          jnp.zeros((DEPTH, LANES - SM_ALR - GLA_RANK, kd), F32)], axis=1).astype(BF16)
    expand = (jnp.arange(D_MODEL)[None, :] // SSD_HEADDIM == jnp.arange(LANES)[:, None]).astype(BF16)
    return dict(
        w_in_t=w_in_t, w_small_t=w_small_t,
        ssd_conv_w=ssd_conv_w, ssd_conv_b=ssd_conv_b[:, None, :],
        ssd_dtb=jnp.pad(ssd_dt_bias, ((0, 0), (0, pad_h)))[:, None, :],
        ssd_a=jnp.pad(-jnp.exp(ssd_a_log), ((0, 0), (0, pad_h)))[:, None, :],
        ssd_dx=jnp.repeat(ssd_d, SSD_HEADDIM, axis=1)[:, None, :],
        ssd_norm=ssd_norm[:, None, :], expand=expand,
        gla_wa=wa, gla_ba=gla_ba[:, None, :],
        gla_norm=gla_norm[:, None, :])


def _group(x, bt, L, mem_k, mem_v, st_ssd=None, st_ssd_conv=None, st_gla=None, st_ffn_conv=None):
    return dict(x=x, bt=bt, L=L, mem_k=mem_k, mem_v=mem_v, st_ssd=st_ssd, st_ssd_conv=st_ssd_conv, st_gla=st_gla,
                st_ffn_conv=st_ffn_conv, cfg=_group_cfg(bt, L))


def _run_trunk(groups, wl, big, norm_final):
    main = groups[0]
    cfg = main["cfg"]
    tm = cfg["tm"]
    xs = [g["x"] for g in groups]
    acc = [dict(h=None, c=None, s=None, ffn=[]) for _ in groups]
    rest = lambda vals: vals[1] if len(vals) > 1 else None
    w_mq_b = _cast_bf16(big["w_mq"], tn=1024)
    for i in range(DEPTH):
        normed = [_norm_small(x, big["norm_mix"], wl["w_small_t"], i, tm=g["cfg"]["tm_norm"])
                  for x, g in zip(xs, groups)]
        xn, small = [n[0] for n in normed], [n[1] for n in normed]
        proj = _proj_in(xn[0], wl["w_in_t"], i, tm=cfg["tm_proj"], rider=rest(xn))
        y_ssd, y_gla = [], []
        for g, a, p, sm in zip(groups, acc, proj, small):
            y, a["h"], a["c"] = _ssd(p, sm, g["st_ssd_conv"], g["st_ssd"], wl, i, a["h"], a["c"],
                                     bt=g["bt"], L=g["L"], Q=g["cfg"]["ssd_q"])
            y_ssd.append(y)
            y, a["s"] = _gla(p, sm, g["st_gla"], wl, i, a["s"], bt=g["bt"], L=g["L"], TB=g["cfg"]["gla_tb"])
            y_gla.append(y)
        merged = _merge(y_ssd[0], y_gla[0], proj[0], big["w_branch"], i, tm=tm, tn=512,
                        rider=rest(list(zip(y_ssd, y_gla, proj))))
        xs = _mm_res(merged[0], big["w_out"], i, xs[0], tm=tm, tn=1024, rider=rest(list(zip(merged, xs))))
        o = []
        for g, x in zip(groups, xs):
            q = _norm_mm(x, big["norm_mem"], w_mq_b, i, None, tm=g["cfg"]["tm"], tn=1024, out_dtype=BF16)
            o.append(_attn(q, g["mem_k"], g["mem_v"], i, bt=g["bt"], L=g["L"], tl=g["cfg"]["attn_tl"]))
        xs = _mm_res(o[0], big["w_mo"], i, xs[0], tm=tm, tn=1024, rider=rest(list(zip(o, xs))))
        act = []
        for g, a, x in zip(groups, acc, xs):
            xf = _norm_cast(x, big["norm_ffn"], i, tm=g["cfg"]["tm_norm"])
            y, f_c = _ffn_up(xf, big["w_ffn_in"], big["ffn_conv_w"], big["ffn_conv_b"], g["st_ffn_conv"], i,
                             L=g["L"], tm=g["cfg"]["tm"], tn=512)
            act.append(y)
            a["ffn"].append(f_c)
        xs = _mm_res(act[0], big["w_ffn_out"], i, xs[0], tm=cfg["tm_norm"], tn=512, rider=rest(list(zip(act, xs))))
    return [(_final_norm(x, norm_final, tm=g["cfg"]["tm_norm"]), a["h"], a["c"], a["s"], jnp.stack(a["ffn"]))
            for g, a, x in zip(groups, acc, xs)]


def _group_cfg(bt, L):
    T = bt * L
    return dict(tm=min(T, 1024), tm_proj=min(T, 1024), tm_norm=min(T, 512), ssd_q=min(L, 128), gla_tb=min(L, 128), attn_tl=min(L, 512))


def kernel(x_prompt, x_sample, mem_prompt, state_ssd, state_ssd_conv, state_gla, state_ffn_conv, cache_mem_k, cache_mem_v, norm_mix, w_in, ssd_conv_w, ssd_conv_b, ssd_dt_bias, ssd_a_log, ssd_d, ssd_norm, gla_wa2, gla_ba, gla_norm, w_branch, w_out, norm_mem, w_mq, w_mk, w_mv, w_mo, norm_ffn, w_ffn_in, ffn_conv_w, ffn_conv_b, w_ffn_out, norm_final):
    wl = _prep_weights(w_in, ssd_conv_w, ssd_conv_b, ssd_dt_bias, ssd_a_log, ssd_d, ssd_norm, gla_wa2, gla_ba, gla_norm)
    big = dict(norm_mix=norm_mix, w_branch=w_branch, w_out=w_out, norm_mem=norm_mem, w_mq=w_mq, w_mo=w_mo,
               norm_ffn=norm_ffn, w_ffn_in=w_ffn_in, ffn_conv_w=ffn_conv_w, ffn_conv_b=ffn_conv_b, w_ffn_out=w_ffn_out)
    pb, pl_len, _ = x_prompt.shape
    sb, sl_len, _ = x_sample.shape
    mem2d = mem_prompt.reshape(pb * MEM_TOKENS, D_MODEL)
    p_mem_k = _mem_kv(mem2d, w_mk, nb=pb)
    p_mem_v = _mem_kv(mem2d, w_mv, nb=pb)
    head_major = (0, 1, 3, 2, 4)

    prompt = _group(x_prompt.reshape(pb * pl_len, D_MODEL), pb, pl_len, p_mem_k, p_mem_v)
    sample = _group(x_sample.reshape(sb * sl_len, D_MODEL), sb, sl_len,
                    jnp.transpose(cache_mem_k, head_major), jnp.transpose(cache_mem_v, head_major),
                    state_ssd, state_ssd_conv, state_gla, state_ffn_conv)
    (y_p, p_ssd, p_ssd_conv, p_gla, p_ffn), (y_s, s_ssd, s_ssd_conv, s_gla, s_ffn) = _run_trunk(
        [prompt, sample], wl, big, norm_final)
    return (y_p.reshape(pb, pl_len, D_MODEL), y_s.reshape(sb, sl_len, D_MODEL),
            p_ssd, p_ssd_conv, p_gla, p_ffn, jnp.transpose(p_mem_k, head_major), jnp.transpose(p_mem_v, head_major),
            s_ssd, s_ssd_conv, s_gla, s_ffn)
```

```python
import functools

import jax
import jax.numpy as jnp
from jax import lax
from jax.experimental import pallas as pl
from jax.experimental.pallas import tpu as pltpu

F32 = jnp.float32
BF16 = jnp.bfloat16

D_MODEL = 2048
DEPTH = 2
EPS = 1e-6
SSD_HEADS = 32
SSD_HEADDIM = 64
SSD_GROUPS = 4
SSD_STATE = 128
SSD_CONV = 4
SSD_CONV_DIM = 3072
GLA_HEADS = 4
GLA_HEAD_K = 256
GLA_HEAD_V = 512
GLA_RANK = 16
GLA_GATE_NORM = 16.0
GLA_CHUNK = 16
MEM_TOKENS = 256
MEM_HEADS = 4
MEM_HEAD_DIM = 512
D_FF = 5632
FFN_CONV = 3

LANES = 128
MXU_COLS = 256
SUBLANES = 8
VMEM_CAP_BYTES = 56 * 2**20

P_Z, P_XS, P_V, P_G, P_GATE, P_BC, P_Q, P_K, P_N = 0, 2048, 4096, 6144, 8192, 12288, 13312, 14336, 15360
SM_DT, SM_ALR = 0, 32

NT_DIMS = (((1,), (1,)), ((), ()))
TN_DIMS = (((0,), (0,)), ((), ()))
HI = lax.Precision.HIGHEST


def _vmem_limit(*block_bytes, scratch=0):
    need = 2 * sum(block_bytes) + scratch + (4 << 20)
    return int(min(max(need, 16 << 20), VMEM_CAP_BYTES))


def _nbytes(shape, dtype):
    n = 1
    for s in shape:
        n *= s
    return n * jnp.dtype(dtype).itemsize


def _silu(x):
    return x * jax.nn.sigmoid(x)


def _softplus(x):
    return jnp.maximum(x, 0.0) + jnp.log(1.0 + jnp.exp(-jnp.abs(x)))


def _log_sigmoid(x):
    return jnp.minimum(x, 0.0) - jnp.log(1.0 + jnp.exp(-jnp.abs(x)))


def _rms(x, gain):
    ms = jnp.mean(x * x, axis=-1, keepdims=True)
    return x * lax.rsqrt(ms + EPS) * gain


def _split2(x):
    hi = x.astype(BF16)
    lo = (x - hi.astype(F32)).astype(BF16)
    return hi, lo


def _select_rows(sel, x):
    hi = x.astype(BF16)
    r1 = x - hi.astype(F32)
    mid = r1.astype(BF16)
    lo = (r1 - mid.astype(F32)).astype(BF16)
    return (jnp.dot(sel, hi, preferred_element_type=F32) + jnp.dot(sel, mid, preferred_element_type=F32)
            + jnp.dot(sel, lo, preferred_element_type=F32))


def _stacked_out(prev, shape, dtype):
    return jax.ShapeDtypeStruct(shape, dtype) if prev is None else jax.ShapeDtypeStruct(prev.shape, prev.dtype)


def _norm_mm_kernel(*refs, has_gain, has_small):
    it = iter(refs)
    x_ref = next(it)
    g_ref = next(it) if has_gain else None
    w_ref = next(it)
    ws_ref = next(it) if has_small else None
    o_ref = next(it)
    os_ref = next(it) if has_small else None
    xn_ref = next(it)

    @pl.when(pl.program_id(1) == 0)
    def _():
        x = x_ref[...].astype(F32)
        if has_gain:
            x = _rms(x, g_ref[...])
        xn_ref[...] = x.astype(BF16)
        if has_small:
            os_ref[...] = jnp.dot(xn_ref[...], ws_ref[...], preferred_element_type=F32)

    o_ref[...] = jnp.dot(xn_ref[...], w_ref[...].astype(BF16), preferred_element_type=F32).astype(o_ref.dtype)


def _norm_mm(x, gain, w, layer, w_small, *, tm, tn, out_dtype):
    m, k = x.shape
    n = w.shape[2]
    has_gain = gain is not None
    has_small = w_small is not None
    in_specs = [pl.BlockSpec((tm, k), lambda i, j: (i, 0))]
    args = [x]
    if has_gain:
        in_specs.append(pl.BlockSpec((None, 1, k), lambda i, j: (layer, 0, 0)))
        args.append(gain.reshape(DEPTH, 1, k))
    in_specs.append(pl.BlockSpec((None, k, tn), lambda i, j: (layer, 0, j)))
    args.append(w)
    out_shape = [jax.ShapeDtypeStruct((m, n), out_dtype)]
    out_specs = [pl.BlockSpec((tm, tn), lambda i, j: (i, j))]
    if has_small:
        in_specs.append(pl.BlockSpec((None, k, LANES), lambda i, j: (layer, 0, 0)))
        args.append(w_small)
        out_shape.append(jax.ShapeDtypeStruct((m, LANES), F32))
        out_specs.append(pl.BlockSpec((tm, LANES), lambda i, j: (i, 0)))
    limit = _vmem_limit(_nbytes((tm, k), x.dtype), _nbytes((k, tn), w.dtype), _nbytes((tm, tn), out_dtype),
                        _nbytes((k, LANES), BF16), _nbytes((tm, LANES), F32), scratch=_nbytes((tm, k), BF16))
    res = pl.pallas_call(
        functools.partial(_norm_mm_kernel, has_gain=has_gain, has_small=has_small),
        grid=(m // tm, n // tn),
        in_specs=in_specs,
        out_specs=out_specs,
        out_shape=out_shape,
        scratch_shapes=[pltpu.VMEM((tm, k), BF16)],
        compiler_params=pltpu.CompilerParams(dimension_semantics=("parallel", "arbitrary"),
                                             vmem_limit_bytes=limit),
        name="norm_mm",
    )(*args)
    return res if has_small else res[0]


def _mem_kv_kernel(x_ref, w_ref, o_ref, xb_ref):
    @pl.when((pl.program_id(0) == 0) & (pl.program_id(1) == 0))
    def _():
        xb_ref[...] = x_ref[...].astype(BF16)

    res = jnp.dot(xb_ref[...], w_ref[...].astype(BF16), preferred_element_type=F32)
    o_ref[...] = res.reshape(o_ref.shape)


def _mem_kv(x, w, *, nb):
    m, k = x.shape
    tn = MEM_HEAD_DIM
    limit = _vmem_limit(_nbytes((m, k), F32), _nbytes((k, tn), F32), _nbytes((m, tn), F32),
                        scratch=_nbytes((m, k), BF16))
    return pl.pallas_call(
        _mem_kv_kernel,
        grid=(DEPTH, MEM_HEADS),
        in_specs=[pl.BlockSpec((m, k), lambda d, h: (0, 0)),
                  pl.BlockSpec((None, k, tn), lambda d, h: (d, 0, h))],
        out_specs=pl.BlockSpec((None, nb, None, m // nb, tn), lambda d, h: (d, 0, h, 0, 0)),
        out_shape=jax.ShapeDtypeStruct((DEPTH, nb, MEM_HEADS, m // nb, tn), F32),
        scratch_shapes=[pltpu.VMEM((m, k), BF16)],
        compiler_params=pltpu.CompilerParams(dimension_semantics=("arbitrary", "arbitrary"),
                                             vmem_limit_bytes=limit),
        name="mem_kv",
    )(x, w)


def _cast_kernel(w_ref, o_ref):
    o_ref[...] = w_ref[...].astype(o_ref.dtype)


def _cast_bf16(w, *, tn):
    d, k, n = w.shape
    return pl.pallas_call(
        _cast_kernel,
        grid=(d, n // tn),
        in_specs=[pl.BlockSpec((None, k, tn), lambda l, j: (l, 0, j))],
        out_specs=pl.BlockSpec((None, k, tn), lambda l, j: (l, 0, j)),
        out_shape=jax.ShapeDtypeStruct(w.shape, BF16),
        compiler_params=pltpu.CompilerParams(dimension_semantics=("parallel", "parallel")),
        name="cast_bf16",
    )(w)


def _norm_cast_kernel(x_ref, g_ref, xn_ref):
    xn_ref[...] = _rms(x_ref[...], g_ref[...]).astype(BF16)


def _norm_cast(x, gain, layer, *, tm):
    m, k = x.shape
    return pl.pallas_call(
        _norm_cast_kernel,
        grid=(m // tm,),
        in_specs=[pl.BlockSpec((tm, k), lambda i: (i, 0)), pl.BlockSpec((None, 1, k), lambda i: (layer, 0, 0))],
        out_specs=pl.BlockSpec((tm, k), lambda i: (i, 0)),
        out_shape=jax.ShapeDtypeStruct((m, k), BF16),
        compiler_params=pltpu.CompilerParams(dimension_semantics=("parallel",)),
        name="norm_cast",
    )(x, gain.reshape(DEPTH, 1, k))


def _norm_small_kernel(x_ref, g_ref, wst_ref, xn_ref, os_ref):
    xn = _rms(x_ref[...], g_ref[...]).astype(BF16)
    xn_ref[...] = xn
    os_ref[...] = lax.dot_general(xn, wst_ref[...].astype(BF16), NT_DIMS, preferred_element_type=F32)


def _norm_small(x, gain, w_small_t, layer, *, tm):
    m, k = x.shape
    return pl.pallas_call(
        _norm_small_kernel,
        grid=(m // tm,),
        in_specs=[pl.BlockSpec((tm, k), lambda i: (i, 0)),
                  pl.BlockSpec((None, 1, k), lambda i: (layer, 0, 0)),
                  pl.BlockSpec((None, LANES, k), lambda i: (layer, 0, 0))],
        out_specs=[pl.BlockSpec((tm, k), lambda i: (i, 0)), pl.BlockSpec((tm, LANES), lambda i: (i, 0))],
        out_shape=[jax.ShapeDtypeStruct((m, k), BF16), jax.ShapeDtypeStruct((m, LANES), F32)],
        compiler_params=pltpu.CompilerParams(dimension_semantics=("parallel",)),
        name="norm_small",
    )(x, gain.reshape(DEPTH, 1, k), w_small_t)


PROJ_TN = 1024
PROJ_SRC = (0, 1024, 2048, 3072, 7200, 8224, 9248, 10272, 11312, 12336, 13360, 14384, 4096, 5152, 6176)


def _rider_steps(n_main, rider, main_fn, rider_fn):
    if not rider:
        main_fn()
        return
    i = pl.program_id(1)
    pl.when(i == 0)(rider_fn)
    pl.when(i > 0)(main_fn)


def _proj_kernel(*refs, n_main, rider):
    it = iter(refs)
    _src_ref, xn_ref = next(it), next(it)
    xn2_ref = next(it) if rider else None
    wt_ref, o_ref = next(it), next(it)
    o2_ref = next(it) if rider else None
    wbf_ref = next(it)

    @pl.when(pl.program_id(1) == 0)
    def _():
        wbf_ref[...] = wt_ref[0].astype(BF16)

    def mm(x_ref, out_ref):
        out_ref[...] = lax.dot_general(x_ref[...], wbf_ref[...], NT_DIMS,
                                       preferred_element_type=F32).astype(out_ref.dtype)

    _rider_steps(n_main, rider, lambda: mm(xn_ref, o_ref), lambda: mm(xn2_ref, o2_ref))


def _proj_in(xn, w_in_t, layer, *, tm, rider=None):
    m, k = xn.shape
    tn = PROJ_TN
    nj = len(PROJ_SRC)
    n_main = m // tm
    assert all(s % SUBLANES == 0 for s in PROJ_SRC)
    src = jnp.array([s // SUBLANES for s in PROJ_SRC], jnp.int32)
    im = (lambda i: i) if rider is None else (lambda i: jnp.maximum(i - 1, 0))
    in_specs = [pl.BlockSpec((tm, k), lambda j, i, s: (im(i), 0))]
    out_specs = [pl.BlockSpec((tm, tn), lambda j, i, s: (im(i), j))]
    out_shape = [jax.ShapeDtypeStruct((m, nj * tn), BF16)]
    args = [src, xn]
    m2 = 0
    if rider is not None:
        m2 = rider.shape[0]
        in_specs.append(pl.BlockSpec((m2, k), lambda j, i, s: (0, 0)))
        out_specs.append(pl.BlockSpec((m2, tn), lambda j, i, s: (0, j)))
        out_shape.append(jax.ShapeDtypeStruct((m2, nj * tn), BF16))
        args.append(rider)
    in_specs.append(pl.BlockSpec((pl.Element(1), pl.Element(tn), pl.Element(k)),
                                 lambda j, i, s: (layer, s[j] * SUBLANES, 0)))
    args.append(w_in_t)
    limit = _vmem_limit(_nbytes((tm + m2, k), BF16), _nbytes((tn, k), F32), _nbytes((tm + m2, tn), BF16),
                        scratch=_nbytes((tn, k), BF16))
    grid_spec = pltpu.PrefetchScalarGridSpec(
        num_scalar_prefetch=1,
        grid=(nj, n_main + (rider is not None)),
        in_specs=in_specs,
        out_specs=out_specs,
        scratch_shapes=[pltpu.VMEM((tn, k), BF16)])
    return pl.pallas_call(
        functools.partial(_proj_kernel, n_main=n_main, rider=rider is not None),
        grid_spec=grid_spec,
        out_shape=out_shape,
        compiler_params=pltpu.CompilerParams(dimension_semantics=("arbitrary", "arbitrary"),
                                             vmem_limit_bytes=limit),
        name="proj_in",
    )(*args)


def _mm_res_kernel(*refs, n_main, rider):
    it = iter(refs)
    x_ref, r_ref = next(it), next(it)
    x2_ref, r2_ref = (next(it), next(it)) if rider else (None, None)
    w_ref, o_ref = next(it), next(it)
    o2_ref = next(it) if rider else None
    wbf_ref = next(it)

    @pl.when(pl.program_id(1) == 0)
    def _():
        wbf_ref[...] = w_ref[...].astype(BF16)

    def mm(xr, rr, outr):
        outr[...] = rr[...] + jnp.dot(xr[...], wbf_ref[...], preferred_element_type=F32)

    _rider_steps(n_main, rider, lambda: mm(x_ref, r_ref, o_ref), lambda: mm(x2_ref, r2_ref, o2_ref))


def _mm_res(x, w, layer, res, *, tm, tn, rider=None):
    m, k = x.shape
    n = w.shape[2]
    n_main = m // tm
    im = (lambda i: i) if rider is None else (lambda i: jnp.maximum(i - 1, 0))
    in_specs = [pl.BlockSpec((tm, k), lambda j, i: (im(i), 0)), pl.BlockSpec((tm, tn), lambda j, i: (im(i), j))]
    out_specs = [pl.BlockSpec((tm, tn), lambda j, i: (im(i), j))]
    out_shape = [jax.ShapeDtypeStruct((m, n), F32)]
    args = [x, res]
    m2 = 0
    if rider is not None:
        m2 = rider[0].shape[0]
        in_specs += [pl.BlockSpec((m2, k), lambda j, i: (0, 0)), pl.BlockSpec((m2, tn), lambda j, i: (0, j))]
        out_specs.append(pl.BlockSpec((m2, tn), lambda j, i: (0, j)))
        out_shape.append(jax.ShapeDtypeStruct((m2, n), F32))
        args += list(rider)
    in_specs.append(pl.BlockSpec((None, k, tn), lambda j, i: (layer, 0, j)))
    args.append(w)
    limit = _vmem_limit(_nbytes((tm + m2, k), BF16), _nbytes((k, tn), F32), 2 * _nbytes((tm + m2, tn), F32),
                        scratch=_nbytes((k, tn), BF16))
    return pl.pallas_call(
        functools.partial(_mm_res_kernel, n_main=n_main, rider=rider is not None),
        grid=(n // tn, n_main + (rider is not None)),
        in_specs=in_specs,
        out_specs=out_specs,
        out_shape=out_shape,
        scratch_shapes=[pltpu.VMEM((k, tn), BF16)],
        compiler_params=pltpu.CompilerParams(dimension_semantics=("parallel", "arbitrary"),
                                             vmem_limit_bytes=limit),
        name="mm_res",
    )(*args)


def _merge_kernel(*refs, n_main, rider):
    it = iter(refs)
    main_in = [next(it) for _ in range(4)]
    rider_in = [next(it) for _ in range(4)] if rider else None
    w_ref, o_ref = next(it), next(it)
    o2_ref = next(it) if rider else None
    wbf_ref = next(it)

    @pl.when(pl.program_id(1) == 0)
    def _():
        wbf_ref[...] = w_ref[...].astype(BF16)

    def mm(ins, outr):
        y0_ref, y1_ref, g0_ref, g1_ref = ins
        b0 = jnp.dot(y0_ref[...], wbf_ref[0], preferred_element_type=F32)
        b1 = jnp.dot(y1_ref[...], wbf_ref[1], preferred_element_type=F32)
        g0 = jax.nn.sigmoid(g0_ref[...].astype(F32))
        g1 = jax.nn.sigmoid(g1_ref[...].astype(F32))
        outr[...] = (g0 * b0 + g1 * b1).astype(outr.dtype)

    _rider_steps(n_main, rider, lambda: mm(main_in, o_ref), lambda: mm(rider_in, o2_ref))


def _merge(y_ssd, y_gla, proj, w_branch, layer, *, tm, tn, rider=None):
    m, k = y_ssd.shape
    n = D_MODEL
    gb = P_GATE // tn
    nb = n // tn
    n_main = m // tm
    im = (lambda i: i) if rider is None else (lambda i: jnp.maximum(i - 1, 0))
    in_specs = [pl.BlockSpec((tm, k), lambda j, i: (im(i), 0)),
                pl.BlockSpec((tm, k), lambda j, i: (im(i), 0)),
                pl.BlockSpec((tm, tn), lambda j, i: (im(i), gb + j)),
                pl.BlockSpec((tm, tn), lambda j, i: (im(i), gb + nb + j))]
    out_specs = [pl.BlockSpec((tm, tn), lambda j, i: (im(i), j))]
    out_shape = [jax.ShapeDtypeStruct((m, n), BF16)]
    args = [y_ssd, y_gla, proj, proj]
    m2 = 0
    if rider is not None:
        y0_2, y1_2, proj2 = rider
        m2 = y0_2.shape[0]
        in_specs += [pl.BlockSpec((m2, k), lambda j, i: (0, 0)),
                     pl.BlockSpec((m2, k), lambda j, i: (0, 0)),
                     pl.BlockSpec((m2, tn), lambda j, i: (0, gb + j)),
                     pl.BlockSpec((m2, tn), lambda j, i: (0, gb + nb + j))]
        out_specs.append(pl.BlockSpec((m2, tn), lambda j, i: (0, j)))
        out_shape.append(jax.ShapeDtypeStruct((m2, n), BF16))
        args += [y0_2, y1_2, proj2, proj2]
    in_specs.append(pl.BlockSpec((None, 2, k, tn), lambda j, i: (layer, 0, 0, j)))
    args.append(w_branch)
    limit = _vmem_limit(2 * _nbytes((tm + m2, k), BF16), 2 * _nbytes((k, tn), F32),
                        3 * _nbytes((tm + m2, tn), BF16), scratch=2 * _nbytes((k, tn), BF16))
    return pl.pallas_call(
        functools.partial(_merge_kernel, n_main=n_main, rider=rider is not None),
        grid=(nb, n_main + (rider is not None)),
        in_specs=in_specs,
        out_specs=out_specs,
        out_shape=out_shape,
        scratch_shapes=[pltpu.VMEM((2, k, tn), BF16)],
        compiler_params=pltpu.CompilerParams(dimension_semantics=("parallel", "arbitrary"),
                                             vmem_limit_bytes=limit),
        name="merge",
    )(*args)


def _final_norm_kernel(x_ref, g_ref, o_ref):
    o_ref[...] = _rms(x_ref[...], g_ref[...])


def _final_norm(x, gain, *, tm):
    m, k = x.shape
    return pl.pallas_call(
        _final_norm_kernel,
        grid=(m // tm,),
        in_specs=[pl.BlockSpec((tm, k), lambda i: (i, 0)), pl.BlockSpec((1, k), lambda i: (0, 0))],
        out_specs=pl.BlockSpec((tm, k), lambda i: (i, 0)),
        out_shape=jax.ShapeDtypeStruct((m, k), F32),
        compiler_params=pltpu.CompilerParams(dimension_semantics=("parallel",)),
        name="final_norm",
    )(x, gain.reshape(1, k))


def _ffn_up_kernel(*refs, tm, tn, seq_tiles, seg):
    short = seg > 0
    it = iter(refs)
    xn_ref, wu_ref, wt_ref, cwu_ref, cwt_ref, cbu_ref, cbt_ref = (next(it) for _ in range(7))
    stu_ref = next(it) if short else None
    stt_ref = next(it) if short else None
    act_ref, cnu_ref, cnt_ref = next(it), next(it), next(it)
    wbu_ref, wbt_ref = next(it), next(it)
    tailu_ref = None if short else next(it)
    tailt_ref = None if short else next(it)
    i = pl.program_id(1)
    sl = SUBLANES
    hw = FFN_CONV - 1

    @pl.when(i == 0)
    def _():
        wbu_ref[...] = wu_ref[...].astype(BF16)
        wbt_ref[...] = wt_ref[...].astype(BF16)
        if not short:
            tailu_ref[...] = jnp.zeros_like(tailu_ref)
            tailt_ref[...] = jnp.zeros_like(tailt_ref)

    xn = xn_ref[...]
    u_raw = jnp.dot(xn, wbu_ref[...], preferred_element_type=F32)
    t_raw = jnp.dot(xn, wbt_ref[...], preferred_element_type=F32)

    if short:
        nseq = tm // seg
        r = lax.broadcasted_iota(jnp.int32, (tm, hw * nseq), 0)
        c = lax.broadcasted_iota(jnp.int32, (tm, hw * nseq), 1)
        halo_u, halo_t = [], []
        for s in range(1, FFN_CONV):
            sel = (((r % seg) < s) & (c == hw * (r // seg) + hw - s + (r % seg))).astype(BF16)
            halo_u.append(_select_rows(sel, stu_ref[...]))
            halo_t.append(_select_rows(sel, stt_ref[...]))
        ro = lax.broadcasted_iota(jnp.int32, (hw * nseq, tm), 0)
        co = lax.broadcasted_iota(jnp.int32, (hw * nseq, tm), 1)
        sel_out = (co == (ro // hw) * seg + seg - hw + (ro % hw)).astype(BF16)
        cnu_ref[...] = _select_rows(sel_out, u_raw)
        cnt_ref[...] = _select_rows(sel_out, t_raw)
        rowmod = lax.broadcasted_iota(jnp.int32, (tm, tn), 0) % seg
        prev_u = prev_t = None
    else:
        start = (i % seq_tiles) == 0
        prev_u = jnp.where(start, 0.0, tailu_ref[...])
        prev_t = jnp.where(start, 0.0, tailt_ref[...])
        tailu_ref[...] = u_raw[tm - sl:tm, :]
        tailt_ref[...] = t_raw[tm - sl:tm, :]
        cnu_ref[...] = u_raw[tm - sl:tm, :]
        cnt_ref[...] = t_raw[tm - sl:tm, :]
        row8 = lax.broadcasted_iota(jnp.int32, (sl, tn), 0)
        halo_u = halo_t = rowmod = None

    def conv(x, prev8, cw_ref, cb_ref, halos):
        y = x * cw_ref[hw:hw + 1, :]
        for s in range(1, FFN_CONV):
            rolled = pltpu.roll(x, s, axis=0)
            if short:
                shifted = jnp.where(rowmod < s, halos[s - 1], rolled)
            else:
                head = jnp.where(row8 < s, pltpu.roll(prev8, s, axis=0), rolled[0:sl, :])
                shifted = jnp.concatenate([head, rolled[sl:, :]], axis=0)
            y = y + shifted * cw_ref[hw - s:hw - s + 1, :]
        return y + cb_ref[...]

    u = conv(u_raw, prev_u, cwu_ref, cbu_ref, halo_u)
    t = conv(t_raw, prev_t, cwt_ref, cbt_ref, halo_t)
    act_ref[...] = (_silu(t) * u).astype(act_ref.dtype)


def _ffn_up(xn, w, conv_w, conv_b, conv0, layer, *, L, tm, tn):
    m, k = xn.shape
    bt = m // L
    nj = D_FF // tn
    hw = FFN_CONV - 1
    short = L < tm
    seg = L if short else 0
    seq_tiles = 1 if short else L // tm
    assert (conv0 is not None) == short, "history rows are only supported for sequences shorter than a block"
    in_specs = [
        pl.BlockSpec((tm, k), lambda j, i: (i, 0)),
        pl.BlockSpec((None, k, tn), lambda j, i: (layer, 0, j)),
        pl.BlockSpec((None, k, tn), lambda j, i: (layer, 0, nj + j)),
        pl.BlockSpec((None, FFN_CONV, tn), lambda j, i: (layer, 0, j)),
        pl.BlockSpec((None, FFN_CONV, tn), lambda j, i: (layer, 0, nj + j)),
        pl.BlockSpec((None, 1, tn), lambda j, i: (layer, 0, j)),
        pl.BlockSpec((None, 1, tn), lambda j, i: (layer, 0, nj + j)),
    ]
    args = [xn, w, w, conv_w, conv_w, conv_b.reshape(DEPTH, 1, 2 * D_FF), conv_b.reshape(DEPTH, 1, 2 * D_FF)]
    scratch = [pltpu.VMEM((k, tn), BF16)] * 2
    if short:
        nst = (m // tm) * (tm // L) * hw
        st2d = conv0.reshape(DEPTH, nst, 2 * D_FF)
        rows = (tm // L) * hw
        in_specs += [pl.BlockSpec((None, rows, tn), lambda j, i: (layer, i, j)),
                     pl.BlockSpec((None, rows, tn), lambda j, i: (layer, i, nj + j))]
        args += [st2d, st2d]
        cn_shape = jax.ShapeDtypeStruct((nst, D_FF), F32)
        cn_spec = pl.BlockSpec((rows, tn), lambda j, i: (i, j))
    else:
        scratch += [pltpu.VMEM((SUBLANES, tn), F32)] * 2
        cn_shape = jax.ShapeDtypeStruct((m // tm, SUBLANES, D_FF), F32)
        cn_spec = pl.BlockSpec((None, SUBLANES, tn), lambda j, i: (i, 0, j))
    limit = _vmem_limit(_nbytes((tm, k), BF16), 2 * _nbytes((k, tn), F32), _nbytes((tm, tn), BF16),
                        scratch=2 * _nbytes((k, tn), BF16) + 6 * _nbytes((tm, tn), F32))
    act, cnu, cnt = pl.pallas_call(
        functools.partial(_ffn_up_kernel, tm=tm, tn=tn, seq_tiles=seq_tiles, seg=seg),
        grid=(nj, m // tm),
        in_specs=in_specs,
        out_specs=[pl.BlockSpec((tm, tn), lambda j, i: (i, j)), cn_spec, cn_spec],
        out_shape=[jax.ShapeDtypeStruct((m, D_FF), BF16), cn_shape, cn_shape],
        scratch_shapes=scratch,
        compiler_params=pltpu.CompilerParams(dimension_semantics=("arbitrary", "arbitrary"),
                                             vmem_limit_bytes=limit),
        name="ffn_up",
    )(*args)
    if short:
        conv_new = jnp.concatenate([cnu, cnt], axis=-1).reshape(bt, hw, 2 * D_FF)
    else:
        last = slice(seq_tiles - 1, None, seq_tiles)
        conv_new = jnp.concatenate([cnu[last, SUBLANES - hw:, :], cnt[last, SUBLANES - hw:, :]], axis=-1)
    return act, conv_new


def _causal_conv(x, tail_ref, w_ref, b, width):
    sl = SUBLANES
    row = lax.broadcasted_iota(jnp.int32, (sl, x.shape[1]), 0)
    prev = tail_ref[...]
    y = x * w_ref[width - 1:width, :]
    for s in range(1, width):
        rolled = pltpu.roll(x, s, axis=0)
        head = jnp.where(row < s, pltpu.roll(prev, s, axis=0), rolled[0:sl, :])
        shifted = jnp.concatenate([head, rolled[sl:, :]], axis=0)
        y = y + shifted * w_ref[width - 1 - s:width - s, :]
    return y + b


def _ssd_kernel(*refs, Q, zero_init, aliased):
    it = iter(refs)
    z_ref, xs_ref, bc_ref, sm_ref = (next(it) for _ in range(4))
    conv0_ref = None if zero_init else next(it)
    h0_ref = None if zero_init else next(it)
    cwx_ref, cwb_ref, cbx_ref, cbb_ref, dtb_ref, a_ref, dx_ref, nrm_ref, exp_ref = (next(it) for _ in range(9))
    if aliased:
        next(it), next(it)
    y_ref, h_out, conv_out, tailx_ref, tailb_ref = (next(it) for _ in range(5))
    h_ref = h_out if aliased else h_out.at[0]
    convout_ref = conv_out if aliased else conv_out.at[0]
    c = pl.program_id(1)
    hp = SSD_HEADDIM
    gw = SSD_HEADS // SSD_GROUPS * hp

    @pl.when(c == 0)
    def _():
        if not aliased:
            h_out[1:] = jnp.zeros((DEPTH - 1,) + h_ref.shape, F32)
            conv_out[1:] = jnp.zeros((DEPTH - 1,) + convout_ref.shape, F32)
        tailx_ref[...] = jnp.zeros_like(tailx_ref)
        tailb_ref[...] = jnp.zeros_like(tailb_ref)
        if zero_init:
            h_ref[...] = jnp.zeros_like(h_ref)
        else:
            h_ref[...] = h0_ref[...]
            tailx_ref[5:8, :] = conv0_ref[:, 0:D_MODEL]
            tailb_ref[5:8, :] = conv0_ref[:, D_MODEL:SSD_CONV_DIM]

    xs_raw = xs_ref[...].astype(F32)
    bc_raw = bc_ref[...].astype(F32)
    xs = _silu(_causal_conv(xs_raw, tailx_ref, cwx_ref, cbx_ref[...], SSD_CONV))
    bc = _silu(_causal_conv(bc_raw, tailb_ref, cwb_ref, cbb_ref[...], SSD_CONV))
    tailx_ref[...] = xs_raw[Q - 8:Q, :]
    tailb_ref[...] = bc_raw[Q - 8:Q, :]

    @pl.when(c == pl.num_programs(1) - 1)
    def _():
        convout_ref[:, 0:D_MODEL] = tailx_ref[5:8, :]
        convout_ref[:, D_MODEL:SSD_CONV_DIM] = tailb_ref[5:8, :]

    dt = _softplus(sm_ref[...] + dtb_ref[...])
    adt = dt * a_ref[...]
    ri = lax.broadcasted_iota(jnp.int32, (Q, Q), 0)
    ci = lax.broadcasted_iota(jnp.int32, (Q, Q), 1)
    causal = ci <= ri
    acum = jnp.dot(causal.astype(F32), adt, precision=HI, preferred_element_type=F32)
    eye = (lax.broadcasted_iota(jnp.int32, (LANES, LANES), 0)
           == lax.broadcasted_iota(jnp.int32, (LANES, LANES), 1)).astype(F32)
    acum_t = lax.dot_general(eye, acum, NT_DIMS, precision=HI, preferred_element_type=F32)
    dt_t = lax.dot_general(eye, dt, NT_DIMS, precision=HI, preferred_element_type=F32)
    a_last = acum[Q - 1:Q, :]
    ea = jnp.exp(acum)
    te = jnp.exp(a_last - acum) * dt
    dec_rows = jnp.broadcast_to(jnp.exp(acum_t[:, Q - 1:Q]), (LANES, LANES))
    ea_hi, ea_lo = _split2(ea)
    te_hi, te_lo = _split2(te)
    expand = exp_ref[...]
    ea_x = (jnp.dot(ea_hi, expand, preferred_element_type=F32)
            + jnp.dot(ea_lo, expand, preferred_element_type=F32))
    te_x = (jnp.dot(te_hi, expand, preferred_element_type=F32)
            + jnp.dot(te_lo, expand, preferred_element_type=F32))
    lane = lax.broadcasted_iota(jnp.int32, (Q, LANES), 1)

    for g in range(SSD_GROUPS):
        gl = slice(g * gw, (g + 1) * gw)
        b_g = bc[:, g * SSD_STATE:(g + 1) * SSD_STATE].astype(BF16)
        c_g = bc[:, (SSD_GROUPS + g) * SSD_STATE:(SSD_GROUPS + g + 1) * SSD_STATE].astype(BF16)
        cb = lax.dot_general(c_g, b_g, NT_DIMS, preferred_element_type=F32)
        h_g = h_ref[8 * g:8 * g + 8].reshape(gw, SSD_STATE)
        y_off = lax.dot_general(c_g, h_g.astype(BF16), NT_DIMS, preferred_element_type=F32)
        x_g = xs[:, gl]
        pairs = []
        for p in range(4):
            x_p = x_g[:, p * LANES:(p + 1) * LANES]
            acc = None
            for s in range(2):
                hh = g * 8 + p * 2 + s
                seg = jnp.broadcast_to(acum[:, hh:hh + 1], (Q, Q)) - jnp.broadcast_to(acum_t[hh:hh + 1, :], (Q, Q))
                decay = jnp.where(causal, jnp.exp(seg), 0.0)
                w_h = (cb * decay * dt_t[hh:hh + 1, :]).astype(BF16)
                x_m = jnp.where((lane // hp) == s, x_p, 0.0).astype(BF16)
                r = jnp.dot(w_h, x_m, preferred_element_type=F32)
                acc = r if acc is None else acc + r
            pairs.append(acc)
        y_g = jnp.concatenate(pairs, axis=1) + y_off * ea_x[:, gl] + dx_ref[:, gl] * x_g
        x_t = (x_g * te_x[:, gl]).astype(BF16)
        upd = lax.dot_general(x_t, b_g, TN_DIMS, preferred_element_type=F32)
        for h in range(8):
            hh = g * 8 + h
            h_ref[hh] = h_ref[hh] * dec_rows[hh:hh + 1, :] + upd[h * hp:(h + 1) * hp, :]
        z_g = z_ref[:, gl].astype(F32)
        y_ref[:, gl] = _rms(y_g * _silu(z_g), nrm_ref[:, gl]).astype(y_ref.dtype)


def _ssd(proj, small, conv0, h0, wl, layer, prev_h, prev_conv, *, bt, L, Q):
    zero_init = h0 is None
    aliased = prev_h is not None
    nc = L // Q
    T = bt * L
    full2 = lambda b, c: (0, 0)
    lay3 = lambda b, c: (layer, 0, 0)
    row = lambda col: (lambda b, c: (b * nc + c, col))
    hshape = (DEPTH, bt, SSD_HEADS, SSD_HEADDIM, SSD_STATE)
    cshape = (DEPTH, bt, SSD_CONV - 1, SSD_CONV_DIM)
    hspec = pl.BlockSpec((None, None, SSD_HEADS, SSD_HEADDIM, SSD_STATE), lambda b, c: (layer, b, 0, 0, 0))
    cspec = pl.BlockSpec((None, None, SSD_CONV - 1, SSD_CONV_DIM), lambda b, c: (layer, b, 0, 0))
    h_out_spec = hspec if aliased else pl.BlockSpec((DEPTH, None, SSD_HEADS, SSD_HEADDIM, SSD_STATE),
                                                    lambda b, c: (0, b, 0, 0, 0))
    c_out_spec = cspec if aliased else pl.BlockSpec((DEPTH, None, SSD_CONV - 1, SSD_CONV_DIM),
                                                    lambda b, c: (0, b, 0, 0))
    in_specs = [
        pl.BlockSpec((Q, D_MODEL), row(P_Z // D_MODEL)),
        pl.BlockSpec((Q, D_MODEL), row(P_XS // D_MODEL)),
        pl.BlockSpec((Q, 1024), row(P_BC // 1024)),
        pl.BlockSpec((Q, LANES), row(0)),
    ]
    args = [proj, proj, proj, small]
    if not zero_init:
        in_specs += [cspec, hspec]
        args += [conv0, h0]
    in_specs += [
        pl.BlockSpec((None, SSD_CONV, D_MODEL), lay3),
        pl.BlockSpec((None, SSD_CONV, 1024), lambda b, c: (layer, 0, D_MODEL // 1024)),
        pl.BlockSpec((None, 1, D_MODEL), lay3),
        pl.BlockSpec((None, 1, 1024), lambda b, c: (layer, 0, D_MODEL // 1024)),
        pl.BlockSpec((None, 1, LANES), lay3),
        pl.BlockSpec((None, 1, LANES), lay3),
        pl.BlockSpec((None, 1, D_MODEL), lay3),
        pl.BlockSpec((None, 1, D_MODEL), lay3),
        pl.BlockSpec((LANES, D_MODEL), full2),
    ]
    args += [wl["ssd_conv_w"], wl["ssd_conv_w"], wl["ssd_conv_b"], wl["ssd_conv_b"], wl["ssd_dtb"], wl["ssd_a"],
             wl["ssd_dx"], wl["ssd_norm"], wl["expand"]]
    aliases = {}
    if aliased:
        aliases = {len(args): 1, len(args) + 1: 2}
        in_specs += [pl.BlockSpec(memory_space=pl.ANY), pl.BlockSpec(memory_space=pl.ANY)]
        args += [prev_h, prev_conv]
    return pl.pallas_call(
        functools.partial(_ssd_kernel, Q=Q, zero_init=zero_init, aliased=aliased),
        grid=(bt, nc),
        in_specs=in_specs,
        out_specs=[pl.BlockSpec((Q, D_MODEL), row(0)), h_out_spec, c_out_spec],
        out_shape=[jax.ShapeDtypeStruct((T, D_MODEL), BF16), _stacked_out(prev_h, hshape, F32),
                   _stacked_out(prev_conv, cshape, F32)],
        scratch_shapes=[pltpu.VMEM((8, D_MODEL), F32), pltpu.VMEM((8, 1024), F32)],
        input_output_aliases=aliases,
        compiler_params=pltpu.CompilerParams(dimension_semantics=("parallel", "arbitrary"),
                                             vmem_limit_bytes=48 << 20),
        name="ssd",
    )(*args)


def _gla_kernel(*refs, TB, zero_init, aliased):
    it = iter(refs)
    q_ref, k_ref, v_ref, g_ref, sm_ref = (next(it) for _ in range(5))
    s0_ref = None if zero_init else next(it)
    wa_ref, ba_ref, gn_ref = (next(it) for _ in range(3))
    if aliased:
        next(it)
    y_ref, s_out = next(it), next(it)
    s_ref = s_out if aliased else s_out.at[0]
    c = pl.program_id(1)
    ck = GLA_CHUNK
    ns = TB // ck
    hk, hv = GLA_HEAD_K, GLA_HEAD_V

    @pl.when(c == 0)
    def _():
        if not aliased:
            s_out[1:] = jnp.zeros((DEPTH - 1,) + s_ref.shape, F32)
        if zero_init:
            s_ref[...] = jnp.zeros_like(s_ref)
        else:
            s_ref[...] = s0_ref[...]

    smb = sm_ref[...].astype(BF16)
    la = _log_sigmoid(jnp.dot(smb, wa_ref[...], preferred_element_type=F32) + ba_ref[...]) * (1.0 / GLA_GATE_NORM)
    rb = lax.broadcasted_iota(jnp.int32, (TB, TB), 0)
    cb = lax.broadcasted_iota(jnp.int32, (TB, TB), 1)
    same = (rb // ck) == (cb // ck)
    msel = jnp.concatenate([same & (cb <= rb), same, (cb // ck) < (rb // ck)], axis=0).astype(BF16)
    sums = _select_rows(msel, la)
    bcum, tot, bprev = sums[0:TB], sums[TB:2 * TB], sums[2 * TB:3 * TB]
    kf = k_ref[...].astype(F32)
    qd = q_ref[...].astype(F32) * (hk ** -0.5) * jnp.exp(bcum)
    qd_b = qd.astype(BF16)
    qs_b = (qd * jnp.exp(bprev)).astype(BF16)
    ki = kf * jnp.exp(-bcum)
    ke = kf * jnp.exp(tot - bcum)
    dtot = jnp.exp(tot)
    ones = jnp.ones((TB, LANES), BF16)
    la_hi = la.astype(BF16)
    la_r = la - la_hi.astype(F32)
    la_mid = la_r.astype(BF16)
    la_lo = (la_r - la_mid.astype(F32)).astype(BF16)
    dec_t = jnp.exp(lax.dot_general(la_hi, ones, TN_DIMS, preferred_element_type=F32)
                    + lax.dot_general(la_mid, ones, TN_DIMS, preferred_element_type=F32)
                    + lax.dot_general(la_lo, ones, TN_DIMS, preferred_element_type=F32))
    rchunk = lax.broadcasted_iota(jnp.int32, (TB, hk), 0) // ck
    row16 = lax.broadcasted_iota(jnp.int32, (ck, TB), 0)
    col16 = lax.broadcasted_iota(jnp.int32, (ck, TB), 1)

    for h in range(GLA_HEADS):
        kl = slice(h * hk, (h + 1) * hk)
        vl = slice(h * hv, (h + 1) * hv)
        ke_h, ki_h = ke[:, kl], ki[:, kl]
        v_h = v_ref[:, vl]
        kbuf = jnp.zeros((TB, hk), F32)
        att = []
        for cc in range(ns):
            in_c = rchunk == cc
            kall = jnp.where(in_c, ki_h, kbuf).astype(BF16)
            a = lax.dot_general(qd_b[cc * ck:(cc + 1) * ck, kl], kall, NT_DIMS, preferred_element_type=F32)
            att.append(jnp.where(col16 <= row16 + cc * ck, a, 0.0))
            kbuf = jnp.where(in_c, ke_h, kbuf * dtot[cc * ck:cc * ck + 1, kl])
        a_full = jnp.concatenate(att, axis=0).astype(BF16)
        s_old = s_ref[h]
        o = (jnp.dot(a_full, v_h, preferred_element_type=F32)
             + jnp.dot(qs_b[:, kl], s_old.astype(BF16), preferred_element_type=F32))
        y_ref[:, vl] = (_rms(o, gn_ref[...]) * _silu(g_ref[:, vl].astype(F32))).astype(y_ref.dtype)
        dcol = jnp.concatenate([dec_t[kl, :]] * (hv // LANES), axis=1)
        s_ref[h] = s_old * dcol + lax.dot_general(kbuf.astype(BF16), v_h, TN_DIMS, preferred_element_type=F32)


def _gla(proj, small, s0, wl, layer, prev_s, *, bt, L, TB):
    zero_init = s0 is None
    aliased = prev_s is not None
    nc = L // TB
    T = bt * L
    lay3 = lambda b, c: (layer, 0, 0)
    row = lambda col: (lambda b, c: (b * nc + c, col))
    kd = GLA_HEADS * GLA_HEAD_K
    sshape = (DEPTH, bt, GLA_HEADS, GLA_HEAD_K, GLA_HEAD_V)
    sspec = pl.BlockSpec((None, None, GLA_HEADS, GLA_HEAD_K, GLA_HEAD_V), lambda b, c: (layer, b, 0, 0, 0))
    s_out_spec = sspec if aliased else pl.BlockSpec((DEPTH, None, GLA_HEADS, GLA_HEAD_K, GLA_HEAD_V),
                                                     lambda b, c: (0, b, 0, 0, 0))
    in_specs = [
        pl.BlockSpec((TB, kd), row(P_Q // kd)),
        pl.BlockSpec((TB, kd), row(P_K // kd)),
        pl.BlockSpec((TB, D_MODEL), row(P_V // D_MODEL)),
        pl.BlockSpec((TB, D_MODEL), row(P_G // D_MODEL)),
        pl.BlockSpec((TB, LANES), row(0)),
    ]
    args = [proj, proj, proj, proj, small]
    if not zero_init:
        in_specs.append(sspec)
        args.append(s0)
    in_specs += [
        pl.BlockSpec((None, LANES, kd), lay3),
        pl.BlockSpec((None, 1, kd), lay3),
        pl.BlockSpec((None, 1, GLA_HEAD_V), lay3),
    ]
    args += [wl["gla_wa"], wl["gla_ba"], wl["gla_norm"]]
    aliases = {}
    if aliased:
        aliases = {len(args): 1}
        in_specs.append(pl.BlockSpec(memory_space=pl.ANY))
        args.append(prev_s)
    return pl.pallas_call(
        functools.partial(_gla_kernel, TB=TB, zero_init=zero_init, aliased=aliased),
        grid=(bt, nc),
        in_specs=in_specs,
        out_specs=[pl.BlockSpec((TB, D_MODEL), row(0)), s_out_spec],
        out_shape=[jax.ShapeDtypeStruct((T, D_MODEL), BF16), _stacked_out(prev_s, sshape, F32)],
        input_output_aliases=aliases,
        compiler_params=pltpu.CompilerParams(dimension_semantics=("parallel", "arbitrary"),
                                             vmem_limit_bytes=48 << 20),
        name="gla",
    )(*args)


def _attn_kernel(q_ref, k_ref, v_ref, o_ref, kb_ref, vb_ref):
    @pl.when(pl.program_id(1) == 0)
    def _():
        kb_ref[...] = k_ref[...].astype(BF16)
        vb_ref[...] = v_ref[...].astype(BF16)

    hd = MEM_HEAD_DIM
    for h in range(MEM_HEADS):
        hl = slice(h * hd, (h + 1) * hd)
        s = lax.dot_general(q_ref[:, hl], kb_ref[h], NT_DIMS, preferred_element_type=F32) * (hd ** -0.5)
        e = jnp.exp(s - jnp.max(s, axis=-1, keepdims=True))
        p = e / jnp.sum(e, axis=-1, keepdims=True)
        o_ref[:, hl] = jnp.dot(p.astype(BF16), vb_ref[h], preferred_element_type=F32).astype(o_ref.dtype)


def _attn(q, mem_k, mem_v, layer, *, bt, L, tl):
    nl = L // tl
    T = bt * L
    kvshape = (MEM_HEADS, MEM_TOKENS, MEM_HEAD_DIM)
    kvspec = pl.BlockSpec((None, None) + kvshape, lambda b, l: (layer, b, 0, 0, 0))
    return pl.pallas_call(
        _attn_kernel,
        grid=(bt, nl),
        in_specs=[pl.BlockSpec((tl, D_MODEL), lambda b, l: (b * nl + l, 0)), kvspec, kvspec],
        out_specs=pl.BlockSpec((tl, D_MODEL), lambda b, l: (b * nl + l, 0)),
        out_shape=jax.ShapeDtypeStruct((T, D_MODEL), BF16),
        scratch_shapes=[pltpu.VMEM(kvshape, BF16), pltpu.VMEM(kvshape, BF16)],
        compiler_params=pltpu.CompilerParams(dimension_semantics=("parallel", "arbitrary"),
                                             vmem_limit_bytes=40 << 20),
        name="mem_attn",
    )(q, mem_k, mem_v)


def _prep_weights(w_in, ssd_conv_w, ssd_conv_b, ssd_dt_bias, ssd_a_log, ssd_d, ssd_norm, gla_wa2, gla_ba, gla_norm):
    w_in_t = jnp.swapaxes(w_in, 1, 2)
    o_dt, o_alr = 5120, 11296
    pad_sm = LANES - SSD_HEADS - GLA_RANK
    w_small_t = jnp.concatenate(
        [w_in_t[:, o_dt:o_dt + SSD_HEADS, :], w_in_t[:, o_alr:o_alr + GLA_RANK, :],
         jnp.zeros((DEPTH, pad_sm, D_MODEL), F32)], axis=1)
    pad_h = LANES - SSD_HEADS
    kd = GLA_HEADS * GLA_HEAD_K
    wa = jnp.concatenate([jnp.zeros((DEPTH, SM_ALR, kd), F32), gla_wa2,
                          jnp.zeros((DEPTH, LANES - SM_ALR - GLA_RANK, kd), F32)], axis=1).astype(BF16)
    expand = (jnp.arange(D_MODEL)[None, :] // SSD_HEADDIM == jnp.arange(LANES)[:, None]).astype(BF16)
    return dict(
        w_in_t=w_in_t, w_small_t=w_small_t,
        ssd_conv_w=ssd_conv_w, ssd_conv_b=ssd_conv_b[:, None, :],
        ssd_dtb=jnp.pad(ssd_dt_bias, ((0, 0), (0, pad_h)))[:, None, :],
        ssd_a=jnp.pad(-jnp.exp(ssd_a_log), ((0, 0), (0, pad_h)))[:, None, :],
        ssd_dx=jnp.repeat(ssd_d, SSD_HEADDIM, axis=1)[:, None, :],
        ssd_norm=ssd_norm[:, None, :], expand=expand,
        gla_wa=wa, gla_ba=gla_ba[:, None, :],
        gla_norm=gla_norm[:, None, :])


def _group(x, bt, L, mem_k, mem_v, st_ssd=None, st_ssd_conv=None, st_gla=None, st_ffn_conv=None):
    return dict(x=x, bt=bt, L=L, mem_k=mem_k, mem_v=mem_v, st_ssd=st_ssd, st_ssd_conv=st_ssd_conv, st_gla=st_gla,
                st_ffn_conv=st_ffn_conv, cfg=_group_cfg(bt, L))


def _run_trunk(groups, wl, big, norm_final):
    main = groups[0]
    cfg = main["cfg"]
    tm = cfg["tm"]
    xs = [g["x"] for g in groups]
    acc = [dict(h=None, c=None, s=None, ffn=[]) for _ in groups]
    rest = lambda vals: vals[1] if len(vals) > 1 else None
    w_mq_b = _cast_bf16(big["w_mq"], tn=1024)
    for i in range(DEPTH):
        normed = [_norm_small(x, big["norm_mix"], wl["w_small_t"], i, tm=g["cfg"]["tm_norm"])
                  for x, g in zip(xs, groups)]
        xn, small = [n[0] for n in normed], [n[1] for n in normed]
        proj = _proj_in(xn[0], wl["w_in_t"], i, tm=cfg["tm_proj"], rider=rest(xn))
        y_ssd, y_gla = [], []
        for g, a, p, sm in zip(groups, acc, proj, small):
            y, a["h"], a["c"] = _ssd(p, sm, g["st_ssd_conv"], g["st_ssd"], wl, i, a["h"], a["c"],
                                     bt=g["bt"], L=g["L"], Q=g["cfg"]["ssd_q"])
            y_ssd.append(y)
            y, a["s"] = _gla(p, sm, g["st_gla"], wl, i, a["s"], bt=g["bt"], L=g["L"], TB=g["cfg"]["gla_tb"])
            y_gla.append(y)
        merged = _merge(y_ssd[0], y_gla[0], proj[0], big["w_branch"], i, tm=tm, tn=512,
                        rider=rest(list(zip(y_ssd, y_gla, proj))))
        xs = _mm_res(merged[0], big["w_out"], i, xs[0], tm=tm, tn=1024, rider=rest(list(zip(merged, xs))))
        o = []
        for g, x in zip(groups, xs):
            q = _norm_mm(x, big["norm_mem"], w_mq_b, i, None, tm=g["cfg"]["tm"], tn=1024, out_dtype=BF16)
            o.append(_attn(q, g["mem_k"], g["mem_v"], i, bt=g["bt"], L=g["L"], tl=g["cfg"]["attn_tl"]))
        xs = _mm_res(o[0], big["w_mo"], i, xs[0], tm=tm, tn=1024, rider=rest(list(zip(o, xs))))
        act = []
        for g, a, x in zip(groups, acc, xs):
            xf = _norm_cast(x, big["norm_ffn"], i, tm=g["cfg"]["tm_norm"])
            y, f_c = _ffn_up(xf, big["w_ffn_in"], big["ffn_conv_w"], big["ffn_conv_b"], g["st_ffn_conv"], i,
                             L=g["L"], tm=g["cfg"]["tm"], tn=512)
            act.append(y)
            a["ffn"].append(f_c)
        xs = _mm_res(act[0], big["w_ffn_out"], i, xs[0], tm=cfg["tm_norm"], tn=512, rider=rest(list(zip(act, xs))))
    return [(_final_norm(x, norm_final, tm=g["cfg"]["tm_norm"]), a["h"], a["c"], a["s"], jnp.stack(a["ffn"]))
            for g, a, x in zip(groups, acc, xs)]


def _group_cfg(bt, L):
    T = bt * L
    return dict(tm=min(T, 1024), tm_proj=min(T, 1024), tm_norm=min(T, 512), ssd_q=min(L, 128), gla_tb=min(L, 128), attn_tl=min(L, 512))


def kernel(x_prompt, x_sample, mem_prompt, state_ssd, state_ssd_conv, state_gla, state_ffn_conv, cache_mem_k, cache_mem_v, norm_mix, w_in, ssd_conv_w, ssd_conv_b, ssd_dt_bias, ssd_a_log, ssd_d, ssd_norm, gla_wa2, gla_ba, gla_norm, w_branch, w_out, norm_mem, w_mq, w_mk, w_mv, w_mo, norm_ffn, w_ffn_in, ffn_conv_w, ffn_conv_b, w_ffn_out, norm_final):
    wl = _prep_weights(w_in, ssd_conv_w, ssd_conv_b, ssd_dt_bias, ssd_a_log, ssd_d, ssd_norm, gla_wa2, gla_ba, gla_norm)
    big = dict(norm_mix=norm_mix, w_branch=w_branch, w_out=w_out, norm_mem=norm_mem, w_mq=w_mq, w_mo=w_mo,
               norm_ffn=norm_ffn, w_ffn_in=w_ffn_in, ffn_conv_w=ffn_conv_w, ffn_conv_b=ffn_conv_b, w_ffn_out=w_ffn_out)
    pb, pl_len, _ = x_prompt.shape
    sb, sl_len, _ = x_sample.shape
    mem2d = mem_prompt.reshape(pb * MEM_TOKENS, D_MODEL)
    p_mem_k = _mem_kv(mem2d, w_mk, nb=pb)
    p_mem_v = _mem_kv(mem2d, w_mv, nb=pb)
    head_major = (0, 1, 3, 2, 4)

    prompt = _group(x_prompt.reshape(pb * pl_len, D_MODEL), pb, pl_len, p_mem_k, p_mem_v)
    sample = _group(x_sample.reshape(sb * sl_len, D_MODEL), sb, sl_len,
                    jnp.transpose(cache_mem_k, head_major), jnp.transpose(cache_mem_v, head_major),
                    state_ssd, state_ssd_conv, state_gla, state_ffn_conv)
    (y_p, p_ssd, p_ssd_conv, p_gla, p_ffn), (y_s, s_ssd, s_ssd_conv, s_gla, s_ffn) = _run_trunk(
        [prompt, sample], wl, big, norm_final)
    return (y_p.reshape(pb, pl_len, D_MODEL), y_s.reshape(sb, sl_len, D_MODEL),
            p_ssd, p_ssd_conv, p_gla, p_ffn, jnp.transpose(p_mem_k, head_major), jnp.transpose(p_mem_v, head_major),
            s_ssd, s_ssd_conv, s_gla, s_ffn)
```
